```python
import math, functools
import jax, jax.numpy as jnp
from jax import lax
import numpy as np

D_MODEL = 1024
BATCH = 32
SEQ = 256
DEPTH = 2
DEC_BATCH = 2
DEC_SEQ = 2048
PAST_LEN = 256

GRID_W = 64
HEAD_DIM = 64
NA_HEADS = 4
NA_WIN_R = 8
NA_WIN_C = 16
MLA_HEADS = 4
MLA_NOPE = 64
MLA_ROPE = 32
MLA_V = 64
MLA_Q_RANK = 192
MLA_KV_RANK = 128
SWA_HEADS = 4
SWA_KV_HEADS = 2
SWA_WINDOW = 128
SWA_BLOCK = 128
POOL_WINDOWS = (2, 4, 8, 16)
POOL_GROUP = 64
N_POOL = len(POOL_WINDOWS)
POOL_WIDTH = N_POOL * POOL_GROUP
BRANCH_W = 256
N_BRANCH = 4
D_FF = 2752
CONV_W = 3
Q_BLOCK = 128
ROPE_BASE = 10000.0
EPS = 1e-6
NEG_INF = -1e30
ATT_SCALE = HEAD_DIM ** -0.5
MLA_SCALE = (MLA_NOPE + MLA_ROPE) ** -0.5

IN_A = 3 * NA_HEADS * HEAD_DIM
IN_B = MLA_Q_RANK + MLA_KV_RANK + MLA_ROPE
IN_C = (SWA_HEADS + 2 * SWA_KV_HEADS) * HEAD_DIM
IN_D = POOL_WIDTH
IN_DIM = IN_A + IN_B + IN_C + IN_D

kernel_name = 'hybrid_diffusion_prefix_trunk_step'


def rmsnorm(x, g):
    xf = x.astype(jnp.float32)
    y = xf * lax.rsqrt(jnp.mean(xf * xf, -1, keepdims=True) + EPS)
    return (y * g.astype(jnp.float32)).astype(x.dtype)


def _heads(t, n):
    return t.reshape(*t.shape[:-1], n, t.shape[-1] // n)


def _rope_1d(x, pos):
    half = x.shape[-1] // 2
    inv = ROPE_BASE ** (-jnp.arange(half, dtype=jnp.float32) / half)
    ang = pos.astype(jnp.float32)[:, None] * inv[None, :]
    cos = jnp.cos(ang)[:, None, :]
    sin = jnp.sin(ang)[:, None, :]
    xf = x.astype(jnp.float32)
    x1, x2 = xf[..., :half], xf[..., half:]
    return jnp.concatenate([x1 * cos - x2 * sin, x1 * sin + x2 * cos], -1).astype(x.dtype)


def rope_2d(x):
    t = jnp.arange(x.shape[1])
    h = x.shape[-1] // 2
    return jnp.concatenate([_rope_1d(x[..., :h], t // GRID_W), _rope_1d(x[..., h:], t % GRID_W)], -1)


def sink_softmax(s, sink):
    m = jnp.maximum(jnp.max(s, -1, keepdims=True), sink)
    e = jnp.exp(s - m)
    return e / (jnp.sum(e, -1, keepdims=True) + jnp.exp(sink - m))


def sweep_queries(fn, *qs):
    blocks = tuple(jnp.moveaxis(t.reshape(t.shape[0], t.shape[1] // Q_BLOCK, Q_BLOCK, *t.shape[2:]), 1, 0)
                   for t in qs)
    out = jnp.moveaxis(lax.map(lambda bl: fn(*bl), blocks), 0, 1)
    return out.reshape(out.shape[0], out.shape[1] * out.shape[2], *out.shape[3:])


def softmax_attend(q, k, v):
    s = jnp.einsum('bqhd,bkhd->bhqk', q, k).astype(jnp.float32) * ATT_SCALE
    p = jax.nn.softmax(s, -1).astype(v.dtype)
    return jnp.einsum('bhqk,bkhd->bqhd', p, v)


def gqa_sink_attend(q, k, v, sink):
    B, Lq, H, d = q.shape
    kvh = k.shape[2]
    g = H // kvh
    qg = q.reshape(B, Lq, kvh, g, d)
    s = jnp.einsum('bqhgd,bkhd->bhgqk', qg, k).astype(jnp.float32) * ATT_SCALE
    p = sink_softmax(s, sink.reshape(1, kvh, g, 1, 1).astype(jnp.float32)).astype(v.dtype)
    return jnp.einsum('bhgqk,bkhd->bqhgd', p, v).reshape(B, Lq, H, d)


def mla_queries(c_q, q_norm, w_uq):
    q = _heads(rmsnorm(c_q, q_norm) @ w_uq, MLA_HEADS)
    return q[..., :MLA_NOPE], q[..., MLA_NOPE:]


def mla_keys_values(c_kv, kv_norm, w_ukv):
    kv = _heads(rmsnorm(c_kv, kv_norm) @ w_ukv, MLA_HEADS)
    return kv[..., :MLA_NOPE], kv[..., MLA_NOPE:]


def mla_attend(qn, qr, kn, kr, v):
    s = (jnp.einsum('bqhd,bkhd->bhqk', qn, kn) + jnp.einsum('bqhr,bkr->bhqk', qr, kr)).astype(jnp.float32) * MLA_SCALE
    p = jax.nn.softmax(s, -1).astype(v.dtype)
    return jnp.einsum('bhqk,bkhd->bqhd', p, v)


def na_latent(q, k, v, k_ctx, v_ctx, rpb):
    B, L, H, d = q.shape
    rows = L // GRID_W
    kr = min(NA_WIN_R, rows)
    qg = q.reshape(B, rows, GRID_W, H, d)
    kg = k.reshape(B, rows, GRID_W, H, d)
    vg = v.reshape(B, rows, GRID_W, H, d)
    r = jnp.arange(rows)
    r_idx = jnp.clip(r - kr // 2, 0, rows - kr)[:, None] + jnp.arange(kr)[None, :]
    cq = jnp.arange(GRID_W)
    c_start = jnp.clip(cq - NA_WIN_C // 2, 0, GRID_W - NA_WIN_C)
    col_ok = (cq[None, :] >= c_start[:, None]) & (cq[None, :] < c_start[:, None] + NA_WIN_C)
    k_rows = kg[:, r_idx]
    v_rows = vg[:, r_idx]
    s = jnp.einsum('brchd,brjwhd->bhrcjw', qg, k_rows).astype(jnp.float32) * ATT_SCALE
    dr = r_idx - r[:, None] + (NA_WIN_R - 1)
    dc = jnp.clip(cq[None, :] - cq[:, None] + (NA_WIN_C - 1), 0, 2 * NA_WIN_C - 2)
    bias = rpb[:, dr[:, None, :, None], dc[None, :, None, :]].astype(jnp.float32)
    s = jnp.where(col_ok[:, None, :], s + bias[None], NEG_INF)
    s = s.reshape(B, H, rows, GRID_W, kr * GRID_W)
    s_ctx = jnp.einsum('brchd,bkhd->bhrck', qg, k_ctx).astype(jnp.float32) * ATT_SCALE
    p = jax.nn.softmax(jnp.concatenate([s, s_ctx], -1), -1).astype(v.dtype)
    p_lat = p[..., :kr * GRID_W].reshape(B, H, rows, GRID_W, kr, GRID_W)
    y = (jnp.einsum('bhrcjw,brjwhd->brchd', p_lat, v_rows)
         + jnp.einsum('bhrck,bkhd->brchd', p[..., kr * GRID_W:], v_ctx))
    return y.reshape(B, L, H * d)


def swa_latent(q, k, v, k_ctx, v_ctx, sink):
    B, L, H, d = q.shape
    kvh = k.shape[2]
    g = H // kvh
    nb = L // SWA_BLOCK
    side = -(-SWA_WINDOW // SWA_BLOCK)
    span = (2 * side + 1) * SWA_BLOCK
    pad = side * SWA_BLOCK
    k_pad = jnp.pad(k, ((0, 0), (pad, pad), (0, 0), (0, 0)))
    v_pad = jnp.pad(v, ((0, 0), (pad, pad), (0, 0), (0, 0)))
    idx = jnp.arange(nb)[:, None] * SWA_BLOCK + jnp.arange(span)[None, :]
    kb = k_pad[:, idx]
    vb = v_pad[:, idx]
    qb = q.reshape(B, nb, SWA_BLOCK, kvh, g, d)
    s = jnp.einsum('bnqhgd,bnjhd->bhgnqj', qb, kb).astype(jnp.float32) * ATT_SCALE
    q_pos = jnp.arange(L).reshape(nb, SWA_BLOCK)
    k_pos = (idx - pad)[:, None, :]
    valid = (jnp.abs(q_pos[:, :, None] - k_pos) <= SWA_WINDOW) & (k_pos >= 0) & (k_pos < L)
    s = jnp.where(valid, s, NEG_INF)
    s_ctx = jnp.einsum('bnqhgd,bkhd->bhgnqk', qb, k_ctx).astype(jnp.float32) * ATT_SCALE
    p = sink_softmax(jnp.concatenate([s, s_ctx], -1),
                     sink.reshape(1, kvh, g, 1, 1, 1).astype(jnp.float32)).astype(v.dtype)
    y = (jnp.einsum('bhgnqj,bnjhd->bnqhgd', p[..., :span], vb)
         + jnp.einsum('bhgnqk,bkhd->bnqhgd', p[..., span:], v_ctx))
    return y.reshape(B, L, H * d)


def swa_split(pc):
    q, k, v = jnp.split(pc, [SWA_HEADS * HEAD_DIM, (SWA_HEADS + SWA_KV_HEADS) * HEAD_DIM], -1)
    return _heads(q, SWA_HEADS), _heads(k, SWA_KV_HEADS), _heads(v, SWA_KV_HEADS)


def pool_mixer(x, pool_w, pool_scale):
    B, L, _ = x.shape
    xf = x.astype(jnp.float32)
    cs = jnp.concatenate([jnp.zeros((B, 1, POOL_WIDTH), jnp.float32), jnp.cumsum(xf, axis=1)], 1)
    t = jnp.arange(L)
    diffs = []
    for gi, w in enumerate(POOL_WINDOWS):
        lo = jnp.clip(t - w // 2, 0, L)
        hi = jnp.clip(t - w // 2 + w, 0, L)
        sl = slice(gi * POOL_GROUP, (gi + 1) * POOL_GROUP)
        mean = (cs[:, hi, sl] - cs[:, lo, sl]) / (hi - lo).astype(jnp.float32)[:, None]
        diffs.append(mean - xf[:, :, sl])
    dlt = jnp.stack(diffs, 2).astype(x.dtype)
    y = jnp.einsum('blgc,gce->blge', dlt, pool_w).reshape(B, L, POOL_WIDTH)
    return y * pool_scale


def conv_ffn(h, w_up, conv_w, w_down):
    L = h.shape[1]
    u = h @ w_up
    pad = CONV_W // 2
    up = jnp.pad(u, ((0, 0), (pad, CONV_W - 1 - pad), (0, 0)))
    uc = up[:, 0:L] * conv_w[0]
    for j in range(1, CONV_W):
        uc = uc + up[:, j:j + L] * conv_w[j]
    a, gt = jnp.split(uc, 2, -1)
    return (jax.nn.silu(gt) * a) @ w_down


def context_mixers(pa, pb, pc, pd, lp):
    B, L, _ = pa.shape
    qa, ka, va = (_heads(t, NA_HEADS) for t in jnp.split(pa, 3, -1))
    ya = sweep_queries(lambda q: softmax_attend(q, ka, va), qa).reshape(B, L, BRANCH_W)
    c_q, c_kv, k_rope = jnp.split(pb, [MLA_Q_RANK, MLA_Q_RANK + MLA_KV_RANK], -1)
    qn, qr = mla_queries(c_q, lp['mla_q_norm'], lp['mla_w_uq'])
    kn, vb = mla_keys_values(c_kv, lp['mla_kv_norm'], lp['mla_w_ukv'])
    yb = sweep_queries(lambda a, b_: mla_attend(a, b_, kn, k_rope, vb), qn, qr).reshape(B, L, BRANCH_W)
    qc, kc, vc = swa_split(pc)
    yc = sweep_queries(lambda q: gqa_sink_attend(q, kc, vc, lp['swa_sink']), qc).reshape(B, L, BRANCH_W)
    yd = pool_mixer(pd, lp['pool_w'], lp['pool_scale'])
    return (ya, yb, yc, yd), (ka, va, c_kv, k_rope, kc, vc)


def latent_mixers(pa, pb, pc, pd, lp, cache):
    na_k, na_v, mla_ckv, mla_krope, swa_k, swa_v = cache
    B, L, _ = pa.shape
    qa, ka, va = (_heads(t, NA_HEADS) for t in jnp.split(pa, 3, -1))
    ya = na_latent(qa, ka, va, na_k, na_v, lp['na_rpb'])
    c_q, c_kv, k_rope = jnp.split(pb, [MLA_Q_RANK, MLA_Q_RANK + MLA_KV_RANK], -1)
    qn, qr = mla_queries(c_q, lp['mla_q_norm'], lp['mla_w_uq'])
    qr = rope_2d(qr)
    kn, vb = mla_keys_values(c_kv, lp['mla_kv_norm'], lp['mla_w_ukv'])
    kr = rope_2d(k_rope[:, :, None, :])[:, :, 0]
    kn_ctx, v_ctx = mla_keys_values(mla_ckv, lp['mla_kv_norm'], lp['mla_w_ukv'])
    kn_all = jnp.concatenate([kn_ctx, kn], 1)
    kr_all = jnp.concatenate([mla_krope, kr], 1)
    v_all = jnp.concatenate([v_ctx, vb], 1)
    yb = sweep_queries(lambda a, b_: mla_attend(a, b_, kn_all, kr_all, v_all), qn, qr).reshape(B, L, BRANCH_W)
    qc, kc, vc = swa_split(pc)
    yc = swa_latent(rope_2d(qc), rope_2d(kc), vc, swa_k, swa_v, lp['swa_sink'])
    yd = pool_mixer(pd, lp['pool_w'], lp['pool_scale'])
    return (ya, yb, yc, yd), ()


def sandwich_layer(x, cvec, mixers, lp):
    mod = jax.nn.silu(cvec) @ lp['w_mod'] + lp['b_mod']
    sh1, sc1, g1, sh2, sc2, g2 = jnp.split(mod, 6, -1)
    h = rmsnorm(x, lp['g_attn_pre']) * (1 + sc1) + sh1
    pa, pb, pc, pd = jnp.split(h @ lp['w_in'], [IN_A, IN_A + IN_B, IN_A + IN_B + IN_C], -1)
    branches, ctx_state = mixers(pa, pb, pc, pd)
    br = jnp.stack(branches, 2)
    gates = jax.nn.sigmoid(h @ lp['w_gate'] + lp['b_gate']).reshape(h.shape[0], h.shape[1], N_BRANCH, D_MODEL)
    merged = jnp.sum(gates * jnp.einsum('blkc,kcd->blkd', br, lp['w_branch']), 2)
    x = x + g1 * rmsnorm(merged @ lp['w_out'], lp['g_attn_post'])
    h = rmsnorm(x, lp['g_ffn_pre']) * (1 + sc2) + sh2
    x = x + g2 * rmsnorm(conv_ffn(h, lp['ffn_w_up'], lp['ffn_conv'], lp['ffn_w_down']), lp['g_ffn_post'])
    return x, ctx_state


def setup_inputs(seed: int = 0) -> dict:
    key = jax.random.key(seed)
    ks = jax.random.split(key, 34)
    f32 = jnp.float32

    def nrm(i, shape, scale=1.0):
        return jax.random.normal(ks[i], shape, f32) * scale

    D = D_MODEL
    return {
        'x_prompt': nrm(0, (BATCH, SEQ, D)),
        'x_sample': nrm(1, (DEC_BATCH, DEC_SEQ, D)),
        'cache_na_k': nrm(2, (DEC_BATCH, DEPTH, PAST_LEN, NA_HEADS, HEAD_DIM)),
        'cache_na_v': nrm(3, (DEC_BATCH, DEPTH, PAST_LEN, NA_HEADS, HEAD_DIM)),
        'cache_mla_ckv': nrm(4, (DEC_BATCH, DEPTH, PAST_LEN, MLA_KV_RANK)),
        'cache_mla_krope': nrm(5, (DEC_BATCH, DEPTH, PAST_LEN, MLA_ROPE)),
        'cache_swa_k': nrm(6, (DEC_BATCH, DEPTH, PAST_LEN, SWA_KV_HEADS, HEAD_DIM)),
        'cache_swa_v': nrm(7, (DEC_BATCH, DEPTH, PAST_LEN, SWA_KV_HEADS, HEAD_DIM)),
        'c': nrm(8, (DEC_BATCH, D)),
        'c_ctx': nrm(9, (D,)),
        'w_mod': nrm(10, (DEPTH, D, 6 * D), 0.3 * D ** -0.5),
        'b_mod': nrm(11, (DEPTH, 6 * D), 0.02),
        'g_attn_pre': 1.0 + nrm(12, (DEPTH, D), 0.05),
        'g_attn_post': 1.0 + nrm(13, (DEPTH, D), 0.05),
        'g_ffn_pre': 1.0 + nrm(14, (DEPTH, D), 0.05),
        'g_ffn_post': 1.0 + nrm(15, (DEPTH, D), 0.05),
        'w_in': nrm(16, (DEPTH, D, IN_DIM), D ** -0.5),
        'w_gate': nrm(17, (DEPTH, D, N_BRANCH * D), D ** -0.5),
        'b_gate': nrm(18, (DEPTH, N_BRANCH * D), 0.02),
        'na_rpb': nrm(19, (DEPTH, NA_HEADS, 2 * NA_WIN_R - 1, 2 * NA_WIN_C - 1), 0.1),
        'mla_q_norm': 1.0 + nrm(20, (DEPTH, MLA_Q_RANK), 0.05),
        'mla_w_uq': nrm(21, (DEPTH, MLA_Q_RANK, MLA_HEADS * (MLA_NOPE + MLA_ROPE)), MLA_Q_RANK ** -0.5),
        'mla_kv_norm': 1.0 + nrm(22, (DEPTH, MLA_KV_RANK), 0.05),
        'mla_w_ukv': nrm(23, (DEPTH, MLA_KV_RANK, MLA_HEADS * (MLA_NOPE + MLA_V)), MLA_KV_RANK ** -0.5),
        'swa_sink': nrm(24, (DEPTH, SWA_HEADS), 0.5),
        'pool_w': nrm(25, (DEPTH, N_POOL, POOL_GROUP, POOL_GROUP), POOL_GROUP ** -0.5),
        'pool_scale': 1.0 + nrm(26, (DEPTH, POOL_WIDTH), 0.1),
        'w_branch': nrm(27, (DEPTH, N_BRANCH, BRANCH_W, D), BRANCH_W ** -0.5),
        'w_out': nrm(28, (DEPTH, D, D), D ** -0.5),
        'ffn_w_up': nrm(29, (DEPTH, D, 2 * D_FF), D ** -0.5),
        'ffn_conv': nrm(30, (DEPTH, CONV_W, 2 * D_FF), CONV_W ** -0.5),
        'ffn_w_down': nrm(31, (DEPTH, D_FF, D), D_FF ** -0.5),
    }


def reference(x_prompt, x_sample, cache_na_k, cache_na_v, cache_mla_ckv, cache_mla_krope, cache_swa_k,
              cache_swa_v, c, c_ctx, w_mod, b_mod, g_attn_pre, g_attn_post, g_ffn_pre, g_ffn_post, w_in,
              w_gate, b_gate, na_rpb, mla_q_norm, mla_w_uq, mla_kv_norm, mla_w_ukv, swa_sink, pool_w,
              pool_scale, w_branch, w_out, ffn_w_up, ffn_conv, ffn_w_down):
    x_p, x_s = x_prompt, x_sample
    c_ctx_vec = c_ctx[None, None, :]
    c_lat = c[:, None, :]
    states = []
    for i in range(DEPTH):
        lp = dict(w_mod=w_mod[i], b_mod=b_mod[i], g_attn_pre=g_attn_pre[i], g_attn_post=g_attn_post[i],
                  g_ffn_pre=g_ffn_pre[i], g_ffn_post=g_ffn_post[i], w_in=w_in[i], w_gate=w_gate[i],
                  b_gate=b_gate[i], na_rpb=na_rpb[i], mla_q_norm=mla_q_norm[i], mla_w_uq=mla_w_uq[i],
                  mla_kv_norm=mla_kv_norm[i], mla_w_ukv=mla_w_ukv[i], swa_sink=swa_sink[i],
                  pool_w=pool_w[i], pool_scale=pool_scale[i], w_branch=w_branch[i], w_out=w_out[i],
                  ffn_w_up=ffn_w_up[i], ffn_conv=ffn_conv[i], ffn_w_down=ffn_w_down[i])
        x_p, st = sandwich_layer(x_p, c_ctx_vec, functools.partial(context_mixers, lp=lp), lp)
        states.append(st)
        cache = (cache_na_k[:, i], cache_na_v[:, i], cache_mla_ckv[:, i], cache_mla_krope[:, i],
                 cache_swa_k[:, i], cache_swa_v[:, i])
        x_s, _ = sandwich_layer(x_s, c_lat, functools.partial(latent_mixers, lp=lp, cache=cache), lp)
    new_na_k = jnp.stack([st[0] for st in states], 1)
    new_na_v = jnp.stack([st[1] for st in states], 1)
    new_mla_ckv = jnp.stack([st[2] for st in states], 1)
    new_mla_krope = jnp.stack([st[3] for st in states], 1)
    new_swa_k = jnp.stack([st[4] for st in states], 1)
    new_swa_v = jnp.stack([st[5] for st in states], 1)
    return (x_p, x_s, new_na_k, new_na_v, new_mla_ckv, new_mla_krope, new_swa_k, new_swa_v)
```

```python
import functools

import jax
import jax.numpy as jnp
import numpy as np
from jax import lax
from jax.experimental import pallas as pl
from jax.experimental.pallas import tpu as pltpu

F32 = jnp.float32
BF16 = jnp.bfloat16

D_MODEL = 1024
BATCH = 32
SEQ = 256
DEPTH = 2
DEC_BATCH = 2
DEC_SEQ = 2048
PAST_LEN = 256
GRID_W = 64
HEAD_DIM = 64
NA_HEADS = 4
NA_WIN_R = 8
NA_WIN_C = 16
MLA_HEADS = 4
MLA_NOPE = 64
MLA_ROPE = 32
MLA_V = 64
MLA_Q_RANK = 192
MLA_KV_RANK = 128
SWA_HEADS = 4
SWA_KV_HEADS = 2
SWA_WINDOW = 128
POOL_WINDOWS = (2, 4, 8, 16)
POOL_GROUP = 64
POOL_WIDTH = 256
BRANCH_W = 256
N_BRANCH = 4
D_FF = 2752
ROPE_BASE = 10000.0
EPS = 1e-6
NEG_INF = -1e30
ATT_SCALE = HEAD_DIM ** -0.5
MLA_SCALE = (MLA_NOPE + MLA_ROPE) ** -0.5

LANE = 128
BF16_SUBLANE = 16
D_FF_PAD = 2816
FF_CHUNK = 256
IN_PAD = 2048
Q_BLOCK = 128
NA_SPAN = 640
SWA_SPAN = 384
VMEM_LIMIT = 56 * 1024 * 1024

_QA, _KA, _VA, _CQ, _CKV, _KR, _QC, _KC, _VC, _PD = 0, 256, 512, 768, 1024, 1152, 1280, 1536, 1664, 1792
_KR_LANE = 64


def _dot(a, b):
    return jnp.dot(a, b, preferred_element_type=F32)


def _dot_nt(a, b):
    return lax.dot_general(a, b, (((1,), (1,)), ((), ())), preferred_element_type=F32)


def _sigmoid(x):
    return 1.0 / (1.0 + jnp.exp(-x))


def _rms(x, g, n=None):
    n = x.shape[-1] if n is None else n
    ms = jnp.sum(x * x, -1, keepdims=True) * (1.0 / n)
    return x * lax.rsqrt(ms + EPS) * g


def _softmax_blocks(blocks, sink=None):
    m = None
    for s in blocks:
        mm = jnp.max(s, -1, keepdims=True)
        m = mm if m is None else jnp.maximum(m, mm)
    if sink is not None:
        m = jnp.maximum(m, sink)
    es = [jnp.exp(s - m) for s in blocks]
    l = None
    for e in es:
        ll = jnp.sum(e, -1, keepdims=True)
        l = ll if l is None else l + ll
    if sink is not None:
        l = l + jnp.exp(sink - m)
    return es, l


def _rope(x, cos, sin, q):
    w = x.shape[-1]
    lane = lax.broadcasted_iota(jnp.int32, x.shape, 1)
    up = pltpu.roll(x, w - q, axis=1)
    dn = pltpu.roll(x, q, axis=1)
    partner = jnp.where((lane & (2 * q - 1)) < q, up, dn)
    return x * cos + partner * sin


def _const_spec(shape):
    n = len(shape)
    return pl.BlockSpec(shape, lambda *_: (0,) * n)


def _params(n_axes):
    return pltpu.CompilerParams(dimension_semantics=("arbitrary",) * n_axes, vmem_limit_bytes=VMEM_LIMIT)


def _mod_kernel(cv_ref, w_ref, b_ref, o_ref):
    cv = cv_ref[...]
    a = (cv * _sigmoid(cv)).astype(BF16)
    o_ref[0] = _dot(a, w_ref[0].astype(BF16)) + b_ref[0]


def _mod_call(cv, w_mod, b_mod):
    tn = 1024
    return pl.pallas_call(
        _mod_kernel,
        grid=(DEPTH, 6 * D_MODEL // tn),
        in_specs=[_const_spec((8, D_MODEL)),
                  pl.BlockSpec((1, D_MODEL, tn), lambda l, j: (l, 0, j)),
                  pl.BlockSpec((1, 1, tn), lambda l, j: (l, 0, j))],
        out_specs=pl.BlockSpec((1, 8, tn), lambda l, j: (l, 0, j)),
        out_shape=jax.ShapeDtypeStruct((DEPTH, 8, 6 * D_MODEL), F32),
        compiler_params=_params(2),
        name="mod",
    )(cv, w_mod, b_mod.reshape(DEPTH, 1, 6 * D_MODEL))


_IN_SLOTS = ((_QA, 256), (_KA, 256), (_VA, 256), (_CQ, 256), (_CKV, 128), (_KR, 128),
             (_QC, 256), (_KC, 128), (_VC, 128), (_PD, 256))


def _inproj_kernel(*refs, latent):
    if latent:
        x_ref, mod_ref, g_ref, w_ref, c64_ref, s64_ref, cm_ref, sm_ref = refs[:8]
        outs = refs[8:]
    else:
        x_ref, mod_ref, g_ref, w_ref = refs[:4]
        outs = refs[4:]
    x = x_ref[...]
    h = _rms(x, g_ref[...]) * (1.0 + mod_ref[0, 1:2, :]) + mod_ref[0, 0:1, :]
    p = _dot(h.astype(BF16), w_ref[...])
    for (off, wd), o_ref in zip(_IN_SLOTS, outs):
        v = p[:, off:off + wd]
        if latent:
            if off == _QC:
                v = _rope(v, c64_ref[...], s64_ref[...], 16)
            elif off == _KC:
                v = _rope(v, c64_ref[:, :128], s64_ref[:, :128], 16)
            elif off == _KR:
                v = _rope(v, cm_ref[:, :128], sm_ref[:, :128], 8)
        o_ref[...] = v.astype(o_ref.dtype)


def _inproj_call(x2d, mod, g, w_in_p, tiles_per_batch, rope_tabs=None):
    t = x2d.shape[0]
    tm = 256
    latent = rope_tabs is not None
    dt = BF16 if latent else F32
    in_specs = [pl.BlockSpec((tm, D_MODEL), lambda i: (i, 0)),
                pl.BlockSpec((1, 6, D_MODEL), lambda i: (i // tiles_per_batch, 0, 0)),
                _const_spec((1, D_MODEL)),
                _const_spec((D_MODEL, IN_PAD))]
    args = [x2d, mod, g, w_in_p]
    if latent:
        c64, s64, cm, sm = rope_tabs
        in_specs += [pl.BlockSpec((tm, 256), lambda i: (i % tiles_per_batch, 0)),
                     pl.BlockSpec((tm, 256), lambda i: (i % tiles_per_batch, 0)),
                     pl.BlockSpec((tm, 512), lambda i: (i % tiles_per_batch, 0)),
                     pl.BlockSpec((tm, 512), lambda i: (i % tiles_per_batch, 0))]
        args += [c64, s64, cm, sm]
    return pl.pallas_call(
        functools.partial(_inproj_kernel, latent=latent),
        grid=(t // tm,),
        in_specs=in_specs,
        out_specs=[pl.BlockSpec((tm, wd), lambda i: (i, 0)) for _, wd in _IN_SLOTS],
        out_shape=[jax.ShapeDtypeStruct((t, wd), dt) for _, wd in _IN_SLOTS],
        compiler_params=_params(1),
        name="inproj_lat" if latent else "inproj_ctx",
    )(*args)


def _ctx_attn_kernel(q_ref, k_ref, v_ref, o_ref):
    q = q_ref[...].astype(BF16)
    k = k_ref[...].astype(BF16)
    v = v_ref[...].astype(BF16)
    ys = []
    for h in range(NA_HEADS):
        sl = slice(h * HEAD_DIM, (h + 1) * HEAD_DIM)
        (e,), l = _softmax_blocks([_dot_nt(q[:, sl], k[:, sl])])
        ys.append(_dot(e.astype(BF16), v[:, sl]) / l)
    o_ref[...] = jnp.concatenate(ys, -1).astype(o_ref.dtype)


def _ctx_attn_call(q, k, v):
    t = q.shape[0]
    spec = pl.BlockSpec((SEQ, 256), lambda b: (b, 0))
    return pl.pallas_call(
        _ctx_attn_kernel, grid=(t // SEQ,), in_specs=[spec, spec, spec], out_specs=spec,
        out_shape=jax.ShapeDtypeStruct((t, 256), BF16), compiler_params=_params(1), name="ctx_attn",
    )(q, k, v)


def _ctx_swa_kernel(sink_ref, q_ref, k_ref, v_ref, o_ref):
    q = q_ref[...].astype(BF16)
    k = k_ref[...].astype(BF16)
    v = v_ref[...].astype(BF16)
    ys = []
    for h in range(SWA_HEADS):
        kv = h // (SWA_HEADS // SWA_KV_HEADS)
        sl = slice(h * HEAD_DIM, (h + 1) * HEAD_DIM)
        kvsl = slice(kv * HEAD_DIM, (kv + 1) * HEAD_DIM)
        (e,), l = _softmax_blocks([_dot_nt(q[:, sl], k[:, kvsl])], sink=sink_ref[h])
        ys.append(_dot(e.astype(BF16), v[:, kvsl]) / l)
    o_ref[...] = jnp.concatenate(ys, -1).astype(o_ref.dtype)


def _ctx_swa_call(sink, q, k, v):
    t = q.shape[0]
    return pl.pallas_call(
        _ctx_swa_kernel, grid=(t // SEQ,),
        in_specs=[pl.BlockSpec(memory_space=pltpu.SMEM),
                  pl.BlockSpec((SEQ, 256), lambda b: (b, 0)),
                  pl.BlockSpec((SEQ, 128), lambda b: (b, 0)),
                  pl.BlockSpec((SEQ, 128), lambda b: (b, 0))],
        out_specs=pl.BlockSpec((SEQ, 256), lambda b: (b, 0)),
        out_shape=jax.ShapeDtypeStruct((t, 256), BF16), compiler_params=_params(1), name="ctx_swa",
    )(sink, q, k, v)


def _mla_q(cq, qn_ref, wuq_ref):
    return _dot(_rms(cq, qn_ref[...], MLA_Q_RANK).astype(BF16), wuq_ref[...])


def _mla_kv(ckv, kr, kvn_ref, wuk_ref, wuv_ref):
    cn = _rms(ckv, kvn_ref[...]).astype(BF16)
    kcat = _dot(cn, wuk_ref[...]) + jnp.concatenate([kr] * MLA_HEADS, -1)
    return kcat, _dot(cn, wuv_ref[...])


def _ctx_mla_kernel(cq_ref, ckv_ref, kr_ref, qn_ref, wuq_ref, kvn_ref, wuk_ref, wuv_ref, o_ref):
    q = (_mla_q(cq_ref[...], qn_ref, wuq_ref) * MLA_SCALE).astype(BF16)
    kcat, v = _mla_kv(ckv_ref[...], kr_ref[...], kvn_ref, wuk_ref, wuv_ref)
    kcat = kcat.astype(BF16)
    v = v.astype(BF16)
    ys = []
    for h in range(MLA_HEADS):
        sl = slice(h * LANE, (h + 1) * LANE)
        (e,), l = _softmax_blocks([_dot_nt(q[:, sl], kcat[:, sl])])
        ys.append(_dot(e.astype(BF16), v[:, h * MLA_V:(h + 1) * MLA_V]) / l)
    o_ref[...] = jnp.concatenate(ys, -1).astype(o_ref.dtype)


def _ctx_mla_call(cq, ckv, kr, mla_w):
    t = cq.shape[0]
    qn, wuq, kvn, wuk, wuv = mla_w
    return pl.pallas_call(
        _ctx_mla_kernel, grid=(t // SEQ,),
        in_specs=[pl.BlockSpec((SEQ, 256), lambda b: (b, 0)),
                  pl.BlockSpec((SEQ, 128), lambda b: (b, 0)),
                  pl.BlockSpec((SEQ, 128), lambda b: (b, 0)),
                  _const_spec(qn.shape), _const_spec(wuq.shape), _const_spec(kvn.shape),
                  _const_spec(wuk.shape), _const_spec(wuv.shape)],
        out_specs=pl.BlockSpec((SEQ, 256), lambda b: (b, 0)),
        out_shape=jax.ShapeDtypeStruct((t, 256), BF16), compiler_params=_params(1), name="ctx_mla",
    )(cq, ckv, kr, qn, wuq, kvn, wuk, wuv)


_POOL_PAD = 8


def _pool_kernel(x_ref, w_ref, sc_ref, o_ref):
    x = x_ref[...].astype(F32)
    n = x.shape[0]
    z = jnp.zeros((_POOL_PAD, POOL_WIDTH), F32)
    xz = jnp.concatenate([z, x, z], 0)
    ne = n + 2 * _POOL_PAD

    def pair(a, s):
        return pltpu.roll(a, s, axis=0) + pltpu.roll(a, ne - s, axis=0)

    s2 = xz + pltpu.roll(xz, 1, axis=0)
    s4 = pair(s2, 1)
    s8 = pair(s4, 2)
    s16 = pair(s8, 4)
    lo, hi = _POOL_PAD, _POOL_PAD + n
    grp = lax.broadcasted_iota(jnp.int32, (n, POOL_WIDTH), 1) >> 6
    t = lax.broadcasted_iota(jnp.int32, (n, POOL_WIDTH), 0)
    tot = jnp.where(grp == 0, s2[lo:hi], jnp.where(grp == 1, s4[lo:hi], jnp.where(grp == 2, s8[lo:hi], s16[lo:hi])))
    half = jnp.where(grp == 0, 1, jnp.where(grp == 1, 2, jnp.where(grp == 2, 4, 8)))
    cnt = jnp.minimum(t + half, n) - jnp.maximum(t - half, 0)
    dlt = (tot / cnt.astype(F32) - x).astype(BF16)
    o_ref[...] = (_dot(dlt, w_ref[...]) * sc_ref[...]).astype(o_ref.dtype)


def _pool_call(pd, w_bd, scale, seq):
    t = pd.shape[0]
    return pl.pallas_call(
        _pool_kernel, grid=(t // seq,),
        in_specs=[pl.BlockSpec((seq, POOL_WIDTH), lambda b: (b, 0)),
                  _const_spec((POOL_WIDTH, POOL_WIDTH)), _const_spec((1, POOL_WIDTH))],
        out_specs=pl.BlockSpec((seq, POOL_WIDTH), lambda b: (b, 0)),
        out_shape=jax.ShapeDtypeStruct((t, POOL_WIDTH), BF16), compiler_params=_params(1), name="pool",
    )(pd, w_bd, scale)


def _lat_na_kernel(q_ref, k_ref, v_ref, kc_ref, vc_ref, bias_ref, o_ref):
    n = pl.program_id(1)
    row0 = jnp.clip(2 * n - NA_WIN_R // 2, 0, DEC_SEQ // GRID_W - NA_SPAN // GRID_W)
    start = pl.multiple_of(row0 * GRID_W, LANE)
    q = q_ref[...]
    k = k_ref[pl.ds(start, NA_SPAN), :]
    v = v_ref[pl.ds(start, NA_SPAN), :]
    kc = kc_ref[...].astype(BF16)
    vc = vc_ref[...].astype(BF16)
    ys = []
    for h in range(NA_HEADS):
        sl = slice(h * HEAD_DIM, (h + 1) * HEAD_DIM)
        s_loc = _dot_nt(q[:, sl], k[:, sl]) + bias_ref[0, h]
        s_ctx = _dot_nt(q[:, sl], kc[:, sl])
        (e_loc, e_ctx), l = _softmax_blocks([s_loc, s_ctx])
        y = _dot(e_loc.astype(BF16), v[:, sl]) + _dot(e_ctx.astype(BF16), vc[:, sl])
        ys.append(y / l)
    o_ref[...] = jnp.concatenate(ys, -1).astype(o_ref.dtype)


def _na_variant(n):
    nq = DEC_SEQ // Q_BLOCK
    return jnp.where(n < 2, 1 + n, jnp.where(n > nq - 3, n - (nq - 5), 0))


def _lat_na_call(q, k, v, cache_k, cache_v, bias, layer):
    nq = DEC_SEQ // Q_BLOCK
    seq_spec = pl.BlockSpec((DEC_SEQ, 256), lambda b, n: (b, 0))
    cache_spec = pl.BlockSpec((None, None, PAST_LEN, 256), lambda b, n: (b, layer, 0, 0))
    return pl.pallas_call(
        _lat_na_kernel, grid=(DEC_BATCH, nq),
        in_specs=[pl.BlockSpec((Q_BLOCK, 256), lambda b, n: (b * nq + n, 0)), seq_spec, seq_spec,
                  cache_spec, cache_spec,
                  pl.BlockSpec((1, NA_HEADS, Q_BLOCK, NA_SPAN), lambda b, n: (_na_variant(n), 0, 0, 0))],
        out_specs=pl.BlockSpec((Q_BLOCK, 256), lambda b, n: (b * nq + n, 0)),
        out_shape=jax.ShapeDtypeStruct((DEC_BATCH * DEC_SEQ, 256), BF16),
        compiler_params=_params(2), name="lat_na",
    )(q, k, v, cache_k, cache_v, bias)


def _lat_swa_kernel(sink_ref, q_ref, k_ref, v_ref, kc_ref, vc_ref, o_ref):
    n = pl.program_id(1)
    start = pl.multiple_of(jnp.clip(n - 1, 0, DEC_SEQ // Q_BLOCK - 3) * Q_BLOCK, LANE)
    q = q_ref[...]
    k = k_ref[pl.ds(start, SWA_SPAN), :]
    v = v_ref[pl.ds(start, SWA_SPAN), :]
    kc = kc_ref[...].astype(BF16)
    vc = vc_ref[...].astype(BF16)
    q_pos = n * Q_BLOCK + lax.broadcasted_iota(jnp.int32, (Q_BLOCK, SWA_SPAN), 0)
    k_pos = start + lax.broadcasted_iota(jnp.int32, (Q_BLOCK, SWA_SPAN), 1)
    valid = jnp.abs(q_pos - k_pos) <= SWA_WINDOW
    ys = []
    for h in range(SWA_HEADS):
        kv = h // (SWA_HEADS // SWA_KV_HEADS)
        sl = slice(h * HEAD_DIM, (h + 1) * HEAD_DIM)
        kvsl = slice(kv * HEAD_DIM, (kv + 1) * HEAD_DIM)
        s_loc = jnp.where(valid, _dot_nt(q[:, sl], k[:, kvsl]), NEG_INF)
        s_ctx = _dot_nt(q[:, sl], kc[:, kvsl])
        (e_loc, e_ctx), l = _softmax_blocks([s_loc, s_ctx], sink=sink_ref[h])
        y = _dot(e_loc.astype(BF16), v[:, kvsl]) + _dot(e_ctx.astype(BF16), vc[:, kvsl])
        ys.append(y / l)
    o_ref[...] = jnp.concatenate(ys, -1).astype(o_ref.dtype)


def _lat_swa_call(sink, q, k, v, cache_k, cache_v, layer):
    nq = DEC_SEQ // Q_BLOCK
    seq_spec = pl.BlockSpec((DEC_SEQ, 128), lambda b, n: (b, 0))
    cache_spec = pl.BlockSpec((None, None, PAST_LEN, 128), lambda b, n: (b, layer, 0, 0))
    return pl.pallas_call(
        _lat_swa_kernel, grid=(DEC_BATCH, nq),
        in_specs=[pl.BlockSpec(memory_space=pltpu.SMEM),
                  pl.BlockSpec((Q_BLOCK, 256), lambda b, n: (b * nq + n, 0)), seq_spec, seq_spec,
                  cache_spec, cache_spec],
        out_specs=pl.BlockSpec((Q_BLOCK, 256), lambda b, n: (b * nq + n, 0)),
        out_shape=jax.ShapeDtypeStruct((DEC_BATCH * DEC_SEQ, 256), BF16),
        compiler_params=_params(2), name="lat_swa",
    )(sink, q, k, v, cache_k, cache_v)


def _lat_mla_kernel(cq_ref, ckv_ref, kr_ref, cckv_ref, ckr_ref, cm_ref, sm_ref,
                    qn_ref, wuq_ref, kvn_ref, wuk_ref, wuv_ref, o_ref, kcat_s, v_s):
    @pl.when(pl.program_id(1) == 0)
    def _():
        kc, vc = _mla_kv(cckv_ref[...], ckr_ref[...], kvn_ref, wuk_ref, wuv_ref)
        kcat_s[0:PAST_LEN, :] = kc.astype(BF16)
        v_s[0:PAST_LEN, :] = vc.astype(BF16)
        kl, vl = _mla_kv(ckv_ref[...].astype(F32), kr_ref[...].astype(F32), kvn_ref, wuk_ref, wuv_ref)
        kcat_s[PAST_LEN:, :] = kl.astype(BF16)
        v_s[PAST_LEN:, :] = vl.astype(BF16)

    q = _mla_q(cq_ref[...].astype(F32), qn_ref, wuq_ref)
    q = (_rope(q, cm_ref[...], sm_ref[...], 8) * MLA_SCALE).astype(BF16)
    ys = []
    for h in range(MLA_HEADS):
        sl = slice(h * LANE, (h + 1) * LANE)
        (e,), l = _softmax_blocks([_dot_nt(q[:, sl], kcat_s[:, sl])])
        ys.append(_dot(e.astype(BF16), v_s[:, h * MLA_V:(h + 1) * MLA_V]) / l)
    o_ref[...] = jnp.concatenate(ys, -1).astype(o_ref.dtype)


def _lat_mla_call(cq, ckv, kr, cache_ckv, cache_kr, cm, sm, mla_w, layer):
    nq = DEC_SEQ // Q_BLOCK
    qn, wuq, kvn, wuk, wuv = mla_w
    seq_spec = pl.BlockSpec((DEC_SEQ, 128), lambda b, n: (b, 0))
    cache_spec = pl.BlockSpec((None, None, PAST_LEN, 128), lambda b, n: (b, layer, 0, 0))
    tab_spec = pl.BlockSpec((Q_BLOCK, 512), lambda b, n: (n, 0))
    return pl.pallas_call(
        _lat_mla_kernel, grid=(DEC_BATCH, nq),
        in_specs=[pl.BlockSpec((Q_BLOCK, 256), lambda b, n: (b * nq + n, 0)), seq_spec, seq_spec,
                  cache_spec, cache_spec, tab_spec, tab_spec,
                  _const_spec(qn.shape), _const_spec(wuq.shape), _const_spec(kvn.shape),
                  _const_spec(wuk.shape), _const_spec(wuv.shape)],
        out_specs=pl.BlockSpec((Q_BLOCK, 256), lambda b, n: (b * nq + n, 0)),
        out_shape=jax.ShapeDtypeStruct((DEC_BATCH * DEC_SEQ, 256), BF16),
        scratch_shapes=[pltpu.VMEM((PAST_LEN + DEC_SEQ, MLA_HEADS * LANE), BF16),
                        pltpu.VMEM((PAST_LEN + DEC_SEQ, MLA_HEADS * MLA_V), BF16)],
        compiler_params=_params(2), name="lat_mla",
    )(cq, ckv, kr, cache_ckv, cache_kr, cm, sm, qn, wuq, kvn, wuk, wuv)


def _merge_kernel(x_ref, mod_ref, gpre_ref, ya_ref, yb_ref, yc_ref, yd_ref,
                  wg_ref, bg_ref, wb_ref, wo_ref, gpost_ref, o_ref):
    x = x_ref[...]
    h = (_rms(x, gpre_ref[...]) * (1.0 + mod_ref[0, 1:2, :]) + mod_ref[0, 0:1, :]).astype(BF16)
    merged = None
    for k, y_ref in enumerate((ya_ref, yb_ref, yc_ref, yd_ref)):
        cols = slice(k * D_MODEL, (k + 1) * D_MODEL)
        gate = _sigmoid(_dot(h, wg_ref[:, cols]) + bg_ref[:, cols])
        term = gate * _dot(y_ref[...], wb_ref[k])
        merged = term if merged is None else merged + term
    o = _dot(merged.astype(BF16), wo_ref[...])
    o_ref[...] = x + mod_ref[0, 2:3, :] * _rms(o, gpost_ref[...])


def _merge_call(x2d, mod, gpre, ys, wg, bg, wb, wo, gpost, tiles_per_batch):
    t = x2d.shape[0]
    tm = 256
    tile = pl.BlockSpec((tm, D_MODEL), lambda i: (i, 0))
    ytile = pl.BlockSpec((tm, BRANCH_W), lambda i: (i, 0))
    return pl.pallas_call(
        _merge_kernel, grid=(t // tm,),
        in_specs=[tile, pl.BlockSpec((1, 6, D_MODEL), lambda i: (i // tiles_per_batch, 0, 0)),
                  _const_spec((1, D_MODEL)), ytile, ytile, ytile, ytile,
                  _const_spec(wg.shape), _const_spec(bg.shape), _const_spec(wb.shape),
                  _const_spec(wo.shape), _const_spec((1, D_MODEL))],
        out_specs=tile,
        out_shape=jax.ShapeDtypeStruct((t, D_MODEL), F32),
        compiler_params=_params(1), name="merge",
    )(x2d, mod, gpre, *ys, wg, bg, wb, wo, gpost)


_HALO = BF16_SUBLANE


def _ffn_kernel(*refs, tiles_per_seq):
    halo = tiles_per_seq > 1
    if halo:
        x_ref, xp_ref, xn_ref, mod_ref, gpre_ref, wa_ref, wg_ref, ca_ref, cg_ref, wd_ref, gpost_ref, o_ref = refs
    else:
        x_ref, mod_ref, gpre_ref, wa_ref, wg_ref, ca_ref, cg_ref, wd_ref, gpost_ref, o_ref = refs
    x = x_ref[...]
    tm = x.shape[0]
    shift, scale, gate = mod_ref[0, 3:4, :], mod_ref[0, 4:5, :], mod_ref[0, 5:6, :]

    def pre(xx):
        return _rms(xx, gpre_ref[...]) * (1.0 + scale) + shift

    h = pre(x).astype(BF16)
    if halo:
        i = pl.program_id(0) % tiles_per_seq
        hp = jnp.where(i == 0, 0.0, pre(xp_ref[...])).astype(BF16)
        hn = jnp.where(i == tiles_per_seq - 1, 0.0, pre(xn_ref[...])).astype(BF16)
        h = jnp.concatenate([hp, h, hn], 0)
    rows = h.shape[0]
    row = lax.broadcasted_iota(jnp.int32, (rows, FF_CHUNK), 0)

    def conv(u, c_ref, cols):
        prev = jnp.where(row == 0, 0.0, pltpu.roll(u, 1, axis=0))
        nxt = jnp.where(row == rows - 1, 0.0, pltpu.roll(u, rows - 1, axis=0))
        uc = prev * c_ref[0:1, cols] + u * c_ref[1:2, cols] + nxt * c_ref[2:3, cols]
        return uc[_HALO:_HALO + tm] if halo else uc

    acc = None
    for c in range(D_FF_PAD // FF_CHUNK):
        cols = slice(c * FF_CHUNK, (c + 1) * FF_CHUNK)
        a = conv(_dot(h, wa_ref[:, cols]), ca_ref, cols)
        g = conv(_dot(h, wg_ref[:, cols]), cg_ref, cols)
        act = (g * _sigmoid(g) * a).astype(BF16)
        part = _dot(act, wd_ref[cols, :])
        acc = part if acc is None else acc + part
    o_ref[...] = x + gate * _rms(acc, gpost_ref[...])


def _ffn_call(x2d, mod, gpre, wa, wg, ca, cg, wd, gpost, seq, tm):
    t = x2d.shape[0]
    tiles_per_seq = seq // tm
    tile = pl.BlockSpec((tm, D_MODEL), lambda i: (i, 0))
    in_specs = [tile]
    args = [x2d]
    if tiles_per_seq > 1:
        r = tm // _HALO
        last = t // _HALO - 1
        in_specs += [pl.BlockSpec((_HALO, D_MODEL), lambda i: (jnp.maximum(i * r - 1, 0), 0)),
                     pl.BlockSpec((_HALO, D_MODEL), lambda i: (jnp.minimum((i + 1) * r, last), 0))]
        args += [x2d, x2d]
    tiles_per_mod = t // tm // mod.shape[0]
    in_specs += [pl.BlockSpec((1, 6, D_MODEL), lambda i: (i // tiles_per_mod, 0, 0)),
                 _const_spec((1, D_MODEL)), _const_spec(wa.shape), _const_spec(wg.shape),
                 _const_spec(ca.shape), _const_spec(cg.shape), _const_spec(wd.shape), _const_spec((1, D_MODEL))]
    args += [mod, gpre, wa, wg, ca, cg, wd, gpost]
    return pl.pallas_call(
        functools.partial(_ffn_kernel, tiles_per_seq=tiles_per_seq), grid=(t // tm,),
        in_specs=in_specs, out_specs=tile,
        out_shape=jax.ShapeDtypeStruct((t, D_MODEL), F32),
        compiler_params=_params(1), name="ffn",
    )(*args)


def _rope_tables():
    t = jnp.arange(DEC_SEQ)
    pos = (t // GRID_W, t % GRID_W)

    def tab(d):
        half = d // 4
        inv = ROPE_BASE ** (-jnp.arange(half, dtype=F32) / half)
        cs, sn = [], []
        for p in pos:
            ang = p.astype(F32)[:, None] * inv[None, :]
            cs += [jnp.cos(ang), jnp.cos(ang)]
            sn += [-jnp.sin(ang), jnp.sin(ang)]
        return jnp.concatenate(cs, -1), jnp.concatenate(sn, -1)

    c64, s64 = tab(HEAD_DIM)
    c32, s32 = tab(MLA_ROPE)
    one = jnp.ones((DEC_SEQ, MLA_NOPE), F32)
    zero = jnp.zeros((DEC_SEQ, MLA_NOPE), F32)
    pad1 = jnp.ones((DEC_SEQ, LANE - MLA_NOPE - MLA_ROPE), F32)
    pad0 = jnp.zeros((DEC_SEQ, LANE - MLA_NOPE - MLA_ROPE), F32)
    cm = jnp.concatenate([one, c32, pad1], -1)
    sm = jnp.concatenate([zero, s32, pad0], -1)
    return (jnp.tile(c64, (1, 4)), jnp.tile(s64, (1, 4)), jnp.tile(cm, (1, 4)), jnp.tile(sm, (1, 4)))


def _na_bias_tables(rpb):
    rows = DEC_SEQ // GRID_W
    nq = DEC_SEQ // Q_BLOCK
    qi = np.arange(Q_BLOCK)[:, None]
    kj = np.arange(NA_SPAN)[None, :]
    drs, dcs, oks = [], [], []
    for n in (2, 0, 1, nq - 2, nq - 1):
        r = 2 * n + qi // GRID_W
        c = qi % GRID_W
        row0 = np.clip(2 * n - NA_WIN_R // 2, 0, rows - NA_SPAN // GRID_W)
        rk = row0 + kj // GRID_W
        w = kj % GRID_W
        r_start = np.clip(r - NA_WIN_R // 2, 0, rows - NA_WIN_R)
        c_start = np.clip(c - NA_WIN_C // 2, 0, GRID_W - NA_WIN_C)
        ok = (rk >= r_start) & (rk < r_start + NA_WIN_R) & (w >= c_start) & (w < c_start + NA_WIN_C)
        drs.append(np.clip(rk - r + NA_WIN_R - 1, 0, 2 * NA_WIN_R - 2))
        dcs.append(np.clip(w - c + NA_WIN_C - 1, 0, 2 * NA_WIN_C - 2))
        oks.append(ok)
    dr, dc, ok = np.stack(drs), np.stack(dcs), np.stack(oks)
    bias = rpb[:, dr, dc]
    return jnp.where(ok[None], bias, NEG_INF).transpose(1, 0, 2, 3)


def _pad_cols(w, n):
    return jnp.pad(w, ((0, 0), (0, n - w.shape[1])))


def _prep_layer(w_in, w_gate, b_gate, na_rpb, mla_q_norm, mla_w_uq, mla_kv_norm, mla_w_ukv, pool_w, pool_scale,
                w_branch, w_out, ffn_w_up, ffn_conv, ffn_w_down):
    a_cols = w_in[:, :768] * jnp.concatenate([jnp.full((256,), ATT_SCALE, F32), jnp.ones((512,), F32)])
    b0 = 768
    cq = _pad_cols(w_in[:, b0:b0 + MLA_Q_RANK], 256)
    ckv = w_in[:, b0 + MLA_Q_RANK:b0 + MLA_Q_RANK + MLA_KV_RANK]
    kr = jnp.pad(w_in[:, b0 + 320:b0 + 352], ((0, 0), (_KR_LANE, LANE - _KR_LANE - MLA_ROPE)))
    c0 = 1120
    c_cols = w_in[:, c0:c0 + 512] * jnp.concatenate([jnp.full((256,), ATT_SCALE, F32), jnp.ones((256,), F32)])
    w_in_p = jnp.concatenate([a_cols, cq, ckv, kr, c_cols, w_in[:, 1632:]], 1).astype(BF16)

    qn = _pad_cols(mla_q_norm[None, :], 256)
    wuq = mla_w_uq.reshape(MLA_Q_RANK, MLA_HEADS, MLA_NOPE + MLA_ROPE)
    wuq = jnp.pad(wuq, ((0, 256 - MLA_Q_RANK), (0, 0), (0, LANE - MLA_NOPE - MLA_ROPE)))
    wuq = wuq.reshape(256, MLA_HEADS * LANE).astype(BF16)
    wukv = mla_w_ukv.reshape(MLA_KV_RANK, MLA_HEADS, MLA_NOPE + MLA_V)
    wuk = jnp.pad(wukv[:, :, :MLA_NOPE], ((0, 0), (0, 0), (0, LANE - MLA_NOPE)))
    wuk = wuk.reshape(MLA_KV_RANK, MLA_HEADS * LANE).astype(BF16)
    wuv = wukv[:, :, MLA_NOPE:].reshape(MLA_KV_RANK, MLA_HEADS * MLA_V).astype(BF16)
    mla_w = (qn, wuq, mla_kv_norm[None, :], wuk, wuv)

    w_bd = jax.scipy.linalg.block_diag(*[pool_w[g] for g in range(len(POOL_WINDOWS))]).astype(BF16)

    pad_ff = D_FF_PAD - D_FF
    wa = _pad_cols(ffn_w_up[:, :D_FF], D_FF_PAD).astype(BF16)
    wg = _pad_cols(ffn_w_up[:, D_FF:], D_FF_PAD).astype(BF16)
    ca = _pad_cols(ffn_conv[:, :D_FF], D_FF_PAD)
    cg = _pad_cols(ffn_conv[:, D_FF:], D_FF_PAD)
    wd = jnp.pad(ffn_w_down, ((0, pad_ff), (0, 0))).astype(BF16)
    return dict(w_in_p=w_in_p, mla_w=mla_w, w_bd=w_bd, pool_scale=pool_scale[None, :],
                na_bias=_na_bias_tables(na_rpb), wgate=w_gate.astype(BF16), bgate=b_gate[None, :],
                wb=w_branch.astype(BF16), wo=w_out.astype(BF16), wa=wa, wg=wg, ca=ca, cg=cg, wd=wd)


def kernel(x_prompt, x_sample, cache_na_k, cache_na_v, cache_mla_ckv, cache_mla_krope, cache_swa_k, cache_swa_v, c, c_ctx, w_mod, b_mod, g_attn_pre, g_attn_post, g_ffn_pre, g_ffn_post, w_in, w_gate, b_gate, na_rpb, mla_q_norm, mla_w_uq, mla_kv_norm, mla_w_ukv, swa_sink, pool_w, pool_scale, w_branch, w_out, ffn_w_up, ffn_conv, ffn_w_down):
    x_p = x_prompt.reshape(BATCH * SEQ, D_MODEL)
    x_s = x_sample.reshape(DEC_BATCH * DEC_SEQ, D_MODEL)

    cv = jnp.concatenate([c_ctx[None, :], c, jnp.zeros((8 - 1 - DEC_BATCH, D_MODEL), F32)], 0)
    mod = _mod_call(cv, w_mod, b_mod).reshape(DEPTH, 8, 6, D_MODEL)
    rope_tabs = _rope_tables()
    cache_na_k = cache_na_k.reshape(DEC_BATCH, DEPTH, PAST_LEN, 256)
    cache_na_v = cache_na_v.reshape(DEC_BATCH, DEPTH, PAST_LEN, 256)
    cache_swa_k = cache_swa_k.reshape(DEC_BATCH, DEPTH, PAST_LEN, 128)
    cache_swa_v = cache_swa_v.reshape(DEC_BATCH, DEPTH, PAST_LEN, 128)
    cache_kr = jnp.pad(cache_mla_krope, ((0, 0), (0, 0), (0, 0), (_KR_LANE, LANE - _KR_LANE - MLA_ROPE)))

    states = []
    for l in range(DEPTH):
        lp = _prep_layer(w_in[l], w_gate[l], b_gate[l], na_rpb[l], mla_q_norm[l], mla_w_uq[l], mla_kv_norm[l],
                         mla_w_ukv[l], pool_w[l], pool_scale[l], w_branch[l], w_out[l], ffn_w_up[l], ffn_conv[l],
                         ffn_w_down[l])
        gpre, gpost = g_attn_pre[l][None, :], g_attn_post[l][None, :]
        fpre, fpost = g_ffn_pre[l][None, :], g_ffn_post[l][None, :]
        ffn_w = (lp["wa"], lp["wg"], lp["ca"], lp["cg"], lp["wd"])
        merge_w = (lp["wgate"], lp["bgate"], lp["wb"], lp["wo"])

        mod_p = mod[l, 0:1]
        qa, ka, va, cq, ckv, kr, qc, kc, vc, pd = _inproj_call(x_p, mod_p, gpre, lp["w_in_p"], BATCH * SEQ // 256)
        ys = (_ctx_attn_call(qa, ka, va), _ctx_mla_call(cq, ckv, kr, lp["mla_w"]),
              _ctx_swa_call(swa_sink[l], qc, kc, vc), _pool_call(pd, lp["w_bd"], lp["pool_scale"], SEQ))
        x_p = _merge_call(x_p, mod_p, gpre, ys, *merge_w, gpost, BATCH * SEQ // 256)
        x_p = _ffn_call(x_p, mod_p, fpre, *ffn_w, fpost, SEQ, SEQ)
        states.append((ka, va, ckv, kr[:, _KR_LANE:_KR_LANE + MLA_ROPE], kc, vc))

        mod_s = mod[l, 1:1 + DEC_BATCH]
        qa, ka, va, cq, ckv, kr, qc, kc, vc, pd = _inproj_call(x_s, mod_s, gpre, lp["w_in_p"], DEC_SEQ // 256,
                                                               rope_tabs)
        ys = (_lat_na_call(qa, ka, va, cache_na_k, cache_na_v, lp["na_bias"], l),
              _lat_mla_call(cq, ckv, kr, cache_mla_ckv, cache_kr, rope_tabs[2], rope_tabs[3], lp["mla_w"], l),
              _lat_swa_call(swa_sink[l], qc, kc, vc, cache_swa_k, cache_swa_v, l),
              _pool_call(pd, lp["w_bd"], lp["pool_scale"], DEC_SEQ))
        x_s = _merge_call(x_s, mod_s, gpre, ys, *merge_w, gpost, DEC_SEQ // 256)
        x_s = _ffn_call(x_s, mod_s, fpre, *ffn_w, fpost, DEC_SEQ, 512)

    def stack(j, *tail):
        return jnp.stack([st[j].reshape(BATCH, SEQ, *tail) for st in states], 1)

    return (x_p.reshape(BATCH, SEQ, D_MODEL), x_s.reshape(DEC_BATCH, DEC_SEQ, D_MODEL),
            stack(0, NA_HEADS, HEAD_DIM), stack(1, NA_HEADS, HEAD_DIM), stack(2, MLA_KV_RANK), stack(3, MLA_ROPE),
            stack(4, SWA_KV_HEADS, HEAD_DIM), stack(5, SWA_KV_HEADS, HEAD_DIM))
```

```python
import functools

import jax
import jax.numpy as jnp
import numpy as np
from jax import lax
from jax.experimental import pallas as pl
from jax.experimental.pallas import tpu as pltpu

F32 = jnp.float32
BF16 = jnp.bfloat16

D_MODEL = 1024
BATCH = 32
SEQ = 256
DEPTH = 2
DEC_BATCH = 2
DEC_SEQ = 2048
PAST_LEN = 256
GRID_W = 64
HEAD_DIM = 64
NA_HEADS = 4
NA_WIN_R = 8
NA_WIN_C = 16
MLA_HEADS = 4
MLA_NOPE = 64
MLA_ROPE = 32
MLA_V = 64
MLA_Q_RANK = 192
MLA_KV_RANK = 128
SWA_HEADS = 4
SWA_KV_HEADS = 2
SWA_WINDOW = 128
POOL_WINDOWS = (2, 4, 8, 16)
POOL_GROUP = 64
POOL_WIDTH = 256
BRANCH_W = 256
N_BRANCH = 4
D_FF = 2752
ROPE_BASE = 10000.0
EPS = 1e-6
NEG_INF = -1e30
ATT_SCALE = HEAD_DIM ** -0.5
MLA_SCALE = (MLA_NOPE + MLA_ROPE) ** -0.5

LANE = 128
BF16_SUBLANE = 16
D_FF_PAD = 2816
FF_CHUNK = 256
IN_PAD = 2048
Q_BLOCK = 128
NA_SPAN = 640
NA_DR = 2 * NA_WIN_R - 1
NA_DC = 2 * NA_WIN_C - 1
SWA_SPAN = 384
VMEM_LIMIT = 56 * 1024 * 1024

_QA, _KA, _VA, _CQ, _CKV, _KR, _QC, _KC, _VC, _PD = 0, 256, 512, 768, 1024, 1152, 1280, 1536, 1664, 1792
_KR_LANE = 64


def _dot(a, b):
    return jnp.dot(a, b, preferred_element_type=F32)


def _dot_nt(a, b):
    return lax.dot_general(a, b, (((1,), (1,)), ((), ())), preferred_element_type=F32)


def _sigmoid(x):
    return 1.0 / (1.0 + jnp.exp(-x))


def _rms(x, g, n=None):
    n = x.shape[-1] if n is None else n
    ms = jnp.sum(x * x, -1, keepdims=True) * (1.0 / n)
    return x * lax.rsqrt(ms + EPS) * g


def _softmax_blocks(blocks, sink=None):
    m = None
    for s in blocks:
        mm = jnp.max(s, -1, keepdims=True)
        m = mm if m is None else jnp.maximum(m, mm)
    if sink is not None:
        m = jnp.maximum(m, sink)
    es = [jnp.exp(s - m) for s in blocks]
    l = None
    for e in es:
        ll = jnp.sum(e, -1, keepdims=True)
        l = ll if l is None else l + ll
    if sink is not None:
        l = l + jnp.exp(sink - m)
    return es, l


def _rope(x, cos, sin, q):
    w = x.shape[-1]
    lane = lax.broadcasted_iota(jnp.int32, x.shape, 1)
    up = pltpu.roll(x, w - q, axis=1)
    dn = pltpu.roll(x, q, axis=1)
    partner = jnp.where((lane & (2 * q - 1)) < q, up, dn)
    return x * cos + partner * sin


def _const_spec(shape):
    n = len(shape)
    return pl.BlockSpec(shape, lambda *_: (0,) * n)


def _params(n_axes):
    return pltpu.CompilerParams(dimension_semantics=("arbitrary",) * n_axes, vmem_limit_bytes=VMEM_LIMIT)


def _mod_kernel(cv_ref, w_ref, b_ref, o_ref):
    cv = cv_ref[...]
    a = (cv * _sigmoid(cv)).astype(BF16)
    o_ref[0] = _dot(a, w_ref[0].astype(BF16)) + b_ref[0]


def _mod_call(cv, w_mod, b_mod):
    tn = 1024
    return pl.pallas_call(
        _mod_kernel,
        grid=(DEPTH, 6 * D_MODEL // tn),
        in_specs=[_const_spec((8, D_MODEL)),
                  pl.BlockSpec((1, D_MODEL, tn), lambda l, j: (l, 0, j)),
                  pl.BlockSpec((1, 1, tn), lambda l, j: (l, 0, j))],
        out_specs=pl.BlockSpec((1, 8, tn), lambda l, j: (l, 0, j)),
        out_shape=jax.ShapeDtypeStruct((DEPTH, 8, 6 * D_MODEL), F32),
        compiler_params=_params(2),
        name="mod",
    )(cv, w_mod, b_mod.reshape(DEPTH, 1, 6 * D_MODEL))


_IN_SLOTS = ((_QA, 256), (_KA, 256), (_VA, 256), (_CQ, 256), (_CKV, 128), (_KR, 128),
             (_QC, 256), (_KC, 128), (_VC, 128), (_PD, 256))


def _inproj_kernel(*refs, latent):
    if latent:
        x_ref, mod_ref, g_ref, w_ref, c64_ref, s64_ref, cm_ref, sm_ref = refs[:8]
        outs = refs[8:]
    else:
        x_ref, mod_ref, g_ref, w_ref = refs[:4]
        outs = refs[4:]
    x = x_ref[...]
    h = _rms(x, g_ref[...]) * (1.0 + mod_ref[0, 1:2, :]) + mod_ref[0, 0:1, :]
    p = _dot(h.astype(BF16), w_ref[...])
    for (off, wd), o_ref in zip(_IN_SLOTS, outs):
        v = p[:, off:off + wd]
        if latent:
            if off == _QC:
                v = _rope(v, c64_ref[...], s64_ref[...], 16)
            elif off == _KC:
                v = _rope(v, c64_ref[:, :128], s64_ref[:, :128], 16)
            elif off == _KR:
                v = _rope(v, cm_ref[:, :128], sm_ref[:, :128], 8)
        o_ref[...] = v.astype(o_ref.dtype)


def _inproj_call(x2d, mod, g, w_in_p, tiles_per_batch, rope_tabs=None):
    t = x2d.shape[0]
    tm = 256
    latent = rope_tabs is not None
    dt = BF16 if latent else F32
    in_specs = [pl.BlockSpec((tm, D_MODEL), lambda i: (i, 0)),
                pl.BlockSpec((1, 6, D_MODEL), lambda i: (i // tiles_per_batch, 0, 0)),
                _const_spec((1, D_MODEL)),
                _const_spec((D_MODEL, IN_PAD))]
    args = [x2d, mod, g, w_in_p]
    if latent:
        c64, s64, cm, sm = rope_tabs
        in_specs += [pl.BlockSpec((tm, 256), lambda i: (i % tiles_per_batch, 0)),
                     pl.BlockSpec((tm, 256), lambda i: (i % tiles_per_batch, 0)),
                     pl.BlockSpec((tm, 512), lambda i: (i % tiles_per_batch, 0)),
                     pl.BlockSpec((tm, 512), lambda i: (i % tiles_per_batch, 0))]
        args += [c64, s64, cm, sm]
    return pl.pallas_call(
        functools.partial(_inproj_kernel, latent=latent),
        grid=(t // tm,),
        in_specs=in_specs,
        out_specs=[pl.BlockSpec((tm, wd), lambda i: (i, 0)) for _, wd in _IN_SLOTS],
        out_shape=[jax.ShapeDtypeStruct((t, wd), dt) for _, wd in _IN_SLOTS],
        compiler_params=_params(1),
        name="inproj_lat" if latent else "inproj_ctx",
    )(*args)


def _ctx_attn_kernel(q_ref, k_ref, v_ref, o_ref):
    q = q_ref[...].astype(BF16)
    k = k_ref[...].astype(BF16)
    v = v_ref[...].astype(BF16)
    ys = []
    for h in range(NA_HEADS):
        sl = slice(h * HEAD_DIM, (h + 1) * HEAD_DIM)
        (e,), l = _softmax_blocks([_dot_nt(q[:, sl], k[:, sl])])
        ys.append(_dot(e.astype(BF16), v[:, sl]) / l)
    o_ref[...] = jnp.concatenate(ys, -1).astype(o_ref.dtype)


def _ctx_attn_call(q, k, v):
    t = q.shape[0]
    spec = pl.BlockSpec((SEQ, 256), lambda b: (b, 0))
    return pl.pallas_call(
        _ctx_attn_kernel, grid=(t // SEQ,), in_specs=[spec, spec, spec], out_specs=spec,
        out_shape=jax.ShapeDtypeStruct((t, 256), BF16), compiler_params=_params(1), name="ctx_attn",
    )(q, k, v)


def _ctx_swa_kernel(sink_ref, q_ref, k_ref, v_ref, o_ref):
    q = q_ref[...].astype(BF16)
    k = k_ref[...].astype(BF16)
    v = v_ref[...].astype(BF16)
    ys = []
    for h in range(SWA_HEADS):
        kv = h // (SWA_HEADS // SWA_KV_HEADS)
        sl = slice(h * HEAD_DIM, (h + 1) * HEAD_DIM)
        kvsl = slice(kv * HEAD_DIM, (kv + 1) * HEAD_DIM)
        (e,), l = _softmax_blocks([_dot_nt(q[:, sl], k[:, kvsl])], sink=sink_ref[h])
        ys.append(_dot(e.astype(BF16), v[:, kvsl]) / l)
    o_ref[...] = jnp.concatenate(ys, -1).astype(o_ref.dtype)


def _ctx_swa_call(sink, q, k, v):
    t = q.shape[0]
    return pl.pallas_call(
        _ctx_swa_kernel, grid=(t // SEQ,),
        in_specs=[pl.BlockSpec(memory_space=pltpu.SMEM),
                  pl.BlockSpec((SEQ, 256), lambda b: (b, 0)),
                  pl.BlockSpec((SEQ, 128), lambda b: (b, 0)),
                  pl.BlockSpec((SEQ, 128), lambda b: (b, 0))],
        out_specs=pl.BlockSpec((SEQ, 256), lambda b: (b, 0)),
        out_shape=jax.ShapeDtypeStruct((t, 256), BF16), compiler_params=_params(1), name="ctx_swa",
    )(sink, q, k, v)


def _mla_q(cq, qn_ref, wuq_ref):
    return _dot(_rms(cq, qn_ref[...], MLA_Q_RANK).astype(BF16), wuq_ref[...])


def _mla_kv(ckv, kr, kvn_ref, wuk_ref, wuv_ref):
    cn = _rms(ckv, kvn_ref[...]).astype(BF16)
    kcat = _dot(cn, wuk_ref[...]) + jnp.concatenate([kr] * MLA_HEADS, -1)
    return kcat, _dot(cn, wuv_ref[...])


def _ctx_mla_kernel(cq_ref, ckv_ref, kr_ref, qn_ref, wuq_ref, kvn_ref, wuk_ref, wuv_ref, o_ref):
    q = (_mla_q(cq_ref[...], qn_ref, wuq_ref) * MLA_SCALE).astype(BF16)
    kcat, v = _mla_kv(ckv_ref[...], kr_ref[...], kvn_ref, wuk_ref, wuv_ref)
    kcat = kcat.astype(BF16)
    v = v.astype(BF16)
    ys = []
    for h in range(MLA_HEADS):
        sl = slice(h * LANE, (h + 1) * LANE)
        (e,), l = _softmax_blocks([_dot_nt(q[:, sl], kcat[:, sl])])
        ys.append(_dot(e.astype(BF16), v[:, h * MLA_V:(h + 1) * MLA_V]) / l)
    o_ref[...] = jnp.concatenate(ys, -1).astype(o_ref.dtype)


def _ctx_mla_call(cq, ckv, kr, mla_w):
    t = cq.shape[0]
    qn, wuq, kvn, wuk, wuv = mla_w
    return pl.pallas_call(
        _ctx_mla_kernel, grid=(t // SEQ,),
        in_specs=[pl.BlockSpec((SEQ, 256), lambda b: (b, 0)),
                  pl.BlockSpec((SEQ, 128), lambda b: (b, 0)),
                  pl.BlockSpec((SEQ, 128), lambda b: (b, 0)),
                  _const_spec(qn.shape), _const_spec(wuq.shape), _const_spec(kvn.shape),
                  _const_spec(wuk.shape), _const_spec(wuv.shape)],
        out_specs=pl.BlockSpec((SEQ, 256), lambda b: (b, 0)),
        out_shape=jax.ShapeDtypeStruct((t, 256), BF16), compiler_params=_params(1), name="ctx_mla",
    )(cq, ckv, kr, qn, wuq, kvn, wuk, wuv)


_POOL_PAD = 8


def _pool_kernel(x_ref, w_ref, sc_ref, o_ref):
    x = x_ref[...].astype(F32)
    n = x.shape[0]
    z = jnp.zeros((_POOL_PAD, POOL_WIDTH), F32)
    xz = jnp.concatenate([z, x, z], 0)
    ne = n + 2 * _POOL_PAD

    def pair(a, s):
        return pltpu.roll(a, s, axis=0) + pltpu.roll(a, ne - s, axis=0)

    s2 = xz + pltpu.roll(xz, 1, axis=0)
    s4 = pair(s2, 1)
    s8 = pair(s4, 2)
    s16 = pair(s8, 4)
    lo, hi = _POOL_PAD, _POOL_PAD + n
    grp = lax.broadcasted_iota(jnp.int32, (n, POOL_WIDTH), 1) >> 6
    t = lax.broadcasted_iota(jnp.int32, (n, POOL_WIDTH), 0)
    tot = jnp.where(grp == 0, s2[lo:hi], jnp.where(grp == 1, s4[lo:hi], jnp.where(grp == 2, s8[lo:hi], s16[lo:hi])))
    half = jnp.where(grp == 0, 1, jnp.where(grp == 1, 2, jnp.where(grp == 2, 4, 8)))
    cnt = jnp.minimum(t + half, n) - jnp.maximum(t - half, 0)
    dlt = (tot / cnt.astype(F32) - x).astype(BF16)
    o_ref[...] = (_dot(dlt, w_ref[...]) * sc_ref[...]).astype(o_ref.dtype)


def _pool_call(pd, w_bd, scale, seq):
    t = pd.shape[0]
    return pl.pallas_call(
        _pool_kernel, grid=(t // seq,),
        in_specs=[pl.BlockSpec((seq, POOL_WIDTH), lambda b: (b, 0)),
                  _const_spec((POOL_WIDTH, POOL_WIDTH)), _const_spec((1, POOL_WIDTH))],
        out_specs=pl.BlockSpec((seq, POOL_WIDTH), lambda b: (b, 0)),
        out_shape=jax.ShapeDtypeStruct((t, POOL_WIDTH), BF16), compiler_params=_params(1), name="pool",
    )(pd, w_bd, scale)


def _lat_na_kernel(q_ref, k_ref, v_ref, kc_ref, vc_ref, t2_ref, o_ref):
    n = pl.program_id(1)
    rows = DEC_SEQ // GRID_W
    row0 = jnp.clip(2 * n - NA_WIN_R // 2, 0, rows - NA_SPAN // GRID_W)
    start = pl.multiple_of(row0 * GRID_W, LANE)
    q = q_ref[...]
    k = k_ref[pl.ds(start, NA_SPAN), :]
    v = v_ref[pl.ds(start, NA_SPAN), :]
    kc = kc_ref[...].astype(BF16)
    vc = vc_ref[...].astype(BF16)

    pairs = NA_SPAN // LANE
    low_half = lax.broadcasted_iota(jnp.int32, (GRID_W, LANE), 1) < GRID_W
    entries, masks = [], []
    for a in range(Q_BLOCK // GRID_W):
        r = 2 * n + a
        r_start = jnp.clip(r - NA_WIN_R // 2, 0, rows - NA_WIN_R)
        for p in range(pairs):
            rk = row0 + 2 * p
            ok0 = ((rk >= r_start) & (rk < r_start + NA_WIN_R)).astype(jnp.int32)
            ok1 = ((rk + 1 >= r_start) & (rk + 1 < r_start + NA_WIN_R)).astype(jnp.int32)
            entries.append(jnp.clip(rk - r + NA_WIN_R, 0, NA_DR))
            masks.append(jnp.where(low_half, ok0, ok1) > 0)

    ys = []
    for h in range(NA_HEADS):
        sl = slice(h * HEAD_DIM, (h + 1) * HEAD_DIM)
        cols = []
        for p in range(pairs):
            blk = [jnp.where(masks[a * pairs + p], t2_ref[h, entries[a * pairs + p]], NEG_INF)
                   for a in range(Q_BLOCK // GRID_W)]
            cols.append(jnp.concatenate(blk, 0))
        s_loc = _dot_nt(q[:, sl], k[:, sl]) + jnp.concatenate(cols, 1)
        s_ctx = _dot_nt(q[:, sl], kc[:, sl])
        (e_loc, e_ctx), l = _softmax_blocks([s_loc, s_ctx])
        y = _dot(e_loc.astype(BF16), v[:, sl]) + _dot(e_ctx.astype(BF16), vc[:, sl])
        ys.append(y / l)
    o_ref[...] = jnp.concatenate(ys, -1).astype(o_ref.dtype)


def _lat_na_call(q, k, v, cache_k, cache_v, t2, layer):
    nq = DEC_SEQ // Q_BLOCK
    seq_spec = pl.BlockSpec((DEC_SEQ, 256), lambda b, n: (b, 0))
    cache_spec = pl.BlockSpec((None, None, PAST_LEN, 256), lambda b, n: (b, layer, 0, 0))
    return pl.pallas_call(
        _lat_na_kernel, grid=(DEC_BATCH, nq),
        in_specs=[pl.BlockSpec((Q_BLOCK, 256), lambda b, n: (b * nq + n, 0)), seq_spec, seq_spec,
                  cache_spec, cache_spec,
                  pl.BlockSpec((None, NA_HEADS, NA_DR + 1, GRID_W, LANE), lambda b, n: (layer, 0, 0, 0, 0))],
        out_specs=pl.BlockSpec((Q_BLOCK, 256), lambda b, n: (b * nq + n, 0)),
        out_shape=jax.ShapeDtypeStruct((DEC_BATCH * DEC_SEQ, 256), BF16),
        compiler_params=_params(2), name="lat_na",
    )(q, k, v, cache_k, cache_v, t2)


def _lat_swa_kernel(sink_ref, q_ref, k_ref, v_ref, kc_ref, vc_ref, o_ref):
    n = pl.program_id(1)
    start = pl.multiple_of(jnp.clip(n - 1, 0, DEC_SEQ // Q_BLOCK - 3) * Q_BLOCK, LANE)
    q = q_ref[...]
    k = k_ref[pl.ds(start, SWA_SPAN), :]
    v = v_ref[pl.ds(start, SWA_SPAN), :]
    kc = kc_ref[...].astype(BF16)
    vc = vc_ref[...].astype(BF16)
    q_pos = n * Q_BLOCK + lax.broadcasted_iota(jnp.int32, (Q_BLOCK, SWA_SPAN), 0)
    k_pos = start + lax.broadcasted_iota(jnp.int32, (Q_BLOCK, SWA_SPAN), 1)
    valid = jnp.abs(q_pos - k_pos) <= SWA_WINDOW
    ys = []
    for h in range(SWA_HEADS):
        kv = h // (SWA_HEADS // SWA_KV_HEADS)
        sl = slice(h * HEAD_DIM, (h + 1) * HEAD_DIM)
        kvsl = slice(kv * HEAD_DIM, (kv + 1) * HEAD_DIM)
        s_loc = jnp.where(valid, _dot_nt(q[:, sl], k[:, kvsl]), NEG_INF)
        s_ctx = _dot_nt(q[:, sl], kc[:, kvsl])
        (e_loc, e_ctx), l = _softmax_blocks([s_loc, s_ctx], sink=sink_ref[h])
        y = _dot(e_loc.astype(BF16), v[:, kvsl]) + _dot(e_ctx.astype(BF16), vc[:, kvsl])
        ys.append(y / l)
    o_ref[...] = jnp.concatenate(ys, -1).astype(o_ref.dtype)


def _lat_swa_call(sink, q, k, v, cache_k, cache_v, layer):
    nq = DEC_SEQ // Q_BLOCK
    seq_spec = pl.BlockSpec((DEC_SEQ, 128), lambda b, n: (b, 0))
    cache_spec = pl.BlockSpec((None, None, PAST_LEN, 128), lambda b, n: (b, layer, 0, 0))
    return pl.pallas_call(
        _lat_swa_kernel, grid=(DEC_BATCH, nq),
        in_specs=[pl.BlockSpec(memory_space=pltpu.SMEM),
                  pl.BlockSpec((Q_BLOCK, 256), lambda b, n: (b * nq + n, 0)), seq_spec, seq_spec,
                  cache_spec, cache_spec],
        out_specs=pl.BlockSpec((Q_BLOCK, 256), lambda b, n: (b * nq + n, 0)),
        out_shape=jax.ShapeDtypeStruct((DEC_BATCH * DEC_SEQ, 256), BF16),
        compiler_params=_params(2), name="lat_swa",
    )(sink, q, k, v, cache_k, cache_v)


def _lat_mla_kernel(cq_ref, ckv_ref, kr_ref, cckv_ref, ckr_ref, cm_ref, sm_ref,
                    qn_ref, wuq_ref, kvn_ref, wuk_ref, wuv_ref, o_ref, kcat_s, v_s):
    @pl.when(pl.program_id(1) == 0)
    def _():
        kc, vc = _mla_kv(cckv_ref[...], ckr_ref[...], kvn_ref, wuk_ref, wuv_ref)
        kcat_s[0:PAST_LEN, :] = kc.astype(BF16)
        v_s[0:PAST_LEN, :] = vc.astype(BF16)
        kl, vl = _mla_kv(ckv_ref[...].astype(F32), kr_ref[...].astype(F32), kvn_ref, wuk_ref, wuv_ref)
        kcat_s[PAST_LEN:, :] = kl.astype(BF16)
        v_s[PAST_LEN:, :] = vl.astype(BF16)

    q = _mla_q(cq_ref[...].astype(F32), qn_ref, wuq_ref)
    q = (_rope(q, cm_ref[...], sm_ref[...], 8) * MLA_SCALE).astype(BF16)
    ys = []
    for h in range(MLA_HEADS):
        sl = slice(h * LANE, (h + 1) * LANE)
        (e,), l = _softmax_blocks([_dot_nt(q[:, sl], kcat_s[:, sl])])
        ys.append(_dot(e.astype(BF16), v_s[:, h * MLA_V:(h + 1) * MLA_V]) / l)
    o_ref[...] = jnp.concatenate(ys, -1).astype(o_ref.dtype)


def _lat_mla_call(cq, ckv, kr, cache_ckv, cache_kr, cm, sm, mla_w, layer):
    nq = DEC_SEQ // Q_BLOCK
    qn, wuq, kvn, wuk, wuv = mla_w
    seq_spec = pl.BlockSpec((DEC_SEQ, 128), lambda b, n: (b, 0))
    cache_spec = pl.BlockSpec((None, None, PAST_LEN, 128), lambda b, n: (b, layer, 0, 0))
    tab_spec = pl.BlockSpec((Q_BLOCK, 512), lambda b, n: (n, 0))
    return pl.pallas_call(
        _lat_mla_kernel, grid=(DEC_BATCH, nq),
        in_specs=[pl.BlockSpec((Q_BLOCK, 256), lambda b, n: (b * nq + n, 0)), seq_spec, seq_spec,
                  cache_spec, cache_spec, tab_spec, tab_spec,
                  _const_spec(qn.shape), _const_spec(wuq.shape), _const_spec(kvn.shape),
                  _const_spec(wuk.shape), _const_spec(wuv.shape)],
        out_specs=pl.BlockSpec((Q_BLOCK, 256), lambda b, n: (b * nq + n, 0)),
        out_shape=jax.ShapeDtypeStruct((DEC_BATCH * DEC_SEQ, 256), BF16),
        scratch_shapes=[pltpu.VMEM((PAST_LEN + DEC_SEQ, MLA_HEADS * LANE), BF16),
                        pltpu.VMEM((PAST_LEN + DEC_SEQ, MLA_HEADS * MLA_V), BF16)],
        compiler_params=_params(2), name="lat_mla",
    )(cq, ckv, kr, cache_ckv, cache_kr, cm, sm, qn, wuq, kvn, wuk, wuv)


def _merge_kernel(x_ref, mod_ref, gpre_ref, ya_ref, yb_ref, yc_ref, yd_ref,
                  wg_ref, bg_ref, wb_ref, wo_ref, gpost_ref, o_ref):
    x = x_ref[...]
    h = (_rms(x, gpre_ref[...]) * (1.0 + mod_ref[0, 1:2, :]) + mod_ref[0, 0:1, :]).astype(BF16)
    merged = None
    for k, y_ref in enumerate((ya_ref, yb_ref, yc_ref, yd_ref)):
        cols = slice(k * D_MODEL, (k + 1) * D_MODEL)
        gate = _sigmoid(_dot(h, wg_ref[:, cols]) + bg_ref[:, cols])
        term = gate * _dot(y_ref[...], wb_ref[k])
        merged = term if merged is None else merged + term
    o = _dot(merged.astype(BF16), wo_ref[...])
    o_ref[...] = x + mod_ref[0, 2:3, :] * _rms(o, gpost_ref[...])


def _merge_call(x2d, mod, gpre, ys, wg, bg, wb, wo, gpost, tiles_per_batch):
    t = x2d.shape[0]
    tm = 256
    tile = pl.BlockSpec((tm, D_MODEL), lambda i: (i, 0))
    ytile = pl.BlockSpec((tm, BRANCH_W), lambda i: (i, 0))
    return pl.pallas_call(
        _merge_kernel, grid=(t // tm,),
        in_specs=[tile, pl.BlockSpec((1, 6, D_MODEL), lambda i: (i // tiles_per_batch, 0, 0)),
                  _const_spec((1, D_MODEL)), ytile, ytile, ytile, ytile,
                  _const_spec(wg.shape), _const_spec(bg.shape), _const_spec(wb.shape),
                  _const_spec(wo.shape), _const_spec((1, D_MODEL))],
        out_specs=tile,
        out_shape=jax.ShapeDtypeStruct((t, D_MODEL), F32),
        compiler_params=_params(1), name="merge",
    )(x2d, mod, gpre, *ys, wg, bg, wb, wo, gpost)


_HALO = BF16_SUBLANE


def _ffn_kernel(*refs, tiles_per_seq):
    halo = tiles_per_seq > 1
    if halo:
        x_ref, xp_ref, xn_ref, mod_ref, gpre_ref, wa_ref, wg_ref, ca_ref, cg_ref, wd_ref, gpost_ref, o_ref = refs
    else:
        x_ref, mod_ref, gpre_ref, wa_ref, wg_ref, ca_ref, cg_ref, wd_ref, gpost_ref, o_ref = refs
    x = x_ref[...]
    tm = x.shape[0]
    shift, scale, gate = mod_ref[0, 3:4, :], mod_ref[0, 4:5, :], mod_ref[0, 5:6, :]

    def pre(xx):
        return _rms(xx, gpre_ref[...]) * (1.0 + scale) + shift

    h = pre(x).astype(BF16)
    if halo:
        i = pl.program_id(0) % tiles_per_seq
        hp = jnp.where(i == 0, 0.0, pre(xp_ref[...])).astype(BF16)
        hn = jnp.where(i == tiles_per_seq - 1, 0.0, pre(xn_ref[...])).astype(BF16)
        h = jnp.concatenate([hp, h, hn], 0)
    rows = h.shape[0]
    row = lax.broadcasted_iota(jnp.int32, (rows, FF_CHUNK), 0)

    def conv(u, c_ref, cols):
        prev = jnp.where(row == 0, 0.0, pltpu.roll(u, 1, axis=0))
        nxt = jnp.where(row == rows - 1, 0.0, pltpu.roll(u, rows - 1, axis=0))
        uc = prev * c_ref[0:1, cols] + u * c_ref[1:2, cols] + nxt * c_ref[2:3, cols]
        return uc[_HALO:_HALO + tm] if halo else uc

    acc = None
    for c in range(D_FF_PAD // FF_CHUNK):
        cols = slice(c * FF_CHUNK, (c + 1) * FF_CHUNK)
        a = conv(_dot(h, wa_ref[:, cols]), ca_ref, cols)
        g = conv(_dot(h, wg_ref[:, cols]), cg_ref, cols)
        act = (g * _sigmoid(g) * a).astype(BF16)
        part = _dot(act, wd_ref[cols, :])
        acc = part if acc is None else acc + part
    o_ref[...] = x + gate * _rms(acc, gpost_ref[...])


def _ffn_call(x2d, mod, gpre, wa, wg, ca, cg, wd, gpost, seq, tm):
    t = x2d.shape[0]
    tiles_per_seq = seq // tm
    tile = pl.BlockSpec((tm, D_MODEL), lambda i: (i, 0))
    in_specs = [tile]
    args = [x2d]
    if tiles_per_seq > 1:
        r = tm // _HALO
        last = t // _HALO - 1
        in_specs += [pl.BlockSpec((_HALO, D_MODEL), lambda i: (jnp.maximum(i * r - 1, 0), 0)),
                     pl.BlockSpec((_HALO, D_MODEL), lambda i: (jnp.minimum((i + 1) * r, last), 0))]
        args += [x2d, x2d]
    tiles_per_mod = t // tm // mod.shape[0]
    in_specs += [pl.BlockSpec((1, 6, D_MODEL), lambda i: (i // tiles_per_mod, 0, 0)),
                 _const_spec((1, D_MODEL)), _const_spec(wa.shape), _const_spec(wg.shape),
                 _const_spec(ca.shape), _const_spec(cg.shape), _const_spec(wd.shape), _const_spec((1, D_MODEL))]
    args += [mod, gpre, wa, wg, ca, cg, wd, gpost]
    return pl.pallas_call(
        functools.partial(_ffn_kernel, tiles_per_seq=tiles_per_seq), grid=(t // tm,),
        in_specs=in_specs, out_specs=tile,
        out_shape=jax.ShapeDtypeStruct((t, D_MODEL), F32),
        compiler_params=_params(1), name="ffn",
    )(*args)


def _rope_tables():
    t = jnp.arange(DEC_SEQ)
    pos = (t // GRID_W, t % GRID_W)

    def tab(d):
        half = d // 4
        inv = ROPE_BASE ** (-jnp.arange(half, dtype=F32) / half)
        cs, sn = [], []
        for p in pos:
            ang = p.astype(F32)[:, None] * inv[None, :]
            cs += [jnp.cos(ang), jnp.cos(ang)]
            sn += [-jnp.sin(ang), jnp.sin(ang)]
        return jnp.concatenate(cs, -1), jnp.concatenate(sn, -1)

    c64, s64 = tab(HEAD_DIM)
    c32, s32 = tab(MLA_ROPE)
    one = jnp.ones((DEC_SEQ, MLA_NOPE), F32)
    zero = jnp.zeros((DEC_SEQ, MLA_NOPE), F32)
    pad1 = jnp.ones((DEC_SEQ, LANE - MLA_NOPE - MLA_ROPE), F32)
    pad0 = jnp.zeros((DEC_SEQ, LANE - MLA_NOPE - MLA_ROPE), F32)
    cm = jnp.concatenate([one, c32, pad1], -1)
    sm = jnp.concatenate([zero, s32, pad0], -1)
    return (jnp.tile(c64, (1, 4)), jnp.tile(s64, (1, 4)), jnp.tile(cm, (1, 4)), jnp.tile(sm, (1, 4)))


def _na_table_kernel(rpb_ref, e_ref, ok_ref, o_ref):
    r = rpb_ref[...]
    r1 = r.astype(BF16)
    r2 = (r - r1.astype(F32)).astype(BF16)
    r3 = (r - r1.astype(F32) - r2.astype(F32)).astype(BF16)
    e = e_ref[...]
    t = _dot(r1, e) + _dot(r2, e) + _dot(r3, e)
    o_ref[...] = jnp.where(ok_ref[...] > 0, t, NEG_INF)


def _na_bias_tables(na_rpb):
    c = np.arange(GRID_W)[:, None]
    w = np.arange(GRID_W)[None, :]
    dc = (w - c + NA_WIN_C - 1).reshape(-1)
    onehot = (np.arange(LANE)[:, None] == dc[None, :]).astype(np.float32)
    c_start = np.clip(c - NA_WIN_C // 2, 0, GRID_W - NA_WIN_C)
    ok = ((w >= c_start) & (w < c_start + NA_WIN_C)).reshape(1, -1).astype(np.int32)
    rows = DEPTH * NA_HEADS * NA_DR
    rpb2 = jnp.pad(na_rpb.reshape(rows, NA_DC), ((0, LANE - rows), (0, LANE - NA_DC)))
    t = pl.pallas_call(
        _na_table_kernel, out_shape=jax.ShapeDtypeStruct((LANE, GRID_W * GRID_W), F32), name="na_table",
        compiler_params=pltpu.CompilerParams(vmem_limit_bytes=VMEM_LIMIT),
    )(rpb2, jnp.asarray(onehot, BF16), jnp.asarray(ok))
    t = t[:rows].reshape(DEPTH, NA_HEADS, NA_DR, GRID_W, GRID_W)
    t = jnp.pad(t, ((0, 0), (0, 0), (1, 1), (0, 0), (0, 0)), constant_values=NEG_INF)
    return jnp.concatenate([t[:, :, :-1], t[:, :, 1:]], -1)


def _pad_cols(w, n):
    return jnp.pad(w, ((0, 0), (0, n - w.shape[1])))


def _prep_layer(w_in, w_gate, b_gate, mla_q_norm, mla_w_uq, mla_kv_norm, mla_w_ukv, pool_w, pool_scale,
                w_branch, w_out, ffn_w_up, ffn_conv, ffn_w_down):
    a_cols = w_in[:, :768] * jnp.concatenate([jnp.full((256,), ATT_SCALE, F32), jnp.ones((512,), F32)])
    b0 = 768
    cq = _pad_cols(w_in[:, b0:b0 + MLA_Q_RANK], 256)
    ckv = w_in[:, b0 + MLA_Q_RANK:b0 + MLA_Q_RANK + MLA_KV_RANK]
    kr = jnp.pad(w_in[:, b0 + 320:b0 + 352], ((0, 0), (_KR_LANE, LANE - _KR_LANE - MLA_ROPE)))
    c0 = 1120
    c_cols = w_in[:, c0:c0 + 512] * jnp.concatenate([jnp.full((256,), ATT_SCALE, F32), jnp.ones((256,), F32)])
    w_in_p = jnp.concatenate([a_cols, cq, ckv, kr, c_cols, w_in[:, 1632:]], 1).astype(BF16)

    qn = _pad_cols(mla_q_norm[None, :], 256)
    wuq = mla_w_uq.reshape(MLA_Q_RANK, MLA_HEADS, MLA_NOPE + MLA_ROPE)
    wuq = jnp.pad(wuq, ((0, 256 - MLA_Q_RANK), (0, 0), (0, LANE - MLA_NOPE - MLA_ROPE)))
    wuq = wuq.reshape(256, MLA_HEADS * LANE).astype(BF16)
    wukv = mla_w_ukv.reshape(MLA_KV_RANK, MLA_HEADS, MLA_NOPE + MLA_V)
    wuk = jnp.pad(wukv[:, :, :MLA_NOPE], ((0, 0), (0, 0), (0, LANE - MLA_NOPE)))
    wuk = wuk.reshape(MLA_KV_RANK, MLA_HEADS * LANE).astype(BF16)
    wuv = wukv[:, :, MLA_NOPE:].reshape(MLA_KV_RANK, MLA_HEADS * MLA_V).astype(BF16)
    mla_w = (qn, wuq, mla_kv_norm[None, :], wuk, wuv)

    w_bd = jax.scipy.linalg.block_diag(*[pool_w[g] for g in range(len(POOL_WINDOWS))]).astype(BF16)

    pad_ff = D_FF_PAD - D_FF
    wa = _pad_cols(ffn_w_up[:, :D_FF], D_FF_PAD).astype(BF16)
    wg = _pad_cols(ffn_w_up[:, D_FF:], D_FF_PAD).astype(BF16)
    ca = _pad_cols(ffn_conv[:, :D_FF], D_FF_PAD)
    cg = _pad_cols(ffn_conv[:, D_FF:], D_FF_PAD)
    wd = jnp.pad(ffn_w_down, ((0, pad_ff), (0, 0))).astype(BF16)
    return dict(w_in_p=w_in_p, mla_w=mla_w, w_bd=w_bd, pool_scale=pool_scale[None, :],
                wgate=w_gate.astype(BF16), bgate=b_gate[None, :],
                wb=w_branch.astype(BF16), wo=w_out.astype(BF16), wa=wa, wg=wg, ca=ca, cg=cg, wd=wd)


def kernel(x_prompt, x_sample, cache_na_k, cache_na_v, cache_mla_ckv, cache_mla_krope, cache_swa_k, cache_swa_v, c, c_ctx, w_mod, b_mod, g_attn_pre, g_attn_post, g_ffn_pre, g_ffn_post, w_in, w_gate, b_gate, na_rpb, mla_q_norm, mla_w_uq, mla_kv_norm, mla_w_ukv, swa_sink, pool_w, pool_scale, w_branch, w_out, ffn_w_up, ffn_conv, ffn_w_down):
    x_p = x_prompt.reshape(BATCH * SEQ, D_MODEL)
    x_s = x_sample.reshape(DEC_BATCH * DEC_SEQ, D_MODEL)

    cv = jnp.concatenate([c_ctx[None, :], c, jnp.zeros((8 - 1 - DEC_BATCH, D_MODEL), F32)], 0)
    mod = _mod_call(cv, w_mod, b_mod).reshape(DEPTH, 8, 6, D_MODEL)
    rope_tabs = _rope_tables()
    na_t2 = _na_bias_tables(na_rpb)
    cache_na_k = cache_na_k.reshape(DEC_BATCH, DEPTH, PAST_LEN, 256)
    cache_na_v = cache_na_v.reshape(DEC_BATCH, DEPTH, PAST_LEN, 256)
    cache_swa_k = cache_swa_k.reshape(DEC_BATCH, DEPTH, PAST_LEN, 128)
    cache_swa_v = cache_swa_v.reshape(DEC_BATCH, DEPTH, PAST_LEN, 128)
    cache_kr = jnp.pad(cache_mla_krope, ((0, 0), (0, 0), (0, 0), (_KR_LANE, LANE - _KR_LANE - MLA_ROPE)))

    states = []
    for l in range(DEPTH):
        lp = _prep_layer(w_in[l], w_gate[l], b_gate[l], mla_q_norm[l], mla_w_uq[l], mla_kv_norm[l],
                         mla_w_ukv[l], pool_w[l], pool_scale[l], w_branch[l], w_out[l], ffn_w_up[l], ffn_conv[l],
                         ffn_w_down[l])
        gpre, gpost = g_attn_pre[l][None, :], g_attn_post[l][None, :]
        fpre, fpost = g_ffn_pre[l][None, :], g_ffn_post[l][None, :]
        ffn_w = (lp["wa"], lp["wg"], lp["ca"], lp["cg"], lp["wd"])
        merge_w = (lp["wgate"], lp["bgate"], lp["wb"], lp["wo"])

        mod_p = mod[l, 0:1]
        qa, ka, va, cq, ckv, kr, qc, kc, vc, pd = _inproj_call(x_p, mod_p, gpre, lp["w_in_p"], BATCH * SEQ // 256)
        ys = (_ctx_attn_call(qa, ka, va), _ctx_mla_call(cq, ckv, kr, lp["mla_w"]),
              _ctx_swa_call(swa_sink[l], qc, kc, vc), _pool_call(pd, lp["w_bd"], lp["pool_scale"], SEQ))
        x_p = _merge_call(x_p, mod_p, gpre, ys, *merge_w, gpost, BATCH * SEQ // 256)
        x_p = _ffn_call(x_p, mod_p, fpre, *ffn_w, fpost, SEQ, SEQ)
        states.append((ka, va, ckv, kr[:, _KR_LANE:_KR_LANE + MLA_ROPE], kc, vc))

        mod_s = mod[l, 1:1 + DEC_BATCH]
        qa, ka, va, cq, ckv, kr, qc, kc, vc, pd = _inproj_call(x_s, mod_s, gpre, lp["w_in_p"], DEC_SEQ // 256,
                                                               rope_tabs)
        ys = (_lat_na_call(qa, ka, va, cache_na_k, cache_na_v, na_t2, l),
              _lat_mla_call(cq, ckv, kr, cache_mla_ckv, cache_kr, rope_tabs[2], rope_tabs[3], lp["mla_w"], l),
              _lat_swa_call(swa_sink[l], qc, kc, vc, cache_swa_k, cache_swa_v, l),
              _pool_call(pd, lp["w_bd"], lp["pool_scale"], DEC_SEQ))
        x_s = _merge_call(x_s, mod_s, gpre, ys, *merge_w, gpost, DEC_SEQ // 256)
        x_s = _ffn_call(x_s, mod_s, fpre, *ffn_w, fpost, DEC_SEQ, 512)

    def stack(j, *tail):
        return jnp.stack([st[j].reshape(BATCH, SEQ, *tail) for st in states], 1)

    return (x_p.reshape(BATCH, SEQ, D_MODEL), x_s.reshape(DEC_BATCH, DEC_SEQ, D_MODEL),
            stack(0, NA_HEADS, HEAD_DIM), stack(1, NA_HEADS, HEAD_DIM), stack(2, MLA_KV_RANK), stack(3, MLA_ROPE),
            stack(4, SWA_KV_HEADS, HEAD_DIM), stack(5, SWA_KV_HEADS, HEAD_DIM))
```

```python
import functools

import jax
import jax.numpy as jnp
import numpy as np
from jax import lax
from jax.experimental import pallas as pl
from jax.experimental.pallas import tpu as pltpu

F32 = jnp.float32
BF16 = jnp.bfloat16

D_MODEL = 1024
BATCH = 32
SEQ = 256
DEPTH = 2
DEC_BATCH = 2
DEC_SEQ = 2048
PAST_LEN = 256
GRID_W = 64
HEAD_DIM = 64
NA_HEADS = 4
NA_WIN_R = 8
NA_WIN_C = 16
MLA_HEADS = 4
MLA_NOPE = 64
MLA_ROPE = 32
MLA_V = 64
MLA_Q_RANK = 192
MLA_KV_RANK = 128
SWA_HEADS = 4
SWA_KV_HEADS = 2
SWA_WINDOW = 128
POOL_WINDOWS = (2, 4, 8, 16)
POOL_GROUP = 64
POOL_WIDTH = 256
BRANCH_W = 256
N_BRANCH = 4
D_FF = 2752
ROPE_BASE = 10000.0
EPS = 1e-6
NEG_INF = -1e30
ATT_SCALE = HEAD_DIM ** -0.5
MLA_SCALE = (MLA_NOPE + MLA_ROPE) ** -0.5

LANE = 128
D_FF_PAD = 2816
FF_CHUNK = 256
IN_PAD = 2048
Q_BLOCK = 128
NA_SPAN = 640
NA_DR = 2 * NA_WIN_R - 1
NA_DC = 2 * NA_WIN_C - 1
SWA_SPAN = 384
MLA_Q_BLOCK = 256
FFN_LOOKAHEAD = 2
CTX_SEQS = 4
VMEM_LIMIT = 56 * 1024 * 1024

_QA, _KA, _VA, _CQ, _CKV, _KR, _QC, _KC, _VC, _PD = 0, 256, 512, 768, 1024, 1152, 1280, 1536, 1664, 1792
_KR_LANE = 64


def _dot(a, b):
    return jnp.dot(a, b, preferred_element_type=F32)


def _dot_nt(a, b):
    return lax.dot_general(a, b, (((1,), (1,)), ((), ())), preferred_element_type=F32)


def _sigmoid(x):
    return 1.0 / (1.0 + jnp.exp(-x))


def _rms(x, g, n=None):
    n = x.shape[-1] if n is None else n
    ms = jnp.sum(x * x, -1, keepdims=True) * (1.0 / n)
    return x * lax.rsqrt(ms + EPS) * g


def _softmax_blocks(blocks, sink=None):
    m = None
    for s in blocks:
        mm = jnp.max(s, -1, keepdims=True)
        m = mm if m is None else jnp.maximum(m, mm)
    if sink is not None:
        m = jnp.maximum(m, sink)
    es = [jnp.exp(s - m) for s in blocks]
    l = None
    for e in es:
        ll = jnp.sum(e, -1, keepdims=True)
        l = ll if l is None else l + ll
    if sink is not None:
        l = l + jnp.exp(sink - m)
    return es, l


def _rope(x, cos, sin, q):
    w = x.shape[-1]
    lane = lax.broadcasted_iota(jnp.int32, x.shape, 1)
    up = pltpu.roll(x, w - q, axis=1)
    dn = pltpu.roll(x, q, axis=1)
    partner = jnp.where((lane & (2 * q - 1)) < q, up, dn)
    return x * cos + partner * sin


def _const_spec(shape):
    n = len(shape)
    return pl.BlockSpec(shape, lambda *_: (0,) * n, pipeline_mode=pl.Buffered(1))


def _params(n_axes):
    return pltpu.CompilerParams(dimension_semantics=("arbitrary",) * n_axes, vmem_limit_bytes=VMEM_LIMIT)


def _mod_kernel(cv_ref, w_ref, b_ref, o_ref):
    cv = cv_ref[...]
    a = (cv * _sigmoid(cv)).astype(BF16)
    o_ref[0] = _dot(a, w_ref[0].astype(BF16)) + b_ref[0]


def _mod_call(cv, w_mod, b_mod):
    tn = 1024
    return pl.pallas_call(
        _mod_kernel,
        grid=(DEPTH, 6 * D_MODEL // tn),
        in_specs=[_const_spec((8, D_MODEL)),
                  pl.BlockSpec((1, D_MODEL, tn), lambda l, j: (l, 0, j)),
                  pl.BlockSpec((1, 1, tn), lambda l, j: (l, 0, j))],
        out_specs=pl.BlockSpec((1, 8, tn), lambda l, j: (l, 0, j)),
        out_shape=jax.ShapeDtypeStruct((DEPTH, 8, 6 * D_MODEL), F32),
        compiler_params=_params(2),
        name="mod",
    )(cv, w_mod, b_mod.reshape(DEPTH, 1, 6 * D_MODEL))


_IN_SLOTS = ((_QA, 256), (_KA, 256), (_VA, 256), (_CQ, 256), (_CKV, 128), (_KR, 128),
             (_QC, 256), (_KC, 128), (_VC, 128), (_PD, 256))


def _inproj_kernel(*refs, latent):
    if latent:
        x_ref, mod_ref, g_ref, w_ref, c64_ref, s64_ref, cm_ref, sm_ref = refs[:8]
        outs = refs[8:]
    else:
        x_ref, mod_ref, g_ref, w_ref = refs[:4]
        outs = refs[4:]
    x = x_ref[...]
    h = _rms(x, g_ref[...]) * (1.0 + mod_ref[0, 1:2, :]) + mod_ref[0, 0:1, :]
    p = _dot(h.astype(BF16), w_ref[...])
    for (off, wd), o_ref in zip(_IN_SLOTS, outs):
        v = p[:, off:off + wd]
        if latent:
            if off == _QC:
                v = _rope(v, c64_ref[...], s64_ref[...], 16)
            elif off == _KC:
                v = _rope(v, c64_ref[:, :128], s64_ref[:, :128], 16)
            elif off == _KR:
                v = _rope(v, cm_ref[:, :128], sm_ref[:, :128], 8)
        o_ref[...] = v.astype(o_ref.dtype)


def _inproj_call(x2d, mod, g, w_in_p, tiles_per_batch, rope_tabs=None):
    t = x2d.shape[0]
    tm = 256
    latent = rope_tabs is not None
    dt = BF16 if latent else F32
    in_specs = [pl.BlockSpec((tm, D_MODEL), lambda i: (i, 0)),
                pl.BlockSpec((1, 6, D_MODEL), lambda i: (i // tiles_per_batch, 0, 0)),
                _const_spec((1, D_MODEL)),
                _const_spec((D_MODEL, IN_PAD))]
    args = [x2d, mod, g, w_in_p]
    if latent:
        c64, s64, cm, sm = rope_tabs
        in_specs += [pl.BlockSpec((tm, 256), lambda i: (i % tiles_per_batch, 0)),
                     pl.BlockSpec((tm, 256), lambda i: (i % tiles_per_batch, 0)),
                     pl.BlockSpec((tm, 512), lambda i: (i % tiles_per_batch, 0)),
                     pl.BlockSpec((tm, 512), lambda i: (i % tiles_per_batch, 0))]
        args += [c64, s64, cm, sm]
    return pl.pallas_call(
        functools.partial(_inproj_kernel, latent=latent),
        grid=(t // tm,),
        in_specs=in_specs,
        out_specs=[pl.BlockSpec((tm, wd), lambda i: (i, 0)) for _, wd in _IN_SLOTS],
        out_shape=[jax.ShapeDtypeStruct((t, wd), dt) for _, wd in _IN_SLOTS],
        compiler_params=_params(1),
        name="inproj_lat" if latent else "inproj_ctx",
    )(*args)


def _ctx_attn_kernel(q_ref, k_ref, v_ref, o_ref, *, seqs):
    q = q_ref[...].astype(BF16)
    k = k_ref[...].astype(BF16)
    v = v_ref[...].astype(BF16)
    scores = []
    for s in range(seqs):
        rows = slice(s * SEQ, (s + 1) * SEQ)
        for h in range(NA_HEADS):
            sl = slice(h * HEAD_DIM, (h + 1) * HEAD_DIM)
            scores.append(_dot_nt(q[rows, sl], k[rows, sl]))
    probs = [_softmax_blocks([sc]) for sc in scores]
    outs = []
    for s in range(seqs):
        rows = slice(s * SEQ, (s + 1) * SEQ)
        ys = []
        for h in range(NA_HEADS):
            sl = slice(h * HEAD_DIM, (h + 1) * HEAD_DIM)
            (e,), l = probs[s * NA_HEADS + h]
            ys.append(_dot(e.astype(BF16), v[rows, sl]) / l)
        outs.append(jnp.concatenate(ys, -1))
    o_ref[...] = (outs[0] if seqs == 1 else jnp.concatenate(outs, 0)).astype(o_ref.dtype)


def _ctx_attn_call(q, k, v):
    seqs = CTX_SEQS
    t = q.shape[0]
    spec = pl.BlockSpec((seqs * SEQ, 256), lambda b: (b, 0))
    return pl.pallas_call(
        functools.partial(_ctx_attn_kernel, seqs=seqs), grid=(t // SEQ // seqs,),
        in_specs=[spec, spec, spec], out_specs=spec,
        out_shape=jax.ShapeDtypeStruct((t, 256), BF16), compiler_params=_params(1), name="ctx_attn",
    )(q, k, v)


def _gqa_operands(q, k, rows_q, rows_k):
    group = SWA_HEADS // SWA_KV_HEADS
    out = []
    for kv in range(SWA_KV_HEADS):
        qs = jnp.concatenate([q[rows_q, (kv * group + g) * HEAD_DIM:(kv * group + g + 1) * HEAD_DIM]
                              for g in range(group)], 0)
        out.append((qs, k[rows_k, kv * HEAD_DIM:(kv + 1) * HEAD_DIM]))
    return out


def _gqa_sink(sink_ref, kv, m):
    group = SWA_HEADS // SWA_KV_HEADS
    row = lax.broadcasted_iota(jnp.int32, (group * m, 1), 0)
    col = jnp.full((group * m, 1), sink_ref[kv * group + group - 1], F32)
    for g in range(group - 2, -1, -1):
        col = jnp.where(row < (g + 1) * m, sink_ref[kv * group + g], col)
    return col


def _ctx_swa_kernel(sink_ref, q_ref, k_ref, v_ref, o_ref, *, seqs):
    q = q_ref[...].astype(BF16)
    k = k_ref[...].astype(BF16)
    v = v_ref[...].astype(BF16)
    group = SWA_HEADS // SWA_KV_HEADS
    scores = []
    for s in range(seqs):
        rows = slice(s * SEQ, (s + 1) * SEQ)
        for qs, ks in _gqa_operands(q, k, rows, rows):
            scores.append(_dot_nt(qs, ks))
    probs = [_softmax_blocks([sc], sink=_gqa_sink(sink_ref, i % SWA_KV_HEADS, SEQ)) for i, sc in enumerate(scores)]
    outs = []
    for s in range(seqs):
        rows = slice(s * SEQ, (s + 1) * SEQ)
        ys = []
        for kv in range(SWA_KV_HEADS):
            (e,), l = probs[s * SWA_KV_HEADS + kv]
            y = _dot(e.astype(BF16), v[rows, kv * HEAD_DIM:(kv + 1) * HEAD_DIM]) / l
            ys += [y[g * SEQ:(g + 1) * SEQ] for g in range(group)]
        outs.append(jnp.concatenate(ys, -1))
    o_ref[...] = (outs[0] if seqs == 1 else jnp.concatenate(outs, 0)).astype(o_ref.dtype)


def _ctx_swa_call(sink, q, k, v):
    seqs = CTX_SEQS
    t = q.shape[0]
    rows = seqs * SEQ
    return pl.pallas_call(
        functools.partial(_ctx_swa_kernel, seqs=seqs), grid=(t // rows,),
        in_specs=[pl.BlockSpec(memory_space=pltpu.SMEM),
                  pl.BlockSpec((rows, 256), lambda b: (b, 0)),
                  pl.BlockSpec((rows, 128), lambda b: (b, 0)),
                  pl.BlockSpec((rows, 128), lambda b: (b, 0))],
        out_specs=pl.BlockSpec((rows, 256), lambda b: (b, 0)),
        out_shape=jax.ShapeDtypeStruct((t, 256), BF16), compiler_params=_params(1), name="ctx_swa",
    )(sink, q, k, v)


def _mla_q(cq, qn_ref, wuq_ref):
    return _dot(_rms(cq, qn_ref[...], MLA_Q_RANK).astype(BF16), wuq_ref[...])


def _mla_kv(ckv, kr, kvn_ref, wuk_ref, wuv_ref):
    cn = _rms(ckv, kvn_ref[...]).astype(BF16)
    kcat = _dot(cn, wuk_ref[...]) + jnp.concatenate([kr] * MLA_HEADS, -1)
    return kcat, _dot(cn, wuv_ref[...])


def _ctx_mla_kernel(cq_ref, ckv_ref, kr_ref, qn_ref, wuq_ref, kvn_ref, wuk_ref, wuv_ref, o_ref, *, seqs):
    q = (_mla_q(cq_ref[...], qn_ref, wuq_ref) * MLA_SCALE).astype(BF16)
    kcat, v = _mla_kv(ckv_ref[...], kr_ref[...], kvn_ref, wuk_ref, wuv_ref)
    kcat = kcat.astype(BF16)
    v = v.astype(BF16)
    scores = []
    for s in range(seqs):
        rows = slice(s * SEQ, (s + 1) * SEQ)
        for h in range(MLA_HEADS):
            sl = slice(h * LANE, (h + 1) * LANE)
            scores.append(_dot_nt(q[rows, sl], kcat[rows, sl]))
    probs = [_softmax_blocks([sc]) for sc in scores]
    outs = []
    for s in range(seqs):
        rows = slice(s * SEQ, (s + 1) * SEQ)
        ys = []
        for h in range(MLA_HEADS):
            (e,), l = probs[s * MLA_HEADS + h]
            ys.append(_dot(e.astype(BF16), v[rows, h * MLA_V:(h + 1) * MLA_V]) / l)
        outs.append(jnp.concatenate(ys, -1))
    o_ref[...] = (outs[0] if seqs == 1 else jnp.concatenate(outs, 0)).astype(o_ref.dtype)


def _ctx_mla_call(cq, ckv, kr, mla_w):
    seqs = CTX_SEQS
    t = cq.shape[0]
    rows = seqs * SEQ
    qn, wuq, kvn, wuk, wuv = mla_w
    return pl.pallas_call(
        functools.partial(_ctx_mla_kernel, seqs=seqs), grid=(t // rows,),
        in_specs=[pl.BlockSpec((rows, 256), lambda b: (b, 0)),
                  pl.BlockSpec((rows, 128), lambda b: (b, 0)),
                  pl.BlockSpec((rows, 128), lambda b: (b, 0)),
                  _const_spec(qn.shape), _const_spec(wuq.shape), _const_spec(kvn.shape),
                  _const_spec(wuk.shape), _const_spec(wuv.shape)],
        out_specs=pl.BlockSpec((rows, 256), lambda b: (b, 0)),
        out_shape=jax.ShapeDtypeStruct((t, 256), BF16), compiler_params=_params(1), name="ctx_mla",
    )(cq, ckv, kr, qn, wuq, kvn, wuk, wuv)


_POOL_PAD = 8


def _pool_kernel(x_ref, w_ref, sc_ref, o_ref, *, n):
    ne = n + 2 * _POOL_PAD
    lo, hi = _POOL_PAD, _POOL_PAD + n
    z = jnp.zeros((_POOL_PAD, POOL_WIDTH), F32)
    grp = lax.broadcasted_iota(jnp.int32, (n, POOL_WIDTH), 1) >> 6
    t = lax.broadcasted_iota(jnp.int32, (n, POOL_WIDTH), 0)
    half = jnp.where(grp == 0, 1, jnp.where(grp == 1, 2, jnp.where(grp == 2, 4, 8)))
    cnt = (jnp.minimum(t + half, n) - jnp.maximum(t - half, 0)).astype(F32)

    def pair(a, s):
        return pltpu.roll(a, s, axis=0) + pltpu.roll(a, ne - s, axis=0)

    for s in range(x_ref.shape[0] // n):
        x = x_ref[s * n:(s + 1) * n, :].astype(F32)
        xz = jnp.concatenate([z, x, z], 0)
        s2 = xz + pltpu.roll(xz, 1, axis=0)
        s4 = pair(s2, 1)
        s8 = pair(s4, 2)
        s16 = pair(s8, 4)
        tot = jnp.where(grp == 0, s2[lo:hi],
                        jnp.where(grp == 1, s4[lo:hi], jnp.where(grp == 2, s8[lo:hi], s16[lo:hi])))
        dlt = (tot / cnt - x).astype(BF16)
        o_ref[s * n:(s + 1) * n, :] = (_dot(dlt, w_ref[...]) * sc_ref[...]).astype(o_ref.dtype)


def _pool_call(pd, w_bd, scale, seq, seqs):
    t = pd.shape[0]
    rows = seq * seqs
    return pl.pallas_call(
        functools.partial(_pool_kernel, n=seq), grid=(t // rows,),
        in_specs=[pl.BlockSpec((rows, POOL_WIDTH), lambda b: (b, 0)),
                  _const_spec((POOL_WIDTH, POOL_WIDTH)), _const_spec((1, POOL_WIDTH))],
        out_specs=pl.BlockSpec((rows, POOL_WIDTH), lambda b: (b, 0)),
        out_shape=jax.ShapeDtypeStruct((t, POOL_WIDTH), BF16), compiler_params=_params(1), name="pool",
    )(pd, w_bd, scale)


def _lat_na_kernel(q_ref, k_ref, v_ref, kc_ref, vc_ref, t2_ref, o_ref):
    n = pl.program_id(1)
    rows = DEC_SEQ // GRID_W
    row0 = jnp.clip(2 * n - NA_WIN_R // 2, 0, rows - NA_SPAN // GRID_W)
    start = pl.multiple_of(row0 * GRID_W, LANE)
    q = q_ref[...]
    k = k_ref[pl.ds(start, NA_SPAN), :]
    v = v_ref[pl.ds(start, NA_SPAN), :]
    kc = kc_ref[...].astype(BF16)
    vc = vc_ref[...].astype(BF16)

    pairs = NA_SPAN // LANE
    low_half = lax.broadcasted_iota(jnp.int32, (GRID_W, LANE), 1) < GRID_W
    entries, masks = [], []
    for a in range(Q_BLOCK // GRID_W):
        r = 2 * n + a
        r_start = jnp.clip(r - NA_WIN_R // 2, 0, rows - NA_WIN_R)
        for p in range(pairs):
            rk = row0 + 2 * p
            ok0 = ((rk >= r_start) & (rk < r_start + NA_WIN_R)).astype(jnp.int32)
            ok1 = ((rk + 1 >= r_start) & (rk + 1 < r_start + NA_WIN_R)).astype(jnp.int32)
            entries.append(jnp.clip(rk - r + NA_WIN_R, 0, NA_DR))
            masks.append(jnp.where(low_half, ok0, ok1) > 0)

    heads = [slice(h * HEAD_DIM, (h + 1) * HEAD_DIM) for h in range(NA_HEADS)]
    raw = [(_dot_nt(q[:, sl], k[:, sl]), _dot_nt(q[:, sl], kc[:, sl])) for sl in heads]
    probs = []
    for h, (s_loc, s_ctx) in enumerate(raw):
        cols = []
        for p in range(pairs):
            blk = [jnp.where(masks[a * pairs + p], t2_ref[h, entries[a * pairs + p]], NEG_INF)
                   for a in range(Q_BLOCK // GRID_W)]
            cols.append(jnp.concatenate(blk, 0))
        probs.append(_softmax_blocks([s_loc + jnp.concatenate(cols, 1), s_ctx]))
    ys = []
    for sl, ((e_loc, e_ctx), l) in zip(heads, probs):
        y = _dot(e_loc.astype(BF16), v[:, sl]) + _dot(e_ctx.astype(BF16), vc[:, sl])
        ys.append(y / l)
    o_ref[...] = jnp.concatenate(ys, -1).astype(o_ref.dtype)


def _lat_na_call(q, k, v, cache_k, cache_v, t2, layer):
    nq = DEC_SEQ // Q_BLOCK
    seq_spec = pl.BlockSpec((DEC_SEQ, 256), lambda b, n: (b, 0))
    cache_spec = pl.BlockSpec((None, None, PAST_LEN, 256), lambda b, n: (b, layer, 0, 0))
    return pl.pallas_call(
        _lat_na_kernel, grid=(DEC_BATCH, nq),
        in_specs=[pl.BlockSpec((Q_BLOCK, 256), lambda b, n: (b * nq + n, 0)), seq_spec, seq_spec,
                  cache_spec, cache_spec,
                  pl.BlockSpec((None, NA_HEADS, NA_DR + 1, GRID_W, LANE), lambda b, n: (layer, 0, 0, 0, 0),
                               pipeline_mode=pl.Buffered(1))],
        out_specs=pl.BlockSpec((Q_BLOCK, 256), lambda b, n: (b * nq + n, 0)),
        out_shape=jax.ShapeDtypeStruct((DEC_BATCH * DEC_SEQ, 256), BF16),
        compiler_params=_params(2), name="lat_na",
    )(q, k, v, cache_k, cache_v, t2)


def _lat_swa_kernel(sink_ref, q_ref, k_ref, v_ref, kc_ref, vc_ref, o_ref):
    n = pl.program_id(1)
    start = pl.multiple_of(jnp.clip(n - 1, 0, DEC_SEQ // Q_BLOCK - 3) * Q_BLOCK, LANE)
    q = q_ref[...]
    k = k_ref[pl.ds(start, SWA_SPAN), :]
    v = v_ref[pl.ds(start, SWA_SPAN), :]
    kc = kc_ref[...].astype(BF16)
    vc = vc_ref[...].astype(BF16)
    group = SWA_HEADS // SWA_KV_HEADS
    m = group * Q_BLOCK
    q_pos = n * Q_BLOCK + (lax.broadcasted_iota(jnp.int32, (m, SWA_SPAN), 0) & (Q_BLOCK - 1))
    k_pos = start + lax.broadcasted_iota(jnp.int32, (m, SWA_SPAN), 1)
    valid = jnp.abs(q_pos - k_pos) <= SWA_WINDOW
    everything = slice(None)
    raw = []
    for (qs, ks), (_, kcs) in zip(_gqa_operands(q, k, everything, everything),
                                  _gqa_operands(q, kc, everything, everything)):
        raw.append((_dot_nt(qs, ks), _dot_nt(qs, kcs)))
    probs = [_softmax_blocks([jnp.where(valid, s_loc, NEG_INF), s_ctx], sink=_gqa_sink(sink_ref, kv, Q_BLOCK))
             for kv, (s_loc, s_ctx) in enumerate(raw)]
    ys = []
    for kv, ((e_loc, e_ctx), l) in enumerate(probs):
        kvsl = slice(kv * HEAD_DIM, (kv + 1) * HEAD_DIM)
        y = (_dot(e_loc.astype(BF16), v[:, kvsl]) + _dot(e_ctx.astype(BF16), vc[:, kvsl])) / l
        ys += [y[g * Q_BLOCK:(g + 1) * Q_BLOCK] for g in range(group)]
    o_ref[...] = jnp.concatenate(ys, -1).astype(o_ref.dtype)


def _lat_swa_call(sink, q, k, v, cache_k, cache_v, layer):
    nq = DEC_SEQ // Q_BLOCK
    seq_spec = pl.BlockSpec((DEC_SEQ, 128), lambda b, n: (b, 0))
    cache_spec = pl.BlockSpec((None, None, PAST_LEN, 128), lambda b, n: (b, layer, 0, 0))
    return pl.pallas_call(
        _lat_swa_kernel, grid=(DEC_BATCH, nq),
        in_specs=[pl.BlockSpec(memory_space=pltpu.SMEM),
                  pl.BlockSpec((Q_BLOCK, 256), lambda b, n: (b * nq + n, 0)), seq_spec, seq_spec,
                  cache_spec, cache_spec],
        out_specs=pl.BlockSpec((Q_BLOCK, 256), lambda b, n: (b * nq + n, 0)),
        out_shape=jax.ShapeDtypeStruct((DEC_BATCH * DEC_SEQ, 256), BF16),
        compiler_params=_params(2), name="lat_swa",
    )(sink, q, k, v, cache_k, cache_v)


def _lat_mla_kernel(cq_ref, ckv_ref, kr_ref, cckv_ref, ckr_ref, cm_ref, sm_ref,
                    qn_ref, wuq_ref, kvn_ref, wuk_ref, wuv_ref, o_ref, kcat_s, v_s):
    @pl.when(pl.program_id(1) == 0)
    def _():
        kc, vc = _mla_kv(cckv_ref[...], ckr_ref[...], kvn_ref, wuk_ref, wuv_ref)
        kcat_s[0:PAST_LEN, :] = kc.astype(BF16)
        v_s[0:PAST_LEN, :] = vc.astype(BF16)
        kl, vl = _mla_kv(ckv_ref[...].astype(F32), kr_ref[...].astype(F32), kvn_ref, wuk_ref, wuv_ref)
        kcat_s[PAST_LEN:, :] = kl.astype(BF16)
        v_s[PAST_LEN:, :] = vl.astype(BF16)

    q = _mla_q(cq_ref[...].astype(F32), qn_ref, wuq_ref)
    q = (_rope(q, cm_ref[...], sm_ref[...], 8) * MLA_SCALE).astype(BF16)
    scores = [_dot_nt(q[:, h * LANE:(h + 1) * LANE], kcat_s[:, h * LANE:(h + 1) * LANE]) for h in range(MLA_HEADS)]
    probs = [_softmax_blocks([sc]) for sc in scores]
    ys = [_dot(e.astype(BF16), v_s[:, h * MLA_V:(h + 1) * MLA_V]) / l for h, ((e,), l) in enumerate(probs)]
    o_ref[...] = jnp.concatenate(ys, -1).astype(o_ref.dtype)


def _lat_mla_call(cq, ckv, kr, cache_ckv, cache_kr, cm, sm, mla_w, layer):
    qb = MLA_Q_BLOCK
    nq = DEC_SEQ // qb
    qn, wuq, kvn, wuk, wuv = mla_w
    seq_spec = pl.BlockSpec((DEC_SEQ, 128), lambda b, n: (b, 0))
    cache_spec = pl.BlockSpec((None, None, PAST_LEN, 128), lambda b, n: (b, layer, 0, 0))
    tab_spec = pl.BlockSpec((qb, 512), lambda b, n: (n, 0))
    return pl.pallas_call(
        _lat_mla_kernel, grid=(DEC_BATCH, nq),
        in_specs=[pl.BlockSpec((qb, 256), lambda b, n: (b * nq + n, 0)), seq_spec, seq_spec,
                  cache_spec, cache_spec, tab_spec, tab_spec,
                  _const_spec(qn.shape), _const_spec(wuq.shape), _const_spec(kvn.shape),
                  _const_spec(wuk.shape), _const_spec(wuv.shape)],
        out_specs=pl.BlockSpec((qb, 256), lambda b, n: (b * nq + n, 0)),
        out_shape=jax.ShapeDtypeStruct((DEC_BATCH * DEC_SEQ, 256), BF16),
        scratch_shapes=[pltpu.VMEM((PAST_LEN + DEC_SEQ, MLA_HEADS * LANE), BF16),
                        pltpu.VMEM((PAST_LEN + DEC_SEQ, MLA_HEADS * MLA_V), BF16)],
        compiler_params=_params(2), name="lat_mla",
    )(cq, ckv, kr, cache_ckv, cache_kr, cm, sm, qn, wuq, kvn, wuk, wuv)


def _merge_kernel(x_ref, mod_ref, gpre_ref, ya_ref, yb_ref, yc_ref, yd_ref,
                  wg_ref, bg_ref, wb_ref, wo_ref, gpost_ref, o_ref):
    x = x_ref[...]
    h = (_rms(x, gpre_ref[...]) * (1.0 + mod_ref[0, 1:2, :]) + mod_ref[0, 0:1, :]).astype(BF16)
    merged = None
    for k, y_ref in enumerate((ya_ref, yb_ref, yc_ref, yd_ref)):
        cols = slice(k * D_MODEL, (k + 1) * D_MODEL)
        gate = _sigmoid(_dot(h, wg_ref[:, cols]) + bg_ref[:, cols])
        term = gate * _dot(y_ref[...], wb_ref[k])
        merged = term if merged is None else merged + term
    o = _dot(merged.astype(BF16), wo_ref[...])
    o_ref[...] = x + mod_ref[0, 2:3, :] * _rms(o, gpost_ref[...])


def _merge_call(x2d, mod, gpre, ys, wg, bg, wb, wo, gpost, tiles_per_batch):
    t = x2d.shape[0]
    tm = 256
    tile = pl.BlockSpec((tm, D_MODEL), lambda i: (i, 0))
    ytile = pl.BlockSpec((tm, BRANCH_W), lambda i: (i, 0))
    return pl.pallas_call(
        _merge_kernel, grid=(t // tm,),
        in_specs=[tile, pl.BlockSpec((1, 6, D_MODEL), lambda i: (i // tiles_per_batch, 0, 0)),
                  _const_spec((1, D_MODEL)), ytile, ytile, ytile, ytile,
                  _const_spec(wg.shape), _const_spec(bg.shape), _const_spec(wb.shape),
                  _const_spec(wo.shape), _const_spec((1, D_MODEL))],
        out_specs=tile,
        out_shape=jax.ShapeDtypeStruct((t, D_MODEL), F32),
        compiler_params=_params(1), name="merge",
    )(x2d, mod, gpre, *ys, wg, bg, wb, wo, gpost)


_GAP = 8


def _ffn_kernel(*refs, seqs, halo):
    if halo:
        x_ref, xp_ref, xn_ref = refs[:3]
        refs = refs[3:]
    else:
        x_ref = refs[0]
        refs = refs[1:]
    mod_ref, gpre_ref, wa_ref, wg_ref, ca_ref, cg_ref, wd_ref, gpost_ref, o_ref = refs
    x = x_ref[...]
    tm = x.shape[0]
    shift, scale, gate = mod_ref[0, 3:4, :], mod_ref[0, 4:5, :], mod_ref[0, 5:6, :]

    def pre(xx):
        return _rms(xx, gpre_ref[...]) * (1.0 + scale) + shift

    h = pre(x)
    if halo:
        i = pl.program_id(0) % halo
        hp = jnp.where(i == 0, 0.0, pre(xp_ref[...]))
        hn = jnp.where(i == halo - 1, 0.0, pre(xn_ref[...]))
        pieces = [hp, h, hn]
        starts = [_GAP]
        seq_len = tm
    else:
        seq_len = tm // seqs
        z = jnp.zeros((_GAP, D_MODEL), F32)
        pieces, starts = [], []
        for s in range(seqs):
            starts.append(s * (seq_len + _GAP))
            pieces += [h[s * seq_len:(s + 1) * seq_len], z]
    hb = jnp.concatenate(pieces, 0).astype(BF16)
    rows = hb.shape[0]

    def up(c):
        cols = slice(c * FF_CHUNK, (c + 1) * FF_CHUNK)
        return _dot(hb, wa_ref[:, cols]), _dot(hb, wg_ref[:, cols])

    def conv(u, c_ref, cols):
        prev = pltpu.roll(u, 1, axis=0)
        nxt = pltpu.roll(u, rows - 1, axis=0)
        return prev * c_ref[0:1, cols] + u * c_ref[1:2, cols] + nxt * c_ref[2:3, cols]

    n_chunks = D_FF_PAD // FF_CHUNK
    pending = [up(c) for c in range(FFN_LOOKAHEAD)]
    acts = []
    for c in range(n_chunks):
        if c + FFN_LOOKAHEAD < n_chunks:
            pending.append(up(c + FFN_LOOKAHEAD))
        ua, ug = pending.pop(0)
        cols = slice(c * FF_CHUNK, (c + 1) * FF_CHUNK)
        a = conv(ua, ca_ref, cols)
        g = conv(ug, cg_ref, cols)
        acts.append((g * _sigmoid(g) * a).astype(BF16))
    acc = _dot(jnp.concatenate(acts, 1), wd_ref[...])
    for s, st in enumerate(starts):
        ys = _rms(acc[st:st + seq_len], gpost_ref[...])
        o_ref[s * seq_len:(s + 1) * seq_len, :] = x[s * seq_len:(s + 1) * seq_len] + gate * ys


def _ffn_call(x2d, mod, gpre, wa, wg, ca, cg, wd, gpost, seq, tm):
    t = x2d.shape[0]
    tile = pl.BlockSpec((tm, D_MODEL), lambda i: (i, 0))
    in_specs = [tile]
    args = [x2d]
    if tm < seq:
        halo, seqs = seq // tm, 1
        r = tm // _GAP
        last = t // _GAP - 1
        in_specs += [pl.BlockSpec((_GAP, D_MODEL), lambda i: (jnp.maximum(i * r - 1, 0), 0)),
                     pl.BlockSpec((_GAP, D_MODEL), lambda i: (jnp.minimum((i + 1) * r, last), 0))]
        args += [x2d, x2d]
    else:
        halo, seqs = 0, tm // seq
    tiles_per_mod = t // tm // mod.shape[0]
    in_specs += [pl.BlockSpec((1, 6, D_MODEL), lambda i: (i // tiles_per_mod, 0, 0)),
                 _const_spec((1, D_MODEL)), _const_spec(wa.shape), _const_spec(wg.shape),
                 _const_spec(ca.shape), _const_spec(cg.shape), _const_spec(wd.shape), _const_spec((1, D_MODEL))]
    args += [mod, gpre, wa, wg, ca, cg, wd, gpost]
    return pl.pallas_call(
        functools.partial(_ffn_kernel, seqs=seqs, halo=halo), grid=(t // tm,),
        in_specs=in_specs, out_specs=tile,
        out_shape=jax.ShapeDtypeStruct((t, D_MODEL), F32),
        compiler_params=_params(1), name="ffn",
    )(*args)


def _rope_tables():
    t = jnp.arange(DEC_SEQ)
    pos = (t // GRID_W, t % GRID_W)

    def tab(d):
        half = d // 4
        inv = ROPE_BASE ** (-jnp.arange(half, dtype=F32) / half)
        cs, sn = [], []
        for p in pos:
            ang = p.astype(F32)[:, None] * inv[None, :]
            cs += [jnp.cos(ang), jnp.cos(ang)]
            sn += [-jnp.sin(ang), jnp.sin(ang)]
        return jnp.concatenate(cs, -1), jnp.concatenate(sn, -1)

    c64, s64 = tab(HEAD_DIM)
    c32, s32 = tab(MLA_ROPE)
    one = jnp.ones((DEC_SEQ, MLA_NOPE), F32)
    zero = jnp.zeros((DEC_SEQ, MLA_NOPE), F32)
    pad1 = jnp.ones((DEC_SEQ, LANE - MLA_NOPE - MLA_ROPE), F32)
    pad0 = jnp.zeros((DEC_SEQ, LANE - MLA_NOPE - MLA_ROPE), F32)
    cm = jnp.concatenate([one, c32, pad1], -1)
    sm = jnp.concatenate([zero, s32, pad0], -1)
    return (jnp.tile(c64, (1, 4)), jnp.tile(s64, (1, 4)), jnp.tile(cm, (1, 4)), jnp.tile(sm, (1, 4)))


def _na_table_kernel(rpb_ref, e_ref, ok_ref, o_ref):
    r = rpb_ref[...]
    r1 = r.astype(BF16)
    r2 = (r - r1.astype(F32)).astype(BF16)
    r3 = (r - r1.astype(F32) - r2.astype(F32)).astype(BF16)
    e = e_ref[...]
    t = _dot(r1, e) + _dot(r2, e) + _dot(r3, e)
    o_ref[...] = jnp.where(ok_ref[...] > 0, t, NEG_INF)


def _na_bias_tables(na_rpb):
    c = np.arange(GRID_W)[:, None]
    w = np.arange(GRID_W)[None, :]
    dc = (w - c + NA_WIN_C - 1).reshape(-1)
    onehot = (np.arange(LANE)[:, None] == dc[None, :]).astype(np.float32)
    c_start = np.clip(c - NA_WIN_C // 2, 0, GRID_W - NA_WIN_C)
    ok = ((w >= c_start) & (w < c_start + NA_WIN_C)).reshape(1, -1).astype(np.int32)
    rows = DEPTH * NA_HEADS * NA_DR
    rpb2 = jnp.pad(na_rpb.reshape(rows, NA_DC), ((0, LANE - rows), (0, LANE - NA_DC)))
    t = pl.pallas_call(
        _na_table_kernel, out_shape=jax.ShapeDtypeStruct((LANE, GRID_W * GRID_W), F32), name="na_table",
        compiler_params=pltpu.CompilerParams(vmem_limit_bytes=VMEM_LIMIT),
    )(rpb2, jnp.asarray(onehot, BF16), jnp.asarray(ok))
    t = t[:rows].reshape(DEPTH, NA_HEADS, NA_DR, GRID_W, GRID_W)
    t = jnp.pad(t, ((0, 0), (0, 0), (1, 1), (0, 0), (0, 0)), constant_values=NEG_INF)
    return jnp.concatenate([t[:, :, :-1], t[:, :, 1:]], -1)


def _pad_cols(w, n):
    return jnp.pad(w, ((0, 0), (0, n - w.shape[1])))


def _prep_layer(w_in, w_gate, b_gate, mla_q_norm, mla_w_uq, mla_kv_norm, mla_w_ukv, pool_w, pool_scale,
                w_branch, w_out, ffn_w_up, ffn_conv, ffn_w_down):
    a_cols = w_in[:, :768] * jnp.concatenate([jnp.full((256,), ATT_SCALE, F32), jnp.ones((512,), F32)])
    b0 = 768
    cq = _pad_cols(w_in[:, b0:b0 + MLA_Q_RANK], 256)
    ckv = w_in[:, b0 + MLA_Q_RANK:b0 + MLA_Q_RANK + MLA_KV_RANK]
    kr = jnp.pad(w_in[:, b0 + 320:b0 + 352], ((0, 0), (_KR_LANE, LANE - _KR_LANE - MLA_ROPE)))
    c0 = 1120
    c_cols = w_in[:, c0:c0 + 512] * jnp.concatenate([jnp.full((256,), ATT_SCALE, F32), jnp.ones((256,), F32)])
    w_in_p = jnp.concatenate([a_cols, cq, ckv, kr, c_cols, w_in[:, 1632:]], 1).astype(BF16)

    qn = _pad_cols(mla_q_norm[None, :], 256)
    wuq = mla_w_uq.reshape(MLA_Q_RANK, MLA_HEADS, MLA_NOPE + MLA_ROPE)
    wuq = jnp.pad(wuq, ((0, 256 - MLA_Q_RANK), (0, 0), (0, LANE - MLA_NOPE - MLA_ROPE)))
    wuq = wuq.reshape(256, MLA_HEADS * LANE).astype(BF16)
    wukv = mla_w_ukv.reshape(MLA_KV_RANK, MLA_HEADS, MLA_NOPE + MLA_V)
    wuk = jnp.pad(wukv[:, :, :MLA_NOPE], ((0, 0), (0, 0), (0, LANE - MLA_NOPE)))
    wuk = wuk.reshape(MLA_KV_RANK, MLA_HEADS * LANE).astype(BF16)
    wuv = wukv[:, :, MLA_NOPE:].reshape(MLA_KV_RANK, MLA_HEADS * MLA_V).astype(BF16)
    mla_w = (qn, wuq, mla_kv_norm[None, :], wuk, wuv)

    w_bd = jax.scipy.linalg.block_diag(*[pool_w[g] for g in range(len(POOL_WINDOWS))]).astype(BF16)

    pad_ff = D_FF_PAD - D_FF
    wa = _pad_cols(ffn_w_up[:, :D_FF], D_FF_PAD).astype(BF16)
    wg = _pad_cols(ffn_w_up[:, D_FF:], D_FF_PAD).astype(BF16)
    ca = _pad_cols(ffn_conv[:, :D_FF], D_FF_PAD)
    cg = _pad_cols(ffn_conv[:, D_FF:], D_FF_PAD)
    wd = jnp.pad(ffn_w_down, ((0, pad_ff), (0, 0))).astype(BF16)
    return dict(w_in_p=w_in_p, mla_w=mla_w, w_bd=w_bd, pool_scale=pool_scale[None, :],
                wgate=w_gate.astype(BF16), bgate=b_gate[None, :],
                wb=w_branch.astype(BF16), wo=w_out.astype(BF16), wa=wa, wg=wg, ca=ca, cg=cg, wd=wd)


def kernel(x_prompt, x_sample, cache_na_k, cache_na_v, cache_mla_ckv, cache_mla_krope, cache_swa_k, cache_swa_v, c, c_ctx, w_mod, b_mod, g_attn_pre, g_attn_post, g_ffn_pre, g_ffn_post, w_in, w_gate, b_gate, na_rpb, mla_q_norm, mla_w_uq, mla_kv_norm, mla_w_ukv, swa_sink, pool_w, pool_scale, w_branch, w_out, ffn_w_up, ffn_conv, ffn_w_down):
    x_p = x_prompt.reshape(BATCH * SEQ, D_MODEL)
    x_s = x_sample.reshape(DEC_BATCH * DEC_SEQ, D_MODEL)

    cv = jnp.concatenate([c_ctx[None, :], c, jnp.zeros((8 - 1 - DEC_BATCH, D_MODEL), F32)], 0)
    mod = _mod_call(cv, w_mod, b_mod).reshape(DEPTH, 8, 6, D_MODEL)
    rope_tabs = _rope_tables()
    na_t2 = _na_bias_tables(na_rpb)
    cache_na_k = cache_na_k.reshape(DEC_BATCH, DEPTH, PAST_LEN, 256)
    cache_na_v = cache_na_v.reshape(DEC_BATCH, DEPTH, PAST_LEN, 256)
    cache_swa_k = cache_swa_k.reshape(DEC_BATCH, DEPTH, PAST_LEN, 128)
    cache_swa_v = cache_swa_v.reshape(DEC_BATCH, DEPTH, PAST_LEN, 128)
    cache_kr = jnp.pad(cache_mla_krope, ((0, 0), (0, 0), (0, 0), (_KR_LANE, LANE - _KR_LANE - MLA_ROPE)))

    states = []
    for l in range(DEPTH):
        lp = _prep_layer(w_in[l], w_gate[l], b_gate[l], mla_q_norm[l], mla_w_uq[l], mla_kv_norm[l],
                         mla_w_ukv[l], pool_w[l], pool_scale[l], w_branch[l], w_out[l], ffn_w_up[l], ffn_conv[l],
                         ffn_w_down[l])
        gpre, gpost = g_attn_pre[l][None, :], g_attn_post[l][None, :]
        fpre, fpost = g_ffn_pre[l][None, :], g_ffn_post[l][None, :]
        ffn_w = (lp["wa"], lp["wg"], lp["ca"], lp["cg"], lp["wd"])
        merge_w = (lp["wgate"], lp["bgate"], lp["wb"], lp["wo"])

        mod_p = mod[l, 0:1]
        qa, ka, va, cq, ckv, kr, qc, kc, vc, pd = _inproj_call(x_p, mod_p, gpre, lp["w_in_p"], BATCH * SEQ // 256)
        ys = (_ctx_attn_call(qa, ka, va), _ctx_mla_call(cq, ckv, kr, lp["mla_w"]),
              _ctx_swa_call(swa_sink[l], qc, kc, vc), _pool_call(pd, lp["w_bd"], lp["pool_scale"], SEQ, CTX_SEQS))
        x_p = _merge_call(x_p, mod_p, gpre, ys, *merge_w, gpost, BATCH * SEQ // 256)
        x_p = _ffn_call(x_p, mod_p, fpre, *ffn_w, fpost, SEQ, SEQ)
        states.append((ka, va, ckv, kr[:, _KR_LANE:_KR_LANE + MLA_ROPE], kc, vc))

        mod_s = mod[l, 1:1 + DEC_BATCH]
        qa, ka, va, cq, ckv, kr, qc, kc, vc, pd = _inproj_call(x_s, mod_s, gpre, lp["w_in_p"], DEC_SEQ // 256,
                                                               rope_tabs)
        ys = (_lat_na_call(qa, ka, va, cache_na_k, cache_na_v, na_t2, l),
              _lat_mla_call(cq, ckv, kr, cache_mla_ckv, cache_kr, rope_tabs[2], rope_tabs[3], lp["mla_w"], l),
              _lat_swa_call(swa_sink[l], qc, kc, vc, cache_swa_k, cache_swa_v, l),
              _pool_call(pd, lp["w_bd"], lp["pool_scale"], DEC_SEQ, 1))
        x_s = _merge_call(x_s, mod_s, gpre, ys, *merge_w, gpost, DEC_SEQ // 256)
        x_s = _ffn_call(x_s, mod_s, fpre, *ffn_w, fpost, DEC_SEQ, 512)

    def stack(j, *tail):
        return jnp.stack([st[j].reshape(BATCH, SEQ, *tail) for st in states], 1)

    return (x_p.reshape(BATCH, SEQ, D_MODEL), x_s.reshape(DEC_BATCH, DEC_SEQ, D_MODEL),
            stack(0, NA_HEADS, HEAD_DIM), stack(1, NA_HEADS, HEAD_DIM), stack(2, MLA_KV_RANK), stack(3, MLA_ROPE),
            stack(4, SWA_KV_HEADS, HEAD_DIM), stack(5, SWA_KV_HEADS, HEAD_DIM))
```

```python
import functools

import jax
import jax.numpy as jnp
import numpy as np
from jax import lax
from jax.experimental import pallas as pl
from jax.experimental.pallas import tpu as pltpu

F32 = jnp.float32
BF16 = jnp.bfloat16

D_MODEL = 1024
BATCH = 32
SEQ = 256
DEPTH = 2
DEC_BATCH = 2
DEC_SEQ = 2048
PAST_LEN = 256
GRID_W = 64
HEAD_DIM = 64
NA_HEADS = 4
NA_WIN_R = 8
NA_WIN_C = 16
MLA_HEADS = 4
MLA_NOPE = 64
MLA_ROPE = 32
MLA_V = 64
MLA_Q_RANK = 192
MLA_KV_RANK = 128
SWA_HEADS = 4
SWA_KV_HEADS = 2
SWA_WINDOW = 128
POOL_WINDOWS = (2, 4, 8, 16)
POOL_GROUP = 64
POOL_WIDTH = 256
BRANCH_W = 256
N_BRANCH = 4
D_FF = 2752
ROPE_BASE = 10000.0
EPS = 1e-6
NEG_INF = -1e30
ATT_SCALE = HEAD_DIM ** -0.5
MLA_SCALE = (MLA_NOPE + MLA_ROPE) ** -0.5

LANE = 128
D_FF_PAD = 2816
FF_CHUNK = 256
IN_PAD = 2048
Q_BLOCK = 128
NA_Q_BLOCK = 256
NA_SPAN = 768
NA_DR = 2 * NA_WIN_R - 1
NA_DC = 2 * NA_WIN_C - 1
SWA_SPAN = 384
MLA_Q_BLOCK = 256
FFN_LOOKAHEAD = 2
CTX_SEQS = 4
VMEM_LIMIT = 56 * 1024 * 1024

_QA, _KA, _VA, _CQ, _CKV, _KR, _QC, _KC, _VC, _PD = 0, 256, 512, 768, 1024, 1152, 1280, 1536, 1664, 1792
_KR_LANE = 64


def _dot(a, b):
    return jnp.dot(a, b, preferred_element_type=F32)


def _dot_nt(a, b):
    return lax.dot_general(a, b, (((1,), (1,)), ((), ())), preferred_element_type=F32)


def _sigmoid(x):
    return 1.0 / (1.0 + jnp.exp(-x))


def _rms(x, g, n=None):
    n = x.shape[-1] if n is None else n
    ms = jnp.sum(x * x, -1, keepdims=True) * (1.0 / n)
    return x * lax.rsqrt(ms + EPS) * g


def _softmax_blocks(blocks, sink=None):
    m = None
    for s in blocks:
        mm = jnp.max(s, -1, keepdims=True)
        m = mm if m is None else jnp.maximum(m, mm)
    if sink is not None:
        m = jnp.maximum(m, sink)
    es = [jnp.exp(s - m) for s in blocks]
    l = None
    for e in es:
        ll = jnp.sum(e, -1, keepdims=True)
        l = ll if l is None else l + ll
    if sink is not None:
        l = l + jnp.exp(sink - m)
    return es, l


def _rope(x, cos, sin, q):
    w = x.shape[-1]
    lane = lax.broadcasted_iota(jnp.int32, x.shape, 1)
    up = pltpu.roll(x, w - q, axis=1)
    dn = pltpu.roll(x, q, axis=1)
    partner = jnp.where((lane & (2 * q - 1)) < q, up, dn)
    return x * cos + partner * sin


def _const_spec(shape):
    n = len(shape)
    return pl.BlockSpec(shape, lambda *_: (0,) * n, pipeline_mode=pl.Buffered(1))


def _layer_spec(arr, layer):
    n = arr.ndim - 1
    return pl.BlockSpec((None,) + arr.shape[1:], lambda *_: (layer,) + (0,) * n, pipeline_mode=pl.Buffered(1))


def _params(n_axes):
    return pltpu.CompilerParams(dimension_semantics=("arbitrary",) * n_axes, vmem_limit_bytes=VMEM_LIMIT)


def _mod_kernel(cv_ref, w_ref, b_ref, o_ref):
    cv = cv_ref[...]
    a = (cv * _sigmoid(cv)).astype(BF16)
    o_ref[0] = _dot(a, w_ref[0].astype(BF16)) + b_ref[0]


def _mod_call(cv, w_mod, b_mod):
    tn = 2048
    return pl.pallas_call(
        _mod_kernel,
        grid=(DEPTH, 6 * D_MODEL // tn),
        in_specs=[_const_spec((8, D_MODEL)),
                  pl.BlockSpec((1, D_MODEL, tn), lambda l, j: (l, 0, j)),
                  pl.BlockSpec((1, 1, tn), lambda l, j: (l, 0, j))],
        out_specs=pl.BlockSpec((1, 8, tn), lambda l, j: (l, 0, j)),
        out_shape=jax.ShapeDtypeStruct((DEPTH, 8, 6 * D_MODEL), F32),
        compiler_params=_params(2),
        name="mod",
    )(cv, w_mod, b_mod.reshape(DEPTH, 1, 6 * D_MODEL))


_IN_SLOTS = ((_QA, 256), (_KA, 256), (_VA, 256), (_CQ, 256), (_CKV, 128), (_KR, 128),
             (_QC, 256), (_KC, 128), (_VC, 128), (_PD, 256))


def _inproj_kernel(*refs, latent):
    if latent:
        x_ref, mod_ref, g_ref, w_ref, c64_ref, s64_ref, cm_ref, sm_ref = refs[:8]
        outs = refs[8:]
    else:
        x_ref, mod_ref, g_ref, w_ref = refs[:4]
        outs = refs[4:]
    x = x_ref[...]
    h = _rms(x, g_ref[...]) * (1.0 + mod_ref[0, 1:2, :]) + mod_ref[0, 0:1, :]
    p = _dot(h.astype(BF16), w_ref[...])
    for (off, wd), o_ref in zip(_IN_SLOTS, outs):
        v = p[:, off:off + wd]
        if latent:
            if off == _QC:
                v = _rope(v, c64_ref[...], s64_ref[...], 16)
            elif off == _KC:
                v = _rope(v, c64_ref[:, :128], s64_ref[:, :128], 16)
            elif off == _KR:
                v = _rope(v, cm_ref[:, :128], sm_ref[:, :128], 8)
        o_ref[...] = v.astype(o_ref.dtype)


def _inproj_call(x2d, mod, w, layer, rope_tabs=None):
    t = x2d.shape[0]
    tm = 512
    latent = rope_tabs is not None
    dts = [BF16 if latent or off in (_QA, _CQ, _QC, _PD) else F32 for off, _ in _IN_SLOTS]
    tiles_per_mod = t // tm // mod.shape[0]
    in_specs = [pl.BlockSpec((tm, D_MODEL), lambda i: (i, 0)),
                pl.BlockSpec((1, 6, D_MODEL), lambda i: (i // tiles_per_mod, 0, 0)),
                _layer_spec(w["g_attn_pre"], layer), _layer_spec(w["w_in_p"], layer)]
    args = [x2d, mod, w["g_attn_pre"], w["w_in_p"]]
    if latent:
        tiles_per_seq = DEC_SEQ // tm
        c64, s64, cm, sm = rope_tabs
        in_specs += [pl.BlockSpec((tm, 256), lambda i: (i % tiles_per_seq, 0)),
                     pl.BlockSpec((tm, 256), lambda i: (i % tiles_per_seq, 0)),
                     pl.BlockSpec((tm, 512), lambda i: (i % tiles_per_seq, 0)),
                     pl.BlockSpec((tm, 512), lambda i: (i % tiles_per_seq, 0))]
        args += [c64, s64, cm, sm]
    return pl.pallas_call(
        functools.partial(_inproj_kernel, latent=latent),
        grid=(t // tm,),
        in_specs=in_specs,
        out_specs=[pl.BlockSpec((tm, wd), lambda i: (i, 0)) for _, wd in _IN_SLOTS],
        out_shape=[jax.ShapeDtypeStruct((t, wd), dt) for (_, wd), dt in zip(_IN_SLOTS, dts)],
        compiler_params=_params(1),
        name="inproj_lat" if latent else "inproj_ctx",
    )(*args)


def _ctx_attn_kernel(q_ref, k_ref, v_ref, o_ref, *, seqs):
    q = q_ref[...].astype(BF16)
    k = k_ref[...].astype(BF16)
    v = v_ref[...].astype(BF16)
    scores = []
    for s in range(seqs):
        rows = slice(s * SEQ, (s + 1) * SEQ)
        for h in range(NA_HEADS):
            sl = slice(h * HEAD_DIM, (h + 1) * HEAD_DIM)
            scores.append(_dot_nt(q[rows, sl], k[rows, sl]))
    probs = [_softmax_blocks([sc]) for sc in scores]
    outs = []
    for s in range(seqs):
        rows = slice(s * SEQ, (s + 1) * SEQ)
        ys = []
        for h in range(NA_HEADS):
            sl = slice(h * HEAD_DIM, (h + 1) * HEAD_DIM)
            (e,), l = probs[s * NA_HEADS + h]
            ys.append(_dot(e.astype(BF16), v[rows, sl]) / l)
        outs.append(jnp.concatenate(ys, -1))
    o_ref[...] = (outs[0] if seqs == 1 else jnp.concatenate(outs, 0)).astype(o_ref.dtype)


def _ctx_attn_call(q, k, v):
    seqs = CTX_SEQS
    t = q.shape[0]
    spec = pl.BlockSpec((seqs * SEQ, 256), lambda b: (b, 0))
    return pl.pallas_call(
        functools.partial(_ctx_attn_kernel, seqs=seqs), grid=(t // SEQ // seqs,),
        in_specs=[spec, spec, spec], out_specs=spec,
        out_shape=jax.ShapeDtypeStruct((t, 256), BF16), compiler_params=_params(1), name="ctx_attn",
    )(q, k, v)


def _gqa_operands(q, k, rows_q, rows_k):
    group = SWA_HEADS // SWA_KV_HEADS
    out = []
    for kv in range(SWA_KV_HEADS):
        qs = jnp.concatenate([q[rows_q, (kv * group + g) * HEAD_DIM:(kv * group + g + 1) * HEAD_DIM]
                              for g in range(group)], 0)
        out.append((qs, k[rows_k, kv * HEAD_DIM:(kv + 1) * HEAD_DIM]))
    return out


def _gqa_sink(sink_ref, kv, m):
    group = SWA_HEADS // SWA_KV_HEADS
    row = lax.broadcasted_iota(jnp.int32, (group * m, 1), 0)
    col = jnp.full((group * m, 1), sink_ref[kv * group + group - 1], F32)
    for g in range(group - 2, -1, -1):
        col = jnp.where(row < (g + 1) * m, sink_ref[kv * group + g], col)
    return col


def _ctx_swa_kernel(sink_ref, q_ref, k_ref, v_ref, o_ref, *, seqs):
    q = q_ref[...].astype(BF16)
    k = k_ref[...].astype(BF16)
    v = v_ref[...].astype(BF16)
    group = SWA_HEADS // SWA_KV_HEADS
    scores = []
    for s in range(seqs):
        rows = slice(s * SEQ, (s + 1) * SEQ)
        for qs, ks in _gqa_operands(q, k, rows, rows):
            scores.append(_dot_nt(qs, ks))
    probs = [_softmax_blocks([sc], sink=_gqa_sink(sink_ref, i % SWA_KV_HEADS, SEQ)) for i, sc in enumerate(scores)]
    outs = []
    for s in range(seqs):
        rows = slice(s * SEQ, (s + 1) * SEQ)
        ys = []
        for kv in range(SWA_KV_HEADS):
            (e,), l = probs[s * SWA_KV_HEADS + kv]
            y = _dot(e.astype(BF16), v[rows, kv * HEAD_DIM:(kv + 1) * HEAD_DIM]) / l
            ys += [y[g * SEQ:(g + 1) * SEQ] for g in range(group)]
        outs.append(jnp.concatenate(ys, -1))
    o_ref[...] = (outs[0] if seqs == 1 else jnp.concatenate(outs, 0)).astype(o_ref.dtype)


def _ctx_swa_call(sink, q, k, v):
    seqs = CTX_SEQS
    t = q.shape[0]
    rows = seqs * SEQ
    return pl.pallas_call(
        functools.partial(_ctx_swa_kernel, seqs=seqs), grid=(t // rows,),
        in_specs=[pl.BlockSpec(memory_space=pltpu.SMEM),
                  pl.BlockSpec((rows, 256), lambda b: (b, 0)),
                  pl.BlockSpec((rows, 128), lambda b: (b, 0)),
                  pl.BlockSpec((rows, 128), lambda b: (b, 0))],
        out_specs=pl.BlockSpec((rows, 256), lambda b: (b, 0)),
        out_shape=jax.ShapeDtypeStruct((t, 256), BF16), compiler_params=_params(1), name="ctx_swa",
    )(sink, q, k, v)


def _mla_q(cq, qn_ref, wuq_ref):
    return _dot(_rms(cq, qn_ref[...], MLA_Q_RANK).astype(BF16), wuq_ref[...])


def _mla_kv(ckv, kr, kvn_ref, wuk_ref, wuv_ref):
    cn = _rms(ckv, kvn_ref[...]).astype(BF16)
    kcat = _dot(cn, wuk_ref[...]) + jnp.concatenate([kr] * MLA_HEADS, -1)
    return kcat, _dot(cn, wuv_ref[...])


def _ctx_mla_kernel(cq_ref, ckv_ref, kr_ref, qn_ref, wuq_ref, kvn_ref, wuk_ref, wuv_ref, o_ref, *, seqs):
    q = (_mla_q(cq_ref[...].astype(F32), qn_ref, wuq_ref) * MLA_SCALE).astype(BF16)
    kcat, v = _mla_kv(ckv_ref[...], kr_ref[...], kvn_ref, wuk_ref, wuv_ref)
    kcat = kcat.astype(BF16)
    v = v.astype(BF16)
    scores = []
    for s in range(seqs):
        rows = slice(s * SEQ, (s + 1) * SEQ)
        for h in range(MLA_HEADS):
            sl = slice(h * LANE, (h + 1) * LANE)
            scores.append(_dot_nt(q[rows, sl], kcat[rows, sl]))
    probs = [_softmax_blocks([sc]) for sc in scores]
    outs = []
    for s in range(seqs):
        rows = slice(s * SEQ, (s + 1) * SEQ)
        ys = []
        for h in range(MLA_HEADS):
            (e,), l = probs[s * MLA_HEADS + h]
            ys.append(_dot(e.astype(BF16), v[rows, h * MLA_V:(h + 1) * MLA_V]) / l)
        outs.append(jnp.concatenate(ys, -1))
    o_ref[...] = (outs[0] if seqs == 1 else jnp.concatenate(outs, 0)).astype(o_ref.dtype)


_MLA_W = ("mla_qn", "mla_wuq", "mla_kvn", "mla_wuk", "mla_wuv")


def _ctx_mla_call(cq, ckv, kr, w, layer):
    seqs = CTX_SEQS
    t = cq.shape[0]
    rows = seqs * SEQ
    return pl.pallas_call(
        functools.partial(_ctx_mla_kernel, seqs=seqs), grid=(t // rows,),
        in_specs=[pl.BlockSpec((rows, 256), lambda b: (b, 0)),
                  pl.BlockSpec((rows, 128), lambda b: (b, 0)),
                  pl.BlockSpec((rows, 128), lambda b: (b, 0))] + [_layer_spec(w[k], layer) for k in _MLA_W],
        out_specs=pl.BlockSpec((rows, 256), lambda b: (b, 0)),
        out_shape=jax.ShapeDtypeStruct((t, 256), BF16), compiler_params=_params(1), name="ctx_mla",
    )(cq, ckv, kr, *[w[k] for k in _MLA_W])


_POOL_PAD = 8


def _pool_kernel(x_ref, w_ref, sc_ref, o_ref, *, n):
    ne = n + 2 * _POOL_PAD
    lo, hi = _POOL_PAD, _POOL_PAD + n
    z = jnp.zeros((_POOL_PAD, POOL_WIDTH), F32)
    grp = lax.broadcasted_iota(jnp.int32, (n, POOL_WIDTH), 1) >> 6
    t = lax.broadcasted_iota(jnp.int32, (n, POOL_WIDTH), 0)
    half = jnp.where(grp == 0, 1, jnp.where(grp == 1, 2, jnp.where(grp == 2, 4, 8)))
    cnt = (jnp.minimum(t + half, n) - jnp.maximum(t - half, 0)).astype(F32)

    def pair(a, s):
        return pltpu.roll(a, s, axis=0) + pltpu.roll(a, ne - s, axis=0)

    for s in range(x_ref.shape[0] // n):
        x = x_ref[s * n:(s + 1) * n, :].astype(F32)
        xz = jnp.concatenate([z, x, z], 0)
        s2 = xz + pltpu.roll(xz, 1, axis=0)
        s4 = pair(s2, 1)
        s8 = pair(s4, 2)
        s16 = pair(s8, 4)
        tot = jnp.where(grp == 0, s2[lo:hi],
                        jnp.where(grp == 1, s4[lo:hi], jnp.where(grp == 2, s8[lo:hi], s16[lo:hi])))
        dlt = (tot / cnt - x).astype(BF16)
        o_ref[s * n:(s + 1) * n, :] = (_dot(dlt, w_ref[...]) * sc_ref[...]).astype(o_ref.dtype)


def _pool_call(pd, w, layer, seq, seqs):
    t = pd.shape[0]
    rows = seq * seqs
    return pl.pallas_call(
        functools.partial(_pool_kernel, n=seq), grid=(t // rows,),
        in_specs=[pl.BlockSpec((rows, POOL_WIDTH), lambda b: (b, 0)),
                  _layer_spec(w["pool_w"], layer), _layer_spec(w["pool_scale"], layer)],
        out_specs=pl.BlockSpec((rows, POOL_WIDTH), lambda b: (b, 0)),
        out_shape=jax.ShapeDtypeStruct((t, POOL_WIDTH), BF16), compiler_params=_params(1), name="pool",
    )(pd, w["pool_w"], w["pool_scale"])


def _lat_na_kernel(q_ref, k_ref, v_ref, kc_ref, vc_ref, t2_ref, o_ref):
    n = pl.program_id(1)
    rows = DEC_SEQ // GRID_W
    q_rows = NA_Q_BLOCK // GRID_W
    row0 = jnp.clip(q_rows * n - NA_WIN_R // 2, 0, rows - NA_SPAN // GRID_W)
    start = pl.multiple_of(row0 * GRID_W, LANE)
    q = q_ref[...]
    k = k_ref[pl.ds(start, NA_SPAN), :]
    v = v_ref[pl.ds(start, NA_SPAN), :]
    kc = kc_ref[...].astype(BF16)
    vc = vc_ref[...].astype(BF16)

    pairs = NA_SPAN // LANE
    low_half = lax.broadcasted_iota(jnp.int32, (GRID_W, LANE), 1) < GRID_W
    entries, masks = [], []
    for a in range(q_rows):
        r = q_rows * n + a
        r_start = jnp.clip(r - NA_WIN_R // 2, 0, rows - NA_WIN_R)
        for p in range(pairs):
            rk = row0 + 2 * p
            ok0 = ((rk >= r_start) & (rk < r_start + NA_WIN_R)).astype(jnp.int32)
            ok1 = ((rk + 1 >= r_start) & (rk + 1 < r_start + NA_WIN_R)).astype(jnp.int32)
            entries.append(jnp.clip(rk - r + NA_WIN_R, 0, NA_DR))
            masks.append(jnp.where(low_half, ok0, ok1) > 0)

    heads = [slice(h * HEAD_DIM, (h + 1) * HEAD_DIM) for h in range(NA_HEADS)]
    raw = [(_dot_nt(q[:, sl], k[:, sl]), _dot_nt(q[:, sl], kc[:, sl])) for sl in heads]
    probs = []
    for h, (s_loc, s_ctx) in enumerate(raw):
        cols = []
        for p in range(pairs):
            blk = [jnp.where(masks[a * pairs + p], t2_ref[h, entries[a * pairs + p]], NEG_INF)
                   for a in range(q_rows)]
            cols.append(jnp.concatenate(blk, 0))
        probs.append(_softmax_blocks([s_loc + jnp.concatenate(cols, 1), s_ctx]))
    ys = []
    for sl, ((e_loc, e_ctx), l) in zip(heads, probs):
        y = _dot(e_loc.astype(BF16), v[:, sl]) + _dot(e_ctx.astype(BF16), vc[:, sl])
        ys.append(y / l)
    o_ref[...] = jnp.concatenate(ys, -1).astype(o_ref.dtype)


def _lat_na_call(q, k, v, cache_k, cache_v, t2, layer):
    nq = DEC_SEQ // NA_Q_BLOCK
    seq_spec = pl.BlockSpec((DEC_SEQ, 256), lambda b, n: (b, 0))
    cache_spec = pl.BlockSpec((None, None, PAST_LEN, 256), lambda b, n: (b, layer, 0, 0))
    return pl.pallas_call(
        _lat_na_kernel, grid=(DEC_BATCH, nq),
        in_specs=[pl.BlockSpec((NA_Q_BLOCK, 256), lambda b, n: (b * nq + n, 0)), seq_spec, seq_spec,
                  cache_spec, cache_spec, _layer_spec(t2, layer)],
        out_specs=pl.BlockSpec((NA_Q_BLOCK, 256), lambda b, n: (b * nq + n, 0)),
        out_shape=jax.ShapeDtypeStruct((DEC_BATCH * DEC_SEQ, 256), BF16),
        compiler_params=_params(2), name="lat_na",
    )(q, k, v, cache_k, cache_v, t2)


def _lat_swa_kernel(sink_ref, q_ref, k_ref, v_ref, kc_ref, vc_ref, o_ref):
    n = pl.program_id(1)
    start = pl.multiple_of(jnp.clip(n - 1, 0, DEC_SEQ // Q_BLOCK - 3) * Q_BLOCK, LANE)
    q = q_ref[...]
    k = k_ref[pl.ds(start, SWA_SPAN), :]
    v = v_ref[pl.ds(start, SWA_SPAN), :]
    kc = kc_ref[...].astype(BF16)
    vc = vc_ref[...].astype(BF16)
    group = SWA_HEADS // SWA_KV_HEADS
    m = group * Q_BLOCK
    q_pos = n * Q_BLOCK + (lax.broadcasted_iota(jnp.int32, (m, SWA_SPAN), 0) & (Q_BLOCK - 1))
    k_pos = start + lax.broadcasted_iota(jnp.int32, (m, SWA_SPAN), 1)
    valid = jnp.abs(q_pos - k_pos) <= SWA_WINDOW
    everything = slice(None)
    raw = []
    for (qs, ks), (_, kcs) in zip(_gqa_operands(q, k, everything, everything),
                                  _gqa_operands(q, kc, everything, everything)):
        raw.append((_dot_nt(qs, ks), _dot_nt(qs, kcs)))
    probs = [_softmax_blocks([jnp.where(valid, s_loc, NEG_INF), s_ctx], sink=_gqa_sink(sink_ref, kv, Q_BLOCK))
             for kv, (s_loc, s_ctx) in enumerate(raw)]
    ys = []
    for kv, ((e_loc, e_ctx), l) in enumerate(probs):
        kvsl = slice(kv * HEAD_DIM, (kv + 1) * HEAD_DIM)
        y = (_dot(e_loc.astype(BF16), v[:, kvsl]) + _dot(e_ctx.astype(BF16), vc[:, kvsl])) / l
        ys += [y[g * Q_BLOCK:(g + 1) * Q_BLOCK] for g in range(group)]
    o_ref[...] = jnp.concatenate(ys, -1).astype(o_ref.dtype)


def _lat_swa_call(sink, q, k, v, cache_k, cache_v, layer):
    nq = DEC_SEQ // Q_BLOCK
    seq_spec = pl.BlockSpec((DEC_SEQ, 128), lambda b, n: (b, 0))
    cache_spec = pl.BlockSpec((None, None, PAST_LEN, 128), lambda b, n: (b, layer, 0, 0))
    return pl.pallas_call(
        _lat_swa_kernel, grid=(DEC_BATCH, nq),
        in_specs=[pl.BlockSpec(memory_space=pltpu.SMEM),
                  pl.BlockSpec((Q_BLOCK, 256), lambda b, n: (b * nq + n, 0)), seq_spec, seq_spec,
                  cache_spec, cache_spec],
        out_specs=pl.BlockSpec((Q_BLOCK, 256), lambda b, n: (b * nq + n, 0)),
        out_shape=jax.ShapeDtypeStruct((DEC_BATCH * DEC_SEQ, 256), BF16),
        compiler_params=_params(2), name="lat_swa",
    )(sink, q, k, v, cache_k, cache_v)


def _lat_mla_kernel(cq_ref, ckv_ref, kr_ref, cckv_ref, ckr_ref, cm_ref, sm_ref,
                    qn_ref, wuq_ref, kvn_ref, wuk_ref, wuv_ref, o_ref, kcat_s, v_s):
    @pl.when(pl.program_id(1) == 0)
    def _():
        kc, vc = _mla_kv(cckv_ref[...], ckr_ref[...], kvn_ref, wuk_ref, wuv_ref)
        kcat_s[0:PAST_LEN, :] = kc.astype(BF16)
        v_s[0:PAST_LEN, :] = vc.astype(BF16)
        kl, vl = _mla_kv(ckv_ref[...].astype(F32), kr_ref[...].astype(F32), kvn_ref, wuk_ref, wuv_ref)
        kcat_s[PAST_LEN:, :] = kl.astype(BF16)
        v_s[PAST_LEN:, :] = vl.astype(BF16)

    q = _mla_q(cq_ref[...].astype(F32), qn_ref, wuq_ref)
    q = (_rope(q, cm_ref[...], sm_ref[...], 8) * MLA_SCALE).astype(BF16)
    scores = [_dot_nt(q[:, h * LANE:(h + 1) * LANE], kcat_s[:, h * LANE:(h + 1) * LANE]) for h in range(MLA_HEADS)]
    probs = [_softmax_blocks([sc]) for sc in scores]
    ys = [_dot(e.astype(BF16), v_s[:, h * MLA_V:(h + 1) * MLA_V]) / l for h, ((e,), l) in enumerate(probs)]
    o_ref[...] = jnp.concatenate(ys, -1).astype(o_ref.dtype)


def _lat_mla_call(cq, ckv, kr, cache_ckv, cache_kr, cm, sm, w, layer):
    qb = MLA_Q_BLOCK
    nq = DEC_SEQ // qb
    seq_spec = pl.BlockSpec((DEC_SEQ, 128), lambda b, n: (b, 0))
    cache_spec = pl.BlockSpec((None, None, PAST_LEN, 128), lambda b, n: (b, layer, 0, 0))
    tab_spec = pl.BlockSpec((qb, 512), lambda b, n: (n, 0))
    return pl.pallas_call(
        _lat_mla_kernel, grid=(DEC_BATCH, nq),
        in_specs=[pl.BlockSpec((qb, 256), lambda b, n: (b * nq + n, 0)), seq_spec, seq_spec,
                  cache_spec, cache_spec, tab_spec, tab_spec] + [_layer_spec(w[k], layer) for k in _MLA_W],
        out_specs=pl.BlockSpec((qb, 256), lambda b, n: (b * nq + n, 0)),
        out_shape=jax.ShapeDtypeStruct((DEC_BATCH * DEC_SEQ, 256), BF16),
        scratch_shapes=[pltpu.VMEM((PAST_LEN + DEC_SEQ, MLA_HEADS * LANE), BF16),
                        pltpu.VMEM((PAST_LEN + DEC_SEQ, MLA_HEADS * MLA_V), BF16)],
        compiler_params=_params(2), name="lat_mla",
    )(cq, ckv, kr, cache_ckv, cache_kr, cm, sm, *[w[k] for k in _MLA_W])


_MERGE_PARTS = 2


def _merge_kernel(x_ref, mod_ref, gpre_ref, ya_ref, yb_ref, yc_ref, yd_ref,
                  wg_ref, bg_ref, wb_ref, wo_ref, gpost_ref, o_ref):
    tm = x_ref.shape[0] // _MERGE_PARTS
    for p in range(_MERGE_PARTS):
        rows = slice(p * tm, (p + 1) * tm)
        x = x_ref[rows, :]
        h = (_rms(x, gpre_ref[...]) * (1.0 + mod_ref[0, 1:2, :]) + mod_ref[0, 0:1, :]).astype(BF16)
        merged = None
        for k, y_ref in enumerate((ya_ref, yb_ref, yc_ref, yd_ref)):
            cols = slice(k * D_MODEL, (k + 1) * D_MODEL)
            gate = _sigmoid(_dot(h, wg_ref[:, cols]) + bg_ref[:, cols])
            term = gate * _dot(y_ref[rows, :], wb_ref[k])
            merged = term if merged is None else merged + term
        o = _dot(merged.astype(BF16), wo_ref[...])
        o_ref[rows, :] = x + mod_ref[0, 2:3, :] * _rms(o, gpost_ref[...])


_MERGE_W = ("w_gate", "b_gate", "w_branch", "w_out", "g_attn_post")


def _merge_call(x2d, mod, ys, w, layer):
    t = x2d.shape[0]
    tm = 512
    tiles_per_mod = t // tm // mod.shape[0]
    tile = pl.BlockSpec((tm, D_MODEL), lambda i: (i, 0))
    ytile = pl.BlockSpec((tm, BRANCH_W), lambda i: (i, 0))
    return pl.pallas_call(
        _merge_kernel, grid=(t // tm,),
        in_specs=[tile, pl.BlockSpec((1, 6, D_MODEL), lambda i: (i // tiles_per_mod, 0, 0)),
                  _layer_spec(w["g_attn_pre"], layer), ytile, ytile, ytile, ytile]
                 + [_layer_spec(w[k], layer) for k in _MERGE_W],
        out_specs=tile,
        out_shape=jax.ShapeDtypeStruct((t, D_MODEL), F32),
        compiler_params=_params(1), name="merge",
    )(x2d, mod, w["g_attn_pre"], *ys, *[w[k] for k in _MERGE_W])


_GAP = 8


def _ffn_kernel(*refs, seqs, halo):
    if halo:
        x_ref, xp_ref, xn_ref = refs[:3]
        refs = refs[3:]
    else:
        x_ref = refs[0]
        refs = refs[1:]
    mod_ref, gpre_ref, wa_ref, wg_ref, ca_ref, cg_ref, wd_ref, gpost_ref, o_ref = refs
    x = x_ref[...]
    tm = x.shape[0]
    shift, scale, gate = mod_ref[0, 3:4, :], mod_ref[0, 4:5, :], mod_ref[0, 5:6, :]

    def pre(xx):
        return _rms(xx, gpre_ref[...]) * (1.0 + scale) + shift

    h = pre(x)
    if halo:
        i = pl.program_id(0) % halo
        hp = jnp.where(i == 0, 0.0, pre(xp_ref[...]))
        hn = jnp.where(i == halo - 1, 0.0, pre(xn_ref[...]))
        pieces = [hp, h, hn]
        starts = [_GAP]
        seq_len = tm
    else:
        seq_len = tm // seqs
        z = jnp.zeros((_GAP, D_MODEL), F32)
        pieces, starts = [], []
        for s in range(seqs):
            starts.append(s * (seq_len + _GAP))
            pieces += [h[s * seq_len:(s + 1) * seq_len], z]
    hb = jnp.concatenate(pieces, 0).astype(BF16)
    rows = hb.shape[0]

    def up(c):
        cols = slice(c * FF_CHUNK, (c + 1) * FF_CHUNK)
        return _dot(hb, wa_ref[:, cols]), _dot(hb, wg_ref[:, cols])

    def conv(u, c_ref, cols):
        prev = pltpu.roll(u, 1, axis=0)
        nxt = pltpu.roll(u, rows - 1, axis=0)
        return prev * c_ref[0:1, cols] + u * c_ref[1:2, cols] + nxt * c_ref[2:3, cols]

    n_chunks = D_FF_PAD // FF_CHUNK
    pending = [up(c) for c in range(FFN_LOOKAHEAD)]
    acts = []
    for c in range(n_chunks):
        if c + FFN_LOOKAHEAD < n_chunks:
            pending.append(up(c + FFN_LOOKAHEAD))
        ua, ug = pending.pop(0)
        cols = slice(c * FF_CHUNK, (c + 1) * FF_CHUNK)
        a = conv(ua, ca_ref, cols)
        g = conv(ug, cg_ref, cols)
        acts.append((g * _sigmoid(g) * a).astype(BF16))
    acc = _dot(jnp.concatenate(acts, 1), wd_ref[...])
    for s, st in enumerate(starts):
        ys = _rms(acc[st:st + seq_len], gpost_ref[...])
        o_ref[s * seq_len:(s + 1) * seq_len, :] = x[s * seq_len:(s + 1) * seq_len] + gate * ys


_FFN_W = ("g_ffn_pre", "ffn_wa", "ffn_wg", "ffn_ca", "ffn_cg", "ffn_wd", "g_ffn_post")


def _ffn_call(x2d, mod, w, layer, seq, tm):
    t = x2d.shape[0]
    tile = pl.BlockSpec((tm, D_MODEL), lambda i: (i, 0))
    in_specs = [tile]
    args = [x2d]
    if tm < seq:
        halo, seqs = seq // tm, 1
        r = tm // _GAP
        last = t // _GAP - 1
        in_specs += [pl.BlockSpec((_GAP, D_MODEL), lambda i: (jnp.maximum(i * r - 1, 0), 0)),
                     pl.BlockSpec((_GAP, D_MODEL), lambda i: (jnp.minimum((i + 1) * r, last), 0))]
        args += [x2d, x2d]
    else:
        halo, seqs = 0, tm // seq
    tiles_per_mod = t // tm // mod.shape[0]
    in_specs += [pl.BlockSpec((1, 6, D_MODEL), lambda i: (i // tiles_per_mod, 0, 0))]
    in_specs += [_layer_spec(w[k], layer) for k in _FFN_W]
    args += [mod] + [w[k] for k in _FFN_W]
    return pl.pallas_call(
        functools.partial(_ffn_kernel, seqs=seqs, halo=halo), grid=(t // tm,),
        in_specs=in_specs, out_specs=tile,
        out_shape=jax.ShapeDtypeStruct((t, D_MODEL), F32),
        compiler_params=_params(1), name="ffn",
    )(*args)


def _rope_tables():
    t = np.arange(DEC_SEQ)
    pos = (t // GRID_W, t % GRID_W)

    def tab(d):
        half = d // 4
        inv = np.float32(ROPE_BASE) ** (-np.arange(half, dtype=np.float32) / np.float32(half))
        cs, sn = [], []
        for p in pos:
            ang = p.astype(np.float32)[:, None] * inv[None, :]
            cs += [np.cos(ang), np.cos(ang)]
            sn += [-np.sin(ang), np.sin(ang)]
        return np.concatenate(cs, -1), np.concatenate(sn, -1)

    c64, s64 = tab(HEAD_DIM)
    c32, s32 = tab(MLA_ROPE)
    pad = LANE - MLA_NOPE - MLA_ROPE
    cm = np.concatenate([np.ones((DEC_SEQ, MLA_NOPE), np.float32), c32, np.ones((DEC_SEQ, pad), np.float32)], -1)
    sm = np.concatenate([np.zeros((DEC_SEQ, MLA_NOPE), np.float32), s32, np.zeros((DEC_SEQ, pad), np.float32)], -1)
    return tuple(jnp.asarray(np.tile(x, (1, 4)), F32) for x in (c64, s64, cm, sm))


def _na_table_kernel(rpb_ref, e_ref, ok_ref, o_ref):
    r = rpb_ref[...]
    r1 = r.astype(BF16)
    r2 = (r - r1.astype(F32)).astype(BF16)
    r3 = (r - r1.astype(F32) - r2.astype(F32)).astype(BF16)
    e = e_ref[...]
    t = _dot(r1, e) + _dot(r2, e) + _dot(r3, e)
    o_ref[...] = jnp.where(ok_ref[...] > 0, t, NEG_INF)


def _na_bias_tables(na_rpb):
    c = np.arange(GRID_W)[:, None]
    w = np.arange(GRID_W)[None, :]
    dc = (w - c + NA_WIN_C - 1).reshape(-1)
    onehot = (np.arange(LANE)[:, None] == dc[None, :]).astype(np.float32)
    c_start = np.clip(c - NA_WIN_C // 2, 0, GRID_W - NA_WIN_C)
    ok = ((w >= c_start) & (w < c_start + NA_WIN_C)).reshape(1, -1).astype(np.int32)
    rows = DEPTH * NA_HEADS * NA_DR
    rpb2 = jnp.pad(na_rpb.reshape(rows, NA_DC), ((0, LANE - rows), (0, LANE - NA_DC)))
    t = pl.pallas_call(
        _na_table_kernel, out_shape=jax.ShapeDtypeStruct((LANE, GRID_W * GRID_W), F32), name="na_table",
        compiler_params=pltpu.CompilerParams(vmem_limit_bytes=VMEM_LIMIT),
    )(rpb2, jnp.asarray(onehot, BF16), jnp.asarray(ok))
    t = t[:rows].reshape(DEPTH, NA_HEADS, NA_DR, GRID_W, GRID_W)
    t = jnp.pad(t, ((0, 0), (0, 0), (1, 1), (0, 0), (0, 0)), constant_values=NEG_INF)
    return jnp.concatenate([t[:, :, :-1], t[:, :, 1:]], -1)


def _pad_last(w, n):
    return jnp.pad(w, ((0, 0),) * (w.ndim - 1) + ((0, n - w.shape[-1]),))


def _prep_weights(g_attn_pre, g_attn_post, g_ffn_pre, g_ffn_post, w_in, w_gate, b_gate, mla_q_norm, mla_w_uq,
                  mla_kv_norm, mla_w_ukv, pool_w, pool_scale, w_branch, w_out, ffn_w_up, ffn_conv, ffn_w_down):
    q_scale = jnp.concatenate([jnp.full((256,), ATT_SCALE, F32), jnp.ones((512,), F32)])
    b0, c0, d0 = 768, 1120, 1632
    kr = jnp.pad(w_in[:, :, b0 + 320:b0 + 352], ((0, 0), (0, 0), (_KR_LANE, LANE - _KR_LANE - MLA_ROPE)))
    w_in_p = jnp.concatenate([w_in[:, :, :b0] * q_scale, _pad_last(w_in[:, :, b0:b0 + MLA_Q_RANK], 256),
                              w_in[:, :, b0 + MLA_Q_RANK:b0 + MLA_Q_RANK + MLA_KV_RANK], kr,
                              w_in[:, :, c0:d0] * q_scale[:512], w_in[:, :, d0:]], 2).astype(BF16)

    wuq = mla_w_uq.reshape(DEPTH, MLA_Q_RANK, MLA_HEADS, MLA_NOPE + MLA_ROPE)
    wuq = jnp.pad(wuq, ((0, 0), (0, 256 - MLA_Q_RANK), (0, 0), (0, LANE - MLA_NOPE - MLA_ROPE)))
    wukv = mla_w_ukv.reshape(DEPTH, MLA_KV_RANK, MLA_HEADS, MLA_NOPE + MLA_V)
    wuk = _pad_last(wukv[..., :MLA_NOPE], LANE)

    eye = np.eye(len(POOL_WINDOWS), dtype=np.float32)
    w_bd = (pool_w[:, :, :, None, :] * eye[None, :, None, :, None]).reshape(DEPTH, POOL_WIDTH, POOL_WIDTH)

    return dict(
        g_attn_pre=g_attn_pre[:, None, :], g_attn_post=g_attn_post[:, None, :],
        g_ffn_pre=g_ffn_pre[:, None, :], g_ffn_post=g_ffn_post[:, None, :],
        w_in_p=w_in_p,
        mla_qn=_pad_last(mla_q_norm[:, None, :], 256),
        mla_wuq=wuq.reshape(DEPTH, 256, MLA_HEADS * LANE).astype(BF16),
        mla_kvn=mla_kv_norm[:, None, :],
        mla_wuk=wuk.reshape(DEPTH, MLA_KV_RANK, MLA_HEADS * LANE).astype(BF16),
        mla_wuv=wukv[..., MLA_NOPE:].reshape(DEPTH, MLA_KV_RANK, MLA_HEADS * MLA_V).astype(BF16),
        pool_w=w_bd.astype(BF16), pool_scale=pool_scale[:, None, :],
        w_gate=w_gate.astype(BF16), b_gate=b_gate[:, None, :],
        w_branch=w_branch.astype(BF16), w_out=w_out.astype(BF16),
        ffn_wa=_pad_last(ffn_w_up[:, :, :D_FF], D_FF_PAD).astype(BF16),
        ffn_wg=_pad_last(ffn_w_up[:, :, D_FF:], D_FF_PAD).astype(BF16),
        ffn_ca=_pad_last(ffn_conv[:, :, :D_FF], D_FF_PAD), ffn_cg=_pad_last(ffn_conv[:, :, D_FF:], D_FF_PAD),
        ffn_wd=jnp.pad(ffn_w_down, ((0, 0), (0, D_FF_PAD - D_FF), (0, 0))).astype(BF16))


def kernel(x_prompt, x_sample, cache_na_k, cache_na_v, cache_mla_ckv, cache_mla_krope, cache_swa_k, cache_swa_v, c, c_ctx, w_mod, b_mod, g_attn_pre, g_attn_post, g_ffn_pre, g_ffn_post, w_in, w_gate, b_gate, na_rpb, mla_q_norm, mla_w_uq, mla_kv_norm, mla_w_ukv, swa_sink, pool_w, pool_scale, w_branch, w_out, ffn_w_up, ffn_conv, ffn_w_down):
    x_p = x_prompt.reshape(BATCH * SEQ, D_MODEL)
    x_s = x_sample.reshape(DEC_BATCH * DEC_SEQ, D_MODEL)

    cv = jnp.concatenate([c_ctx[None, :], c, jnp.zeros((8 - 1 - DEC_BATCH, D_MODEL), F32)], 0)
    mod = _mod_call(cv, w_mod, b_mod).reshape(DEPTH, 8, 6, D_MODEL)
    w = _prep_weights(g_attn_pre, g_attn_post, g_ffn_pre, g_ffn_post, w_in, w_gate, b_gate, mla_q_norm, mla_w_uq,
                      mla_kv_norm, mla_w_ukv, pool_w, pool_scale, w_branch, w_out, ffn_w_up, ffn_conv, ffn_w_down)
    rope_tabs = _rope_tables()
    na_t2 = _na_bias_tables(na_rpb)
    cache_na_k = cache_na_k.reshape(DEC_BATCH, DEPTH, PAST_LEN, 256)
    cache_na_v = cache_na_v.reshape(DEC_BATCH, DEPTH, PAST_LEN, 256)
    cache_swa_k = cache_swa_k.reshape(DEC_BATCH, DEPTH, PAST_LEN, 128)
    cache_swa_v = cache_swa_v.reshape(DEC_BATCH, DEPTH, PAST_LEN, 128)
    cache_kr = jnp.pad(cache_mla_krope, ((0, 0), (0, 0), (0, 0), (_KR_LANE, LANE - _KR_LANE - MLA_ROPE)))

    states = []
    for l in range(DEPTH):
        mod_p = mod[l, 0:1]
        qa, ka, va, cq, ckv, kr, qc, kc, vc, pd = _inproj_call(x_p, mod_p, w, l)
        ys = (_ctx_attn_call(qa, ka, va), _ctx_mla_call(cq, ckv, kr, w, l),
              _ctx_swa_call(swa_sink[l], qc, kc, vc), _pool_call(pd, w, l, SEQ, CTX_SEQS))
        x_p = _merge_call(x_p, mod_p, ys, w, l)
        x_p = _ffn_call(x_p, mod_p, w, l, SEQ, SEQ)
        states.append((ka, va, ckv, kr[:, _KR_LANE:_KR_LANE + MLA_ROPE], kc, vc))

        mod_s = mod[l, 1:1 + DEC_BATCH]
        qa, ka, va, cq, ckv, kr, qc, kc, vc, pd = _inproj_call(x_s, mod_s, w, l, rope_tabs)
        ys = (_lat_na_call(qa, ka, va, cache_na_k, cache_na_v, na_t2, l),
              _lat_mla_call(cq, ckv, kr, cache_mla_ckv, cache_kr, rope_tabs[2], rope_tabs[3], w, l),
              _lat_swa_call(swa_sink[l], qc, kc, vc, cache_swa_k, cache_swa_v, l),
              _pool_call(pd, w, l, DEC_SEQ, 1))
        x_s = _merge_call(x_s, mod_s, ys, w, l)
        x_s = _ffn_call(x_s, mod_s, w, l, DEC_SEQ, 512)

    def stack(j, *tail):
        return jnp.stack([st[j].reshape(BATCH, SEQ, *tail) for st in states], 1)

    return (x_p.reshape(BATCH, SEQ, D_MODEL), x_s.reshape(DEC_BATCH, DEC_SEQ, D_MODEL),
            stack(0, NA_HEADS, HEAD_DIM), stack(1, NA_HEADS, HEAD_DIM), stack(2, MLA_KV_RANK), stack(3, MLA_ROPE),
            stack(4, SWA_KV_HEADS, HEAD_DIM), stack(5, SWA_KV_HEADS, HEAD_DIM))
```

```python
import functools

import jax
import jax.numpy as jnp
import numpy as np
from jax import lax
from jax.experimental import pallas as pl
from jax.experimental.pallas import tpu as pltpu

F32 = jnp.float32
BF16 = jnp.bfloat16

D_MODEL = 1024
BATCH = 32
SEQ = 256
DEPTH = 2
DEC_BATCH = 2
DEC_SEQ = 2048
PAST_LEN = 256
GRID_W = 64
HEAD_DIM = 64
NA_HEADS = 4
NA_WIN_R = 8
NA_WIN_C = 16
MLA_HEADS = 4
MLA_NOPE = 64
MLA_ROPE = 32
MLA_V = 64
MLA_Q_RANK = 192
MLA_KV_RANK = 128
SWA_HEADS = 4
SWA_KV_HEADS = 2
SWA_WINDOW = 128
POOL_WINDOWS = (2, 4, 8, 16)
POOL_GROUP = 64
POOL_WIDTH = 256
BRANCH_W = 256
N_BRANCH = 4
D_FF = 2752
ROPE_BASE = 10000.0
EPS = 1e-6
NEG_INF = -1e30
ATT_SCALE = HEAD_DIM ** -0.5
MLA_SCALE = (MLA_NOPE + MLA_ROPE) ** -0.5

LANE = 128
D_FF_PAD = 2816
FF_CHUNK = 256
IN_PAD = 2048
Q_BLOCK = 128
NA_Q_BLOCK = 256
NA_SPAN = 768
NA_DR = 2 * NA_WIN_R - 1
NA_DC = 2 * NA_WIN_C - 1
SWA_SPAN = 384
MLA_Q_BLOCK = 256
FFN_LOOKAHEAD = 2
CTX_SEQS = 4
VMEM_LIMIT = 56 * 1024 * 1024

_QA, _KA, _VA, _CQ, _CKV, _KR, _QC, _KC, _VC, _PD = 0, 256, 512, 768, 1024, 1152, 1280, 1536, 1664, 1792
_KR_LANE = 64


def _dot(a, b):
    return jnp.dot(a, b, preferred_element_type=F32)


def _dot_nt(a, b):
    return lax.dot_general(a, b, (((1,), (1,)), ((), ())), preferred_element_type=F32)


def _sigmoid(x):
    return 1.0 / (1.0 + jnp.exp(-x))


def _rms(x, g, n=None):
    n = x.shape[-1] if n is None else n
    ms = jnp.sum(x * x, -1, keepdims=True) * (1.0 / n)
    return x * lax.rsqrt(ms + EPS) * g


def _softmax_blocks(blocks, sink=None):
    m = None
    for s in blocks:
        mm = jnp.max(s, -1, keepdims=True)
        m = mm if m is None else jnp.maximum(m, mm)
    if sink is not None:
        m = jnp.maximum(m, sink)
    es = [jnp.exp(s - m) for s in blocks]
    l = None
    for e in es:
        ll = jnp.sum(e, -1, keepdims=True)
        l = ll if l is None else l + ll
    if sink is not None:
        l = l + jnp.exp(sink - m)
    return es, l


def _rope(x, cos, sin, q):
    w = x.shape[-1]
    lane = lax.broadcasted_iota(jnp.int32, x.shape, 1)
    up = pltpu.roll(x, w - q, axis=1)
    dn = pltpu.roll(x, q, axis=1)
    partner = jnp.where((lane & (2 * q - 1)) < q, up, dn)
    return x * cos + partner * sin


def _const_spec(shape):
    n = len(shape)
    return pl.BlockSpec(shape, lambda *_: (0,) * n, pipeline_mode=pl.Buffered(1))


def _layer_spec(arr, layer):
    n = arr.ndim - 1
    return pl.BlockSpec((None,) + arr.shape[1:], lambda *_: (layer,) + (0,) * n, pipeline_mode=pl.Buffered(1))


def _params(n_axes):
    return pltpu.CompilerParams(dimension_semantics=("arbitrary",) * n_axes, vmem_limit_bytes=VMEM_LIMIT)


def _mod_kernel(cv_ref, w_ref, b_ref, o_ref):
    cv = cv_ref[...]
    a = (cv * _sigmoid(cv)).astype(BF16)
    o_ref[0] = _dot(a, w_ref[0].astype(BF16)) + b_ref[0]


def _mod_call(cv, w_mod, b_mod):
    tn = 2048
    return pl.pallas_call(
        _mod_kernel,
        grid=(DEPTH, 6 * D_MODEL // tn),
        in_specs=[_const_spec((8, D_MODEL)),
                  pl.BlockSpec((1, D_MODEL, tn), lambda l, j: (l, 0, j)),
                  pl.BlockSpec((1, 1, tn), lambda l, j: (l, 0, j))],
        out_specs=pl.BlockSpec((1, 8, tn), lambda l, j: (l, 0, j)),
        out_shape=jax.ShapeDtypeStruct((DEPTH, 8, 6 * D_MODEL), F32),
        compiler_params=_params(2),
        name="mod",
    )(cv, w_mod, b_mod.reshape(DEPTH, 1, 6 * D_MODEL))


_IN_SLOTS = ((_QA, 256), (_KA, 256), (_VA, 256), (_CQ, 256), (_CKV, 128), (_KR, 128),
             (_QC, 256), (_KC, 128), (_VC, 128), (_PD, 256))
_CTX_SLOTS = ((_QA, 256, BF16, False), (_KA, 256, F32, True), (_VA, 256, F32, True), (_VA, 256, BF16, False),
              (_CQ, 256, BF16, False), (_CKV, 128, F32, False), (_KR, 128, F32, False),
              (_QC, 256, BF16, False), (_KC, 128, F32, True), (_VC, 128, F32, True), (_VC, 128, BF16, False),
              (_PD, 256, BF16, False))
_LAT_SLOTS = tuple((off, wd, BF16, False) for off, wd in _IN_SLOTS)


def _inproj_kernel(*refs, latent):
    if latent:
        x_ref, mod_ref, g_ref, w_ref, c64_ref, s64_ref, cm_ref, sm_ref = refs[:8]
        outs = refs[8:]
    else:
        x_ref, mod_ref, g_ref, w_ref = refs[:4]
        outs = refs[4:]
    x = x_ref[...]
    h = _rms(x, g_ref[...]) * (1.0 + mod_ref[0, 1:2, :]) + mod_ref[0, 0:1, :]
    p = _dot(h.astype(BF16), w_ref[...])
    for (off, wd, _, transposed), o_ref in zip(_LAT_SLOTS if latent else _CTX_SLOTS, outs):
        v = p[:, off:off + wd]
        if latent:
            if off == _QC:
                v = _rope(v, c64_ref[...], s64_ref[...], 16)
            elif off == _KC:
                v = _rope(v, c64_ref[:, :128], s64_ref[:, :128], 16)
            elif off == _KR:
                v = _rope(v, cm_ref[:, :128], sm_ref[:, :128], 8)
        if transposed:
            for b in range(o_ref.shape[0]):
                o_ref[b] = v[b * SEQ:(b + 1) * SEQ].T.astype(o_ref.dtype)
        else:
            o_ref[...] = v.astype(o_ref.dtype)


def _inproj_call(x2d, mod, w, layer, rope_tabs=None):
    t = x2d.shape[0]
    tm = 512
    latent = rope_tabs is not None
    slots = _LAT_SLOTS if latent else _CTX_SLOTS
    tiles_per_mod = t // tm // mod.shape[0]
    in_specs = [pl.BlockSpec((tm, D_MODEL), lambda i: (i, 0)),
                pl.BlockSpec((1, 6, D_MODEL), lambda i: (i // tiles_per_mod, 0, 0)),
                _layer_spec(w["g_attn_pre"], layer), _layer_spec(w["w_in_p"], layer)]
    args = [x2d, mod, w["g_attn_pre"], w["w_in_p"]]
    if latent:
        tiles_per_seq = DEC_SEQ // tm
        c64, s64, cm, sm = rope_tabs
        in_specs += [pl.BlockSpec((tm, 256), lambda i: (i % tiles_per_seq, 0)),
                     pl.BlockSpec((tm, 256), lambda i: (i % tiles_per_seq, 0)),
                     pl.BlockSpec((tm, 512), lambda i: (i % tiles_per_seq, 0)),
                     pl.BlockSpec((tm, 512), lambda i: (i % tiles_per_seq, 0))]
        args += [c64, s64, cm, sm]
    return pl.pallas_call(
        functools.partial(_inproj_kernel, latent=latent),
        grid=(t // tm,),
        in_specs=in_specs,
        out_specs=[pl.BlockSpec((tm // SEQ, wd, SEQ), lambda i: (i, 0, 0)) if tr
                   else pl.BlockSpec((tm, wd), lambda i: (i, 0)) for _, wd, _, tr in slots],
        out_shape=[jax.ShapeDtypeStruct((t // SEQ, wd, SEQ) if tr else (t, wd), dt) for _, wd, dt, tr in slots],
        compiler_params=_params(1),
        name="inproj_lat" if latent else "inproj_ctx",
    )(*args)


def _ctx_attn_kernel(q_ref, kt_ref, v_ref, o_ref, *, seqs):
    q = q_ref[...]
    kt = kt_ref[...].astype(BF16)
    v = v_ref[...]
    scores = []
    for s in range(seqs):
        rows = slice(s * SEQ, (s + 1) * SEQ)
        for h in range(NA_HEADS):
            sl = slice(h * HEAD_DIM, (h + 1) * HEAD_DIM)
            scores.append(_dot(q[rows, sl], kt[s, sl, :]))
    probs = [_softmax_blocks([sc]) for sc in scores]
    outs = []
    for s in range(seqs):
        rows = slice(s * SEQ, (s + 1) * SEQ)
        ys = []
        for h in range(NA_HEADS):
            sl = slice(h * HEAD_DIM, (h + 1) * HEAD_DIM)
            (e,), l = probs[s * NA_HEADS + h]
            ys.append(_dot(e.astype(BF16), v[rows, sl]) / l)
        outs.append(jnp.concatenate(ys, -1))
    o_ref[...] = (outs[0] if seqs == 1 else jnp.concatenate(outs, 0)).astype(o_ref.dtype)


def _ctx_attn_call(q, k, v):
    seqs = CTX_SEQS
    t = q.shape[0]
    spec = pl.BlockSpec((seqs * SEQ, 256), lambda b: (b, 0))
    kt_spec = pl.BlockSpec((seqs, 256, SEQ), lambda b: (b, 0, 0))
    return pl.pallas_call(
        functools.partial(_ctx_attn_kernel, seqs=seqs), grid=(t // SEQ // seqs,),
        in_specs=[spec, kt_spec, spec], out_specs=spec,
        out_shape=jax.ShapeDtypeStruct((t, 256), BF16), compiler_params=_params(1), name="ctx_attn",
    )(q, k, v)


def _gqa_operands(q, k, rows_q, rows_k):
    group = SWA_HEADS // SWA_KV_HEADS
    out = []
    for kv in range(SWA_KV_HEADS):
        qs = jnp.concatenate([q[rows_q, (kv * group + g) * HEAD_DIM:(kv * group + g + 1) * HEAD_DIM]
                              for g in range(group)], 0)
        out.append((qs, k[rows_k, kv * HEAD_DIM:(kv + 1) * HEAD_DIM]))
    return out


def _gqa_sink(sink_ref, kv, m):
    group = SWA_HEADS // SWA_KV_HEADS
    row = lax.broadcasted_iota(jnp.int32, (group * m, 1), 0)
    col = jnp.full((group * m, 1), sink_ref[kv * group + group - 1], F32)
    for g in range(group - 2, -1, -1):
        col = jnp.where(row < (g + 1) * m, sink_ref[kv * group + g], col)
    return col


def _ctx_swa_kernel(sink_ref, q_ref, kt_ref, v_ref, o_ref, *, seqs):
    q = q_ref[...]
    kt = kt_ref[...].astype(BF16)
    v = v_ref[...]
    group = SWA_HEADS // SWA_KV_HEADS
    scores = []
    for s in range(seqs):
        rows = slice(s * SEQ, (s + 1) * SEQ)
        for kv, (qs, _) in enumerate(_gqa_operands(q, q, rows, rows)):
            scores.append(_dot(qs, kt[s, kv * HEAD_DIM:(kv + 1) * HEAD_DIM, :]))
    probs = [_softmax_blocks([sc], sink=_gqa_sink(sink_ref, i % SWA_KV_HEADS, SEQ)) for i, sc in enumerate(scores)]
    outs = []
    for s in range(seqs):
        rows = slice(s * SEQ, (s + 1) * SEQ)
        ys = []
        for kv in range(SWA_KV_HEADS):
            (e,), l = probs[s * SWA_KV_HEADS + kv]
            y = _dot(e.astype(BF16), v[rows, kv * HEAD_DIM:(kv + 1) * HEAD_DIM]) / l
            ys += [y[g * SEQ:(g + 1) * SEQ] for g in range(group)]
        outs.append(jnp.concatenate(ys, -1))
    o_ref[...] = (outs[0] if seqs == 1 else jnp.concatenate(outs, 0)).astype(o_ref.dtype)


def _ctx_swa_call(sink, q, k, v):
    seqs = CTX_SEQS
    t = q.shape[0]
    rows = seqs * SEQ
    return pl.pallas_call(
        functools.partial(_ctx_swa_kernel, seqs=seqs), grid=(t // rows,),
        in_specs=[pl.BlockSpec(memory_space=pltpu.SMEM),
                  pl.BlockSpec((rows, 256), lambda b: (b, 0)),
                  pl.BlockSpec((seqs, 128, SEQ), lambda b: (b, 0, 0)),
                  pl.BlockSpec((rows, 128), lambda b: (b, 0))],
        out_specs=pl.BlockSpec((rows, 256), lambda b: (b, 0)),
        out_shape=jax.ShapeDtypeStruct((t, 256), BF16), compiler_params=_params(1), name="ctx_swa",
    )(sink, q, k, v)


def _mla_q(cq, qn_ref, wuq_ref):
    return _dot(_rms(cq, qn_ref[...], MLA_Q_RANK).astype(BF16), wuq_ref[...])


def _mla_kv(ckv, kr, kvn_ref, wuk_ref, wuv_ref):
    cn = _rms(ckv, kvn_ref[...]).astype(BF16)
    kcat = _dot(cn, wuk_ref[...]) + jnp.concatenate([kr] * MLA_HEADS, -1)
    return kcat, _dot(cn, wuv_ref[...])


def _ctx_mla_kernel(cq_ref, ckv_ref, kr_ref, qn_ref, wuq_ref, kvn_ref, wuk_ref, wuv_ref, o_ref, *, seqs):
    q = (_mla_q(cq_ref[...].astype(F32), qn_ref, wuq_ref) * MLA_SCALE).astype(BF16)
    kcat, v = _mla_kv(ckv_ref[...], kr_ref[...], kvn_ref, wuk_ref, wuv_ref)
    kcat = kcat.astype(BF16)
    v = v.astype(BF16)
    scores = []
    for s in range(seqs):
        rows = slice(s * SEQ, (s + 1) * SEQ)
        for h in range(MLA_HEADS):
            sl = slice(h * LANE, (h + 1) * LANE)
            scores.append(_dot_nt(q[rows, sl], kcat[rows, sl]))
    probs = [_softmax_blocks([sc]) for sc in scores]
    outs = []
    for s in range(seqs):
        rows = slice(s * SEQ, (s + 1) * SEQ)
        ys = []
        for h in range(MLA_HEADS):
            (e,), l = probs[s * MLA_HEADS + h]
            ys.append(_dot(e.astype(BF16), v[rows, h * MLA_V:(h + 1) * MLA_V]) / l)
        outs.append(jnp.concatenate(ys, -1))
    o_ref[...] = (outs[0] if seqs == 1 else jnp.concatenate(outs, 0)).astype(o_ref.dtype)


_MLA_W = ("mla_qn", "mla_wuq", "mla_kvn", "mla_wuk", "mla_wuv")


def _ctx_mla_call(cq, ckv, kr, w, layer):
    seqs = CTX_SEQS
    t = cq.shape[0]
    rows = seqs * SEQ
    return pl.pallas_call(
        functools.partial(_ctx_mla_kernel, seqs=seqs), grid=(t // rows,),
        in_specs=[pl.BlockSpec((rows, 256), lambda b: (b, 0)),
                  pl.BlockSpec((rows, 128), lambda b: (b, 0)),
                  pl.BlockSpec((rows, 128), lambda b: (b, 0))] + [_layer_spec(w[k], layer) for k in _MLA_W],
        out_specs=pl.BlockSpec((rows, 256), lambda b: (b, 0)),
        out_shape=jax.ShapeDtypeStruct((t, 256), BF16), compiler_params=_params(1), name="ctx_mla",
    )(cq, ckv, kr, *[w[k] for k in _MLA_W])


_POOL_PAD = 8


def _pool_kernel(x_ref, w_ref, sc_ref, o_ref, *, n):
    ne = n + 2 * _POOL_PAD
    lo, hi = _POOL_PAD, _POOL_PAD + n
    z = jnp.zeros((_POOL_PAD, POOL_WIDTH), F32)
    grp = lax.broadcasted_iota(jnp.int32, (n, POOL_WIDTH), 1) >> 6
    t = lax.broadcasted_iota(jnp.int32, (n, POOL_WIDTH), 0)
    half = jnp.where(grp == 0, 1, jnp.where(grp == 1, 2, jnp.where(grp == 2, 4, 8)))
    cnt = (jnp.minimum(t + half, n) - jnp.maximum(t - half, 0)).astype(F32)

    def pair(a, s):
        return pltpu.roll(a, s, axis=0) + pltpu.roll(a, ne - s, axis=0)

    for s in range(x_ref.shape[0] // n):
        x = x_ref[s * n:(s + 1) * n, :].astype(F32)
        xz = jnp.concatenate([z, x, z], 0)
        s2 = xz + pltpu.roll(xz, 1, axis=0)
        s4 = pair(s2, 1)
        s8 = pair(s4, 2)
        s16 = pair(s8, 4)
        tot = jnp.where(grp == 0, s2[lo:hi],
                        jnp.where(grp == 1, s4[lo:hi], jnp.where(grp == 2, s8[lo:hi], s16[lo:hi])))
        dlt = (tot / cnt - x).astype(BF16)
        o_ref[s * n:(s + 1) * n, :] = (_dot(dlt, w_ref[...]) * sc_ref[...]).astype(o_ref.dtype)


def _pool_call(pd, w, layer, seq, seqs):
    t = pd.shape[0]
    rows = seq * seqs
    return pl.pallas_call(
        functools.partial(_pool_kernel, n=seq), grid=(t // rows,),
        in_specs=[pl.BlockSpec((rows, POOL_WIDTH), lambda b: (b, 0)),
                  _layer_spec(w["pool_w"], layer), _layer_spec(w["pool_scale"], layer)],
        out_specs=pl.BlockSpec((rows, POOL_WIDTH), lambda b: (b, 0)),
        out_shape=jax.ShapeDtypeStruct((t, POOL_WIDTH), BF16), compiler_params=_params(1), name="pool",
    )(pd, w["pool_w"], w["pool_scale"])


def _lat_na_kernel(q_ref, k_ref, v_ref, kc_ref, vc_ref, t2_ref, o_ref):
    n = pl.program_id(1)
    rows = DEC_SEQ // GRID_W
    q_rows = NA_Q_BLOCK // GRID_W
    row0 = jnp.clip(q_rows * n - NA_WIN_R // 2, 0, rows - NA_SPAN // GRID_W)
    start = pl.multiple_of(row0 * GRID_W, LANE)
    q = q_ref[...]
    k = k_ref[pl.ds(start, NA_SPAN), :]
    v = v_ref[pl.ds(start, NA_SPAN), :]
    kc = kc_ref[...].astype(BF16)
    vc = vc_ref[...].astype(BF16)

    pairs = NA_SPAN // LANE
    low_half = lax.broadcasted_iota(jnp.int32, (GRID_W, LANE), 1) < GRID_W
    entries, masks = [], []
    for a in range(q_rows):
        r = q_rows * n + a
        r_start = jnp.clip(r - NA_WIN_R // 2, 0, rows - NA_WIN_R)
        for p in range(pairs):
            rk = row0 + 2 * p
            ok0 = ((rk >= r_start) & (rk < r_start + NA_WIN_R)).astype(jnp.int32)
            ok1 = ((rk + 1 >= r_start) & (rk + 1 < r_start + NA_WIN_R)).astype(jnp.int32)
            entries.append(jnp.clip(rk - r + NA_WIN_R, 0, NA_DR))
            masks.append(jnp.where(low_half, ok0, ok1) > 0)

    heads = [slice(h * HEAD_DIM, (h + 1) * HEAD_DIM) for h in range(NA_HEADS)]
    raw = [(_dot_nt(q[:, sl], k[:, sl]), _dot_nt(q[:, sl], kc[:, sl])) for sl in heads]
    probs = []
    for h, (s_loc, s_ctx) in enumerate(raw):
        cols = []
        for p in range(pairs):
            blk = [jnp.where(masks[a * pairs + p], t2_ref[h, entries[a * pairs + p]], NEG_INF)
                   for a in range(q_rows)]
            cols.append(jnp.concatenate(blk, 0))
        probs.append(_softmax_blocks([s_loc + jnp.concatenate(cols, 1), s_ctx]))
    ys = []
    for sl, ((e_loc, e_ctx), l) in zip(heads, probs):
        y = _dot(e_loc.astype(BF16), v[:, sl]) + _dot(e_ctx.astype(BF16), vc[:, sl])
        ys.append(y / l)
    o_ref[...] = jnp.concatenate(ys, -1).astype(o_ref.dtype)


def _lat_na_call(q, k, v, cache_k, cache_v, t2, layer):
    nq = DEC_SEQ // NA_Q_BLOCK
    seq_spec = pl.BlockSpec((DEC_SEQ, 256), lambda b, n: (b, 0))
    cache_spec = pl.BlockSpec((None, None, PAST_LEN, 256), lambda b, n: (b, layer, 0, 0))
    return pl.pallas_call(
        _lat_na_kernel, grid=(DEC_BATCH, nq),
        in_specs=[pl.BlockSpec((NA_Q_BLOCK, 256), lambda b, n: (b * nq + n, 0)), seq_spec, seq_spec,
                  cache_spec, cache_spec, _layer_spec(t2, layer)],
        out_specs=pl.BlockSpec((NA_Q_BLOCK, 256), lambda b, n: (b * nq + n, 0)),
        out_shape=jax.ShapeDtypeStruct((DEC_BATCH * DEC_SEQ, 256), BF16),
        compiler_params=_params(2), name="lat_na",
    )(q, k, v, cache_k, cache_v, t2)


def _lat_swa_kernel(sink_ref, q_ref, k_ref, v_ref, kc_ref, vc_ref, o_ref):
    n = pl.program_id(1)
    start = pl.multiple_of(jnp.clip(n - 1, 0, DEC_SEQ // Q_BLOCK - 3) * Q_BLOCK, LANE)
    q = q_ref[...]
    k = k_ref[pl.ds(start, SWA_SPAN), :]
    v = v_ref[pl.ds(start, SWA_SPAN), :]
    kc = kc_ref[...].astype(BF16)
    vc = vc_ref[...].astype(BF16)
    group = SWA_HEADS // SWA_KV_HEADS
    m = group * Q_BLOCK
    q_pos = n * Q_BLOCK + (lax.broadcasted_iota(jnp.int32, (m, SWA_SPAN), 0) & (Q_BLOCK - 1))
    k_pos = start + lax.broadcasted_iota(jnp.int32, (m, SWA_SPAN), 1)
    valid = jnp.abs(q_pos - k_pos) <= SWA_WINDOW
    everything = slice(None)
    raw = []
    for (qs, ks), (_, kcs) in zip(_gqa_operands(q, k, everything, everything),
                                  _gqa_operands(q, kc, everything, everything)):
        raw.append((_dot_nt(qs, ks), _dot_nt(qs, kcs)))
    probs = [_softmax_blocks([jnp.where(valid, s_loc, NEG_INF), s_ctx], sink=_gqa_sink(sink_ref, kv, Q_BLOCK))
             for kv, (s_loc, s_ctx) in enumerate(raw)]
    ys = []
    for kv, ((e_loc, e_ctx), l) in enumerate(probs):
        kvsl = slice(kv * HEAD_DIM, (kv + 1) * HEAD_DIM)
        y = (_dot(e_loc.astype(BF16), v[:, kvsl]) + _dot(e_ctx.astype(BF16), vc[:, kvsl])) / l
        ys += [y[g * Q_BLOCK:(g + 1) * Q_BLOCK] for g in range(group)]
    o_ref[...] = jnp.concatenate(ys, -1).astype(o_ref.dtype)


def _lat_swa_call(sink, q, k, v, cache_k, cache_v, layer):
    nq = DEC_SEQ // Q_BLOCK
    seq_spec = pl.BlockSpec((DEC_SEQ, 128), lambda b, n: (b, 0))
    cache_spec = pl.BlockSpec((None, None, PAST_LEN, 128), lambda b, n: (b, layer, 0, 0))
    return pl.pallas_call(
        _lat_swa_kernel, grid=(DEC_BATCH, nq),
        in_specs=[pl.BlockSpec(memory_space=pltpu.SMEM),
                  pl.BlockSpec((Q_BLOCK, 256), lambda b, n: (b * nq + n, 0)), seq_spec, seq_spec,
                  cache_spec, cache_spec],
        out_specs=pl.BlockSpec((Q_BLOCK, 256), lambda b, n: (b * nq + n, 0)),
        out_shape=jax.ShapeDtypeStruct((DEC_BATCH * DEC_SEQ, 256), BF16),
        compiler_params=_params(2), name="lat_swa",
    )(sink, q, k, v, cache_k, cache_v)


def _lat_mla_kernel(cq_ref, ckv_ref, kr_ref, cckv_ref, ckr_ref, cm_ref, sm_ref,
                    qn_ref, wuq_ref, kvn_ref, wuk_ref, wuv_ref, o_ref, kcat_s, v_s):
    @pl.when(pl.program_id(1) == 0)
    def _():
        kc, vc = _mla_kv(cckv_ref[...], ckr_ref[...], kvn_ref, wuk_ref, wuv_ref)
        kcat_s[0:PAST_LEN, :] = kc.astype(BF16)
        v_s[0:PAST_LEN, :] = vc.astype(BF16)
        kl, vl = _mla_kv(ckv_ref[...].astype(F32), kr_ref[...].astype(F32), kvn_ref, wuk_ref, wuv_ref)
        kcat_s[PAST_LEN:, :] = kl.astype(BF16)
        v_s[PAST_LEN:, :] = vl.astype(BF16)

    q = _mla_q(cq_ref[...].astype(F32), qn_ref, wuq_ref)
    q = (_rope(q, cm_ref[...], sm_ref[...], 8) * MLA_SCALE).astype(BF16)
    scores = [_dot_nt(q[:, h * LANE:(h + 1) * LANE], kcat_s[:, h * LANE:(h + 1) * LANE]) for h in range(MLA_HEADS)]
    probs = [_softmax_blocks([sc]) for sc in scores]
    ys = [_dot(e.astype(BF16), v_s[:, h * MLA_V:(h + 1) * MLA_V]) / l for h, ((e,), l) in enumerate(probs)]
    o_ref[...] = jnp.concatenate(ys, -1).astype(o_ref.dtype)


def _lat_mla_call(cq, ckv, kr, cache_ckv, cache_kr, cm, sm, w, layer):
    qb = MLA_Q_BLOCK
    nq = DEC_SEQ // qb
    seq_spec = pl.BlockSpec((DEC_SEQ, 128), lambda b, n: (b, 0))
    cache_spec = pl.BlockSpec((None, None, PAST_LEN, 128), lambda b, n: (b, layer, 0, 0))
    tab_spec = pl.BlockSpec((qb, 512), lambda b, n: (n, 0))
    return pl.pallas_call(
        _lat_mla_kernel, grid=(DEC_BATCH, nq),
        in_specs=[pl.BlockSpec((qb, 256), lambda b, n: (b * nq + n, 0)), seq_spec, seq_spec,
                  cache_spec, cache_spec, tab_spec, tab_spec] + [_layer_spec(w[k], layer) for k in _MLA_W],
        out_specs=pl.BlockSpec((qb, 256), lambda b, n: (b * nq + n, 0)),
        out_shape=jax.ShapeDtypeStruct((DEC_BATCH * DEC_SEQ, 256), BF16),
        scratch_shapes=[pltpu.VMEM((PAST_LEN + DEC_SEQ, MLA_HEADS * LANE), BF16),
                        pltpu.VMEM((PAST_LEN + DEC_SEQ, MLA_HEADS * MLA_V), BF16)],
        compiler_params=_params(2), name="lat_mla",
    )(cq, ckv, kr, cache_ckv, cache_kr, cm, sm, *[w[k] for k in _MLA_W])


_MERGE_PARTS = 2


def _merge_kernel(x_ref, mod_ref, gpre_ref, ya_ref, yb_ref, yc_ref, yd_ref,
                  wg_ref, bg_ref, wb_ref, wo_ref, gpost_ref, o_ref):
    tm = x_ref.shape[0] // _MERGE_PARTS
    for p in range(_MERGE_PARTS):
        rows = slice(p * tm, (p + 1) * tm)
        x = x_ref[rows, :]
        h = (_rms(x, gpre_ref[...]) * (1.0 + mod_ref[0, 1:2, :]) + mod_ref[0, 0:1, :]).astype(BF16)
        merged = None
        for k, y_ref in enumerate((ya_ref, yb_ref, yc_ref, yd_ref)):
            cols = slice(k * D_MODEL, (k + 1) * D_MODEL)
            gate = _sigmoid(_dot(h, wg_ref[:, cols]) + bg_ref[:, cols])
            term = gate * _dot(y_ref[rows, :], wb_ref[k])
            merged = term if merged is None else merged + term
        o = _dot(merged.astype(BF16), wo_ref[...])
        o_ref[rows, :] = x + mod_ref[0, 2:3, :] * _rms(o, gpost_ref[...])


_MERGE_W = ("w_gate", "b_gate", "w_branch", "w_out", "g_attn_post")


def _merge_call(x2d, mod, ys, w, layer):
    t = x2d.shape[0]
    tm = 512
    tiles_per_mod = t // tm // mod.shape[0]
    tile = pl.BlockSpec((tm, D_MODEL), lambda i: (i, 0))
    ytile = pl.BlockSpec((tm, BRANCH_W), lambda i: (i, 0))
    return pl.pallas_call(
        _merge_kernel, grid=(t // tm,),
        in_specs=[tile, pl.BlockSpec((1, 6, D_MODEL), lambda i: (i // tiles_per_mod, 0, 0)),
                  _layer_spec(w["g_attn_pre"], layer), ytile, ytile, ytile, ytile]
                 + [_layer_spec(w[k], layer) for k in _MERGE_W],
        out_specs=tile,
        out_shape=jax.ShapeDtypeStruct((t, D_MODEL), F32),
        compiler_params=_params(1), name="merge",
    )(x2d, mod, w["g_attn_pre"], *ys, *[w[k] for k in _MERGE_W])


_GAP = 8


def _ffn_kernel(*refs, seqs, halo):
    if halo:
        x_ref, xp_ref, xn_ref = refs[:3]
        refs = refs[3:]
    else:
        x_ref = refs[0]
        refs = refs[1:]
    mod_ref, gpre_ref, wa_ref, wg_ref, ca_ref, cg_ref, wd_ref, gpost_ref, o_ref = refs
    x = x_ref[...]
    tm = x.shape[0]
    shift, scale, gate = mod_ref[0, 3:4, :], mod_ref[0, 4:5, :], mod_ref[0, 5:6, :]

    def pre(xx):
        return _rms(xx, gpre_ref[...]) * (1.0 + scale) + shift

    h = pre(x)
    if halo:
        i = pl.program_id(0) % halo
        hp = jnp.where(i == 0, 0.0, pre(xp_ref[...]))
        hn = jnp.where(i == halo - 1, 0.0, pre(xn_ref[...]))
        pieces = [hp, h, hn]
        starts = [_GAP]
        seq_len = tm
    else:
        seq_len = tm // seqs
        pieces = [h]
        starts = [s * seq_len for s in range(seqs)]
    hb = jnp.concatenate(pieces, 0).astype(BF16)
    rows = hb.shape[0]
    edge_row = lax.broadcasted_iota(jnp.int32, (_GAP, FF_CHUNK), 0)

    def zero_edges(a, first):
        out = []
        for st in starts:
            seg = a[st:st + seq_len]
            if first:
                out += [jnp.where(edge_row == 0, 0.0, seg[:_GAP]), seg[_GAP:]]
            else:
                out += [seg[:seq_len - _GAP], jnp.where(edge_row == _GAP - 1, 0.0, seg[seq_len - _GAP:])]
        return jnp.concatenate(out, 0)

    def up(c):
        cols = slice(c * FF_CHUNK, (c + 1) * FF_CHUNK)
        return _dot(hb, wa_ref[:, cols]), _dot(hb, wg_ref[:, cols])

    def conv(u, c_ref, cols):
        prev = pltpu.roll(u, 1, axis=0)
        nxt = pltpu.roll(u, rows - 1, axis=0)
        if not halo:
            prev, nxt = zero_edges(prev, True), zero_edges(nxt, False)
        return prev * c_ref[0:1, cols] + u * c_ref[1:2, cols] + nxt * c_ref[2:3, cols]

    n_chunks = D_FF_PAD // FF_CHUNK
    pending = [up(c) for c in range(FFN_LOOKAHEAD)]
    acts = []
    for c in range(n_chunks):
        if c + FFN_LOOKAHEAD < n_chunks:
            pending.append(up(c + FFN_LOOKAHEAD))
        ua, ug = pending.pop(0)
        cols = slice(c * FF_CHUNK, (c + 1) * FF_CHUNK)
        a = conv(ua, ca_ref, cols)
        g = conv(ug, cg_ref, cols)
        acts.append((g * _sigmoid(g) * a).astype(BF16))
    acc = _dot(jnp.concatenate(acts, 1), wd_ref[...])
    for s, st in enumerate(starts):
        ys = _rms(acc[st:st + seq_len], gpost_ref[...])
        o_ref[s * seq_len:(s + 1) * seq_len, :] = x[s * seq_len:(s + 1) * seq_len] + gate * ys


_FFN_W = ("g_ffn_pre", "ffn_wa", "ffn_wg", "ffn_ca", "ffn_cg", "ffn_wd", "g_ffn_post")


def _ffn_call(x2d, mod, w, layer, seq, tm):
    t = x2d.shape[0]
    tile = pl.BlockSpec((tm, D_MODEL), lambda i: (i, 0))
    in_specs = [tile]
    args = [x2d]
    if tm < seq:
        halo, seqs = seq // tm, 1
        r = tm // _GAP
        last = t // _GAP - 1
        in_specs += [pl.BlockSpec((_GAP, D_MODEL), lambda i: (jnp.maximum(i * r - 1, 0), 0)),
                     pl.BlockSpec((_GAP, D_MODEL), lambda i: (jnp.minimum((i + 1) * r, last), 0))]
        args += [x2d, x2d]
    else:
        halo, seqs = 0, tm // seq
    tiles_per_mod = t // tm // mod.shape[0]
    in_specs += [pl.BlockSpec((1, 6, D_MODEL), lambda i: (i // tiles_per_mod, 0, 0))]
    in_specs += [_layer_spec(w[k], layer) for k in _FFN_W]
    args += [mod] + [w[k] for k in _FFN_W]
    return pl.pallas_call(
        functools.partial(_ffn_kernel, seqs=seqs, halo=halo), grid=(t // tm,),
        in_specs=in_specs, out_specs=tile,
        out_shape=jax.ShapeDtypeStruct((t, D_MODEL), F32),
        compiler_params=_params(1), name="ffn",
    )(*args)


def _rope_tables():
    t = np.arange(DEC_SEQ)
    pos = (t // GRID_W, t % GRID_W)

    def tab(d):
        half = d // 4
        inv = np.float32(ROPE_BASE) ** (-np.arange(half, dtype=np.float32) / np.float32(half))
        cs, sn = [], []
        for p in pos:
            ang = p.astype(np.float32)[:, None] * inv[None, :]
            cs += [np.cos(ang), np.cos(ang)]
            sn += [-np.sin(ang), np.sin(ang)]
        return np.concatenate(cs, -1), np.concatenate(sn, -1)

    c64, s64 = tab(HEAD_DIM)
    c32, s32 = tab(MLA_ROPE)
    pad = LANE - MLA_NOPE - MLA_ROPE
    cm = np.concatenate([np.ones((DEC_SEQ, MLA_NOPE), np.float32), c32, np.ones((DEC_SEQ, pad), np.float32)], -1)
    sm = np.concatenate([np.zeros((DEC_SEQ, MLA_NOPE), np.float32), s32, np.zeros((DEC_SEQ, pad), np.float32)], -1)
    return tuple(jnp.asarray(np.tile(x, (1, 4)), F32) for x in (c64, s64, cm, sm))


def _na_table_kernel(rpb_ref, e_ref, ok_ref, o_ref):
    r = rpb_ref[...]
    r1 = r.astype(BF16)
    r2 = (r - r1.astype(F32)).astype(BF16)
    r3 = (r - r1.astype(F32) - r2.astype(F32)).astype(BF16)
    e = e_ref[...]
    t = _dot(r1, e) + _dot(r2, e) + _dot(r3, e)
    o_ref[...] = jnp.where(ok_ref[...] > 0, t, NEG_INF)


def _na_bias_tables(na_rpb):
    c = np.arange(GRID_W)[:, None]
    w = np.arange(GRID_W)[None, :]
    dc = (w - c + NA_WIN_C - 1).reshape(-1)
    onehot = (np.arange(LANE)[:, None] == dc[None, :]).astype(np.float32)
    c_start = np.clip(c - NA_WIN_C // 2, 0, GRID_W - NA_WIN_C)
    ok = ((w >= c_start) & (w < c_start + NA_WIN_C)).reshape(1, -1).astype(np.int32)
    rows = DEPTH * NA_HEADS * NA_DR
    rpb2 = jnp.pad(na_rpb.reshape(rows, NA_DC), ((0, LANE - rows), (0, LANE - NA_DC)))
    t = pl.pallas_call(
        _na_table_kernel, out_shape=jax.ShapeDtypeStruct((LANE, GRID_W * GRID_W), F32), name="na_table",
        compiler_params=pltpu.CompilerParams(vmem_limit_bytes=VMEM_LIMIT),
    )(rpb2, jnp.asarray(onehot, BF16), jnp.asarray(ok))
    t = t[:rows].reshape(DEPTH, NA_HEADS, NA_DR, GRID_W, GRID_W)
    t = jnp.pad(t, ((0, 0), (0, 0), (1, 1), (0, 0), (0, 0)), constant_values=NEG_INF)
    return jnp.concatenate([t[:, :, :-1], t[:, :, 1:]], -1)


def _pad_last(w, n):
    return jnp.pad(w, ((0, 0),) * (w.ndim - 1) + ((0, n - w.shape[-1]),))


def _prep_weights(g_attn_pre, g_attn_post, g_ffn_pre, g_ffn_post, w_in, w_gate, b_gate, mla_q_norm, mla_w_uq,
                  mla_kv_norm, mla_w_ukv, pool_w, pool_scale, w_branch, w_out, ffn_w_up, ffn_conv, ffn_w_down):
    q_scale = jnp.concatenate([jnp.full((256,), ATT_SCALE, F32), jnp.ones((512,), F32)])
    b0, c0, d0 = 768, 1120, 1632
    kr = jnp.pad(w_in[:, :, b0 + 320:b0 + 352], ((0, 0), (0, 0), (_KR_LANE, LANE - _KR_LANE - MLA_ROPE)))
    w_in_p = jnp.concatenate([w_in[:, :, :b0] * q_scale, _pad_last(w_in[:, :, b0:b0 + MLA_Q_RANK], 256),
                              w_in[:, :, b0 + MLA_Q_RANK:b0 + MLA_Q_RANK + MLA_KV_RANK], kr,
                              w_in[:, :, c0:d0] * q_scale[:512], w_in[:, :, d0:]], 2).astype(BF16)

    wuq = mla_w_uq.reshape(DEPTH, MLA_Q_RANK, MLA_HEADS, MLA_NOPE + MLA_ROPE)
    wuq = jnp.pad(wuq, ((0, 0), (0, 256 - MLA_Q_RANK), (0, 0), (0, LANE - MLA_NOPE - MLA_ROPE)))
    wukv = mla_w_ukv.reshape(DEPTH, MLA_KV_RANK, MLA_HEADS, MLA_NOPE + MLA_V)
    wuk = _pad_last(wukv[..., :MLA_NOPE], LANE)

    eye = np.eye(len(POOL_WINDOWS), dtype=np.float32)
    w_bd = (pool_w[:, :, :, None, :] * eye[None, :, None, :, None]).reshape(DEPTH, POOL_WIDTH, POOL_WIDTH)

    return dict(
        g_attn_pre=g_attn_pre[:, None, :], g_attn_post=g_attn_post[:, None, :],
        g_ffn_pre=g_ffn_pre[:, None, :], g_ffn_post=g_ffn_post[:, None, :],
        w_in_p=w_in_p,
        mla_qn=_pad_last(mla_q_norm[:, None, :], 256),
        mla_wuq=wuq.reshape(DEPTH, 256, MLA_HEADS * LANE).astype(BF16),
        mla_kvn=mla_kv_norm[:, None, :],
        mla_wuk=wuk.reshape(DEPTH, MLA_KV_RANK, MLA_HEADS * LANE).astype(BF16),
        mla_wuv=wukv[..., MLA_NOPE:].reshape(DEPTH, MLA_KV_RANK, MLA_HEADS * MLA_V).astype(BF16),
        pool_w=w_bd.astype(BF16), pool_scale=pool_scale[:, None, :],
        w_gate=w_gate.astype(BF16), b_gate=b_gate[:, None, :],
        w_branch=w_branch.astype(BF16), w_out=w_out.astype(BF16),
        ffn_wa=_pad_last(ffn_w_up[:, :, :D_FF], D_FF_PAD).astype(BF16),
        ffn_wg=_pad_last(ffn_w_up[:, :, D_FF:], D_FF_PAD).astype(BF16),
        ffn_ca=_pad_last(ffn_conv[:, :, :D_FF], D_FF_PAD), ffn_cg=_pad_last(ffn_conv[:, :, D_FF:], D_FF_PAD),
        ffn_wd=jnp.pad(ffn_w_down, ((0, 0), (0, D_FF_PAD - D_FF), (0, 0))).astype(BF16))


def kernel(x_prompt, x_sample, cache_na_k, cache_na_v, cache_mla_ckv, cache_mla_krope, cache_swa_k, cache_swa_v, c, c_ctx, w_mod, b_mod, g_attn_pre, g_attn_post, g_ffn_pre, g_ffn_post, w_in, w_gate, b_gate, na_rpb, mla_q_norm, mla_w_uq, mla_kv_norm, mla_w_ukv, swa_sink, pool_w, pool_scale, w_branch, w_out, ffn_w_up, ffn_conv, ffn_w_down):
    x_p = x_prompt.reshape(BATCH * SEQ, D_MODEL)
    x_s = x_sample.reshape(DEC_BATCH * DEC_SEQ, D_MODEL)

    cv = jnp.concatenate([c_ctx[None, :], c, jnp.zeros((8 - 1 - DEC_BATCH, D_MODEL), F32)], 0)
    mod = _mod_call(cv, w_mod, b_mod).reshape(DEPTH, 8, 6, D_MODEL)
    w = _prep_weights(g_attn_pre, g_attn_post, g_ffn_pre, g_ffn_post, w_in, w_gate, b_gate, mla_q_norm, mla_w_uq,
                      mla_kv_norm, mla_w_ukv, pool_w, pool_scale, w_branch, w_out, ffn_w_up, ffn_conv, ffn_w_down)
    rope_tabs = _rope_tables()
    na_t2 = _na_bias_tables(na_rpb)
    cache_na_k = cache_na_k.reshape(DEC_BATCH, DEPTH, PAST_LEN, 256)
    cache_na_v = cache_na_v.reshape(DEC_BATCH, DEPTH, PAST_LEN, 256)
    cache_swa_k = cache_swa_k.reshape(DEC_BATCH, DEPTH, PAST_LEN, 128)
    cache_swa_v = cache_swa_v.reshape(DEC_BATCH, DEPTH, PAST_LEN, 128)
    cache_kr = jnp.pad(cache_mla_krope, ((0, 0), (0, 0), (0, 0), (_KR_LANE, LANE - _KR_LANE - MLA_ROPE)))

    states = []
    for l in range(DEPTH):
        mod_p = mod[l, 0:1]
        qa, ka_t, va_t, va, cq, ckv, kr, qc, kc_t, vc_t, vc, pd = _inproj_call(x_p, mod_p, w, l)
        ys = (_ctx_attn_call(qa, ka_t, va), _ctx_mla_call(cq, ckv, kr, w, l),
              _ctx_swa_call(swa_sink[l], qc, kc_t, vc), _pool_call(pd, w, l, SEQ, CTX_SEQS))
        x_p = _merge_call(x_p, mod_p, ys, w, l)
        x_p = _ffn_call(x_p, mod_p, w, l, SEQ, 2 * SEQ)
        states.append((ka_t, va_t, ckv, kr[:, _KR_LANE:_KR_LANE + MLA_ROPE], kc_t, vc_t))

        mod_s = mod[l, 1:1 + DEC_BATCH]
        qa, ka, va, cq, ckv, kr, qc, kc, vc, pd = _inproj_call(x_s, mod_s, w, l, rope_tabs)
        ys = (_lat_na_call(qa, ka, va, cache_na_k, cache_na_v, na_t2, l),
              _lat_mla_call(cq, ckv, kr, cache_mla_ckv, cache_kr, rope_tabs[2], rope_tabs[3], w, l),
              _lat_swa_call(swa_sink[l], qc, kc, vc, cache_swa_k, cache_swa_v, l),
              _pool_call(pd, w, l, DEC_SEQ, 1))
        x_s = _merge_call(x_s, mod_s, ys, w, l)
        x_s = _ffn_call(x_s, mod_s, w, l, DEC_SEQ, 512)

    def stack(j, *tail):
        return jnp.stack([st[j].reshape(BATCH, SEQ, *tail) for st in states], 1)

    def stack_t(j, heads):
        a = jnp.stack([st[j] for st in states], 1).reshape(BATCH, DEPTH, heads, HEAD_DIM, SEQ)
        return a.transpose(0, 1, 4, 2, 3)

    return (x_p.reshape(BATCH, SEQ, D_MODEL), x_s.reshape(DEC_BATCH, DEC_SEQ, D_MODEL),
            stack_t(0, NA_HEADS), stack_t(1, NA_HEADS), stack(2, MLA_KV_RANK), stack(3, MLA_ROPE),
            stack_t(4, SWA_KV_HEADS), stack_t(5, SWA_KV_HEADS))
```

```python
import functools

import jax
import jax.numpy as jnp
import numpy as np
from jax import lax
from jax.experimental import pallas as pl
from jax.experimental.pallas import tpu as pltpu

F32 = jnp.float32
BF16 = jnp.bfloat16

D_MODEL = 1024
BATCH = 32
SEQ = 256
DEPTH = 2
DEC_BATCH = 2
DEC_SEQ = 2048
PAST_LEN = 256
GRID_W = 64
HEAD_DIM = 64
NA_HEADS = 4
NA_WIN_R = 8
NA_WIN_C = 16
MLA_HEADS = 4
MLA_NOPE = 64
MLA_ROPE = 32
MLA_V = 64
MLA_Q_RANK = 192
MLA_KV_RANK = 128
SWA_HEADS = 4
SWA_KV_HEADS = 2
SWA_WINDOW = 128
POOL_WINDOWS = (2, 4, 8, 16)
POOL_GROUP = 64
POOL_WIDTH = 256
BRANCH_W = 256
N_BRANCH = 4
D_FF = 2752
ROPE_BASE = 10000.0
EPS = 1e-6
NEG_INF = -1e30
ATT_SCALE = HEAD_DIM ** -0.5
MLA_SCALE = (MLA_NOPE + MLA_ROPE) ** -0.5
LOG2E = 1.4426950408889634

LANE = 128
D_FF_PAD = 2816
FF_CHUNK = 256
Q_BLOCK = 128
NA_Q_BLOCK = 256
NA_SPAN = 768
NA_DR = 2 * NA_WIN_R - 1
NA_DC = 2 * NA_WIN_C - 1
SWA_SPAN = 384
MLA_Q_BLOCK = 256
CTX_SEQS = 4
VMEM_LIMIT = 56 * 1024 * 1024

_QA, _KA, _VA, _CQ, _CKV, _KR, _QC, _KC, _VC, _PD = 0, 256, 512, 768, 1024, 1152, 1280, 1536, 1664, 1792
_KR_LANE = 64


def _dot(a, b):
    return jnp.dot(a, b, preferred_element_type=F32)


def _dot_nt(a, b):
    return lax.dot_general(a, b, (((1,), (1,)), ((), ())), preferred_element_type=F32)


def _sigmoid(x):
    return 1.0 / (1.0 + jnp.exp(-x))


def _rms(x, g, n=None):
    n = x.shape[-1] if n is None else n
    ms = jnp.sum(x * x, -1, keepdims=True) * (1.0 / n)
    return x * lax.rsqrt(ms + EPS) * g


def _softmax_blocks(blocks, sink=None):
    m = None
    for s in blocks:
        mm = jnp.max(s, -1, keepdims=True)
        m = mm if m is None else jnp.maximum(m, mm)
    if sink is not None:
        m = jnp.maximum(m, sink)
    es = [jnp.exp2(s - m) for s in blocks]
    l = None
    for e in es:
        ll = jnp.sum(e, -1, keepdims=True)
        l = ll if l is None else l + ll
    if sink is not None:
        l = l + jnp.exp2(sink - m)
    return es, l


def _rope(x, cos, sin, q):
    w = x.shape[-1]
    lane = lax.broadcasted_iota(jnp.int32, x.shape, 1)
    up = pltpu.roll(x, w - q, axis=1)
    dn = pltpu.roll(x, q, axis=1)
    partner = jnp.where((lane & (2 * q - 1)) < q, up, dn)
    return x * cos + partner * sin


def _const_spec(shape):
    n = len(shape)
    return pl.BlockSpec(shape, lambda *_: (0,) * n, pipeline_mode=pl.Buffered(1))


def _layer_spec(arr, layer):
    n = arr.ndim - 1
    return pl.BlockSpec((None,) + arr.shape[1:], lambda *_: (layer,) + (0,) * n, pipeline_mode=pl.Buffered(1))


def _params(n_axes):
    return pltpu.CompilerParams(dimension_semantics=("arbitrary",) * n_axes, vmem_limit_bytes=VMEM_LIMIT)


def _mod_kernel(cv_ref, w_ref, b_ref, o_ref):
    cv = cv_ref[...]
    a = (cv * _sigmoid(cv)).astype(BF16)
    o_ref[0] = _dot(a, w_ref[0].astype(BF16)) + b_ref[0]


def _mod_call(cv, w_mod, b_mod):
    tn = 2048
    return pl.pallas_call(
        _mod_kernel,
        grid=(DEPTH, 6 * D_MODEL // tn),
        in_specs=[_const_spec((8, D_MODEL)),
                  pl.BlockSpec((1, D_MODEL, tn), lambda l, j: (l, 0, j)),
                  pl.BlockSpec((1, 1, tn), lambda l, j: (l, 0, j))],
        out_specs=pl.BlockSpec((1, 8, tn), lambda l, j: (l, 0, j)),
        out_shape=jax.ShapeDtypeStruct((DEPTH, 8, 6 * D_MODEL), F32),
        compiler_params=_params(2),
        name="mod",
    )(cv, w_mod, b_mod.reshape(DEPTH, 1, 6 * D_MODEL))


_IN_SLOTS = ((_QA, 256), (_KA, 256), (_VA, 256), (_CQ, 256), (_CKV, 128), (_KR, 128),
             (_QC, 256), (_KC, 128), (_VC, 128), (_PD, 256))
_CTX_SLOTS = ((_QA, 256, BF16, False), (_KA, 256, F32, True), (_VA, 256, F32, True), (_VA, 256, BF16, False),
              (_CQ, 256, BF16, False), (_CKV, 128, F32, False), (_KR, 128, F32, False),
              (_QC, 256, BF16, False), (_KC, 128, F32, True), (_VC, 128, F32, True), (_VC, 128, BF16, False),
              (_PD, 256, BF16, False))
_LAT_SLOTS = tuple((off, wd, BF16, False) for off, wd in _IN_SLOTS)


def _inproj_kernel(*refs, latent):
    if latent:
        x_ref, mod_ref, g_ref, w_ref, c64_ref, s64_ref, cm_ref, sm_ref = refs[:8]
        outs = refs[8:]
    else:
        x_ref, mod_ref, g_ref, w_ref, pw_ref, ps_ref = refs[:6]
        outs = refs[6:]
    x = x_ref[...]
    h = _rms(x, g_ref[...]) * (1.0 + mod_ref[0, 1:2, :]) + mod_ref[0, 0:1, :]
    p = _dot(h.astype(BF16), w_ref[...])
    for (off, wd, _, transposed), o_ref in zip(_LAT_SLOTS if latent else _CTX_SLOTS, outs):
        v = p[:, off:off + wd]
        if latent:
            if off == _QC:
                v = _rope(v, c64_ref[...], s64_ref[...], 16)
            elif off == _KC:
                v = _rope(v, c64_ref[:, :128], s64_ref[:, :128], 16)
            elif off == _KR:
                v = _rope(v, cm_ref[:, :128], sm_ref[:, :128], 8)
        if transposed:
            for b in range(o_ref.shape[0]):
                o_ref[b] = v[b * SEQ:(b + 1) * SEQ].T.astype(o_ref.dtype)
        elif off == _PD and not latent:
            ys = _pool_mix([v[b * SEQ:(b + 1) * SEQ] for b in range(v.shape[0] // SEQ)], pw_ref, ps_ref)
            for b, y in enumerate(ys):
                o_ref[b * SEQ:(b + 1) * SEQ, :] = y.astype(o_ref.dtype)
        else:
            o_ref[...] = v.astype(o_ref.dtype)


def _inproj_call(x2d, mod, w, layer, rope_tabs=None):
    t = x2d.shape[0]
    tm = 512
    latent = rope_tabs is not None
    slots = _LAT_SLOTS if latent else _CTX_SLOTS
    tiles_per_mod = t // tm // mod.shape[0]
    in_specs = [pl.BlockSpec((tm, D_MODEL), lambda i: (i, 0)),
                pl.BlockSpec((1, 6, D_MODEL), lambda i: (i // tiles_per_mod, 0, 0)),
                _layer_spec(w["g_attn_pre"], layer), _layer_spec(w["w_in_p"], layer)]
    args = [x2d, mod, w["g_attn_pre"], w["w_in_p"]]
    if latent:
        tiles_per_seq = DEC_SEQ // tm
        c64, s64, cm, sm = rope_tabs
        in_specs += [pl.BlockSpec((tm, 256), lambda i: (i % tiles_per_seq, 0)),
                     pl.BlockSpec((tm, 256), lambda i: (i % tiles_per_seq, 0)),
                     pl.BlockSpec((tm, 512), lambda i: (i % tiles_per_seq, 0)),
                     pl.BlockSpec((tm, 512), lambda i: (i % tiles_per_seq, 0))]
        args += [c64, s64, cm, sm]
    else:
        in_specs += [_layer_spec(w["pool_w"], layer), _layer_spec(w["pool_scale"], layer)]
        args += [w["pool_w"], w["pool_scale"]]
    return pl.pallas_call(
        functools.partial(_inproj_kernel, latent=latent),
        grid=(t // tm,),
        in_specs=in_specs,
        out_specs=[pl.BlockSpec((tm // SEQ, wd, SEQ), lambda i: (i, 0, 0)) if tr
                   else pl.BlockSpec((tm, wd), lambda i: (i, 0)) for _, wd, _, tr in slots],
        out_shape=[jax.ShapeDtypeStruct((t // SEQ, wd, SEQ) if tr else (t, wd), dt) for _, wd, dt, tr in slots],
        compiler_params=_params(1),
        name="inproj_lat" if latent else "inproj_ctx",
    )(*args)


def _ctx_attn_kernel(q_ref, kt_ref, v_ref, o_ref, *, seqs):
    q = q_ref[...]
    kt = kt_ref[...].astype(BF16)
    v = v_ref[...]
    scores = []
    for s in range(seqs):
        rows = slice(s * SEQ, (s + 1) * SEQ)
        for h in range(NA_HEADS):
            sl = slice(h * HEAD_DIM, (h + 1) * HEAD_DIM)
            scores.append(_dot(q[rows, sl], kt[s, sl, :]))
    probs = [_softmax_blocks([sc]) for sc in scores]
    outs = []
    for s in range(seqs):
        rows = slice(s * SEQ, (s + 1) * SEQ)
        ys = []
        for h in range(NA_HEADS):
            sl = slice(h * HEAD_DIM, (h + 1) * HEAD_DIM)
            (e,), l = probs[s * NA_HEADS + h]
            ys.append(_dot(e.astype(BF16), v[rows, sl]) / l)
        outs.append(jnp.concatenate(ys, -1))
    o_ref[...] = (outs[0] if seqs == 1 else jnp.concatenate(outs, 0)).astype(o_ref.dtype)


def _ctx_attn_call(q, k, v):
    seqs = CTX_SEQS
    t = q.shape[0]
    spec = pl.BlockSpec((seqs * SEQ, 256), lambda b: (b, 0))
    kt_spec = pl.BlockSpec((seqs, 256, SEQ), lambda b: (b, 0, 0))
    return pl.pallas_call(
        functools.partial(_ctx_attn_kernel, seqs=seqs), grid=(t // SEQ // seqs,),
        in_specs=[spec, kt_spec, spec], out_specs=spec,
        out_shape=jax.ShapeDtypeStruct((t, 256), BF16), compiler_params=_params(1), name="ctx_attn",
    )(q, k, v)


def _gqa_operands(q, k, rows_q, rows_k):
    group = SWA_HEADS // SWA_KV_HEADS
    out = []
    for kv in range(SWA_KV_HEADS):
        qs = jnp.concatenate([q[rows_q, (kv * group + g) * HEAD_DIM:(kv * group + g + 1) * HEAD_DIM]
                              for g in range(group)], 0)
        out.append((qs, k[rows_k, kv * HEAD_DIM:(kv + 1) * HEAD_DIM]))
    return out


def _gqa_sink(sink_ref, kv, m):
    group = SWA_HEADS // SWA_KV_HEADS
    row = lax.broadcasted_iota(jnp.int32, (group * m, 1), 0)
    col = jnp.full((group * m, 1), sink_ref[kv * group + group - 1] * LOG2E, F32)
    for g in range(group - 2, -1, -1):
        col = jnp.where(row < (g + 1) * m, sink_ref[kv * group + g] * LOG2E, col)
    return col


def _ctx_swa_kernel(sink_ref, q_ref, kt_ref, v_ref, o_ref, *, seqs):
    q = q_ref[...]
    kt = kt_ref[...].astype(BF16)
    v = v_ref[...]
    group = SWA_HEADS // SWA_KV_HEADS
    scores = []
    for s in range(seqs):
        rows = slice(s * SEQ, (s + 1) * SEQ)
        for kv, (qs, _) in enumerate(_gqa_operands(q, q, rows, rows)):
            scores.append(_dot(qs, kt[s, kv * HEAD_DIM:(kv + 1) * HEAD_DIM, :]))
    probs = [_softmax_blocks([sc], sink=_gqa_sink(sink_ref, i % SWA_KV_HEADS, SEQ)) for i, sc in enumerate(scores)]
    outs = []
    for s in range(seqs):
        rows = slice(s * SEQ, (s + 1) * SEQ)
        ys = []
        for kv in range(SWA_KV_HEADS):
            (e,), l = probs[s * SWA_KV_HEADS + kv]
            y = _dot(e.astype(BF16), v[rows, kv * HEAD_DIM:(kv + 1) * HEAD_DIM]) / l
            ys += [y[g * SEQ:(g + 1) * SEQ] for g in range(group)]
        outs.append(jnp.concatenate(ys, -1))
    o_ref[...] = (outs[0] if seqs == 1 else jnp.concatenate(outs, 0)).astype(o_ref.dtype)


def _ctx_swa_call(sink, q, k, v):
    seqs = CTX_SEQS
    t = q.shape[0]
    rows = seqs * SEQ
    return pl.pallas_call(
        functools.partial(_ctx_swa_kernel, seqs=seqs), grid=(t // rows,),
        in_specs=[pl.BlockSpec(memory_space=pltpu.SMEM),
                  pl.BlockSpec((rows, 256), lambda b: (b, 0)),
                  pl.BlockSpec((seqs, 128, SEQ), lambda b: (b, 0, 0)),
                  pl.BlockSpec((rows, 128), lambda b: (b, 0))],
        out_specs=pl.BlockSpec((rows, 256), lambda b: (b, 0)),
        out_shape=jax.ShapeDtypeStruct((t, 256), BF16), compiler_params=_params(1), name="ctx_swa",
    )(sink, q, k, v)


def _mla_q(cq, qn_ref, wuq_ref):
    return _dot(_rms(cq, qn_ref[...], MLA_Q_RANK).astype(BF16), wuq_ref[...])


def _mla_kv(ckv, kr, kvn_ref, wuk_ref, wuv_ref, values_t=False):
    cn = _rms(ckv, kvn_ref[...]).astype(BF16)
    kcat = _dot(cn, wuk_ref[...]) + jnp.concatenate([kr] * MLA_HEADS, -1)
    return kcat, (_dot_nt(wuv_ref[...], cn) if values_t else _dot(cn, wuv_ref[...]))


def _ctx_mla_kernel(cq_ref, ckv_ref, kr_ref, qn_ref, wuq_ref, kvn_ref, wuk_ref, wuv_ref, o_ref, *, seqs):
    q = (_mla_q(cq_ref[...].astype(F32), qn_ref, wuq_ref) * (MLA_SCALE * LOG2E)).astype(BF16)
    kcat, v = _mla_kv(ckv_ref[...], kr_ref[...], kvn_ref, wuk_ref, wuv_ref)
    kcat = kcat.astype(BF16)
    v = v.astype(BF16)
    scores = []
    for s in range(seqs):
        rows = slice(s * SEQ, (s + 1) * SEQ)
        for h in range(MLA_HEADS):
            sl = slice(h * LANE, (h + 1) * LANE)
            scores.append(_dot_nt(q[rows, sl], kcat[rows, sl]))
    probs = [_softmax_blocks([sc]) for sc in scores]
    outs = []
    for s in range(seqs):
        rows = slice(s * SEQ, (s + 1) * SEQ)
        ys = []
        for h in range(MLA_HEADS):
            (e,), l = probs[s * MLA_HEADS + h]
            ys.append(_dot(e.astype(BF16), v[rows, h * MLA_V:(h + 1) * MLA_V]) / l)
        outs.append(jnp.concatenate(ys, -1))
    o_ref[...] = (outs[0] if seqs == 1 else jnp.concatenate(outs, 0)).astype(o_ref.dtype)


_MLA_W = ("mla_qn", "mla_wuq", "mla_kvn", "mla_wuk", "mla_wuv")
_MLA_W_LAT = _MLA_W[:-1] + ("mla_wuv_t",)


def _ctx_mla_call(cq, ckv, kr, w, layer):
    seqs = CTX_SEQS
    t = cq.shape[0]
    rows = seqs * SEQ
    return pl.pallas_call(
        functools.partial(_ctx_mla_kernel, seqs=seqs), grid=(t // rows,),
        in_specs=[pl.BlockSpec((rows, 256), lambda b: (b, 0)),
                  pl.BlockSpec((rows, 128), lambda b: (b, 0)),
                  pl.BlockSpec((rows, 128), lambda b: (b, 0))] + [_layer_spec(w[k], layer) for k in _MLA_W],
        out_specs=pl.BlockSpec((rows, 256), lambda b: (b, 0)),
        out_shape=jax.ShapeDtypeStruct((t, 256), BF16), compiler_params=_params(1), name="ctx_mla",
    )(cq, ckv, kr, *[w[k] for k in _MLA_W])


_POOL_PAD = 8


def _pool_mix(xs, w_ref, sc_ref):
    n = xs[0].shape[0]
    ne = n + 2 * _POOL_PAD
    lo, hi = _POOL_PAD, _POOL_PAD + n
    z = jnp.zeros((_POOL_PAD, POOL_WIDTH), F32)
    grp = lax.broadcasted_iota(jnp.int32, (n, POOL_WIDTH), 1) >> 6
    t = lax.broadcasted_iota(jnp.int32, (n, POOL_WIDTH), 0)
    half = jnp.where(grp == 0, 1, jnp.where(grp == 1, 2, jnp.where(grp == 2, 4, 8)))
    cnt = (jnp.minimum(t + half, n) - jnp.maximum(t - half, 0)).astype(F32)

    def pair(a, s):
        return pltpu.roll(a, s, axis=0) + pltpu.roll(a, ne - s, axis=0)

    out = []
    for x in xs:
        xz = jnp.concatenate([z, x, z], 0)
        s2 = xz + pltpu.roll(xz, 1, axis=0)
        s4 = pair(s2, 1)
        s8 = pair(s4, 2)
        s16 = pair(s8, 4)
        tot = jnp.where(grp == 0, s2[lo:hi],
                        jnp.where(grp == 1, s4[lo:hi], jnp.where(grp == 2, s8[lo:hi], s16[lo:hi])))
        dlt = (tot / cnt - x).astype(BF16)
        out.append(_dot(dlt, w_ref[...]) * sc_ref[...])
    return out


def _pool_kernel(x_ref, w_ref, sc_ref, o_ref, *, n):
    xs = [x_ref[s * n:(s + 1) * n, :].astype(F32) for s in range(x_ref.shape[0] // n)]
    for s, y in enumerate(_pool_mix(xs, w_ref, sc_ref)):
        o_ref[s * n:(s + 1) * n, :] = y.astype(o_ref.dtype)


def _pool_call(pd, w, layer, seq):
    t = pd.shape[0]
    rows = seq
    return pl.pallas_call(
        functools.partial(_pool_kernel, n=seq), grid=(t // rows,),
        in_specs=[pl.BlockSpec((rows, POOL_WIDTH), lambda b: (b, 0)),
                  _layer_spec(w["pool_w"], layer), _layer_spec(w["pool_scale"], layer)],
        out_specs=pl.BlockSpec((rows, POOL_WIDTH), lambda b: (b, 0)),
        out_shape=jax.ShapeDtypeStruct((t, POOL_WIDTH), BF16), compiler_params=_params(1), name="pool",
    )(pd, w["pool_w"], w["pool_scale"])


def _lat_na_kernel(q_ref, k_ref, v_ref, kc_ref, vc_ref, t2_ref, o_ref):
    n = pl.program_id(1)
    rows = DEC_SEQ // GRID_W
    q_rows = NA_Q_BLOCK // GRID_W
    row0 = jnp.clip(q_rows * n - NA_WIN_R // 2, 0, rows - NA_SPAN // GRID_W)
    start = pl.multiple_of(row0 * GRID_W, LANE)
    q = q_ref[...]
    k = k_ref[pl.ds(start, NA_SPAN), :]
    v = v_ref[pl.ds(start, NA_SPAN), :]
    kc = kc_ref[...].astype(BF16)
    vc = vc_ref[...].astype(BF16)

    pairs = NA_SPAN // LANE
    low_half = lax.broadcasted_iota(jnp.int32, (GRID_W, LANE), 1) < GRID_W
    entries, masks = [], []
    for a in range(q_rows):
        r = q_rows * n + a
        r_start = jnp.clip(r - NA_WIN_R // 2, 0, rows - NA_WIN_R)
        for p in range(pairs):
            rk = row0 + 2 * p
            ok0 = ((rk >= r_start) & (rk < r_start + NA_WIN_R)).astype(jnp.int32)
            ok1 = ((rk + 1 >= r_start) & (rk + 1 < r_start + NA_WIN_R)).astype(jnp.int32)
            entries.append(jnp.clip(rk - r + NA_WIN_R, 0, NA_DR))
            masks.append(jnp.where(low_half, ok0, ok1) > 0)

    heads = [slice(h * HEAD_DIM, (h + 1) * HEAD_DIM) for h in range(NA_HEADS)]
    raw = [(_dot_nt(q[:, sl], k[:, sl]), _dot_nt(q[:, sl], kc[:, sl])) for sl in heads]
    probs = []
    for h, (s_loc, s_ctx) in enumerate(raw):
        cols = []
        for p in range(pairs):
            blk = [jnp.where(masks[a * pairs + p], t2_ref[h, entries[a * pairs + p]], NEG_INF)
                   for a in range(q_rows)]
            cols.append(jnp.concatenate(blk, 0))
        probs.append(_softmax_blocks([s_loc + jnp.concatenate(cols, 1), s_ctx]))
    ys = []
    for sl, ((e_loc, e_ctx), l) in zip(heads, probs):
        y = _dot(e_loc.astype(BF16), v[:, sl]) + _dot(e_ctx.astype(BF16), vc[:, sl])
        ys.append(y / l)
    o_ref[...] = jnp.concatenate(ys, -1).astype(o_ref.dtype)


def _lat_na_call(q, k, v, cache_k, cache_v, t2, layer):
    nq = DEC_SEQ // NA_Q_BLOCK
    seq_spec = pl.BlockSpec((DEC_SEQ, 256), lambda b, n: (b, 0))
    cache_spec = pl.BlockSpec((None, None, PAST_LEN, 256), lambda b, n: (b, layer, 0, 0))
    return pl.pallas_call(
        _lat_na_kernel, grid=(DEC_BATCH, nq),
        in_specs=[pl.BlockSpec((NA_Q_BLOCK, 256), lambda b, n: (b * nq + n, 0)), seq_spec, seq_spec,
                  cache_spec, cache_spec, _layer_spec(t2, layer)],
        out_specs=pl.BlockSpec((NA_Q_BLOCK, 256), lambda b, n: (b * nq + n, 0)),
        out_shape=jax.ShapeDtypeStruct((DEC_BATCH * DEC_SEQ, 256), BF16),
        compiler_params=_params(2), name="lat_na",
    )(q, k, v, cache_k, cache_v, t2)


def _lat_swa_kernel(sink_ref, q_ref, k_ref, v_ref, kc_ref, vc_ref, o_ref):
    n = pl.program_id(1)
    start = pl.multiple_of(jnp.clip(n - 1, 0, DEC_SEQ // Q_BLOCK - 3) * Q_BLOCK, LANE)
    q = q_ref[...]
    k = k_ref[pl.ds(start, SWA_SPAN), :]
    v = v_ref[pl.ds(start, SWA_SPAN), :]
    kc = kc_ref[...].astype(BF16)
    vc = vc_ref[...].astype(BF16)
    group = SWA_HEADS // SWA_KV_HEADS
    m = group * Q_BLOCK
    q_pos = n * Q_BLOCK + (lax.broadcasted_iota(jnp.int32, (m, SWA_SPAN), 0) & (Q_BLOCK - 1))
    k_pos = start + lax.broadcasted_iota(jnp.int32, (m, SWA_SPAN), 1)
    valid = jnp.abs(q_pos - k_pos) <= SWA_WINDOW
    everything = slice(None)
    raw = []
    for (qs, ks), (_, kcs) in zip(_gqa_operands(q, k, everything, everything),
                                  _gqa_operands(q, kc, everything, everything)):
        raw.append((_dot_nt(qs, ks), _dot_nt(qs, kcs)))
    probs = [_softmax_blocks([jnp.where(valid, s_loc, NEG_INF), s_ctx], sink=_gqa_sink(sink_ref, kv, Q_BLOCK))
             for kv, (s_loc, s_ctx) in enumerate(raw)]
    ys = []
    for kv, ((e_loc, e_ctx), l) in enumerate(probs):
        kvsl = slice(kv * HEAD_DIM, (kv + 1) * HEAD_DIM)
        y = (_dot(e_loc.astype(BF16), v[:, kvsl]) + _dot(e_ctx.astype(BF16), vc[:, kvsl])) / l
        ys += [y[g * Q_BLOCK:(g + 1) * Q_BLOCK] for g in range(group)]
    o_ref[...] = jnp.concatenate(ys, -1).astype(o_ref.dtype)


def _lat_swa_call(sink, q, k, v, cache_k, cache_v, layer):
    nq = DEC_SEQ // Q_BLOCK
    seq_spec = pl.BlockSpec((DEC_SEQ, 128), lambda b, n: (b, 0))
    cache_spec = pl.BlockSpec((None, None, PAST_LEN, 128), lambda b, n: (b, layer, 0, 0))
    return pl.pallas_call(
        _lat_swa_kernel, grid=(DEC_BATCH, nq),
        in_specs=[pl.BlockSpec(memory_space=pltpu.SMEM),
                  pl.BlockSpec((Q_BLOCK, 256), lambda b, n: (b * nq + n, 0)), seq_spec, seq_spec,
                  cache_spec, cache_spec],
        out_specs=pl.BlockSpec((Q_BLOCK, 256), lambda b, n: (b * nq + n, 0)),
        out_shape=jax.ShapeDtypeStruct((DEC_BATCH * DEC_SEQ, 256), BF16),
        compiler_params=_params(2), name="lat_swa",
    )(sink, q, k, v, cache_k, cache_v)


def _lat_mla_kernel(cq_ref, ckv_ref, kr_ref, cckv_ref, ckr_ref, cm_ref, sm_ref,
                    qn_ref, wuq_ref, kvn_ref, wuk_ref, wuvt_ref, o_ref, kcat_s, vt_s):
    @pl.when(pl.program_id(1) == 0)
    def _():
        kc, vc = _mla_kv(cckv_ref[...], ckr_ref[...], kvn_ref, wuk_ref, wuvt_ref, True)
        kcat_s[0:PAST_LEN, :] = kc.astype(BF16)
        vt_s[:, 0:PAST_LEN] = vc.astype(BF16)
        kl, vl = _mla_kv(ckv_ref[...].astype(F32), kr_ref[...].astype(F32), kvn_ref, wuk_ref, wuvt_ref, True)
        kcat_s[PAST_LEN:, :] = kl.astype(BF16)
        vt_s[:, PAST_LEN:] = vl.astype(BF16)

    q = _mla_q(cq_ref[...].astype(F32), qn_ref, wuq_ref)
    q = (_rope(q, cm_ref[...], sm_ref[...], 8) * (MLA_SCALE * LOG2E)).astype(BF16)
    scores = [_dot_nt(kcat_s[:, h * LANE:(h + 1) * LANE], q[:, h * LANE:(h + 1) * LANE]) for h in range(MLA_HEADS)]
    ys = []
    for h, st in enumerate(scores):
        e = jnp.exp2(st - jnp.max(st, 0, keepdims=True))
        l = jnp.sum(e, 0, keepdims=True)
        ys.append(_dot(vt_s[h * MLA_V:(h + 1) * MLA_V, :], e.astype(BF16)) / l)
    o_ref[...] = jnp.concatenate(ys, 0).T.astype(o_ref.dtype)


def _lat_mla_call(cq, ckv, kr, cache_ckv, cache_kr, cm, sm, w, layer):
    qb = MLA_Q_BLOCK
    nq = DEC_SEQ // qb
    seq_spec = pl.BlockSpec((DEC_SEQ, 128), lambda b, n: (b, 0))
    cache_spec = pl.BlockSpec((None, None, PAST_LEN, 128), lambda b, n: (b, layer, 0, 0))
    tab_spec = pl.BlockSpec((qb, 512), lambda b, n: (n, 0))
    return pl.pallas_call(
        _lat_mla_kernel, grid=(DEC_BATCH, nq),
        in_specs=[pl.BlockSpec((qb, 256), lambda b, n: (b * nq + n, 0)), seq_spec, seq_spec,
                  cache_spec, cache_spec, tab_spec, tab_spec] + [_layer_spec(w[k], layer) for k in _MLA_W_LAT],
        out_specs=pl.BlockSpec((qb, 256), lambda b, n: (b * nq + n, 0)),
        out_shape=jax.ShapeDtypeStruct((DEC_BATCH * DEC_SEQ, 256), BF16),
        scratch_shapes=[pltpu.VMEM((PAST_LEN + DEC_SEQ, MLA_HEADS * LANE), BF16),
                        pltpu.VMEM((MLA_HEADS * MLA_V, PAST_LEN + DEC_SEQ), BF16)],
        compiler_params=_params(2), name="lat_mla",
    )(cq, ckv, kr, cache_ckv, cache_kr, cm, sm, *[w[k] for k in _MLA_W_LAT])


_MERGE_PARTS = 2


def _merge_kernel(x_ref, mod_ref, gpre_ref, ya_ref, yb_ref, yc_ref, yd_ref,
                  wg_ref, bg_ref, wb_ref, wo_ref, gpost_ref, o_ref):
    tm = x_ref.shape[0] // _MERGE_PARTS
    for p in range(_MERGE_PARTS):
        rows = slice(p * tm, (p + 1) * tm)
        x = x_ref[rows, :]
        h = (_rms(x, gpre_ref[...]) * (1.0 + mod_ref[0, 1:2, :]) + mod_ref[0, 0:1, :]).astype(BF16)
        merged = None
        for k, y_ref in enumerate((ya_ref, yb_ref, yc_ref, yd_ref)):
            cols = slice(k * D_MODEL, (k + 1) * D_MODEL)
            gate = _sigmoid(_dot(h, wg_ref[:, cols]) + bg_ref[:, cols])
            term = gate * _dot(y_ref[rows, :], wb_ref[k])
            merged = term if merged is None else merged + term
        o = _dot(merged.astype(BF16), wo_ref[...])
        o_ref[rows, :] = x + mod_ref[0, 2:3, :] * _rms(o, gpost_ref[...])


_MERGE_W = ("w_gate", "b_gate", "w_branch", "w_out", "g_attn_post")


def _merge_call(x2d, mod, ys, w, layer):
    t = x2d.shape[0]
    tm = 512
    tiles_per_mod = t // tm // mod.shape[0]
    tile = pl.BlockSpec((tm, D_MODEL), lambda i: (i, 0))
    ytile = pl.BlockSpec((tm, BRANCH_W), lambda i: (i, 0))
    return pl.pallas_call(
        _merge_kernel, grid=(t // tm,),
        in_specs=[tile, pl.BlockSpec((1, 6, D_MODEL), lambda i: (i // tiles_per_mod, 0, 0)),
                  _layer_spec(w["g_attn_pre"], layer), ytile, ytile, ytile, ytile]
                 + [_layer_spec(w[k], layer) for k in _MERGE_W],
        out_specs=tile,
        out_shape=jax.ShapeDtypeStruct((t, D_MODEL), F32),
        compiler_params=_params(1), name="merge",
    )(x2d, mod, w["g_attn_pre"], *ys, *[w[k] for k in _MERGE_W])


_GAP = 8


def _ffn_kernel(*refs, seqs, halo):
    if halo:
        x_ref, xp_ref, xn_ref = refs[:3]
        refs = refs[3:]
    else:
        x_ref = refs[0]
        refs = refs[1:]
    mod_ref, gpre_ref, wa_ref, wg_ref, ca_ref, cg_ref, wd_ref, gpost_ref, o_ref = refs
    x = x_ref[...]
    tm = x.shape[0]
    shift, scale, gate = mod_ref[0, 3:4, :], mod_ref[0, 4:5, :], mod_ref[0, 5:6, :]

    def pre(xx):
        return _rms(xx, gpre_ref[...]) * (1.0 + scale) + shift

    h = pre(x)
    if halo:
        i = pl.program_id(0) % halo
        hp = jnp.where(i == 0, 0.0, pre(xp_ref[...]))
        hn = jnp.where(i == halo - 1, 0.0, pre(xn_ref[...]))
        pieces = [hp, h, hn]
        starts = [_GAP]
        seq_len = tm
    else:
        seq_len = tm // seqs
        pieces = [h]
        starts = [s * seq_len for s in range(seqs)]
    hb = jnp.concatenate(pieces, 0).astype(BF16)
    rows = hb.shape[0]
    edge_row = lax.broadcasted_iota(jnp.int32, (_GAP, FF_CHUNK), 0)

    def zero_edges(a, first):
        out = []
        for st in starts:
            seg = a[st:st + seq_len]
            if first:
                out += [jnp.where(edge_row == 0, 0.0, seg[:_GAP]), seg[_GAP:]]
            else:
                out += [seg[:seq_len - _GAP], jnp.where(edge_row == _GAP - 1, 0.0, seg[seq_len - _GAP:])]
        return jnp.concatenate(out, 0)

    def up(c):
        cols = slice(c * FF_CHUNK, (c + 1) * FF_CHUNK)
        return _dot(hb, wa_ref[:, cols]), _dot(hb, wg_ref[:, cols])

    def conv(u, c_ref, cols):
        prev = pltpu.roll(u, 1, axis=0)
        nxt = pltpu.roll(u, rows - 1, axis=0)
        if not halo:
            prev, nxt = zero_edges(prev, True), zero_edges(nxt, False)
        return prev * c_ref[0:1, cols] + u * c_ref[1:2, cols] + nxt * c_ref[2:3, cols]

    acts = []
    for c in range(D_FF_PAD // FF_CHUNK):
        ua, ug = up(c)
        cols = slice(c * FF_CHUNK, (c + 1) * FF_CHUNK)
        a = conv(ua, ca_ref, cols)
        g = conv(ug, cg_ref, cols)
        acts.append((g * _sigmoid(g) * a).astype(BF16))
    acc = _dot(jnp.concatenate(acts, 1), wd_ref[...])
    for s, st in enumerate(starts):
        ys = _rms(acc[st:st + seq_len], gpost_ref[...])
        o_ref[s * seq_len:(s + 1) * seq_len, :] = x[s * seq_len:(s + 1) * seq_len] + gate * ys


_FFN_W = ("g_ffn_pre", "ffn_wa", "ffn_wg", "ffn_ca", "ffn_cg", "ffn_wd", "g_ffn_post")


def _ffn_call(x2d, mod, w, layer, seq, tm):
    t = x2d.shape[0]
    tile = pl.BlockSpec((tm, D_MODEL), lambda i: (i, 0))
    in_specs = [tile]
    args = [x2d]
    if tm < seq:
        halo, seqs = seq // tm, 1
        r = tm // _GAP
        last = t // _GAP - 1
        in_specs += [pl.BlockSpec((_GAP, D_MODEL), lambda i: (jnp.maximum(i * r - 1, 0), 0)),
                     pl.BlockSpec((_GAP, D_MODEL), lambda i: (jnp.minimum((i + 1) * r, last), 0))]
        args += [x2d, x2d]
    else:
        halo, seqs = 0, tm // seq
    tiles_per_mod = t // tm // mod.shape[0]
    in_specs += [pl.BlockSpec((1, 6, D_MODEL), lambda i: (i // tiles_per_mod, 0, 0))]
    in_specs += [_layer_spec(w[k], layer) for k in _FFN_W]
    args += [mod] + [w[k] for k in _FFN_W]
    return pl.pallas_call(
        functools.partial(_ffn_kernel, seqs=seqs, halo=halo), grid=(t // tm,),
        in_specs=in_specs, out_specs=tile,
        out_shape=jax.ShapeDtypeStruct((t, D_MODEL), F32),
        compiler_params=_params(1), name="ffn",
    )(*args)


def _rope_tables():
    t = np.arange(DEC_SEQ)
    pos = (t // GRID_W, t % GRID_W)

    def tab(d):
        half = d // 4
        inv = np.float32(ROPE_BASE) ** (-np.arange(half, dtype=np.float32) / np.float32(half))
        cs, sn = [], []
        for p in pos:
            ang = p.astype(np.float32)[:, None] * inv[None, :]
            cs += [np.cos(ang), np.cos(ang)]
            sn += [-np.sin(ang), np.sin(ang)]
        return np.concatenate(cs, -1), np.concatenate(sn, -1)

    c64, s64 = tab(HEAD_DIM)
    c32, s32 = tab(MLA_ROPE)
    pad = LANE - MLA_NOPE - MLA_ROPE
    cm = np.concatenate([np.ones((DEC_SEQ, MLA_NOPE), np.float32), c32, np.ones((DEC_SEQ, pad), np.float32)], -1)
    sm = np.concatenate([np.zeros((DEC_SEQ, MLA_NOPE), np.float32), s32, np.zeros((DEC_SEQ, pad), np.float32)], -1)
    return tuple(jnp.asarray(np.tile(x, (1, 4)), F32) for x in (c64, s64, cm, sm))


def _na_table_kernel(rpb_ref, e_ref, ok_ref, o_ref):
    r = rpb_ref[...]
    r1 = r.astype(BF16)
    r2 = (r - r1.astype(F32)).astype(BF16)
    r3 = (r - r1.astype(F32) - r2.astype(F32)).astype(BF16)
    e = e_ref[...]
    t = _dot(r1, e) + _dot(r2, e) + _dot(r3, e)
    o_ref[...] = jnp.where(ok_ref[...] > 0, t * LOG2E, NEG_INF)


def _na_bias_tables(na_rpb):
    c = np.arange(GRID_W)[:, None]
    w = np.arange(GRID_W)[None, :]
    dc = (w - c + NA_WIN_C - 1).reshape(-1)
    onehot = (np.arange(LANE)[:, None] == dc[None, :]).astype(np.float32)
    c_start = np.clip(c - NA_WIN_C // 2, 0, GRID_W - NA_WIN_C)
    ok = ((w >= c_start) & (w < c_start + NA_WIN_C)).reshape(1, -1).astype(np.int32)
    rows = DEPTH * NA_HEADS * NA_DR
    rpb2 = jnp.pad(na_rpb.reshape(rows, NA_DC), ((0, LANE - rows), (0, LANE - NA_DC)))
    t = pl.pallas_call(
        _na_table_kernel, out_shape=jax.ShapeDtypeStruct((LANE, GRID_W * GRID_W), F32), name="na_table",
        compiler_params=pltpu.CompilerParams(vmem_limit_bytes=VMEM_LIMIT),
    )(rpb2, jnp.asarray(onehot, BF16), jnp.asarray(ok))
    t = t[:rows].reshape(DEPTH, NA_HEADS, NA_DR, GRID_W, GRID_W)
    t = jnp.pad(t, ((0, 0), (0, 0), (1, 1), (0, 0), (0, 0)), constant_values=NEG_INF)
    return jnp.concatenate([t[:, :, :-1], t[:, :, 1:]], -1)


def _pad_last(w, n):
    return jnp.pad(w, ((0, 0),) * (w.ndim - 1) + ((0, n - w.shape[-1]),))


def _prep_weights(g_attn_pre, g_attn_post, g_ffn_pre, g_ffn_post, w_in, w_gate, b_gate, mla_q_norm, mla_w_uq,
                  mla_kv_norm, mla_w_ukv, pool_w, pool_scale, w_branch, w_out, ffn_w_up, ffn_conv, ffn_w_down):
    q_scale = jnp.concatenate([jnp.full((256,), ATT_SCALE * LOG2E, F32), jnp.ones((512,), F32)])
    b0, c0, d0 = 768, 1120, 1632
    kr = jnp.pad(w_in[:, :, b0 + 320:b0 + 352], ((0, 0), (0, 0), (_KR_LANE, LANE - _KR_LANE - MLA_ROPE)))
    w_in_p = jnp.concatenate([w_in[:, :, :b0] * q_scale, _pad_last(w_in[:, :, b0:b0 + MLA_Q_RANK], 256),
                              w_in[:, :, b0 + MLA_Q_RANK:b0 + MLA_Q_RANK + MLA_KV_RANK], kr,
                              w_in[:, :, c0:d0] * q_scale[:512], w_in[:, :, d0:]], 2).astype(BF16)

    wuq = mla_w_uq.reshape(DEPTH, MLA_Q_RANK, MLA_HEADS, MLA_NOPE + MLA_ROPE)
    wuq = jnp.pad(wuq, ((0, 0), (0, 256 - MLA_Q_RANK), (0, 0), (0, LANE - MLA_NOPE - MLA_ROPE)))
    wukv = mla_w_ukv.reshape(DEPTH, MLA_KV_RANK, MLA_HEADS, MLA_NOPE + MLA_V)
    wuk = _pad_last(wukv[..., :MLA_NOPE], LANE)
    wuv = wukv[..., MLA_NOPE:].reshape(DEPTH, MLA_KV_RANK, MLA_HEADS * MLA_V).astype(BF16)

    eye = np.eye(len(POOL_WINDOWS), dtype=np.float32)
    w_bd = (pool_w[:, :, :, None, :] * eye[None, :, None, :, None]).reshape(DEPTH, POOL_WIDTH, POOL_WIDTH)

    return dict(
        g_attn_pre=g_attn_pre[:, None, :], g_attn_post=g_attn_post[:, None, :],
        g_ffn_pre=g_ffn_pre[:, None, :], g_ffn_post=g_ffn_post[:, None, :],
        w_in_p=w_in_p,
        mla_qn=_pad_last(mla_q_norm[:, None, :], 256),
        mla_wuq=wuq.reshape(DEPTH, 256, MLA_HEADS * LANE).astype(BF16),
        mla_kvn=mla_kv_norm[:, None, :],
        mla_wuk=wuk.reshape(DEPTH, MLA_KV_RANK, MLA_HEADS * LANE).astype(BF16),
        mla_wuv=wuv, mla_wuv_t=wuv.transpose(0, 2, 1),
        pool_w=w_bd.astype(BF16), pool_scale=pool_scale[:, None, :],
        w_gate=w_gate.astype(BF16), b_gate=b_gate[:, None, :],
        w_branch=w_branch.astype(BF16), w_out=w_out.astype(BF16),
        ffn_wa=_pad_last(ffn_w_up[:, :, :D_FF], D_FF_PAD).astype(BF16),
        ffn_wg=_pad_last(ffn_w_up[:, :, D_FF:], D_FF_PAD).astype(BF16),
        ffn_ca=_pad_last(ffn_conv[:, :, :D_FF], D_FF_PAD), ffn_cg=_pad_last(ffn_conv[:, :, D_FF:], D_FF_PAD),
        ffn_wd=jnp.pad(ffn_w_down, ((0, 0), (0, D_FF_PAD - D_FF), (0, 0))).astype(BF16))


def kernel(x_prompt, x_sample, cache_na_k, cache_na_v, cache_mla_ckv, cache_mla_krope, cache_swa_k, cache_swa_v, c, c_ctx, w_mod, b_mod, g_attn_pre, g_attn_post, g_ffn_pre, g_ffn_post, w_in, w_gate, b_gate, na_rpb, mla_q_norm, mla_w_uq, mla_kv_norm, mla_w_ukv, swa_sink, pool_w, pool_scale, w_branch, w_out, ffn_w_up, ffn_conv, ffn_w_down):
    x_p = x_prompt.reshape(BATCH * SEQ, D_MODEL)
    x_s = x_sample.reshape(DEC_BATCH * DEC_SEQ, D_MODEL)

    cv = jnp.concatenate([c_ctx[None, :], c, jnp.zeros((8 - 1 - DEC_BATCH, D_MODEL), F32)], 0)
    mod = _mod_call(cv, w_mod, b_mod).reshape(DEPTH, 8, 6, D_MODEL)
    w = _prep_weights(g_attn_pre, g_attn_post, g_ffn_pre, g_ffn_post, w_in, w_gate, b_gate, mla_q_norm, mla_w_uq,
                      mla_kv_norm, mla_w_ukv, pool_w, pool_scale, w_branch, w_out, ffn_w_up, ffn_conv, ffn_w_down)
    rope_tabs = _rope_tables()
    na_t2 = _na_bias_tables(na_rpb)
    cache_na_k = cache_na_k.reshape(DEC_BATCH, DEPTH, PAST_LEN, 256)
    cache_na_v = cache_na_v.reshape(DEC_BATCH, DEPTH, PAST_LEN, 256)
    cache_swa_k = cache_swa_k.reshape(DEC_BATCH, DEPTH, PAST_LEN, 128)
    cache_swa_v = cache_swa_v.reshape(DEC_BATCH, DEPTH, PAST_LEN, 128)
    cache_kr = jnp.pad(cache_mla_krope, ((0, 0), (0, 0), (0, 0), (_KR_LANE, LANE - _KR_LANE - MLA_ROPE)))

    states = []
    for l in range(DEPTH):
        mod_p = mod[l, 0:1]
        qa, ka_t, va_t, va, cq, ckv, kr, qc, kc_t, vc_t, vc, yd = _inproj_call(x_p, mod_p, w, l)
        ys = (_ctx_attn_call(qa, ka_t, va), _ctx_mla_call(cq, ckv, kr, w, l),
              _ctx_swa_call(swa_sink[l], qc, kc_t, vc), yd)
        x_p = _merge_call(x_p, mod_p, ys, w, l)
        x_p = _ffn_call(x_p, mod_p, w, l, SEQ, 2 * SEQ)
        states.append((ka_t, va_t, ckv, kr[:, _KR_LANE:_KR_LANE + MLA_ROPE], kc_t, vc_t))

        mod_s = mod[l, 1:1 + DEC_BATCH]
        qa, ka, va, cq, ckv, kr, qc, kc, vc, pd = _inproj_call(x_s, mod_s, w, l, rope_tabs)
        ys = (_lat_na_call(qa, ka, va, cache_na_k, cache_na_v, na_t2, l),
              _lat_mla_call(cq, ckv, kr, cache_mla_ckv, cache_kr, rope_tabs[2], rope_tabs[3], w, l),
              _lat_swa_call(swa_sink[l], qc, kc, vc, cache_swa_k, cache_swa_v, l),
              _pool_call(pd, w, l, DEC_SEQ))
        x_s = _merge_call(x_s, mod_s, ys, w, l)
        x_s = _ffn_call(x_s, mod_s, w, l, DEC_SEQ, 512)

    def stack(j, *tail):
        return jnp.stack([st[j].reshape(BATCH, SEQ, *tail) for st in states], 1)

    def stack_t(j, heads):
        a = jnp.stack([st[j] for st in states], 1).reshape(BATCH, DEPTH, heads, HEAD_DIM, SEQ)
        return a.transpose(0, 1, 4, 2, 3)

    return (x_p.reshape(BATCH, SEQ, D_MODEL), x_s.reshape(DEC_BATCH, DEC_SEQ, D_MODEL),
            stack_t(0, NA_HEADS), stack_t(1, NA_HEADS), stack(2, MLA_KV_RANK), stack(3, MLA_ROPE),
            stack_t(4, SWA_KV_HEADS), stack_t(5, SWA_KV_HEADS))
```

```python
import functools

import jax
import jax.numpy as jnp
import numpy as np
from jax import lax
from jax.experimental import pallas as pl
from jax.experimental.pallas import tpu as pltpu

F32 = jnp.float32
BF16 = jnp.bfloat16

D_MODEL = 1024
BATCH = 32
SEQ = 256
DEPTH = 2
DEC_BATCH = 2
DEC_SEQ = 2048
PAST_LEN = 256
GRID_W = 64
HEAD_DIM = 64
NA_HEADS = 4
NA_WIN_R = 8
NA_WIN_C = 16
MLA_HEADS = 4
MLA_NOPE = 64
MLA_ROPE = 32
MLA_V = 64
MLA_Q_RANK = 192
MLA_KV_RANK = 128
SWA_HEADS = 4
SWA_KV_HEADS = 2
SWA_WINDOW = 128
POOL_WINDOWS = (2, 4, 8, 16)
POOL_GROUP = 64
POOL_WIDTH = 256
BRANCH_W = 256
N_BRANCH = 4
D_FF = 2752
ROPE_BASE = 10000.0
EPS = 1e-6
NEG_INF = -1e30
ATT_SCALE = HEAD_DIM ** -0.5
MLA_SCALE = (MLA_NOPE + MLA_ROPE) ** -0.5
LOG2E = 1.4426950408889634

LANE = 128
D_FF_PAD = 2816
FF_CHUNK = 256
Q_BLOCK = 128
NA_Q_BLOCK = 256
NA_SPAN = 768
NA_DR = 2 * NA_WIN_R - 1
NA_DC = 2 * NA_WIN_C - 1
SWA_SPAN = 384
MLA_Q_BLOCK = 256
CTX_SEQS = 4
VMEM_LIMIT = 56 * 1024 * 1024

_QA, _KA, _VA, _CQ, _CKV, _KR, _QC, _KC, _VC, _PD = 0, 256, 512, 768, 1024, 1152, 1280, 1536, 1664, 1792
_KR_LANE = 64


def _dot(a, b):
    return jnp.dot(a, b, preferred_element_type=F32)


def _dot_nt(a, b):
    return lax.dot_general(a, b, (((1,), (1,)), ((), ())), preferred_element_type=F32)


def _sigmoid(x):
    return 1.0 / (1.0 + jnp.exp2(x * -LOG2E))


def _rms(x, g, n=None):
    n = x.shape[-1] if n is None else n
    ms = jnp.sum(x * x, -1, keepdims=True) * (1.0 / n)
    return x * lax.rsqrt(ms + EPS) * g


def _softmax_blocks(blocks, sink=None):
    m = None
    for s in blocks:
        mm = jnp.max(s, -1, keepdims=True)
        m = mm if m is None else jnp.maximum(m, mm)
    if sink is not None:
        m = jnp.maximum(m, sink)
    es = [jnp.exp2(s - m) for s in blocks]
    l = None
    for e in es:
        ll = jnp.sum(e, -1, keepdims=True)
        l = ll if l is None else l + ll
    if sink is not None:
        l = l + jnp.exp2(sink - m)
    return es, l


def _rope(x, cos, sin, q):
    w = x.shape[-1]
    lane = lax.broadcasted_iota(jnp.int32, x.shape, 1)
    up = pltpu.roll(x, w - q, axis=1)
    dn = pltpu.roll(x, q, axis=1)
    partner = jnp.where((lane & (2 * q - 1)) < q, up, dn)
    return x * cos + partner * sin


def _const_spec(shape):
    n = len(shape)
    return pl.BlockSpec(shape, lambda *_: (0,) * n, pipeline_mode=pl.Buffered(1))


def _layer_spec(arr, layer):
    n = arr.ndim - 1
    return pl.BlockSpec((None,) + arr.shape[1:], lambda *_: (layer,) + (0,) * n, pipeline_mode=pl.Buffered(1))


def _params(n_axes):
    return pltpu.CompilerParams(dimension_semantics=("arbitrary",) * n_axes, vmem_limit_bytes=VMEM_LIMIT)


def _mod_kernel(cv_ref, w_ref, b_ref, o_ref):
    cv = cv_ref[...]
    a = (cv * _sigmoid(cv)).astype(BF16)
    o_ref[0] = _dot(a, w_ref[0].astype(BF16)) + b_ref[0]


def _mod_call(cv, w_mod, b_mod):
    tn = 2048
    return pl.pallas_call(
        _mod_kernel,
        grid=(DEPTH, 6 * D_MODEL // tn),
        in_specs=[_const_spec((8, D_MODEL)),
                  pl.BlockSpec((1, D_MODEL, tn), lambda l, j: (l, 0, j)),
                  pl.BlockSpec((1, 1, tn), lambda l, j: (l, 0, j))],
        out_specs=pl.BlockSpec((1, 8, tn), lambda l, j: (l, 0, j)),
        out_shape=jax.ShapeDtypeStruct((DEPTH, 8, 6 * D_MODEL), F32),
        compiler_params=_params(2),
        name="mod",
    )(cv, w_mod, b_mod.reshape(DEPTH, 1, 6 * D_MODEL))


_IN_SLOTS = ((_QA, 256), (_KA, 256), (_VA, 256), (_CQ, 256), (_CKV, 128), (_KR, 128),
             (_QC, 256), (_KC, 128), (_VC, 128), (_PD, 256))
_CTX_SLOTS = ((_QA, 256, BF16, False), (_KA, 256, F32, True), (_VA, 256, F32, True), (_VA, 256, BF16, False),
              (_CQ, 256, BF16, False), (_CKV, 128, F32, False), (_KR, 128, F32, False), (_KR, 128, F32, True),
              (_QC, 256, BF16, False), (_KC, 128, F32, True), (_VC, 128, F32, True), (_VC, 128, BF16, False),
              (_PD, 256, BF16, False))
_LAT_SLOTS = tuple((off, wd, BF16, False) for off, wd in _IN_SLOTS)


def _inproj_kernel(*refs, latent):
    if latent:
        x_ref, mod_ref, g_ref, w_ref, c64_ref, s64_ref, cm_ref, sm_ref = refs[:8]
        outs = refs[8:]
    else:
        x_ref, mod_ref, g_ref, w_ref, pw_ref, ps_ref = refs[:6]
        outs = refs[6:]
    x = x_ref[...]
    h = _rms(x, g_ref[...]) * (1.0 + mod_ref[0, 1:2, :]) + mod_ref[0, 0:1, :]
    p = _dot(h.astype(BF16), w_ref[...])
    for (off, wd, _, transposed), o_ref in zip(_LAT_SLOTS if latent else _CTX_SLOTS, outs):
        v = p[:, off:off + wd]
        if latent:
            if off == _QC:
                v = _rope(v, c64_ref[...], s64_ref[...], 16)
            elif off == _KC:
                v = _rope(v, c64_ref[:, :128], s64_ref[:, :128], 16)
            elif off == _KR:
                v = _rope(v, cm_ref[:, :128], sm_ref[:, :128], 8)
        if transposed:
            for b in range(o_ref.shape[0]):
                o_ref[b] = v[b * SEQ:(b + 1) * SEQ].T.astype(o_ref.dtype)
        elif off == _PD and not latent:
            ys = _pool_mix([v[b * SEQ:(b + 1) * SEQ] for b in range(v.shape[0] // SEQ)], pw_ref, ps_ref)
            for b, y in enumerate(ys):
                o_ref[b * SEQ:(b + 1) * SEQ, :] = y.astype(o_ref.dtype)
        else:
            o_ref[...] = v.astype(o_ref.dtype)


def _inproj_call(x2d, mod, w, layer, rope_tabs=None):
    t = x2d.shape[0]
    tm = 512
    latent = rope_tabs is not None
    slots = _LAT_SLOTS if latent else _CTX_SLOTS
    tiles_per_mod = t // tm // mod.shape[0]
    in_specs = [pl.BlockSpec((tm, D_MODEL), lambda i: (i, 0)),
                pl.BlockSpec((1, 6, D_MODEL), lambda i: (i // tiles_per_mod, 0, 0)),
                _layer_spec(w["g_attn_pre"], layer), _layer_spec(w["w_in_p"], layer)]
    args = [x2d, mod, w["g_attn_pre"], w["w_in_p"]]
    if latent:
        tiles_per_seq = DEC_SEQ // tm
        c64, s64, cm, sm = rope_tabs
        in_specs += [pl.BlockSpec((tm, 256), lambda i: (i % tiles_per_seq, 0)),
                     pl.BlockSpec((tm, 256), lambda i: (i % tiles_per_seq, 0)),
                     pl.BlockSpec((tm, 512), lambda i: (i % tiles_per_seq, 0)),
                     pl.BlockSpec((tm, 512), lambda i: (i % tiles_per_seq, 0))]
        args += [c64, s64, cm, sm]
    else:
        in_specs += [_layer_spec(w["pool_w"], layer), _layer_spec(w["pool_scale"], layer)]
        args += [w["pool_w"], w["pool_scale"]]
    return pl.pallas_call(
        functools.partial(_inproj_kernel, latent=latent),
        grid=(t // tm,),
        in_specs=in_specs,
        out_specs=[pl.BlockSpec((tm // SEQ, wd, SEQ), lambda i: (i, 0, 0)) if tr
                   else pl.BlockSpec((tm, wd), lambda i: (i, 0)) for _, wd, _, tr in slots],
        out_shape=[jax.ShapeDtypeStruct((t // SEQ, wd, SEQ) if tr else (t, wd), dt) for _, wd, dt, tr in slots],
        compiler_params=_params(1),
        name="inproj_lat" if latent else "inproj_ctx",
    )(*args)


def _ctx_attn_kernel(q_ref, kt_ref, v_ref, o_ref, *, seqs):
    q = q_ref[...]
    kt = kt_ref[...].astype(BF16)
    v = v_ref[...]
    scores = []
    for s in range(seqs):
        rows = slice(s * SEQ, (s + 1) * SEQ)
        for h in range(NA_HEADS):
            sl = slice(h * HEAD_DIM, (h + 1) * HEAD_DIM)
            scores.append(_dot(q[rows, sl], kt[s, sl, :]))
    probs = [_softmax_blocks([sc]) for sc in scores]
    outs = []
    for s in range(seqs):
        rows = slice(s * SEQ, (s + 1) * SEQ)
        ys = []
        for h in range(NA_HEADS):
            sl = slice(h * HEAD_DIM, (h + 1) * HEAD_DIM)
            (e,), l = probs[s * NA_HEADS + h]
            ys.append(_dot(e.astype(BF16), v[rows, sl]) / l)
        outs.append(jnp.concatenate(ys, -1))
    o_ref[...] = (outs[0] if seqs == 1 else jnp.concatenate(outs, 0)).astype(o_ref.dtype)


def _ctx_attn_call(q, k, v):
    seqs = CTX_SEQS
    t = q.shape[0]
    spec = pl.BlockSpec((seqs * SEQ, 256), lambda b: (b, 0))
    kt_spec = pl.BlockSpec((seqs, 256, SEQ), lambda b: (b, 0, 0))
    return pl.pallas_call(
        functools.partial(_ctx_attn_kernel, seqs=seqs), grid=(t // SEQ // seqs,),
        in_specs=[spec, kt_spec, spec], out_specs=spec,
        out_shape=jax.ShapeDtypeStruct((t, 256), BF16), compiler_params=_params(1), name="ctx_attn",
    )(q, k, v)


def _gqa_operands(q, k, rows_q, rows_k):
    group = SWA_HEADS // SWA_KV_HEADS
    out = []
    for kv in range(SWA_KV_HEADS):
        qs = jnp.concatenate([q[rows_q, (kv * group + g) * HEAD_DIM:(kv * group + g + 1) * HEAD_DIM]
                              for g in range(group)], 0)
        out.append((qs, k[rows_k, kv * HEAD_DIM:(kv + 1) * HEAD_DIM]))
    return out


def _gqa_sink(sink_ref, kv, m):
    group = SWA_HEADS // SWA_KV_HEADS
    row = lax.broadcasted_iota(jnp.int32, (group * m, 1), 0)
    col = jnp.full((group * m, 1), sink_ref[kv * group + group - 1] * LOG2E, F32)
    for g in range(group - 2, -1, -1):
        col = jnp.where(row < (g + 1) * m, sink_ref[kv * group + g] * LOG2E, col)
    return col


def _ctx_swa_kernel(sink_ref, q_ref, kt_ref, v_ref, o_ref, *, seqs):
    q = q_ref[...]
    kt = kt_ref[...].astype(BF16)
    v = v_ref[...]
    group = SWA_HEADS // SWA_KV_HEADS
    scores = []
    for s in range(seqs):
        rows = slice(s * SEQ, (s + 1) * SEQ)
        for kv, (qs, _) in enumerate(_gqa_operands(q, q, rows, rows)):
            scores.append(_dot(qs, kt[s, kv * HEAD_DIM:(kv + 1) * HEAD_DIM, :]))
    probs = [_softmax_blocks([sc], sink=_gqa_sink(sink_ref, i % SWA_KV_HEADS, SEQ)) for i, sc in enumerate(scores)]
    outs = []
    for s in range(seqs):
        rows = slice(s * SEQ, (s + 1) * SEQ)
        ys = []
        for kv in range(SWA_KV_HEADS):
            (e,), l = probs[s * SWA_KV_HEADS + kv]
            y = _dot(e.astype(BF16), v[rows, kv * HEAD_DIM:(kv + 1) * HEAD_DIM]) / l
            ys += [y[g * SEQ:(g + 1) * SEQ] for g in range(group)]
        outs.append(jnp.concatenate(ys, -1))
    o_ref[...] = (outs[0] if seqs == 1 else jnp.concatenate(outs, 0)).astype(o_ref.dtype)


def _ctx_swa_call(sink, q, k, v):
    seqs = CTX_SEQS
    t = q.shape[0]
    rows = seqs * SEQ
    return pl.pallas_call(
        functools.partial(_ctx_swa_kernel, seqs=seqs), grid=(t // rows,),
        in_specs=[pl.BlockSpec(memory_space=pltpu.SMEM),
                  pl.BlockSpec((rows, 256), lambda b: (b, 0)),
                  pl.BlockSpec((seqs, 128, SEQ), lambda b: (b, 0, 0)),
                  pl.BlockSpec((rows, 128), lambda b: (b, 0))],
        out_specs=pl.BlockSpec((rows, 256), lambda b: (b, 0)),
        out_shape=jax.ShapeDtypeStruct((t, 256), BF16), compiler_params=_params(1), name="ctx_swa",
    )(sink, q, k, v)


def _mla_q(cq, qn_ref, wuq_ref):
    return _dot(_rms(cq, qn_ref[...], MLA_Q_RANK).astype(BF16), wuq_ref[...])


def _mla_kv(ckv, kr, kvn_ref, wuk_ref, wuv_ref, values_t=False):
    cn = _rms(ckv, kvn_ref[...]).astype(BF16)
    kcat = _dot(cn, wuk_ref[...]) + jnp.concatenate([kr] * MLA_HEADS, -1)
    return kcat, (_dot_nt(wuv_ref[...], cn) if values_t else _dot(cn, wuv_ref[...]))


def _ctx_mla_kernel(cq_ref, ckv_ref, kr_ref, qn_ref, wuq_ref, kvn_ref, wuk_ref, wuv_ref, o_ref, *, seqs):
    q = (_mla_q(cq_ref[...].astype(F32), qn_ref, wuq_ref) * (MLA_SCALE * LOG2E)).astype(BF16)
    kcat, v = _mla_kv(ckv_ref[...], kr_ref[...], kvn_ref, wuk_ref, wuv_ref)
    kcat = kcat.astype(BF16)
    v = v.astype(BF16)
    scores = []
    for s in range(seqs):
        rows = slice(s * SEQ, (s + 1) * SEQ)
        for h in range(MLA_HEADS):
            sl = slice(h * LANE, (h + 1) * LANE)
            scores.append(_dot_nt(q[rows, sl], kcat[rows, sl]))
    probs = [_softmax_blocks([sc]) for sc in scores]
    outs = []
    for s in range(seqs):
        rows = slice(s * SEQ, (s + 1) * SEQ)
        ys = []
        for h in range(MLA_HEADS):
            (e,), l = probs[s * MLA_HEADS + h]
            ys.append(_dot(e.astype(BF16), v[rows, h * MLA_V:(h + 1) * MLA_V]) / l)
        outs.append(jnp.concatenate(ys, -1))
    o_ref[...] = (outs[0] if seqs == 1 else jnp.concatenate(outs, 0)).astype(o_ref.dtype)


_MLA_W = ("mla_qn", "mla_wuq", "mla_kvn", "mla_wuk", "mla_wuv")
_MLA_W_LAT = _MLA_W[:-1] + ("mla_wuv_t",)


def _ctx_mla_call(cq, ckv, kr, w, layer):
    seqs = CTX_SEQS
    t = cq.shape[0]
    rows = seqs * SEQ
    return pl.pallas_call(
        functools.partial(_ctx_mla_kernel, seqs=seqs), grid=(t // rows,),
        in_specs=[pl.BlockSpec((rows, 256), lambda b: (b, 0)),
                  pl.BlockSpec((rows, 128), lambda b: (b, 0)),
                  pl.BlockSpec((rows, 128), lambda b: (b, 0))] + [_layer_spec(w[k], layer) for k in _MLA_W],
        out_specs=pl.BlockSpec((rows, 256), lambda b: (b, 0)),
        out_shape=jax.ShapeDtypeStruct((t, 256), BF16), compiler_params=_params(1), name="ctx_mla",
    )(cq, ckv, kr, *[w[k] for k in _MLA_W])


_POOL_PAD = 8


def _pool_mix(xs, w_ref, sc_ref):
    n = xs[0].shape[0]
    ne = n + 2 * _POOL_PAD
    lo, hi = _POOL_PAD, _POOL_PAD + n
    z = jnp.zeros((_POOL_PAD, POOL_WIDTH), F32)
    grp = lax.broadcasted_iota(jnp.int32, (n, POOL_WIDTH), 1) >> 6
    t = lax.broadcasted_iota(jnp.int32, (n, POOL_WIDTH), 0)
    half = jnp.where(grp == 0, 1, jnp.where(grp == 1, 2, jnp.where(grp == 2, 4, 8)))
    cnt = (jnp.minimum(t + half, n) - jnp.maximum(t - half, 0)).astype(F32)

    def pair(a, s):
        return pltpu.roll(a, s, axis=0) + pltpu.roll(a, ne - s, axis=0)

    out = []
    for x in xs:
        xz = jnp.concatenate([z, x, z], 0)
        s2 = xz + pltpu.roll(xz, 1, axis=0)
        s4 = pair(s2, 1)
        s8 = pair(s4, 2)
        s16 = pair(s8, 4)
        tot = jnp.where(grp == 0, s2[lo:hi],
                        jnp.where(grp == 1, s4[lo:hi], jnp.where(grp == 2, s8[lo:hi], s16[lo:hi])))
        dlt = (tot / cnt - x).astype(BF16)
        out.append(_dot(dlt, w_ref[...]) * sc_ref[...])
    return out


def _pool_kernel(x_ref, w_ref, sc_ref, o_ref, *, n):
    xs = [x_ref[s * n:(s + 1) * n, :].astype(F32) for s in range(x_ref.shape[0] // n)]
    for s, y in enumerate(_pool_mix(xs, w_ref, sc_ref)):
        o_ref[s * n:(s + 1) * n, :] = y.astype(o_ref.dtype)


def _pool_call(pd, w, layer, seq):
    t = pd.shape[0]
    rows = seq
    return pl.pallas_call(
        functools.partial(_pool_kernel, n=seq), grid=(t // rows,),
        in_specs=[pl.BlockSpec((rows, POOL_WIDTH), lambda b: (b, 0)),
                  _layer_spec(w["pool_w"], layer), _layer_spec(w["pool_scale"], layer)],
        out_specs=pl.BlockSpec((rows, POOL_WIDTH), lambda b: (b, 0)),
        out_shape=jax.ShapeDtypeStruct((t, POOL_WIDTH), BF16), compiler_params=_params(1), name="pool",
    )(pd, w["pool_w"], w["pool_scale"])


def _lat_na_kernel(q_ref, k_ref, v_ref, kc_ref, vc_ref, t2_ref, o_ref):
    n = pl.program_id(1)
    rows = DEC_SEQ // GRID_W
    q_rows = NA_Q_BLOCK // GRID_W
    row0 = jnp.clip(q_rows * n - NA_WIN_R // 2, 0, rows - NA_SPAN // GRID_W)
    start = pl.multiple_of(row0 * GRID_W, LANE)
    q = q_ref[...]
    k = k_ref[pl.ds(start, NA_SPAN), :]
    v = v_ref[pl.ds(start, NA_SPAN), :]
    kc = kc_ref[...].astype(BF16)
    vc = vc_ref[...].astype(BF16)

    pairs = NA_SPAN // LANE
    low_half = lax.broadcasted_iota(jnp.int32, (GRID_W, LANE), 1) < GRID_W
    entries, masks = [], []
    for a in range(q_rows):
        r = q_rows * n + a
        r_start = jnp.clip(r - NA_WIN_R // 2, 0, rows - NA_WIN_R)
        for p in range(pairs):
            rk = row0 + 2 * p
            ok0 = ((rk >= r_start) & (rk < r_start + NA_WIN_R)).astype(jnp.int32)
            ok1 = ((rk + 1 >= r_start) & (rk + 1 < r_start + NA_WIN_R)).astype(jnp.int32)
            entries.append(jnp.clip(rk - r + NA_WIN_R, 0, NA_DR))
            masks.append(jnp.where(low_half, ok0, ok1) > 0)

    heads = [slice(h * HEAD_DIM, (h + 1) * HEAD_DIM) for h in range(NA_HEADS)]
    raw = [(_dot_nt(q[:, sl], k[:, sl]), _dot_nt(q[:, sl], kc[:, sl])) for sl in heads]
    probs = []
    for h, (s_loc, s_ctx) in enumerate(raw):
        cols = []
        for p in range(pairs):
            blk = [jnp.where(masks[a * pairs + p], t2_ref[h, entries[a * pairs + p]], NEG_INF)
                   for a in range(q_rows)]
            cols.append(jnp.concatenate(blk, 0))
        probs.append(_softmax_blocks([s_loc + jnp.concatenate(cols, 1), s_ctx]))
    ys = []
    for sl, ((e_loc, e_ctx), l) in zip(heads, probs):
        y = _dot(e_loc.astype(BF16), v[:, sl]) + _dot(e_ctx.astype(BF16), vc[:, sl])
        ys.append(y / l)
    o_ref[...] = jnp.concatenate(ys, -1).astype(o_ref.dtype)


def _lat_na_call(q, k, v, cache_k, cache_v, t2, layer):
    nq = DEC_SEQ // NA_Q_BLOCK
    seq_spec = pl.BlockSpec((DEC_SEQ, 256), lambda b, n: (b, 0))
    cache_spec = pl.BlockSpec((None, None, PAST_LEN, 256), lambda b, n: (b, layer, 0, 0))
    return pl.pallas_call(
        _lat_na_kernel, grid=(DEC_BATCH, nq),
        in_specs=[pl.BlockSpec((NA_Q_BLOCK, 256), lambda b, n: (b * nq + n, 0)), seq_spec, seq_spec,
                  cache_spec, cache_spec, _layer_spec(t2, layer)],
        out_specs=pl.BlockSpec((NA_Q_BLOCK, 256), lambda b, n: (b * nq + n, 0)),
        out_shape=jax.ShapeDtypeStruct((DEC_BATCH * DEC_SEQ, 256), BF16),
        compiler_params=_params(2), name="lat_na",
    )(q, k, v, cache_k, cache_v, t2)


def _lat_swa_kernel(sink_ref, q_ref, k_ref, v_ref, kc_ref, vc_ref, o_ref):
    n = pl.program_id(1)
    start = pl.multiple_of(jnp.clip(n - 1, 0, DEC_SEQ // Q_BLOCK - 3) * Q_BLOCK, LANE)
    q = q_ref[...]
    k = k_ref[pl.ds(start, SWA_SPAN), :]
    v = v_ref[pl.ds(start, SWA_SPAN), :]
    kc = kc_ref[...].astype(BF16)
    vc = vc_ref[...].astype(BF16)
    group = SWA_HEADS // SWA_KV_HEADS
    m = group * Q_BLOCK
    q_pos = n * Q_BLOCK + (lax.broadcasted_iota(jnp.int32, (m, SWA_SPAN), 0) & (Q_BLOCK - 1))
    k_pos = start + lax.broadcasted_iota(jnp.int32, (m, SWA_SPAN), 1)
    valid = jnp.abs(q_pos - k_pos) <= SWA_WINDOW
    everything = slice(None)
    raw = []
    for (qs, ks), (_, kcs) in zip(_gqa_operands(q, k, everything, everything),
                                  _gqa_operands(q, kc, everything, everything)):
        raw.append((_dot_nt(qs, ks), _dot_nt(qs, kcs)))
    probs = [_softmax_blocks([jnp.where(valid, s_loc, NEG_INF), s_ctx], sink=_gqa_sink(sink_ref, kv, Q_BLOCK))
             for kv, (s_loc, s_ctx) in enumerate(raw)]
    ys = []
    for kv, ((e_loc, e_ctx), l) in enumerate(probs):
        kvsl = slice(kv * HEAD_DIM, (kv + 1) * HEAD_DIM)
        y = (_dot(e_loc.astype(BF16), v[:, kvsl]) + _dot(e_ctx.astype(BF16), vc[:, kvsl])) / l
        ys += [y[g * Q_BLOCK:(g + 1) * Q_BLOCK] for g in range(group)]
    o_ref[...] = jnp.concatenate(ys, -1).astype(o_ref.dtype)


def _lat_swa_call(sink, q, k, v, cache_k, cache_v, layer):
    nq = DEC_SEQ // Q_BLOCK
    seq_spec = pl.BlockSpec((DEC_SEQ, 128), lambda b, n: (b, 0))
    cache_spec = pl.BlockSpec((None, None, PAST_LEN, 128), lambda b, n: (b, layer, 0, 0))
    return pl.pallas_call(
        _lat_swa_kernel, grid=(DEC_BATCH, nq),
        in_specs=[pl.BlockSpec(memory_space=pltpu.SMEM),
                  pl.BlockSpec((Q_BLOCK, 256), lambda b, n: (b * nq + n, 0)), seq_spec, seq_spec,
                  cache_spec, cache_spec],
        out_specs=pl.BlockSpec((Q_BLOCK, 256), lambda b, n: (b * nq + n, 0)),
        out_shape=jax.ShapeDtypeStruct((DEC_BATCH * DEC_SEQ, 256), BF16),
        compiler_params=_params(2), name="lat_swa",
    )(sink, q, k, v, cache_k, cache_v)


def _lat_mla_kernel(cq_ref, ckv_ref, kr_ref, cckv_ref, ckr_ref, cm_ref, sm_ref,
                    qn_ref, wuq_ref, kvn_ref, wuk_ref, wuvt_ref, o_ref, kcat_s, vt_s):
    @pl.when(pl.program_id(1) == 0)
    def _():
        kc, vc = _mla_kv(cckv_ref[...], ckr_ref[...], kvn_ref, wuk_ref, wuvt_ref, True)
        kcat_s[0:PAST_LEN, :] = kc.astype(BF16)
        vt_s[:, 0:PAST_LEN] = vc.astype(BF16)
        kl, vl = _mla_kv(ckv_ref[...].astype(F32), kr_ref[...].astype(F32), kvn_ref, wuk_ref, wuvt_ref, True)
        kcat_s[PAST_LEN:, :] = kl.astype(BF16)
        vt_s[:, PAST_LEN:] = vl.astype(BF16)

    q = _mla_q(cq_ref[...].astype(F32), qn_ref, wuq_ref)
    q = (_rope(q, cm_ref[...], sm_ref[...], 8) * (MLA_SCALE * LOG2E)).astype(BF16)
    scores = [_dot_nt(kcat_s[:, h * LANE:(h + 1) * LANE], q[:, h * LANE:(h + 1) * LANE]) for h in range(MLA_HEADS)]
    ys = []
    for h, st in enumerate(scores):
        e = jnp.exp2(st - jnp.max(st, 0, keepdims=True))
        l = jnp.sum(e, 0, keepdims=True)
        ys.append(_dot(vt_s[h * MLA_V:(h + 1) * MLA_V, :], e.astype(BF16)) / l)
    o_ref[...] = jnp.concatenate(ys, 0).T.astype(o_ref.dtype)


def _lat_mla_call(cq, ckv, kr, cache_ckv, cache_kr, cm, sm, w, layer):
    qb = MLA_Q_BLOCK
    nq = DEC_SEQ // qb
    seq_spec = pl.BlockSpec((DEC_SEQ, 128), lambda b, n: (b, 0))
    cache_spec = pl.BlockSpec((None, None, PAST_LEN, 128), lambda b, n: (b, layer, 0, 0))
    tab_spec = pl.BlockSpec((qb, 512), lambda b, n: (n, 0))
    return pl.pallas_call(
        _lat_mla_kernel, grid=(DEC_BATCH, nq),
        in_specs=[pl.BlockSpec((qb, 256), lambda b, n: (b * nq + n, 0)), seq_spec, seq_spec,
                  cache_spec, cache_spec, tab_spec, tab_spec] + [_layer_spec(w[k], layer) for k in _MLA_W_LAT],
        out_specs=pl.BlockSpec((qb, 256), lambda b, n: (b * nq + n, 0)),
        out_shape=jax.ShapeDtypeStruct((DEC_BATCH * DEC_SEQ, 256), BF16),
        scratch_shapes=[pltpu.VMEM((PAST_LEN + DEC_SEQ, MLA_HEADS * LANE), BF16),
                        pltpu.VMEM((MLA_HEADS * MLA_V, PAST_LEN + DEC_SEQ), BF16)],
        compiler_params=_params(2), name="lat_mla",
    )(cq, ckv, kr, cache_ckv, cache_kr, cm, sm, *[w[k] for k in _MLA_W_LAT])


_MERGE_PARTS = 2


def _merge_kernel(x_ref, mod_ref, gpre_ref, ya_ref, yb_ref, yc_ref, yd_ref,
                  wg_ref, bg_ref, wb_ref, wo_ref, gpost_ref, o_ref):
    tm = x_ref.shape[0] // _MERGE_PARTS
    for p in range(_MERGE_PARTS):
        rows = slice(p * tm, (p + 1) * tm)
        x = x_ref[rows, :]
        h = (_rms(x, gpre_ref[...]) * (1.0 + mod_ref[0, 1:2, :]) + mod_ref[0, 0:1, :]).astype(BF16)
        merged = None
        for k, y_ref in enumerate((ya_ref, yb_ref, yc_ref, yd_ref)):
            cols = slice(k * D_MODEL, (k + 1) * D_MODEL)
            gate = _sigmoid(_dot(h, wg_ref[:, cols]) + bg_ref[:, cols])
            term = gate * _dot(y_ref[rows, :], wb_ref[k])
            merged = term if merged is None else merged + term
        o = _dot(merged.astype(BF16), wo_ref[...])
        o_ref[rows, :] = x + mod_ref[0, 2:3, :] * _rms(o, gpost_ref[...])


_MERGE_W = ("w_gate", "b_gate", "w_branch", "w_out", "g_attn_post")


def _merge_call(x2d, mod, ys, w, layer):
    t = x2d.shape[0]
    tm = 512
    tiles_per_mod = t // tm // mod.shape[0]
    tile = pl.BlockSpec((tm, D_MODEL), lambda i: (i, 0))
    ytile = pl.BlockSpec((tm, BRANCH_W), lambda i: (i, 0))
    return pl.pallas_call(
        _merge_kernel, grid=(t // tm,),
        in_specs=[tile, pl.BlockSpec((1, 6, D_MODEL), lambda i: (i // tiles_per_mod, 0, 0)),
                  _layer_spec(w["g_attn_pre"], layer), ytile, ytile, ytile, ytile]
                 + [_layer_spec(w[k], layer) for k in _MERGE_W],
        out_specs=tile,
        out_shape=jax.ShapeDtypeStruct((t, D_MODEL), F32),
        compiler_params=_params(1), name="merge",
    )(x2d, mod, w["g_attn_pre"], *ys, *[w[k] for k in _MERGE_W])


_GAP = 8


def _ffn_kernel(*refs, seqs, halo):
    if halo:
        x_ref, xp_ref, xn_ref = refs[:3]
        refs = refs[3:]
    else:
        x_ref = refs[0]
        refs = refs[1:]
    mod_ref, gpre_ref, wu_ref, cu_ref, wd_ref, gpost_ref, o_ref = refs
    x = x_ref[...]
    tm = x.shape[0]
    shift, scale, gate = mod_ref[0, 3:4, :], mod_ref[0, 4:5, :], mod_ref[0, 5:6, :]

    def pre(xx):
        return _rms(xx, gpre_ref[...]) * (1.0 + scale) + shift

    h = pre(x)
    if halo:
        i = pl.program_id(0) % halo
        hp = jnp.where(i == 0, 0.0, pre(xp_ref[...]))
        hn = jnp.where(i == halo - 1, 0.0, pre(xn_ref[...]))
        pieces = [hp, h, hn]
        starts = [_GAP]
        seq_len = tm
    else:
        seq_len = tm // seqs
        pieces = [h]
        starts = [s * seq_len for s in range(seqs)]
    hb = jnp.concatenate(pieces, 0).astype(BF16)
    rows = hb.shape[0]
    edge_row = lax.broadcasted_iota(jnp.int32, (_GAP, FF_CHUNK), 0)

    def zero_edges(a, first):
        out = []
        for st in starts:
            seg = a[st:st + seq_len]
            if first:
                out += [jnp.where(edge_row == 0, 0.0, seg[:_GAP]), seg[_GAP:]]
            else:
                out += [seg[:seq_len - _GAP], jnp.where(edge_row == _GAP - 1, 0.0, seg[seq_len - _GAP:])]
        return jnp.concatenate(out, 0)

    def up(cols):
        return _dot(hb, wu_ref[:, cols])

    def conv(u, cols):
        prev = pltpu.roll(u, 1, axis=0)
        nxt = pltpu.roll(u, rows - 1, axis=0)
        if not halo:
            prev, nxt = zero_edges(prev, True), zero_edges(nxt, False)
        return prev * cu_ref[0:1, cols] + u * cu_ref[1:2, cols] + nxt * cu_ref[2:3, cols]

    acts = []
    for c in range(D_FF_PAD // FF_CHUNK):
        a_cols = slice(c * FF_CHUNK, (c + 1) * FF_CHUNK)
        g_cols = slice(D_FF_PAD + c * FF_CHUNK, D_FF_PAD + (c + 1) * FF_CHUNK)
        ua, ug = up(a_cols), up(g_cols)
        a = conv(ua, a_cols)
        g = conv(ug, g_cols)
        acts.append((g * _sigmoid(g) * a).astype(BF16))
    acc = _dot(jnp.concatenate(acts, 1), wd_ref[...])
    for s, st in enumerate(starts):
        ys = _rms(acc[st:st + seq_len], gpost_ref[...])
        o_ref[s * seq_len:(s + 1) * seq_len, :] = x[s * seq_len:(s + 1) * seq_len] + gate * ys


_FFN_W = ("g_ffn_pre", "ffn_wu", "ffn_cu", "ffn_wd", "g_ffn_post")


def _ffn_call(x2d, mod, w, layer, seq, tm):
    t = x2d.shape[0]
    tile = pl.BlockSpec((tm, D_MODEL), lambda i: (i, 0))
    in_specs = [tile]
    args = [x2d]
    if tm < seq:
        halo, seqs = seq // tm, 1
        r = tm // _GAP
        last = t // _GAP - 1
        in_specs += [pl.BlockSpec((_GAP, D_MODEL), lambda i: (jnp.maximum(i * r - 1, 0), 0)),
                     pl.BlockSpec((_GAP, D_MODEL), lambda i: (jnp.minimum((i + 1) * r, last), 0))]
        args += [x2d, x2d]
    else:
        halo, seqs = 0, tm // seq
    tiles_per_mod = t // tm // mod.shape[0]
    in_specs += [pl.BlockSpec((1, 6, D_MODEL), lambda i: (i // tiles_per_mod, 0, 0))]
    in_specs += [_layer_spec(w[k], layer) for k in _FFN_W]
    args += [mod] + [w[k] for k in _FFN_W]
    return pl.pallas_call(
        functools.partial(_ffn_kernel, seqs=seqs, halo=halo), grid=(t // tm,),
        in_specs=in_specs, out_specs=tile,
        out_shape=jax.ShapeDtypeStruct((t, D_MODEL), F32),
        compiler_params=_params(1), name="ffn",
    )(*args)


def _rope_tables():
    t = np.arange(DEC_SEQ)
    pos = (t // GRID_W, t % GRID_W)

    def tab(d):
        half = d // 4
        inv = np.float32(ROPE_BASE) ** (-np.arange(half, dtype=np.float32) / np.float32(half))
        cs, sn = [], []
        for p in pos:
            ang = p.astype(np.float32)[:, None] * inv[None, :]
            cs += [np.cos(ang), np.cos(ang)]
            sn += [-np.sin(ang), np.sin(ang)]
        return np.concatenate(cs, -1), np.concatenate(sn, -1)

    c64, s64 = tab(HEAD_DIM)
    c32, s32 = tab(MLA_ROPE)
    pad = LANE - MLA_NOPE - MLA_ROPE
    cm = np.concatenate([np.ones((DEC_SEQ, MLA_NOPE), np.float32), c32, np.ones((DEC_SEQ, pad), np.float32)], -1)
    sm = np.concatenate([np.zeros((DEC_SEQ, MLA_NOPE), np.float32), s32, np.zeros((DEC_SEQ, pad), np.float32)], -1)
    return tuple(jnp.asarray(np.tile(x, (1, 4)), F32) for x in (c64, s64, cm, sm))


def _na_table_kernel(rpb_ref, e_ref, ok_ref, o_ref):
    r = rpb_ref[...]
    r1 = r.astype(BF16)
    r2 = (r - r1.astype(F32)).astype(BF16)
    r3 = (r - r1.astype(F32) - r2.astype(F32)).astype(BF16)
    e = e_ref[...]
    t = _dot(r1, e) + _dot(r2, e) + _dot(r3, e)
    o_ref[...] = jnp.where(ok_ref[...] > 0, t * LOG2E, NEG_INF)


def _na_bias_tables(na_rpb):
    c = np.arange(GRID_W)[:, None]
    w = np.arange(GRID_W)[None, :]
    dc = (w - c + NA_WIN_C - 1).reshape(-1)
    onehot = (np.arange(LANE)[:, None] == dc[None, :]).astype(np.float32)
    c_start = np.clip(c - NA_WIN_C // 2, 0, GRID_W - NA_WIN_C)
    ok = ((w >= c_start) & (w < c_start + NA_WIN_C)).reshape(1, -1).astype(np.int32)
    rows = DEPTH * NA_HEADS * NA_DR
    rpb2 = jnp.pad(na_rpb.reshape(rows, NA_DC), ((0, LANE - rows), (0, LANE - NA_DC)))
    t = pl.pallas_call(
        _na_table_kernel, out_shape=jax.ShapeDtypeStruct((LANE, GRID_W * GRID_W), F32), name="na_table",
        compiler_params=pltpu.CompilerParams(vmem_limit_bytes=VMEM_LIMIT),
    )(rpb2, jnp.asarray(onehot, BF16), jnp.asarray(ok))
    t = t[:rows].reshape(DEPTH, NA_HEADS, NA_DR, GRID_W, GRID_W)
    t = jnp.pad(t, ((0, 0), (0, 0), (1, 1), (0, 0), (0, 0)), constant_values=NEG_INF)
    return jnp.concatenate([t[:, :, :-1], t[:, :, 1:]], -1)


def _pad_last(w, n):
    return jnp.pad(w, ((0, 0),) * (w.ndim - 1) + ((0, n - w.shape[-1]),))


def _prep_weights(g_attn_pre, g_attn_post, g_ffn_pre, g_ffn_post, w_in, w_gate, b_gate, mla_q_norm, mla_w_uq,
                  mla_kv_norm, mla_w_ukv, pool_w, pool_scale, w_branch, w_out, ffn_w_up, ffn_conv, ffn_w_down):
    q_scale = jnp.concatenate([jnp.full((256,), ATT_SCALE * LOG2E, F32), jnp.ones((512,), F32)])
    b0, c0, d0 = 768, 1120, 1632
    kr = jnp.pad(w_in[:, :, b0 + 320:b0 + 352], ((0, 0), (0, 0), (_KR_LANE, LANE - _KR_LANE - MLA_ROPE)))
    w_in_p = jnp.concatenate([w_in[:, :, :b0] * q_scale, _pad_last(w_in[:, :, b0:b0 + MLA_Q_RANK], 256),
                              w_in[:, :, b0 + MLA_Q_RANK:b0 + MLA_Q_RANK + MLA_KV_RANK], kr,
                              w_in[:, :, c0:d0] * q_scale[:512], w_in[:, :, d0:]], 2).astype(BF16)

    wuq = mla_w_uq.reshape(DEPTH, MLA_Q_RANK, MLA_HEADS, MLA_NOPE + MLA_ROPE)
    wuq = jnp.pad(wuq, ((0, 0), (0, 256 - MLA_Q_RANK), (0, 0), (0, LANE - MLA_NOPE - MLA_ROPE)))
    wukv = mla_w_ukv.reshape(DEPTH, MLA_KV_RANK, MLA_HEADS, MLA_NOPE + MLA_V)
    wuk = _pad_last(wukv[..., :MLA_NOPE], LANE)
    wuv = wukv[..., MLA_NOPE:].reshape(DEPTH, MLA_KV_RANK, MLA_HEADS * MLA_V).astype(BF16)

    def halves(a):
        a = _pad_last(a.reshape(a.shape[:-1] + (2, D_FF)), D_FF_PAD)
        return a.reshape(a.shape[:-2] + (2 * D_FF_PAD,))

    eye = np.eye(len(POOL_WINDOWS), dtype=np.float32)
    w_bd = (pool_w[:, :, :, None, :] * eye[None, :, None, :, None]).reshape(DEPTH, POOL_WIDTH, POOL_WIDTH)

    return dict(
        g_attn_pre=g_attn_pre[:, None, :], g_attn_post=g_attn_post[:, None, :],
        g_ffn_pre=g_ffn_pre[:, None, :], g_ffn_post=g_ffn_post[:, None, :],
        w_in_p=w_in_p,
        mla_qn=_pad_last(mla_q_norm[:, None, :], 256),
        mla_wuq=wuq.reshape(DEPTH, 256, MLA_HEADS * LANE).astype(BF16),
        mla_kvn=mla_kv_norm[:, None, :],
        mla_wuk=wuk.reshape(DEPTH, MLA_KV_RANK, MLA_HEADS * LANE).astype(BF16),
        mla_wuv=wuv, mla_wuv_t=wuv.transpose(0, 2, 1),
        pool_w=w_bd.astype(BF16), pool_scale=pool_scale[:, None, :],
        w_gate=w_gate.astype(BF16), b_gate=b_gate[:, None, :],
        w_branch=w_branch.astype(BF16), w_out=w_out.astype(BF16),
        ffn_wu=halves(ffn_w_up).astype(BF16), ffn_cu=halves(ffn_conv),
        ffn_wd=jnp.pad(ffn_w_down, ((0, 0), (0, D_FF_PAD - D_FF), (0, 0))).astype(BF16))


def kernel(x_prompt, x_sample, cache_na_k, cache_na_v, cache_mla_ckv, cache_mla_krope, cache_swa_k, cache_swa_v, c, c_ctx, w_mod, b_mod, g_attn_pre, g_attn_post, g_ffn_pre, g_ffn_post, w_in, w_gate, b_gate, na_rpb, mla_q_norm, mla_w_uq, mla_kv_norm, mla_w_ukv, swa_sink, pool_w, pool_scale, w_branch, w_out, ffn_w_up, ffn_conv, ffn_w_down):
    x_p = x_prompt.reshape(BATCH * SEQ, D_MODEL)
    x_s = x_sample.reshape(DEC_BATCH * DEC_SEQ, D_MODEL)

    cv = jnp.concatenate([c_ctx[None, :], c, jnp.zeros((8 - 1 - DEC_BATCH, D_MODEL), F32)], 0)
    mod = _mod_call(cv, w_mod, b_mod).reshape(DEPTH, 8, 6, D_MODEL)
    w = _prep_weights(g_attn_pre, g_attn_post, g_ffn_pre, g_ffn_post, w_in, w_gate, b_gate, mla_q_norm, mla_w_uq,
                      mla_kv_norm, mla_w_ukv, pool_w, pool_scale, w_branch, w_out, ffn_w_up, ffn_conv, ffn_w_down)
    rope_tabs = _rope_tables()
    na_t2 = _na_bias_tables(na_rpb)
    cache_na_k = cache_na_k.reshape(DEC_BATCH, DEPTH, PAST_LEN, 256)
    cache_na_v = cache_na_v.reshape(DEC_BATCH, DEPTH, PAST_LEN, 256)
    cache_swa_k = cache_swa_k.reshape(DEC_BATCH, DEPTH, PAST_LEN, 128)
    cache_swa_v = cache_swa_v.reshape(DEC_BATCH, DEPTH, PAST_LEN, 128)
    cache_kr = jnp.pad(cache_mla_krope, ((0, 0), (0, 0), (0, 0), (_KR_LANE, LANE - _KR_LANE - MLA_ROPE)))

    states = []
    for l in range(DEPTH):
        mod_p = mod[l, 0:1]
        qa, ka_t, va_t, va, cq, ckv, kr, kr_t, qc, kc_t, vc_t, vc, yd = _inproj_call(x_p, mod_p, w, l)
        ys = (_ctx_attn_call(qa, ka_t, va), _ctx_mla_call(cq, ckv, kr, w, l),
              _ctx_swa_call(swa_sink[l], qc, kc_t, vc), yd)
        x_p = _merge_call(x_p, mod_p, ys, w, l)
        x_p = _ffn_call(x_p, mod_p, w, l, SEQ, 2 * SEQ)
        states.append((ka_t, va_t, ckv, kr_t[:, _KR_LANE:_KR_LANE + MLA_ROPE, :], kc_t, vc_t))

        mod_s = mod[l, 1:1 + DEC_BATCH]
        qa, ka, va, cq, ckv, kr, qc, kc, vc, pd = _inproj_call(x_s, mod_s, w, l, rope_tabs)
        ys = (_lat_na_call(qa, ka, va, cache_na_k, cache_na_v, na_t2, l),
              _lat_mla_call(cq, ckv, kr, cache_mla_ckv, cache_kr, rope_tabs[2], rope_tabs[3], w, l),
              _lat_swa_call(swa_sink[l], qc, kc, vc, cache_swa_k, cache_swa_v, l),
              _pool_call(pd, w, l, DEC_SEQ))
        x_s = _merge_call(x_s, mod_s, ys, w, l)
        x_s = _ffn_call(x_s, mod_s, w, l, DEC_SEQ, 1024)

    def stack(j, *tail):
        return jnp.stack([st[j].reshape(BATCH, SEQ, *tail) for st in states], 1)

    def stack_t(j, heads):
        a = jnp.stack([st[j] for st in states], 1).reshape(BATCH, DEPTH, heads, HEAD_DIM, SEQ)
        return a.transpose(0, 1, 4, 2, 3)

    return (x_p.reshape(BATCH, SEQ, D_MODEL), x_s.reshape(DEC_BATCH, DEC_SEQ, D_MODEL),
            stack_t(0, NA_HEADS), stack_t(1, NA_HEADS), stack(2, MLA_KV_RANK),
            jnp.stack([st[3] for st in states], 1).transpose(0, 1, 3, 2),
            stack_t(4, SWA_KV_HEADS), stack_t(5, SWA_KV_HEADS))
```

```python
import functools

import jax
import jax.numpy as jnp
import numpy as np
from jax import lax
from jax.experimental import pallas as pl
from jax.experimental.pallas import tpu as pltpu

F32 = jnp.float32
BF16 = jnp.bfloat16

D_MODEL = 1024
BATCH = 32
SEQ = 256
DEPTH = 2
DEC_BATCH = 2
DEC_SEQ = 2048
PAST_LEN = 256
GRID_W = 64
HEAD_DIM = 64
NA_HEADS = 4
NA_WIN_R = 8
NA_WIN_C = 16
MLA_HEADS = 4
MLA_NOPE = 64
MLA_ROPE = 32
MLA_V = 64
MLA_Q_RANK = 192
MLA_KV_RANK = 128
SWA_HEADS = 4
SWA_KV_HEADS = 2
SWA_WINDOW = 128
POOL_WINDOWS = (2, 4, 8, 16)
POOL_GROUP = 64
POOL_WIDTH = 256
BRANCH_W = 256
N_BRANCH = 4
D_FF = 2752
ROPE_BASE = 10000.0
EPS = 1e-6
NEG_INF = -1e30
ATT_SCALE = HEAD_DIM ** -0.5
MLA_SCALE = (MLA_NOPE + MLA_ROPE) ** -0.5
LOG2E = 1.4426950408889634

LANE = 128
D_FF_PAD = 2816
FF_CHUNK = 256
Q_BLOCK = 128
NA_Q_BLOCK = 256
NA_SPAN = 768
NA_DR = 2 * NA_WIN_R - 1
NA_DC = 2 * NA_WIN_C - 1
SWA_SPAN = 384
MLA_Q_BLOCK = 256
CTX_SEQS = 4
VMEM_LIMIT = 56 * 1024 * 1024

_QA, _KA, _VA, _CQ, _CKV, _KR, _QC, _KC, _VC, _PD = 0, 256, 512, 768, 1024, 1152, 1280, 1536, 1664, 1792
_KR_LANE = 64


def _dot(a, b):
    return jnp.dot(a, b, preferred_element_type=F32)


def _dot_nt(a, b):
    return lax.dot_general(a, b, (((1,), (1,)), ((), ())), preferred_element_type=F32)


def _sigmoid(x):
    return 1.0 / (1.0 + jnp.exp2(x * -LOG2E))


def _rms(x, g, n=None):
    n = x.shape[-1] if n is None else n
    ms = jnp.sum(x * x, -1, keepdims=True) * (1.0 / n)
    return x * lax.rsqrt(ms + EPS) * g


def _softmax_blocks(blocks, sink=None):
    m = None
    for s in blocks:
        mm = jnp.max(s, -1, keepdims=True)
        m = mm if m is None else jnp.maximum(m, mm)
    if sink is not None:
        m = jnp.maximum(m, sink)
    es = [jnp.exp2(s - m) for s in blocks]
    l = None
    for e in es:
        ll = jnp.sum(e, -1, keepdims=True)
        l = ll if l is None else l + ll
    if sink is not None:
        l = l + jnp.exp2(sink - m)
    return es, l


def _rope(x, cos, sin, q):
    w = x.shape[-1]
    lane = lax.broadcasted_iota(jnp.int32, x.shape, 1)
    up = pltpu.roll(x, w - q, axis=1)
    dn = pltpu.roll(x, q, axis=1)
    partner = jnp.where((lane & (2 * q - 1)) < q, up, dn)
    return x * cos + partner * sin


def _const_spec(shape):
    n = len(shape)
    return pl.BlockSpec(shape, lambda *_: (0,) * n, pipeline_mode=pl.Buffered(1))


def _layer_spec(arr, layer):
    n = arr.ndim - 1
    return pl.BlockSpec((None,) + arr.shape[1:], lambda *_: (layer,) + (0,) * n, pipeline_mode=pl.Buffered(1))


def _params(n_axes):
    return pltpu.CompilerParams(dimension_semantics=("arbitrary",) * n_axes, vmem_limit_bytes=VMEM_LIMIT)


def _mod_kernel(cv_ref, w_ref, b_ref, o_ref):
    cv = cv_ref[...]
    a = (cv * _sigmoid(cv)).astype(BF16)
    o_ref[0] = _dot(a, w_ref[0].astype(BF16)) + b_ref[0]


def _mod_call(cv, w_mod, b_mod):
    tn = 2048
    return pl.pallas_call(
        _mod_kernel,
        grid=(DEPTH, 6 * D_MODEL // tn),
        in_specs=[_const_spec((8, D_MODEL)),
                  pl.BlockSpec((1, D_MODEL, tn), lambda l, j: (l, 0, j)),
                  pl.BlockSpec((1, 1, tn), lambda l, j: (l, 0, j))],
        out_specs=pl.BlockSpec((1, 8, tn), lambda l, j: (l, 0, j)),
        out_shape=jax.ShapeDtypeStruct((DEPTH, 8, 6 * D_MODEL), F32),
        compiler_params=_params(2),
        name="mod",
    )(cv, w_mod, b_mod.reshape(DEPTH, 1, 6 * D_MODEL))


_IN_SLOTS = ((_QA, 256), (_KA, 256), (_VA, 256), (_CQ, 256), (_CKV, 128), (_KR, 128),
             (_QC, 256), (_KC, 128), (_VC, 128), (_PD, 256))
_CTX_SLOTS = ((_QA, 256, BF16, False), (_KA, 256, F32, True), (_VA, 256, F32, True), (_VA, 256, BF16, False),
              (_CQ, 256, BF16, False), (_CKV, 128, F32, False), (_KR, 128, F32, False), (_KR, 128, F32, True),
              (_QC, 256, BF16, False), (_KC, 128, F32, True), (_VC, 128, F32, True), (_VC, 128, BF16, False),
              (_PD, 256, BF16, False))
_LAT_SLOTS = tuple((off, wd, BF16, False) for off, wd in _IN_SLOTS)


def _inproj_kernel(*refs, latent):
    if latent:
        x_ref, mod_ref, g_ref, w_ref, c64_ref, s64_ref, cm_ref, sm_ref = refs[:8]
        outs = refs[8:]
    else:
        x_ref, mod_ref, g_ref, w_ref, pw_ref, ps_ref = refs[:6]
        outs = refs[6:]
    x = x_ref[...]
    h = _rms(x, g_ref[...]) * (1.0 + mod_ref[0, 1:2, :]) + mod_ref[0, 0:1, :]
    p = _dot(h.astype(BF16), w_ref[...])
    for (off, wd, _, transposed), o_ref in zip(_LAT_SLOTS if latent else _CTX_SLOTS, outs):
        v = p[:, off:off + wd]
        if latent:
            if off == _QC:
                v = _rope(v, c64_ref[...], s64_ref[...], 16)
            elif off == _KC:
                v = _rope(v, c64_ref[:, :128], s64_ref[:, :128], 16)
            elif off == _KR:
                v = _rope(v, cm_ref[:, :128], sm_ref[:, :128], 8)
        if transposed:
            for b in range(o_ref.shape[0]):
                o_ref[b] = v[b * SEQ:(b + 1) * SEQ].T.astype(o_ref.dtype)
        elif off == _PD and not latent:
            ys = _pool_mix([v[b * SEQ:(b + 1) * SEQ] for b in range(v.shape[0] // SEQ)], pw_ref, ps_ref)
            for b, y in enumerate(ys):
                o_ref[b * SEQ:(b + 1) * SEQ, :] = y.astype(o_ref.dtype)
        else:
            o_ref[...] = v.astype(o_ref.dtype)


def _inproj_call(x2d, mod, w, layer, rope_tabs=None):
    t = x2d.shape[0]
    tm = 512
    latent = rope_tabs is not None
    slots = _LAT_SLOTS if latent else _CTX_SLOTS
    tiles_per_mod = t // tm // mod.shape[0]
    in_specs = [pl.BlockSpec((tm, D_MODEL), lambda i: (i, 0)),
                pl.BlockSpec((1, 6, D_MODEL), lambda i: (i // tiles_per_mod, 0, 0)),
                _layer_spec(w["g_attn_pre"], layer), _layer_spec(w["w_in_p"], layer)]
    args = [x2d, mod, w["g_attn_pre"], w["w_in_p"]]
    if latent:
        tiles_per_seq = DEC_SEQ // tm
        c64, s64, cm, sm = rope_tabs
        in_specs += [pl.BlockSpec((tm, 256), lambda i: (i % tiles_per_seq, 0)),
                     pl.BlockSpec((tm, 256), lambda i: (i % tiles_per_seq, 0)),
                     pl.BlockSpec((tm, 512), lambda i: (i % tiles_per_seq, 0)),
                     pl.BlockSpec((tm, 512), lambda i: (i % tiles_per_seq, 0))]
        args += [c64, s64, cm, sm]
    else:
        in_specs += [_layer_spec(w["pool_w"], layer), _layer_spec(w["pool_scale"], layer)]
        args += [w["pool_w"], w["pool_scale"]]
    return pl.pallas_call(
        functools.partial(_inproj_kernel, latent=latent),
        grid=(t // tm,),
        in_specs=in_specs,
        out_specs=[pl.BlockSpec((tm // SEQ, wd, SEQ), lambda i: (i, 0, 0)) if tr
                   else pl.BlockSpec((tm, wd), lambda i: (i, 0)) for _, wd, _, tr in slots],
        out_shape=[jax.ShapeDtypeStruct((t // SEQ, wd, SEQ) if tr else (t, wd), dt) for _, wd, dt, tr in slots],
        compiler_params=_params(1),
        name="inproj_lat" if latent else "inproj_ctx",
    )(*args)


def _ctx_attn_kernel(q_ref, kt_ref, v_ref, o_ref, *, seqs):
    q = q_ref[...]
    kt = kt_ref[...].astype(BF16)
    v = v_ref[...]
    scores = []
    for s in range(seqs):
        rows = slice(s * SEQ, (s + 1) * SEQ)
        for h in range(NA_HEADS):
            sl = slice(h * HEAD_DIM, (h + 1) * HEAD_DIM)
            scores.append(_dot(q[rows, sl], kt[s, sl, :]))
    probs = [_softmax_blocks([sc]) for sc in scores]
    outs = []
    for s in range(seqs):
        rows = slice(s * SEQ, (s + 1) * SEQ)
        ys = []
        for h in range(NA_HEADS):
            sl = slice(h * HEAD_DIM, (h + 1) * HEAD_DIM)
            (e,), l = probs[s * NA_HEADS + h]
            ys.append(_dot(e.astype(BF16), v[rows, sl]) / l)
        outs.append(jnp.concatenate(ys, -1))
    o_ref[...] = (outs[0] if seqs == 1 else jnp.concatenate(outs, 0)).astype(o_ref.dtype)


def _ctx_attn_call(q, k, v):
    seqs = CTX_SEQS
    t = q.shape[0]
    spec = pl.BlockSpec((seqs * SEQ, 256), lambda b: (b, 0))
    kt_spec = pl.BlockSpec((seqs, 256, SEQ), lambda b: (b, 0, 0))
    return pl.pallas_call(
        functools.partial(_ctx_attn_kernel, seqs=seqs), grid=(t // SEQ // seqs,),
        in_specs=[spec, kt_spec, spec], out_specs=spec,
        out_shape=jax.ShapeDtypeStruct((t, 256), BF16), compiler_params=_params(1), name="ctx_attn",
    )(q, k, v)


def _gqa_operands(q, k, rows_q, rows_k):
    group = SWA_HEADS // SWA_KV_HEADS
    out = []
    for kv in range(SWA_KV_HEADS):
        qs = jnp.concatenate([q[rows_q, (kv * group + g) * HEAD_DIM:(kv * group + g + 1) * HEAD_DIM]
                              for g in range(group)], 0)
        out.append((qs, k[rows_k, kv * HEAD_DIM:(kv + 1) * HEAD_DIM]))
    return out


def _gqa_sink(sink_ref, kv, m):
    group = SWA_HEADS // SWA_KV_HEADS
    row = lax.broadcasted_iota(jnp.int32, (group * m, 1), 0)
    col = jnp.full((group * m, 1), sink_ref[kv * group + group - 1] * LOG2E, F32)
    for g in range(group - 2, -1, -1):
        col = jnp.where(row < (g + 1) * m, sink_ref[kv * group + g] * LOG2E, col)
    return col


def _ctx_swa_kernel(sink_ref, q_ref, kt_ref, v_ref, o_ref, *, seqs):
    q = q_ref[...]
    kt = kt_ref[...].astype(BF16)
    v = v_ref[...]
    group = SWA_HEADS // SWA_KV_HEADS
    scores = []
    for s in range(seqs):
        rows = slice(s * SEQ, (s + 1) * SEQ)
        for kv, (qs, _) in enumerate(_gqa_operands(q, q, rows, rows)):
            scores.append(_dot(qs, kt[s, kv * HEAD_DIM:(kv + 1) * HEAD_DIM, :]))
    probs = [_softmax_blocks([sc], sink=_gqa_sink(sink_ref, i % SWA_KV_HEADS, SEQ)) for i, sc in enumerate(scores)]
    outs = []
    for s in range(seqs):
        rows = slice(s * SEQ, (s + 1) * SEQ)
        ys = []
        for kv in range(SWA_KV_HEADS):
            (e,), l = probs[s * SWA_KV_HEADS + kv]
            y = _dot(e.astype(BF16), v[rows, kv * HEAD_DIM:(kv + 1) * HEAD_DIM]) / l
            ys += [y[g * SEQ:(g + 1) * SEQ] for g in range(group)]
        outs.append(jnp.concatenate(ys, -1))
    o_ref[...] = (outs[0] if seqs == 1 else jnp.concatenate(outs, 0)).astype(o_ref.dtype)


def _ctx_swa_call(sink, q, k, v):
    seqs = CTX_SEQS
    t = q.shape[0]
    rows = seqs * SEQ
    return pl.pallas_call(
        functools.partial(_ctx_swa_kernel, seqs=seqs), grid=(t // rows,),
        in_specs=[pl.BlockSpec(memory_space=pltpu.SMEM),
                  pl.BlockSpec((rows, 256), lambda b: (b, 0)),
                  pl.BlockSpec((seqs, 128, SEQ), lambda b: (b, 0, 0)),
                  pl.BlockSpec((rows, 128), lambda b: (b, 0))],
        out_specs=pl.BlockSpec((rows, 256), lambda b: (b, 0)),
        out_shape=jax.ShapeDtypeStruct((t, 256), BF16), compiler_params=_params(1), name="ctx_swa",
    )(sink, q, k, v)


def _mla_q(cq, qn_ref, wuq_ref):
    return _dot(_rms(cq, qn_ref[...], MLA_Q_RANK).astype(BF16), wuq_ref[...])


def _mla_kv(ckv, kr, kvn_ref, wuk_ref, wuv_ref, values_t=False):
    cn = _rms(ckv, kvn_ref[...]).astype(BF16)
    kcat = _dot(cn, wuk_ref[...]) + jnp.concatenate([kr] * MLA_HEADS, -1)
    return kcat, (_dot_nt(wuv_ref[...], cn) if values_t else _dot(cn, wuv_ref[...]))


def _ctx_mla_kernel(cq_ref, ckv_ref, kr_ref, qn_ref, wuq_ref, kvn_ref, wuk_ref, wuv_ref, o_ref, *, seqs):
    q = (_mla_q(cq_ref[...].astype(F32), qn_ref, wuq_ref) * (MLA_SCALE * LOG2E)).astype(BF16)
    kcat, v = _mla_kv(ckv_ref[...], kr_ref[...], kvn_ref, wuk_ref, wuv_ref)
    kcat = kcat.astype(BF16)
    v = v.astype(BF16)
    scores = []
    for s in range(seqs):
        rows = slice(s * SEQ, (s + 1) * SEQ)
        for h in range(MLA_HEADS):
            sl = slice(h * LANE, (h + 1) * LANE)
            scores.append(_dot_nt(q[rows, sl], kcat[rows, sl]))
    probs = [_softmax_blocks([sc]) for sc in scores]
    outs = []
    for s in range(seqs):
        rows = slice(s * SEQ, (s + 1) * SEQ)
        ys = []
        for h in range(MLA_HEADS):
            (e,), l = probs[s * MLA_HEADS + h]
            ys.append(_dot(e.astype(BF16), v[rows, h * MLA_V:(h + 1) * MLA_V]) / l)
        outs.append(jnp.concatenate(ys, -1))
    o_ref[...] = (outs[0] if seqs == 1 else jnp.concatenate(outs, 0)).astype(o_ref.dtype)


_MLA_W = ("mla_qn", "mla_wuq", "mla_kvn", "mla_wuk", "mla_wuv")
_MLA_W_LAT = _MLA_W[:-1] + ("mla_wuv_t",)


def _ctx_mla_call(cq, ckv, kr, w, layer):
    seqs = CTX_SEQS
    t = cq.shape[0]
    rows = seqs * SEQ
    return pl.pallas_call(
        functools.partial(_ctx_mla_kernel, seqs=seqs), grid=(t // rows,),
        in_specs=[pl.BlockSpec((rows, 256), lambda b: (b, 0)),
                  pl.BlockSpec((rows, 128), lambda b: (b, 0)),
                  pl.BlockSpec((rows, 128), lambda b: (b, 0))] + [_layer_spec(w[k], layer) for k in _MLA_W],
        out_specs=pl.BlockSpec((rows, 256), lambda b: (b, 0)),
        out_shape=jax.ShapeDtypeStruct((t, 256), BF16), compiler_params=_params(1), name="ctx_mla",
    )(cq, ckv, kr, *[w[k] for k in _MLA_W])


_POOL_PAD = 8


def _pool_mix(xs, w_ref, sc_ref):
    n = xs[0].shape[0]
    ne = n + 2 * _POOL_PAD
    lo, hi = _POOL_PAD, _POOL_PAD + n
    z = jnp.zeros((_POOL_PAD, POOL_WIDTH), F32)
    grp = lax.broadcasted_iota(jnp.int32, (n, POOL_WIDTH), 1) >> 6
    t = lax.broadcasted_iota(jnp.int32, (n, POOL_WIDTH), 0)
    half = jnp.where(grp == 0, 1, jnp.where(grp == 1, 2, jnp.where(grp == 2, 4, 8)))
    cnt = (jnp.minimum(t + half, n) - jnp.maximum(t - half, 0)).astype(F32)

    def pair(a, s):
        return pltpu.roll(a, s, axis=0) + pltpu.roll(a, ne - s, axis=0)

    out = []
    for x in xs:
        xz = jnp.concatenate([z, x, z], 0)
        s2 = xz + pltpu.roll(xz, 1, axis=0)
        s4 = pair(s2, 1)
        s8 = pair(s4, 2)
        s16 = pair(s8, 4)
        tot = jnp.where(grp == 0, s2[lo:hi],
                        jnp.where(grp == 1, s4[lo:hi], jnp.where(grp == 2, s8[lo:hi], s16[lo:hi])))
        dlt = (tot / cnt - x).astype(BF16)
        out.append(_dot(dlt, w_ref[...]) * sc_ref[...])
    return out


def _pool_kernel(x_ref, w_ref, sc_ref, o_ref, *, n):
    xs = [x_ref[s * n:(s + 1) * n, :].astype(F32) for s in range(x_ref.shape[0] // n)]
    for s, y in enumerate(_pool_mix(xs, w_ref, sc_ref)):
        o_ref[s * n:(s + 1) * n, :] = y.astype(o_ref.dtype)


def _pool_call(pd, w, layer, seq):
    t = pd.shape[0]
    rows = seq
    return pl.pallas_call(
        functools.partial(_pool_kernel, n=seq), grid=(t // rows,),
        in_specs=[pl.BlockSpec((rows, POOL_WIDTH), lambda b: (b, 0)),
                  _layer_spec(w["pool_w"], layer), _layer_spec(w["pool_scale"], layer)],
        out_specs=pl.BlockSpec((rows, POOL_WIDTH), lambda b: (b, 0)),
        out_shape=jax.ShapeDtypeStruct((t, POOL_WIDTH), BF16), compiler_params=_params(1), name="pool",
    )(pd, w["pool_w"], w["pool_scale"])


def _lat_na_kernel(q_ref, k_ref, v_ref, kc_ref, vc_ref, t2_ref, o_ref):
    n = pl.program_id(1)
    rows = DEC_SEQ // GRID_W
    q_rows = NA_Q_BLOCK // GRID_W
    row0 = jnp.clip(q_rows * n - NA_WIN_R // 2, 0, rows - NA_SPAN // GRID_W)
    start = pl.multiple_of(row0 * GRID_W, LANE)
    q = q_ref[...]
    k = k_ref[pl.ds(start, NA_SPAN), :]
    v = v_ref[pl.ds(start, NA_SPAN), :]
    kc = kc_ref[...].astype(BF16)
    vc = vc_ref[...].astype(BF16)

    pairs = NA_SPAN // LANE
    low_half = lax.broadcasted_iota(jnp.int32, (GRID_W, LANE), 1) < GRID_W
    entries, masks = [], []
    for a in range(q_rows):
        r = q_rows * n + a
        r_start = jnp.clip(r - NA_WIN_R // 2, 0, rows - NA_WIN_R)
        for p in range(pairs):
            rk = row0 + 2 * p
            ok0 = ((rk >= r_start) & (rk < r_start + NA_WIN_R)).astype(jnp.int32)
            ok1 = ((rk + 1 >= r_start) & (rk + 1 < r_start + NA_WIN_R)).astype(jnp.int32)
            entries.append(jnp.clip(rk - r + NA_WIN_R, 0, NA_DR))
            masks.append(jnp.where(low_half, ok0, ok1) > 0)

    heads = [slice(h * HEAD_DIM, (h + 1) * HEAD_DIM) for h in range(NA_HEADS)]
    raw = [(_dot_nt(q[:, sl], k[:, sl]), _dot_nt(q[:, sl], kc[:, sl])) for sl in heads]
    probs = []
    for h, (s_loc, s_ctx) in enumerate(raw):
        cols = []
        for p in range(pairs):
            blk = [jnp.where(masks[a * pairs + p], t2_ref[h, entries[a * pairs + p]], NEG_INF)
                   for a in range(q_rows)]
            cols.append(jnp.concatenate(blk, 0))
        probs.append(_softmax_blocks([s_loc + jnp.concatenate(cols, 1), s_ctx]))
    ys = []
    for sl, ((e_loc, e_ctx), l) in zip(heads, probs):
        y = _dot(e_loc.astype(BF16), v[:, sl]) + _dot(e_ctx.astype(BF16), vc[:, sl])
        ys.append(y / l)
    o_ref[...] = jnp.concatenate(ys, -1).astype(o_ref.dtype)


def _lat_na_call(q, k, v, cache_k, cache_v, t2, layer):
    nq = DEC_SEQ // NA_Q_BLOCK
    seq_spec = pl.BlockSpec((DEC_SEQ, 256), lambda b, n: (b, 0))
    cache_spec = pl.BlockSpec((None, None, PAST_LEN, 256), lambda b, n: (b, layer, 0, 0))
    return pl.pallas_call(
        _lat_na_kernel, grid=(DEC_BATCH, nq),
        in_specs=[pl.BlockSpec((NA_Q_BLOCK, 256), lambda b, n: (b * nq + n, 0)), seq_spec, seq_spec,
                  cache_spec, cache_spec, _layer_spec(t2, layer)],
        out_specs=pl.BlockSpec((NA_Q_BLOCK, 256), lambda b, n: (b * nq + n, 0)),
        out_shape=jax.ShapeDtypeStruct((DEC_BATCH * DEC_SEQ, 256), BF16),
        compiler_params=_params(2), name="lat_na",
    )(q, k, v, cache_k, cache_v, t2)


def _lat_swa_kernel(sink_ref, q_ref, k_ref, v_ref, kc_ref, vc_ref, o_ref):
    n = pl.program_id(1)
    start = pl.multiple_of(jnp.clip(n - 1, 0, DEC_SEQ // Q_BLOCK - 3) * Q_BLOCK, LANE)
    q = q_ref[...]
    k = k_ref[pl.ds(start, SWA_SPAN), :]
    v = v_ref[pl.ds(start, SWA_SPAN), :]
    kc = kc_ref[...].astype(BF16)
    vc = vc_ref[...].astype(BF16)
    group = SWA_HEADS // SWA_KV_HEADS
    m = group * Q_BLOCK
    q_pos = n * Q_BLOCK + (lax.broadcasted_iota(jnp.int32, (m, SWA_SPAN), 0) & (Q_BLOCK - 1))
    k_pos = start + lax.broadcasted_iota(jnp.int32, (m, SWA_SPAN), 1)
    valid = jnp.abs(q_pos - k_pos) <= SWA_WINDOW
    everything = slice(None)
    raw = []
    for (qs, ks), (_, kcs) in zip(_gqa_operands(q, k, everything, everything),
                                  _gqa_operands(q, kc, everything, everything)):
        raw.append((_dot_nt(qs, ks), _dot_nt(qs, kcs)))
    probs = [_softmax_blocks([jnp.where(valid, s_loc, NEG_INF), s_ctx], sink=_gqa_sink(sink_ref, kv, Q_BLOCK))
             for kv, (s_loc, s_ctx) in enumerate(raw)]
    ys = []
    for kv, ((e_loc, e_ctx), l) in enumerate(probs):
        kvsl = slice(kv * HEAD_DIM, (kv + 1) * HEAD_DIM)
        y = (_dot(e_loc.astype(BF16), v[:, kvsl]) + _dot(e_ctx.astype(BF16), vc[:, kvsl])) / l
        ys += [y[g * Q_BLOCK:(g + 1) * Q_BLOCK] for g in range(group)]
    o_ref[...] = jnp.concatenate(ys, -1).astype(o_ref.dtype)


def _lat_swa_call(sink, q, k, v, cache_k, cache_v, layer):
    nq = DEC_SEQ // Q_BLOCK
    seq_spec = pl.BlockSpec((DEC_SEQ, 128), lambda b, n: (b, 0))
    cache_spec = pl.BlockSpec((None, None, PAST_LEN, 128), lambda b, n: (b, layer, 0, 0))
    return pl.pallas_call(
        _lat_swa_kernel, grid=(DEC_BATCH, nq),
        in_specs=[pl.BlockSpec(memory_space=pltpu.SMEM),
                  pl.BlockSpec((Q_BLOCK, 256), lambda b, n: (b * nq + n, 0)), seq_spec, seq_spec,
                  cache_spec, cache_spec],
        out_specs=pl.BlockSpec((Q_BLOCK, 256), lambda b, n: (b * nq + n, 0)),
        out_shape=jax.ShapeDtypeStruct((DEC_BATCH * DEC_SEQ, 256), BF16),
        compiler_params=_params(2), name="lat_swa",
    )(sink, q, k, v, cache_k, cache_v)


def _lat_mla_kernel(cq_ref, ckv_ref, kr_ref, cckv_ref, ckr_ref, cm_ref, sm_ref,
                    qn_ref, wuq_ref, kvn_ref, wuk_ref, wuvt_ref, o_ref, kcat_s, vt_s):
    @pl.when(pl.program_id(1) == 0)
    def _():
        kc, vc = _mla_kv(cckv_ref[...], ckr_ref[...], kvn_ref, wuk_ref, wuvt_ref, True)
        kcat_s[0:PAST_LEN, :] = kc.astype(BF16)
        vt_s[:, 0:PAST_LEN] = vc.astype(BF16)
        kl, vl = _mla_kv(ckv_ref[...].astype(F32), kr_ref[...].astype(F32), kvn_ref, wuk_ref, wuvt_ref, True)
        kcat_s[PAST_LEN:, :] = kl.astype(BF16)
        vt_s[:, PAST_LEN:] = vl.astype(BF16)

    q = _mla_q(cq_ref[...].astype(F32), qn_ref, wuq_ref)
    q = (_rope(q, cm_ref[...], sm_ref[...], 8) * (MLA_SCALE * LOG2E)).astype(BF16)
    scores = [_dot_nt(kcat_s[:, h * LANE:(h + 1) * LANE], q[:, h * LANE:(h + 1) * LANE]) for h in range(MLA_HEADS)]
    ys = []
    for h, st in enumerate(scores):
        e = jnp.exp2(st - jnp.max(st, 0, keepdims=True))
        l = jnp.sum(e, 0, keepdims=True)
        ys.append(_dot(vt_s[h * MLA_V:(h + 1) * MLA_V, :], e.astype(BF16)) / l)
    o_ref[...] = jnp.concatenate(ys, 0).T.astype(o_ref.dtype)


def _lat_mla_call(cq, ckv, kr, cache_ckv, cache_kr, cm, sm, w, layer):
    qb = MLA_Q_BLOCK
    nq = DEC_SEQ // qb
    seq_spec = pl.BlockSpec((DEC_SEQ, 128), lambda b, n: (b, 0))
    cache_spec = pl.BlockSpec((None, None, PAST_LEN, 128), lambda b, n: (b, layer, 0, 0))
    tab_spec = pl.BlockSpec((qb, 512), lambda b, n: (n, 0))
    return pl.pallas_call(
        _lat_mla_kernel, grid=(DEC_BATCH, nq),
        in_specs=[pl.BlockSpec((qb, 256), lambda b, n: (b * nq + n, 0)), seq_spec, seq_spec,
                  cache_spec, cache_spec, tab_spec, tab_spec] + [_layer_spec(w[k], layer) for k in _MLA_W_LAT],
        out_specs=pl.BlockSpec((qb, 256), lambda b, n: (b * nq + n, 0)),
        out_shape=jax.ShapeDtypeStruct((DEC_BATCH * DEC_SEQ, 256), BF16),
        scratch_shapes=[pltpu.VMEM((PAST_LEN + DEC_SEQ, MLA_HEADS * LANE), BF16),
                        pltpu.VMEM((MLA_HEADS * MLA_V, PAST_LEN + DEC_SEQ), BF16)],
        compiler_params=_params(2), name="lat_mla",
    )(cq, ckv, kr, cache_ckv, cache_kr, cm, sm, *[w[k] for k in _MLA_W_LAT])


_MERGE_PARTS = 2


def _merge_kernel(x_ref, mod_ref, gpre_ref, ya_ref, yb_ref, yc_ref, yd_ref,
                  wg_ref, bg_ref, wb_ref, wo_ref, gpost_ref, o_ref):
    tm = x_ref.shape[0] // _MERGE_PARTS
    for p in range(_MERGE_PARTS):
        rows = slice(p * tm, (p + 1) * tm)
        x = x_ref[rows, :]
        h = (_rms(x, gpre_ref[...]) * (1.0 + mod_ref[0, 1:2, :]) + mod_ref[0, 0:1, :]).astype(BF16)
        merged = None
        for k, y_ref in enumerate((ya_ref, yb_ref, yc_ref, yd_ref)):
            cols = slice(k * D_MODEL, (k + 1) * D_MODEL)
            gate = _sigmoid(_dot(h, wg_ref[:, cols]) + bg_ref[:, cols])
            term = gate * _dot(y_ref[rows, :], wb_ref[k])
            merged = term if merged is None else merged + term
        o = _dot(merged.astype(BF16), wo_ref[...])
        o_ref[rows, :] = x + mod_ref[0, 2:3, :] * _rms(o, gpost_ref[...])


_MERGE_W = ("w_gate", "b_gate", "w_branch", "w_out", "g_attn_post")


def _merge_call(x2d, mod, ys, w, layer):
    t = x2d.shape[0]
    tm = 512
    tiles_per_mod = t // tm // mod.shape[0]
    tile = pl.BlockSpec((tm, D_MODEL), lambda i: (i, 0))
    ytile = pl.BlockSpec((tm, BRANCH_W), lambda i: (i, 0))
    return pl.pallas_call(
        _merge_kernel, grid=(t // tm,),
        in_specs=[tile, pl.BlockSpec((1, 6, D_MODEL), lambda i: (i // tiles_per_mod, 0, 0)),
                  _layer_spec(w["g_attn_pre"], layer), ytile, ytile, ytile, ytile]
                 + [_layer_spec(w[k], layer) for k in _MERGE_W],
        out_specs=tile,
        out_shape=jax.ShapeDtypeStruct((t, D_MODEL), F32),
        compiler_params=_params(1), name="merge",
    )(x2d, mod, w["g_attn_pre"], *ys, *[w[k] for k in _MERGE_W])


_GAP = 8


def _ffn_kernel(*refs, seqs, halo):
    if halo:
        x_ref, xp_ref, xn_ref = refs[:3]
        refs = refs[3:]
    else:
        x_ref = refs[0]
        refs = refs[1:]
    mod_ref, gpre_ref, wa_ref, wg_ref, ca_ref, cg_ref, wd_ref, gpost_ref, o_ref = refs
    x = x_ref[...]
    tm = x.shape[0]
    shift, scale, gate = mod_ref[0, 3:4, :], mod_ref[0, 4:5, :], mod_ref[0, 5:6, :]

    def pre(xx):
        return _rms(xx, gpre_ref[...]) * (1.0 + scale) + shift

    h = pre(x)
    if halo:
        i = pl.program_id(0) % halo
        hp = jnp.where(i == 0, 0.0, pre(xp_ref[...]))
        hn = jnp.where(i == halo - 1, 0.0, pre(xn_ref[...]))
        pieces = [hp, h, hn]
        starts = [_GAP]
        seq_len = tm
    else:
        seq_len = tm // seqs
        pieces = [h]
        starts = [s * seq_len for s in range(seqs)]
    hb = jnp.concatenate(pieces, 0).astype(BF16)
    rows = hb.shape[0]
    edge_row = lax.broadcasted_iota(jnp.int32, (_GAP, FF_CHUNK), 0)

    def zero_edges(a, first):
        out = []
        for st in starts:
            seg = a[st:st + seq_len]
            if first:
                out += [jnp.where(edge_row == 0, 0.0, seg[:_GAP]), seg[_GAP:]]
            else:
                out += [seg[:seq_len - _GAP], jnp.where(edge_row == _GAP - 1, 0.0, seg[seq_len - _GAP:])]
        return jnp.concatenate(out, 0)

    def up(c):
        cols = slice(c * FF_CHUNK, (c + 1) * FF_CHUNK)
        return _dot(hb, wa_ref[:, cols]), _dot(hb, wg_ref[:, cols])

    def conv(u, c_ref, cols):
        prev = pltpu.roll(u, 1, axis=0)
        nxt = pltpu.roll(u, rows - 1, axis=0)
        if not halo:
            prev, nxt = zero_edges(prev, True), zero_edges(nxt, False)
        return prev * c_ref[0:1, cols] + u * c_ref[1:2, cols] + nxt * c_ref[2:3, cols]

    acts = []
    for c in range(D_FF_PAD // FF_CHUNK):
        ua, ug = up(c)
        cols = slice(c * FF_CHUNK, (c + 1) * FF_CHUNK)
        a = conv(ua, ca_ref, cols)
        g = conv(ug, cg_ref, cols)
        acts.append((g * _sigmoid(g) * a).astype(BF16))
    acc = _dot(jnp.concatenate(acts, 1), wd_ref[...])
    for s, st in enumerate(starts):
        ys = _rms(acc[st:st + seq_len], gpost_ref[...])
        o_ref[s * seq_len:(s + 1) * seq_len, :] = x[s * seq_len:(s + 1) * seq_len] + gate * ys


_FFN_W = ("g_ffn_pre", "ffn_wa", "ffn_wg", "ffn_ca", "ffn_cg", "ffn_wd", "g_ffn_post")


def _ffn_call(x2d, mod, w, layer, seq, tm):
    t = x2d.shape[0]
    tile = pl.BlockSpec((tm, D_MODEL), lambda i: (i, 0))
    in_specs = [tile]
    args = [x2d]
    if tm < seq:
        halo, seqs = seq // tm, 1
        r = tm // _GAP
        last = t // _GAP - 1
        in_specs += [pl.BlockSpec((_GAP, D_MODEL), lambda i: (jnp.maximum(i * r - 1, 0), 0)),
                     pl.BlockSpec((_GAP, D_MODEL), lambda i: (jnp.minimum((i + 1) * r, last), 0))]
        args += [x2d, x2d]
    else:
        halo, seqs = 0, tm // seq
    tiles_per_mod = t // tm // mod.shape[0]
    in_specs += [pl.BlockSpec((1, 6, D_MODEL), lambda i: (i // tiles_per_mod, 0, 0))]
    in_specs += [_layer_spec(w[k], layer) for k in _FFN_W]
    args += [mod] + [w[k] for k in _FFN_W]
    return pl.pallas_call(
        functools.partial(_ffn_kernel, seqs=seqs, halo=halo), grid=(t // tm,),
        in_specs=in_specs, out_specs=tile,
        out_shape=jax.ShapeDtypeStruct((t, D_MODEL), F32),
        compiler_params=_params(1), name="ffn",
    )(*args)


def _rope_tables():
    t = np.arange(DEC_SEQ)
    pos = (t // GRID_W, t % GRID_W)

    def tab(d):
        half = d // 4
        inv = np.float32(ROPE_BASE) ** (-np.arange(half, dtype=np.float32) / np.float32(half))
        cs, sn = [], []
        for p in pos:
            ang = p.astype(np.float32)[:, None] * inv[None, :]
            cs += [np.cos(ang), np.cos(ang)]
            sn += [-np.sin(ang), np.sin(ang)]
        return np.concatenate(cs, -1), np.concatenate(sn, -1)

    c64, s64 = tab(HEAD_DIM)
    c32, s32 = tab(MLA_ROPE)
    pad = LANE - MLA_NOPE - MLA_ROPE
    cm = np.concatenate([np.ones((DEC_SEQ, MLA_NOPE), np.float32), c32, np.ones((DEC_SEQ, pad), np.float32)], -1)
    sm = np.concatenate([np.zeros((DEC_SEQ, MLA_NOPE), np.float32), s32, np.zeros((DEC_SEQ, pad), np.float32)], -1)
    return tuple(jnp.asarray(np.tile(x, (1, 4)), F32) for x in (c64, s64, cm, sm))


def _na_table_kernel(rpb_ref, e_ref, ok_ref, o_ref):
    r = rpb_ref[...]
    r1 = r.astype(BF16)
    r2 = (r - r1.astype(F32)).astype(BF16)
    r3 = (r - r1.astype(F32) - r2.astype(F32)).astype(BF16)
    e = e_ref[...]
    t = _dot(r1, e) + _dot(r2, e) + _dot(r3, e)
    o_ref[...] = jnp.where(ok_ref[...] > 0, t * LOG2E, NEG_INF)


def _na_bias_tables(na_rpb):
    c = np.arange(GRID_W)[:, None]
    w = np.arange(GRID_W)[None, :]
    dc = (w - c + NA_WIN_C - 1).reshape(-1)
    onehot = (np.arange(LANE)[:, None] == dc[None, :]).astype(np.float32)
    c_start = np.clip(c - NA_WIN_C // 2, 0, GRID_W - NA_WIN_C)
    ok = ((w >= c_start) & (w < c_start + NA_WIN_C)).reshape(1, -1).astype(np.int32)
    rows = DEPTH * NA_HEADS * NA_DR
    rpb2 = jnp.pad(na_rpb.reshape(rows, NA_DC), ((0, LANE - rows), (0, LANE - NA_DC)))
    t = pl.pallas_call(
        _na_table_kernel, out_shape=jax.ShapeDtypeStruct((LANE, GRID_W * GRID_W), F32), name="na_table",
        compiler_params=pltpu.CompilerParams(vmem_limit_bytes=VMEM_LIMIT),
    )(rpb2, jnp.asarray(onehot, BF16), jnp.asarray(ok))
    t = t[:rows].reshape(DEPTH, NA_HEADS, NA_DR, GRID_W, GRID_W)
    t = jnp.pad(t, ((0, 0), (0, 0), (1, 1), (0, 0), (0, 0)), constant_values=NEG_INF)
    return jnp.concatenate([t[:, :, :-1], t[:, :, 1:]], -1)


def _pad_last(w, n):
    return jnp.pad(w, ((0, 0),) * (w.ndim - 1) + ((0, n - w.shape[-1]),))


def _prep_weights(g_attn_pre, g_attn_post, g_ffn_pre, g_ffn_post, w_in, w_gate, b_gate, mla_q_norm, mla_w_uq,
                  mla_kv_norm, mla_w_ukv, pool_w, pool_scale, w_branch, w_out, ffn_w_up, ffn_conv, ffn_w_down):
    q_scale = jnp.concatenate([jnp.full((256,), ATT_SCALE * LOG2E, F32), jnp.ones((512,), F32)])
    b0, c0, d0 = 768, 1120, 1632
    kr = jnp.pad(w_in[:, :, b0 + 320:b0 + 352], ((0, 0), (0, 0), (_KR_LANE, LANE - _KR_LANE - MLA_ROPE)))
    w_in_p = jnp.concatenate([w_in[:, :, :b0] * q_scale, _pad_last(w_in[:, :, b0:b0 + MLA_Q_RANK], 256),
                              w_in[:, :, b0 + MLA_Q_RANK:b0 + MLA_Q_RANK + MLA_KV_RANK], kr,
                              w_in[:, :, c0:d0] * q_scale[:512], w_in[:, :, d0:]], 2).astype(BF16)

    wuq = mla_w_uq.reshape(DEPTH, MLA_Q_RANK, MLA_HEADS, MLA_NOPE + MLA_ROPE)
    wuq = jnp.pad(wuq, ((0, 0), (0, 256 - MLA_Q_RANK), (0, 0), (0, LANE - MLA_NOPE - MLA_ROPE)))
    wukv = mla_w_ukv.reshape(DEPTH, MLA_KV_RANK, MLA_HEADS, MLA_NOPE + MLA_V)
    wuk = _pad_last(wukv[..., :MLA_NOPE], LANE)
    wuv = wukv[..., MLA_NOPE:].reshape(DEPTH, MLA_KV_RANK, MLA_HEADS * MLA_V).astype(BF16)

    eye = np.eye(len(POOL_WINDOWS), dtype=np.float32)
    w_bd = (pool_w[:, :, :, None, :] * eye[None, :, None, :, None]).reshape(DEPTH, POOL_WIDTH, POOL_WIDTH)

    return dict(
        g_attn_pre=g_attn_pre[:, None, :], g_attn_post=g_attn_post[:, None, :],
        g_ffn_pre=g_ffn_pre[:, None, :], g_ffn_post=g_ffn_post[:, None, :],
        w_in_p=w_in_p,
        mla_qn=_pad_last(mla_q_norm[:, None, :], 256),
        mla_wuq=wuq.reshape(DEPTH, 256, MLA_HEADS * LANE).astype(BF16),
        mla_kvn=mla_kv_norm[:, None, :],
        mla_wuk=wuk.reshape(DEPTH, MLA_KV_RANK, MLA_HEADS * LANE).astype(BF16),
        mla_wuv=wuv, mla_wuv_t=wuv.transpose(0, 2, 1),
        pool_w=w_bd.astype(BF16), pool_scale=pool_scale[:, None, :],
        w_gate=w_gate.astype(BF16), b_gate=b_gate[:, None, :],
        w_branch=w_branch.astype(BF16), w_out=w_out.astype(BF16),
        ffn_wa=_pad_last(ffn_w_up[:, :, :D_FF], D_FF_PAD).astype(BF16),
        ffn_wg=_pad_last(ffn_w_up[:, :, D_FF:], D_FF_PAD).astype(BF16),
        ffn_ca=_pad_last(ffn_conv[:, :, :D_FF], D_FF_PAD), ffn_cg=_pad_last(ffn_conv[:, :, D_FF:], D_FF_PAD),
        ffn_wd=jnp.pad(ffn_w_down, ((0, 0), (0, D_FF_PAD - D_FF), (0, 0))).astype(BF16))


def kernel(x_prompt, x_sample, cache_na_k, cache_na_v, cache_mla_ckv, cache_mla_krope, cache_swa_k, cache_swa_v, c, c_ctx, w_mod, b_mod, g_attn_pre, g_attn_post, g_ffn_pre, g_ffn_post, w_in, w_gate, b_gate, na_rpb, mla_q_norm, mla_w_uq, mla_kv_norm, mla_w_ukv, swa_sink, pool_w, pool_scale, w_branch, w_out, ffn_w_up, ffn_conv, ffn_w_down):
    x_p = x_prompt.reshape(BATCH * SEQ, D_MODEL)
    x_s = x_sample.reshape(DEC_BATCH * DEC_SEQ, D_MODEL)

    cv = jnp.concatenate([c_ctx[None, :], c, jnp.zeros((8 - 1 - DEC_BATCH, D_MODEL), F32)], 0)
    mod = _mod_call(cv, w_mod, b_mod).reshape(DEPTH, 8, 6, D_MODEL)
    w = _prep_weights(g_attn_pre, g_attn_post, g_ffn_pre, g_ffn_post, w_in, w_gate, b_gate, mla_q_norm, mla_w_uq,
                      mla_kv_norm, mla_w_ukv, pool_w, pool_scale, w_branch, w_out, ffn_w_up, ffn_conv, ffn_w_down)
    rope_tabs = _rope_tables()
    na_t2 = _na_bias_tables(na_rpb)
    cache_na_k = cache_na_k.reshape(DEC_BATCH, DEPTH, PAST_LEN, 256)
    cache_na_v = cache_na_v.reshape(DEC_BATCH, DEPTH, PAST_LEN, 256)
    cache_swa_k = cache_swa_k.reshape(DEC_BATCH, DEPTH, PAST_LEN, 128)
    cache_swa_v = cache_swa_v.reshape(DEC_BATCH, DEPTH, PAST_LEN, 128)
    cache_kr = jnp.pad(cache_mla_krope, ((0, 0), (0, 0), (0, 0), (_KR_LANE, LANE - _KR_LANE - MLA_ROPE)))

    states = []
    for l in range(DEPTH):
        mod_p = mod[l, 0:1]
        qa, ka_t, va_t, va, cq, ckv, kr, kr_t, qc, kc_t, vc_t, vc, yd = _inproj_call(x_p, mod_p, w, l)
        ys = (_ctx_attn_call(qa, ka_t, va), _ctx_mla_call(cq, ckv, kr, w, l),
              _ctx_swa_call(swa_sink[l], qc, kc_t, vc), yd)
        x_p = _merge_call(x_p, mod_p, ys, w, l)
        x_p = _ffn_call(x_p, mod_p, w, l, SEQ, 2 * SEQ)
        states.append((ka_t, va_t, ckv, kr_t[:, _KR_LANE:_KR_LANE + MLA_ROPE, :], kc_t, vc_t))

        mod_s = mod[l, 1:1 + DEC_BATCH]
        qa, ka, va, cq, ckv, kr, qc, kc, vc, pd = _inproj_call(x_s, mod_s, w, l, rope_tabs)
        ys = (_lat_na_call(qa, ka, va, cache_na_k, cache_na_v, na_t2, l),
              _lat_mla_call(cq, ckv, kr, cache_mla_ckv, cache_kr, rope_tabs[2], rope_tabs[3], w, l),
              _lat_swa_call(swa_sink[l], qc, kc, vc, cache_swa_k, cache_swa_v, l),
              _pool_call(pd, w, l, DEC_SEQ))
        x_s = _merge_call(x_s, mod_s, ys, w, l)
        x_s = _ffn_call(x_s, mod_s, w, l, DEC_SEQ, 1024)

    def stack(j, *tail):
        return jnp.stack([st[j].reshape(BATCH, SEQ, *tail) for st in states], 1)

    def stack_t(j, heads):
        a = jnp.stack([st[j] for st in states], 1).reshape(BATCH, DEPTH, heads, HEAD_DIM, SEQ)
        return a.transpose(0, 1, 4, 2, 3)

    return (x_p.reshape(BATCH, SEQ, D_MODEL), x_s.reshape(DEC_BATCH, DEC_SEQ, D_MODEL),
            stack_t(0, NA_HEADS), stack_t(1, NA_HEADS), stack(2, MLA_KV_RANK),
            jnp.stack([st[3] for st in states], 1).transpose(0, 1, 3, 2),
            stack_t(4, SWA_KV_HEADS), stack_t(5, SWA_KV_HEADS))
```

```python
import functools

import jax
import jax.numpy as jnp
import numpy as np
from jax import lax
from jax.experimental import pallas as pl
from jax.experimental.pallas import tpu as pltpu

F32 = jnp.float32
BF16 = jnp.bfloat16

D_MODEL = 1024
BATCH = 32
SEQ = 256
DEPTH = 2
DEC_BATCH = 2
DEC_SEQ = 2048
PAST_LEN = 256
GRID_W = 64
HEAD_DIM = 64
NA_HEADS = 4
NA_WIN_R = 8
NA_WIN_C = 16
MLA_HEADS = 4
MLA_NOPE = 64
MLA_ROPE = 32
MLA_V = 64
MLA_Q_RANK = 192
MLA_KV_RANK = 128
SWA_HEADS = 4
SWA_KV_HEADS = 2
SWA_WINDOW = 128
POOL_WINDOWS = (2, 4, 8, 16)
POOL_GROUP = 64
POOL_WIDTH = 256
BRANCH_W = 256
N_BRANCH = 4
D_FF = 2752
ROPE_BASE = 10000.0
EPS = 1e-6
NEG_INF = -1e30
ATT_SCALE = HEAD_DIM ** -0.5
MLA_SCALE = (MLA_NOPE + MLA_ROPE) ** -0.5
LOG2E = 1.4426950408889634

LANE = 128
D_FF_PAD = 2816
FF_CHUNK = 256
Q_BLOCK = 128
NA_Q_BLOCK = 256
NA_SPAN = 768
NA_DR = 2 * NA_WIN_R - 1
NA_DC = 2 * NA_WIN_C - 1
SWA_SPAN = 384
MLA_Q_BLOCK = 256
CTX_SEQS = 4
VMEM_LIMIT = 56 * 1024 * 1024

_QA, _KA, _VA, _CQ, _CKV, _KR, _QC, _KC, _VC, _PD = 0, 256, 512, 768, 1024, 1152, 1280, 1536, 1664, 1792
_KR_LANE = 64


def _dot(a, b):
    return jnp.dot(a, b, preferred_element_type=F32)


def _dot_nt(a, b):
    return lax.dot_general(a, b, (((1,), (1,)), ((), ())), preferred_element_type=F32)


def _sigmoid(x):
    return 1.0 / (1.0 + jnp.exp2(x * -LOG2E))


def _rms(x, g, n=None):
    n = x.shape[-1] if n is None else n
    ms = jnp.sum(x * x, -1, keepdims=True) * (1.0 / n)
    return x * lax.rsqrt(ms + EPS) * g


def _softmax_blocks(blocks, sink=None):
    m = None
    for s in blocks:
        mm = jnp.max(s, -1, keepdims=True)
        m = mm if m is None else jnp.maximum(m, mm)
    if sink is not None:
        m = jnp.maximum(m, sink)
    es = [jnp.exp2(s - m) for s in blocks]
    l = None
    for e in es:
        ll = jnp.sum(e, -1, keepdims=True)
        l = ll if l is None else l + ll
    if sink is not None:
        l = l + jnp.exp2(sink - m)
    return es, l


def _rope(x, cos, sin, q):
    w = x.shape[-1]
    lane = lax.broadcasted_iota(jnp.int32, x.shape, 1)
    up = pltpu.roll(x, w - q, axis=1)
    dn = pltpu.roll(x, q, axis=1)
    partner = jnp.where((lane & (2 * q - 1)) < q, up, dn)
    return x * cos + partner * sin


def _const_spec(shape):
    n = len(shape)
    return pl.BlockSpec(shape, lambda *_: (0,) * n, pipeline_mode=pl.Buffered(1))


def _layer_spec(arr, layer):
    n = arr.ndim - 1
    return pl.BlockSpec((None,) + arr.shape[1:], lambda *_: (layer,) + (0,) * n, pipeline_mode=pl.Buffered(1))


def _params(n_axes):
    return pltpu.CompilerParams(dimension_semantics=("arbitrary",) * n_axes, vmem_limit_bytes=VMEM_LIMIT)


def _mod_kernel(cv_ref, w_ref, b_ref, o_ref):
    cv = cv_ref[...]
    a = (cv * _sigmoid(cv)).astype(BF16)
    o_ref[0] = _dot(a, w_ref[0].astype(BF16)) + b_ref[0]


def _mod_call(cv, w_mod, b_mod):
    tn = 2048
    return pl.pallas_call(
        _mod_kernel,
        grid=(DEPTH, 6 * D_MODEL // tn),
        in_specs=[_const_spec((8, D_MODEL)),
                  pl.BlockSpec((1, D_MODEL, tn), lambda l, j: (l, 0, j)),
                  pl.BlockSpec((1, 1, tn), lambda l, j: (l, 0, j))],
        out_specs=pl.BlockSpec((1, 8, tn), lambda l, j: (l, 0, j)),
        out_shape=jax.ShapeDtypeStruct((DEPTH, 8, 6 * D_MODEL), F32),
        compiler_params=_params(2),
        name="mod",
    )(cv, w_mod, b_mod.reshape(DEPTH, 1, 6 * D_MODEL))


_IN_SLOTS = ((_QA, 256), (_KA, 256), (_VA, 256), (_CQ, 256), (_CKV, 128), (_KR, 128),
             (_QC, 256), (_KC, 128), (_VC, 128), (_PD, 256))
_CTX_SLOTS = ((_QA, 256, BF16, False), (_KA, 256, F32, True), (_VA, 256, F32, True), (_VA, 256, BF16, False),
              (_CQ, 256, BF16, False), (_CKV, 128, F32, False), (_KR, 128, F32, False), (_KR, 128, F32, True),
              (_QC, 256, BF16, False), (_KC, 128, F32, True), (_VC, 128, F32, True), (_VC, 128, BF16, False),
              (_PD, 256, BF16, False))
_LAT_SLOTS = tuple((off, wd, BF16, False) for off, wd in _IN_SLOTS)


def _inproj_kernel(*refs, latent):
    if latent:
        x_ref, mod_ref, g_ref, w_ref, c64_ref, s64_ref, cm_ref, sm_ref = refs[:8]
        outs = refs[8:]
    else:
        x_ref, mod_ref, g_ref, w_ref, pw_ref, ps_ref = refs[:6]
        outs = refs[6:]
    x = x_ref[...]
    h = _rms(x, g_ref[...]) * (1.0 + mod_ref[0, 1:2, :]) + mod_ref[0, 0:1, :]
    p = _dot_nt(h.astype(BF16), w_ref[...])
    for (off, wd, _, transposed), o_ref in zip(_LAT_SLOTS if latent else _CTX_SLOTS, outs):
        v = p[:, off:off + wd]
        if latent:
            if off == _QC:
                v = _rope(v, c64_ref[...], s64_ref[...], 16)
            elif off == _KC:
                v = _rope(v, c64_ref[:, :128], s64_ref[:, :128], 16)
            elif off == _KR:
                v = _rope(v, cm_ref[:, :128], sm_ref[:, :128], 8)
        if transposed:
            for b in range(o_ref.shape[0]):
                o_ref[b] = v[b * SEQ:(b + 1) * SEQ].T.astype(o_ref.dtype)
        elif off == _PD and not latent:
            ys = _pool_mix([v[b * SEQ:(b + 1) * SEQ] for b in range(v.shape[0] // SEQ)], pw_ref, ps_ref)
            for b, y in enumerate(ys):
                o_ref[b * SEQ:(b + 1) * SEQ, :] = y.astype(o_ref.dtype)
        else:
            o_ref[...] = v.astype(o_ref.dtype)


def _inproj_call(x2d, mod, w, layer, rope_tabs=None):
    t = x2d.shape[0]
    tm = 512
    latent = rope_tabs is not None
    slots = _LAT_SLOTS if latent else _CTX_SLOTS
    tiles_per_mod = t // tm // mod.shape[0]
    in_specs = [pl.BlockSpec((tm, D_MODEL), lambda i: (i, 0)),
                pl.BlockSpec((1, 6, D_MODEL), lambda i: (i // tiles_per_mod, 0, 0)),
                _layer_spec(w["g_attn_pre"], layer), _layer_spec(w["w_in_p"], layer)]
    args = [x2d, mod, w["g_attn_pre"], w["w_in_p"]]
    if latent:
        tiles_per_seq = DEC_SEQ // tm
        c64, s64, cm, sm = rope_tabs
        in_specs += [pl.BlockSpec((tm, 256), lambda i: (i % tiles_per_seq, 0)),
                     pl.BlockSpec((tm, 256), lambda i: (i % tiles_per_seq, 0)),
                     pl.BlockSpec((tm, 512), lambda i: (i % tiles_per_seq, 0)),
                     pl.BlockSpec((tm, 512), lambda i: (i % tiles_per_seq, 0))]
        args += [c64, s64, cm, sm]
    else:
        in_specs += [_layer_spec(w["pool_w"], layer), _layer_spec(w["pool_scale"], layer)]
        args += [w["pool_w"], w["pool_scale"]]
    return pl.pallas_call(
        functools.partial(_inproj_kernel, latent=latent),
        grid=(t // tm,),
        in_specs=in_specs,
        out_specs=[pl.BlockSpec((tm // SEQ, wd, SEQ), lambda i: (i, 0, 0)) if tr
                   else pl.BlockSpec((tm, wd), lambda i: (i, 0)) for _, wd, _, tr in slots],
        out_shape=[jax.ShapeDtypeStruct((t // SEQ, wd, SEQ) if tr else (t, wd), dt) for _, wd, dt, tr in slots],
        compiler_params=_params(1),
        name="inproj_lat" if latent else "inproj_ctx",
    )(*args)


def _ctx_attn_kernel(q_ref, kt_ref, v_ref, o_ref, *, seqs):
    q = q_ref[...]
    kt = kt_ref[...].astype(BF16)
    v = v_ref[...]
    scores = []
    for s in range(seqs):
        rows = slice(s * SEQ, (s + 1) * SEQ)
        for h in range(NA_HEADS):
            sl = slice(h * HEAD_DIM, (h + 1) * HEAD_DIM)
            scores.append(_dot(q[rows, sl], kt[s, sl, :]))
    probs = [_softmax_blocks([sc]) for sc in scores]
    outs = []
    for s in range(seqs):
        rows = slice(s * SEQ, (s + 1) * SEQ)
        ys = []
        for h in range(NA_HEADS):
            sl = slice(h * HEAD_DIM, (h + 1) * HEAD_DIM)
            (e,), l = probs[s * NA_HEADS + h]
            ys.append(_dot(e.astype(BF16), v[rows, sl]) / l)
        outs.append(jnp.concatenate(ys, -1))
    o_ref[...] = (outs[0] if seqs == 1 else jnp.concatenate(outs, 0)).astype(o_ref.dtype)


def _ctx_attn_call(q, k, v):
    seqs = CTX_SEQS
    t = q.shape[0]
    spec = pl.BlockSpec((seqs * SEQ, 256), lambda b: (b, 0))
    kt_spec = pl.BlockSpec((seqs, 256, SEQ), lambda b: (b, 0, 0))
    return pl.pallas_call(
        functools.partial(_ctx_attn_kernel, seqs=seqs), grid=(t // SEQ // seqs,),
        in_specs=[spec, kt_spec, spec], out_specs=spec,
        out_shape=jax.ShapeDtypeStruct((t, 256), BF16), compiler_params=_params(1), name="ctx_attn",
    )(q, k, v)


def _gqa_operands(q, k, rows_q, rows_k):
    group = SWA_HEADS // SWA_KV_HEADS
    out = []
    for kv in range(SWA_KV_HEADS):
        qs = jnp.concatenate([q[rows_q, (kv * group + g) * HEAD_DIM:(kv * group + g + 1) * HEAD_DIM]
                              for g in range(group)], 0)
        out.append((qs, k[rows_k, kv * HEAD_DIM:(kv + 1) * HEAD_DIM]))
    return out


def _gqa_sink(sink_ref, kv, m):
    group = SWA_HEADS // SWA_KV_HEADS
    row = lax.broadcasted_iota(jnp.int32, (group * m, 1), 0)
    col = jnp.full((group * m, 1), sink_ref[kv * group + group - 1] * LOG2E, F32)
    for g in range(group - 2, -1, -1):
        col = jnp.where(row < (g + 1) * m, sink_ref[kv * group + g] * LOG2E, col)
    return col


def _ctx_swa_kernel(sink_ref, q_ref, kt_ref, v_ref, o_ref, *, seqs):
    q = q_ref[...]
    kt = kt_ref[...].astype(BF16)
    v = v_ref[...]
    group = SWA_HEADS // SWA_KV_HEADS
    scores = []
    for s in range(seqs):
        rows = slice(s * SEQ, (s + 1) * SEQ)
        for kv, (qs, _) in enumerate(_gqa_operands(q, q, rows, rows)):
            scores.append(_dot(qs, kt[s, kv * HEAD_DIM:(kv + 1) * HEAD_DIM, :]))
    probs = [_softmax_blocks([sc], sink=_gqa_sink(sink_ref, i % SWA_KV_HEADS, SEQ)) for i, sc in enumerate(scores)]
    outs = []
    for s in range(seqs):
        rows = slice(s * SEQ, (s + 1) * SEQ)
        ys = []
        for kv in range(SWA_KV_HEADS):
            (e,), l = probs[s * SWA_KV_HEADS + kv]
            y = _dot(e.astype(BF16), v[rows, kv * HEAD_DIM:(kv + 1) * HEAD_DIM]) / l
            ys += [y[g * SEQ:(g + 1) * SEQ] for g in range(group)]
        outs.append(jnp.concatenate(ys, -1))
    o_ref[...] = (outs[0] if seqs == 1 else jnp.concatenate(outs, 0)).astype(o_ref.dtype)


def _ctx_swa_call(sink, q, k, v):
    seqs = CTX_SEQS
    t = q.shape[0]
    rows = seqs * SEQ
    return pl.pallas_call(
        functools.partial(_ctx_swa_kernel, seqs=seqs), grid=(t // rows,),
        in_specs=[pl.BlockSpec(memory_space=pltpu.SMEM),
                  pl.BlockSpec((rows, 256), lambda b: (b, 0)),
                  pl.BlockSpec((seqs, 128, SEQ), lambda b: (b, 0, 0)),
                  pl.BlockSpec((rows, 128), lambda b: (b, 0))],
        out_specs=pl.BlockSpec((rows, 256), lambda b: (b, 0)),
        out_shape=jax.ShapeDtypeStruct((t, 256), BF16), compiler_params=_params(1), name="ctx_swa",
    )(sink, q, k, v)


def _mla_q(cq, qn_ref, wuq_ref):
    return _dot(_rms(cq, qn_ref[...], MLA_Q_RANK).astype(BF16), wuq_ref[...])


def _mla_kv(ckv, kr, kvn_ref, wuk_ref, wuv_ref, values_t=False):
    cn = _rms(ckv, kvn_ref[...]).astype(BF16)
    kcat = _dot(cn, wuk_ref[...]) + jnp.concatenate([kr] * MLA_HEADS, -1)
    return kcat, (_dot_nt(wuv_ref[...], cn) if values_t else _dot(cn, wuv_ref[...]))


def _ctx_mla_kernel(cq_ref, ckv_ref, kr_ref, qn_ref, wuq_ref, kvn_ref, wuk_ref, wuv_ref, o_ref, *, seqs):
    q = (_mla_q(cq_ref[...].astype(F32), qn_ref, wuq_ref) * (MLA_SCALE * LOG2E)).astype(BF16)
    kcat, v = _mla_kv(ckv_ref[...], kr_ref[...], kvn_ref, wuk_ref, wuv_ref)
    kcat = kcat.astype(BF16)
    v = v.astype(BF16)
    scores = []
    for s in range(seqs):
        rows = slice(s * SEQ, (s + 1) * SEQ)
        for h in range(MLA_HEADS):
            sl = slice(h * LANE, (h + 1) * LANE)
            scores.append(_dot_nt(q[rows, sl], kcat[rows, sl]))
    probs = [_softmax_blocks([sc]) for sc in scores]
    outs = []
    for s in range(seqs):
        rows = slice(s * SEQ, (s + 1) * SEQ)
        ys = []
        for h in range(MLA_HEADS):
            (e,), l = probs[s * MLA_HEADS + h]
            ys.append(_dot(e.astype(BF16), v[rows, h * MLA_V:(h + 1) * MLA_V]) / l)
        outs.append(jnp.concatenate(ys, -1))
    o_ref[...] = (outs[0] if seqs == 1 else jnp.concatenate(outs, 0)).astype(o_ref.dtype)


_MLA_W = ("mla_qn", "mla_wuq", "mla_kvn", "mla_wuk", "mla_wuv")
_MLA_W_LAT = _MLA_W[:-1] + ("mla_wuv_t",)


def _ctx_mla_call(cq, ckv, kr, w, layer):
    seqs = CTX_SEQS
    t = cq.shape[0]
    rows = seqs * SEQ
    return pl.pallas_call(
        functools.partial(_ctx_mla_kernel, seqs=seqs), grid=(t // rows,),
        in_specs=[pl.BlockSpec((rows, 256), lambda b: (b, 0)),
                  pl.BlockSpec((rows, 128), lambda b: (b, 0)),
                  pl.BlockSpec((rows, 128), lambda b: (b, 0))] + [_layer_spec(w[k], layer) for k in _MLA_W],
        out_specs=pl.BlockSpec((rows, 256), lambda b: (b, 0)),
        out_shape=jax.ShapeDtypeStruct((t, 256), BF16), compiler_params=_params(1), name="ctx_mla",
    )(cq, ckv, kr, *[w[k] for k in _MLA_W])


_POOL_PAD = 8


def _pool_mix(xs, w_ref, sc_ref):
    n = xs[0].shape[0]
    ne = n + 2 * _POOL_PAD
    lo, hi = _POOL_PAD, _POOL_PAD + n
    z = jnp.zeros((_POOL_PAD, POOL_WIDTH), F32)
    grp = lax.broadcasted_iota(jnp.int32, (n, POOL_WIDTH), 1) >> 6
    t = lax.broadcasted_iota(jnp.int32, (n, POOL_WIDTH), 0)
    half = jnp.where(grp == 0, 1, jnp.where(grp == 1, 2, jnp.where(grp == 2, 4, 8)))
    cnt = (jnp.minimum(t + half, n) - jnp.maximum(t - half, 0)).astype(F32)

    def pair(a, s):
        return pltpu.roll(a, s, axis=0) + pltpu.roll(a, ne - s, axis=0)

    out = []
    for x in xs:
        xz = jnp.concatenate([z, x, z], 0)
        s2 = xz + pltpu.roll(xz, 1, axis=0)
        s4 = pair(s2, 1)
        s8 = pair(s4, 2)
        s16 = pair(s8, 4)
        tot = jnp.where(grp == 0, s2[lo:hi],
                        jnp.where(grp == 1, s4[lo:hi], jnp.where(grp == 2, s8[lo:hi], s16[lo:hi])))
        dlt = (tot / cnt - x).astype(BF16)
        out.append(_dot(dlt, w_ref[...]) * sc_ref[...])
    return out


def _pool_kernel(x_ref, w_ref, sc_ref, o_ref, *, n):
    xs = [x_ref[s * n:(s + 1) * n, :].astype(F32) for s in range(x_ref.shape[0] // n)]
    for s, y in enumerate(_pool_mix(xs, w_ref, sc_ref)):
        o_ref[s * n:(s + 1) * n, :] = y.astype(o_ref.dtype)


def _pool_call(pd, w, layer, seq):
    t = pd.shape[0]
    rows = seq
    return pl.pallas_call(
        functools.partial(_pool_kernel, n=seq), grid=(t // rows,),
        in_specs=[pl.BlockSpec((rows, POOL_WIDTH), lambda b: (b, 0)),
                  _layer_spec(w["pool_w"], layer), _layer_spec(w["pool_scale"], layer)],
        out_specs=pl.BlockSpec((rows, POOL_WIDTH), lambda b: (b, 0)),
        out_shape=jax.ShapeDtypeStruct((t, POOL_WIDTH), BF16), compiler_params=_params(1), name="pool",
    )(pd, w["pool_w"], w["pool_scale"])


def _lat_na_kernel(q_ref, k_ref, v_ref, kc_ref, vc_ref, t2_ref, o_ref):
    n = pl.program_id(1)
    rows = DEC_SEQ // GRID_W
    q_rows = NA_Q_BLOCK // GRID_W
    row0 = jnp.clip(q_rows * n - NA_WIN_R // 2, 0, rows - NA_SPAN // GRID_W)
    start = pl.multiple_of(row0 * GRID_W, LANE)
    q = q_ref[...]
    k = k_ref[pl.ds(start, NA_SPAN), :]
    v = v_ref[pl.ds(start, NA_SPAN), :]
    kc = kc_ref[...].astype(BF16)
    vc = vc_ref[...].astype(BF16)

    pairs = NA_SPAN // LANE
    low_half = lax.broadcasted_iota(jnp.int32, (GRID_W, LANE), 1) < GRID_W
    entries, masks = [], []
    for a in range(q_rows):
        r = q_rows * n + a
        r_start = jnp.clip(r - NA_WIN_R // 2, 0, rows - NA_WIN_R)
        for p in range(pairs):
            rk = row0 + 2 * p
            ok0 = ((rk >= r_start) & (rk < r_start + NA_WIN_R)).astype(jnp.int32)
            ok1 = ((rk + 1 >= r_start) & (rk + 1 < r_start + NA_WIN_R)).astype(jnp.int32)
            entries.append(jnp.clip(rk - r + NA_WIN_R, 0, NA_DR))
            masks.append(jnp.where(low_half, ok0, ok1) > 0)

    heads = [slice(h * HEAD_DIM, (h + 1) * HEAD_DIM) for h in range(NA_HEADS)]
    raw = [(_dot_nt(q[:, sl], k[:, sl]), _dot_nt(q[:, sl], kc[:, sl])) for sl in heads]
    probs = []
    for h, (s_loc, s_ctx) in enumerate(raw):
        cols = []
        for p in range(pairs):
            blk = [jnp.where(masks[a * pairs + p], t2_ref[h, entries[a * pairs + p]], NEG_INF)
                   for a in range(q_rows)]
            cols.append(jnp.concatenate(blk, 0))
        probs.append(_softmax_blocks([s_loc + jnp.concatenate(cols, 1), s_ctx]))
    ys = []
    for sl, ((e_loc, e_ctx), l) in zip(heads, probs):
        y = _dot(e_loc.astype(BF16), v[:, sl]) + _dot(e_ctx.astype(BF16), vc[:, sl])
        ys.append(y / l)
    o_ref[...] = jnp.concatenate(ys, -1).astype(o_ref.dtype)


def _lat_na_call(q, k, v, cache_k, cache_v, t2, layer):
    nq = DEC_SEQ // NA_Q_BLOCK
    seq_spec = pl.BlockSpec((DEC_SEQ, 256), lambda b, n: (b, 0))
    cache_spec = pl.BlockSpec((None, None, PAST_LEN, 256), lambda b, n: (b, layer, 0, 0))
    return pl.pallas_call(
        _lat_na_kernel, grid=(DEC_BATCH, nq),
        in_specs=[pl.BlockSpec((NA_Q_BLOCK, 256), lambda b, n: (b * nq + n, 0)), seq_spec, seq_spec,
                  cache_spec, cache_spec, _layer_spec(t2, layer)],
        out_specs=pl.BlockSpec((NA_Q_BLOCK, 256), lambda b, n: (b * nq + n, 0)),
        out_shape=jax.ShapeDtypeStruct((DEC_BATCH * DEC_SEQ, 256), BF16),
        compiler_params=_params(2), name="lat_na",
    )(q, k, v, cache_k, cache_v, t2)


def _lat_swa_kernel(sink_ref, q_ref, k_ref, v_ref, kc_ref, vc_ref, o_ref):
    n = pl.program_id(1)
    start = pl.multiple_of(jnp.clip(n - 1, 0, DEC_SEQ // Q_BLOCK - 3) * Q_BLOCK, LANE)
    q = q_ref[...]
    k = k_ref[pl.ds(start, SWA_SPAN), :]
    v = v_ref[pl.ds(start, SWA_SPAN), :]
    kc = kc_ref[...].astype(BF16)
    vc = vc_ref[...].astype(BF16)
    group = SWA_HEADS // SWA_KV_HEADS
    m = group * Q_BLOCK
    q_pos = n * Q_BLOCK + (lax.broadcasted_iota(jnp.int32, (m, SWA_SPAN), 0) & (Q_BLOCK - 1))
    k_pos = start + lax.broadcasted_iota(jnp.int32, (m, SWA_SPAN), 1)
    valid = jnp.abs(q_pos - k_pos) <= SWA_WINDOW
    everything = slice(None)
    raw = []
    for (qs, ks), (_, kcs) in zip(_gqa_operands(q, k, everything, everything),
                                  _gqa_operands(q, kc, everything, everything)):
        raw.append((_dot_nt(qs, ks), _dot_nt(qs, kcs)))
    probs = [_softmax_blocks([jnp.where(valid, s_loc, NEG_INF), s_ctx], sink=_gqa_sink(sink_ref, kv, Q_BLOCK))
             for kv, (s_loc, s_ctx) in enumerate(raw)]
    ys = []
    for kv, ((e_loc, e_ctx), l) in enumerate(probs):
        kvsl = slice(kv * HEAD_DIM, (kv + 1) * HEAD_DIM)
        y = (_dot(e_loc.astype(BF16), v[:, kvsl]) + _dot(e_ctx.astype(BF16), vc[:, kvsl])) / l
        ys += [y[g * Q_BLOCK:(g + 1) * Q_BLOCK] for g in range(group)]
    o_ref[...] = jnp.concatenate(ys, -1).astype(o_ref.dtype)


def _lat_swa_call(sink, q, k, v, cache_k, cache_v, layer):
    nq = DEC_SEQ // Q_BLOCK
    seq_spec = pl.BlockSpec((DEC_SEQ, 128), lambda b, n: (b, 0))
    cache_spec = pl.BlockSpec((None, None, PAST_LEN, 128), lambda b, n: (b, layer, 0, 0))
    return pl.pallas_call(
        _lat_swa_kernel, grid=(DEC_BATCH, nq),
        in_specs=[pl.BlockSpec(memory_space=pltpu.SMEM),
                  pl.BlockSpec((Q_BLOCK, 256), lambda b, n: (b * nq + n, 0)), seq_spec, seq_spec,
                  cache_spec, cache_spec],
        out_specs=pl.BlockSpec((Q_BLOCK, 256), lambda b, n: (b * nq + n, 0)),
        out_shape=jax.ShapeDtypeStruct((DEC_BATCH * DEC_SEQ, 256), BF16),
        compiler_params=_params(2), name="lat_swa",
    )(sink, q, k, v, cache_k, cache_v)


def _lat_mla_kernel(cq_ref, ckv_ref, kr_ref, cckv_ref, ckr_ref, cm_ref, sm_ref,
                    qn_ref, wuq_ref, kvn_ref, wuk_ref, wuvt_ref, o_ref, kcat_s, vt_s):
    @pl.when(pl.program_id(1) == 0)
    def _():
        kc, vc = _mla_kv(cckv_ref[...], ckr_ref[...], kvn_ref, wuk_ref, wuvt_ref, True)
        kcat_s[0:PAST_LEN, :] = kc.astype(BF16)
        vt_s[:, 0:PAST_LEN] = vc.astype(BF16)
        kl, vl = _mla_kv(ckv_ref[...].astype(F32), kr_ref[...].astype(F32), kvn_ref, wuk_ref, wuvt_ref, True)
        kcat_s[PAST_LEN:, :] = kl.astype(BF16)
        vt_s[:, PAST_LEN:] = vl.astype(BF16)

    q = _mla_q(cq_ref[...].astype(F32), qn_ref, wuq_ref)
    q = (_rope(q, cm_ref[...], sm_ref[...], 8) * (MLA_SCALE * LOG2E)).astype(BF16)
    scores = [_dot_nt(kcat_s[:, h * LANE:(h + 1) * LANE], q[:, h * LANE:(h + 1) * LANE]) for h in range(MLA_HEADS)]
    ys = []
    for h, st in enumerate(scores):
        e = jnp.exp2(st - jnp.max(st, 0, keepdims=True))
        l = jnp.sum(e, 0, keepdims=True)
        ys.append(_dot(vt_s[h * MLA_V:(h + 1) * MLA_V, :], e.astype(BF16)) / l)
    o_ref[...] = jnp.concatenate(ys, 0).T.astype(o_ref.dtype)


def _lat_mla_call(cq, ckv, kr, cache_ckv, cache_kr, cm, sm, w, layer):
    qb = MLA_Q_BLOCK
    nq = DEC_SEQ // qb
    seq_spec = pl.BlockSpec((DEC_SEQ, 128), lambda b, n: (b, 0))
    cache_spec = pl.BlockSpec((None, None, PAST_LEN, 128), lambda b, n: (b, layer, 0, 0))
    tab_spec = pl.BlockSpec((qb, 512), lambda b, n: (n, 0))
    return pl.pallas_call(
        _lat_mla_kernel, grid=(DEC_BATCH, nq),
        in_specs=[pl.BlockSpec((qb, 256), lambda b, n: (b * nq + n, 0)), seq_spec, seq_spec,
                  cache_spec, cache_spec, tab_spec, tab_spec] + [_layer_spec(w[k], layer) for k in _MLA_W_LAT],
        out_specs=pl.BlockSpec((qb, 256), lambda b, n: (b * nq + n, 0)),
        out_shape=jax.ShapeDtypeStruct((DEC_BATCH * DEC_SEQ, 256), BF16),
        scratch_shapes=[pltpu.VMEM((PAST_LEN + DEC_SEQ, MLA_HEADS * LANE), BF16),
                        pltpu.VMEM((MLA_HEADS * MLA_V, PAST_LEN + DEC_SEQ), BF16)],
        compiler_params=_params(2), name="lat_mla",
    )(cq, ckv, kr, cache_ckv, cache_kr, cm, sm, *[w[k] for k in _MLA_W_LAT])


_MERGE_PARTS = 2


def _merge_kernel(x_ref, mod_ref, gpre_ref, ya_ref, yb_ref, yc_ref, yd_ref,
                  wg_ref, bg_ref, wb_ref, wo_ref, gpost_ref, o_ref):
    tm = x_ref.shape[0] // _MERGE_PARTS
    for p in range(_MERGE_PARTS):
        rows = slice(p * tm, (p + 1) * tm)
        x = x_ref[rows, :]
        h = (_rms(x, gpre_ref[...]) * (1.0 + mod_ref[0, 1:2, :]) + mod_ref[0, 0:1, :]).astype(BF16)
        merged = None
        for k, y_ref in enumerate((ya_ref, yb_ref, yc_ref, yd_ref)):
            cols = slice(k * D_MODEL, (k + 1) * D_MODEL)
            gate = _sigmoid(_dot(h, wg_ref[:, cols]) + bg_ref[:, cols])
            term = gate * _dot(y_ref[rows, :], wb_ref[k])
            merged = term if merged is None else merged + term
        o = _dot(merged.astype(BF16), wo_ref[...])
        o_ref[rows, :] = x + mod_ref[0, 2:3, :] * _rms(o, gpost_ref[...])


_MERGE_W = ("w_gate", "b_gate", "w_branch", "w_out", "g_attn_post")


def _merge_call(x2d, mod, ys, w, layer):
    t = x2d.shape[0]
    tm = 512
    tiles_per_mod = t // tm // mod.shape[0]
    tile = pl.BlockSpec((tm, D_MODEL), lambda i: (i, 0))
    ytile = pl.BlockSpec((tm, BRANCH_W), lambda i: (i, 0))
    return pl.pallas_call(
        _merge_kernel, grid=(t // tm,),
        in_specs=[tile, pl.BlockSpec((1, 6, D_MODEL), lambda i: (i // tiles_per_mod, 0, 0)),
                  _layer_spec(w["g_attn_pre"], layer), ytile, ytile, ytile, ytile]
                 + [_layer_spec(w[k], layer) for k in _MERGE_W],
        out_specs=tile,
        out_shape=jax.ShapeDtypeStruct((t, D_MODEL), F32),
        compiler_params=_params(1), name="merge",
    )(x2d, mod, w["g_attn_pre"], *ys, *[w[k] for k in _MERGE_W])


_GAP = 8


def _ffn_kernel(*refs, seqs, halo):
    if halo:
        x_ref, xp_ref, xn_ref = refs[:3]
        refs = refs[3:]
    else:
        x_ref = refs[0]
        refs = refs[1:]
    mod_ref, gpre_ref, wa_ref, wg_ref, ca_ref, cg_ref, wd_ref, gpost_ref, o_ref = refs
    x = x_ref[...]
    tm = x.shape[0]
    shift, scale, gate = mod_ref[0, 3:4, :], mod_ref[0, 4:5, :], mod_ref[0, 5:6, :]

    def pre(xx):
        return _rms(xx, gpre_ref[...]) * (1.0 + scale) + shift

    h = pre(x)
    if halo:
        i = pl.program_id(0) % halo
        hp = jnp.where(i == 0, 0.0, pre(xp_ref[...]))
        hn = jnp.where(i == halo - 1, 0.0, pre(xn_ref[...]))
        pieces = [hp, h, hn]
        starts = [_GAP]
        seq_len = tm
    else:
        seq_len = tm // seqs
        pieces = [h]
        starts = [s * seq_len for s in range(seqs)]
    hb = jnp.concatenate(pieces, 0).astype(BF16)
    rows = hb.shape[0]
    edge_row = lax.broadcasted_iota(jnp.int32, (_GAP, FF_CHUNK), 0)

    def zero_edges(a, first):
        out = []
        for st in starts:
            seg = a[st:st + seq_len]
            if first:
                out += [jnp.where(edge_row == 0, 0.0, seg[:_GAP]), seg[_GAP:]]
            else:
                out += [seg[:seq_len - _GAP], jnp.where(edge_row == _GAP - 1, 0.0, seg[seq_len - _GAP:])]
        return jnp.concatenate(out, 0)

    def up(c):
        cols = slice(c * FF_CHUNK, (c + 1) * FF_CHUNK)
        return _dot(hb, wa_ref[:, cols]), _dot(hb, wg_ref[:, cols])

    def conv(u, c_ref, cols):
        prev = pltpu.roll(u, 1, axis=0)
        nxt = pltpu.roll(u, rows - 1, axis=0)
        if not halo:
            prev, nxt = zero_edges(prev, True), zero_edges(nxt, False)
        return prev * c_ref[0:1, cols] + u * c_ref[1:2, cols] + nxt * c_ref[2:3, cols]

    acts = []
    for c in range(D_FF_PAD // FF_CHUNK):
        ua, ug = up(c)
        cols = slice(c * FF_CHUNK, (c + 1) * FF_CHUNK)
        a = conv(ua, ca_ref, cols)
        g = conv(ug, cg_ref, cols)
        acts.append((g * _sigmoid(g) * a).astype(BF16))
    acc = _dot(jnp.concatenate(acts, 1), wd_ref[...])
    for s, st in enumerate(starts):
        ys = _rms(acc[st:st + seq_len], gpost_ref[...])
        o_ref[s * seq_len:(s + 1) * seq_len, :] = x[s * seq_len:(s + 1) * seq_len] + gate * ys


_FFN_W = ("g_ffn_pre", "ffn_wa", "ffn_wg", "ffn_ca", "ffn_cg", "ffn_wd", "g_ffn_post")


def _ffn_call(x2d, mod, w, layer, seq, tm):
    t = x2d.shape[0]
    tile = pl.BlockSpec((tm, D_MODEL), lambda i: (i, 0))
    in_specs = [tile]
    args = [x2d]
    if tm < seq:
        halo, seqs = seq // tm, 1
        r = tm // _GAP
        last = t // _GAP - 1
        in_specs += [pl.BlockSpec((_GAP, D_MODEL), lambda i: (jnp.maximum(i * r - 1, 0), 0)),
                     pl.BlockSpec((_GAP, D_MODEL), lambda i: (jnp.minimum((i + 1) * r, last), 0))]
        args += [x2d, x2d]
    else:
        halo, seqs = 0, tm // seq
    tiles_per_mod = t // tm // mod.shape[0]
    in_specs += [pl.BlockSpec((1, 6, D_MODEL), lambda i: (i // tiles_per_mod, 0, 0))]
    in_specs += [_layer_spec(w[k], layer) for k in _FFN_W]
    args += [mod] + [w[k] for k in _FFN_W]
    return pl.pallas_call(
        functools.partial(_ffn_kernel, seqs=seqs, halo=halo), grid=(t // tm,),
        in_specs=in_specs, out_specs=tile,
        out_shape=jax.ShapeDtypeStruct((t, D_MODEL), F32),
        compiler_params=_params(1), name="ffn",
    )(*args)


def _rope_tables():
    t = np.arange(DEC_SEQ)
    pos = (t // GRID_W, t % GRID_W)

    def tab(d):
        half = d // 4
        inv = np.float32(ROPE_BASE) ** (-np.arange(half, dtype=np.float32) / np.float32(half))
        cs, sn = [], []
        for p in pos:
            ang = p.astype(np.float32)[:, None] * inv[None, :]
            cs += [np.cos(ang), np.cos(ang)]
            sn += [-np.sin(ang), np.sin(ang)]
        return np.concatenate(cs, -1), np.concatenate(sn, -1)

    c64, s64 = tab(HEAD_DIM)
    c32, s32 = tab(MLA_ROPE)
    pad = LANE - MLA_NOPE - MLA_ROPE
    cm = np.concatenate([np.ones((DEC_SEQ, MLA_NOPE), np.float32), c32, np.ones((DEC_SEQ, pad), np.float32)], -1)
    sm = np.concatenate([np.zeros((DEC_SEQ, MLA_NOPE), np.float32), s32, np.zeros((DEC_SEQ, pad), np.float32)], -1)
    return tuple(jnp.asarray(np.tile(x, (1, 4)), F32) for x in (c64, s64, cm, sm))


def _na_table_kernel(rpb_ref, e_ref, ok_ref, o_ref):
    r = rpb_ref[...]
    r1 = r.astype(BF16)
    r2 = (r - r1.astype(F32)).astype(BF16)
    r3 = (r - r1.astype(F32) - r2.astype(F32)).astype(BF16)
    e = e_ref[...]
    t = _dot(r1, e) + _dot(r2, e) + _dot(r3, e)
    o_ref[...] = jnp.where(ok_ref[...] > 0, t * LOG2E, NEG_INF)


def _na_bias_tables(na_rpb):
    c = np.arange(GRID_W)[:, None]
    w = np.arange(GRID_W)[None, :]
    dc = (w - c + NA_WIN_C - 1).reshape(-1)
    onehot = (np.arange(LANE)[:, None] == dc[None, :]).astype(np.float32)
    c_start = np.clip(c - NA_WIN_C // 2, 0, GRID_W - NA_WIN_C)
    ok = ((w >= c_start) & (w < c_start + NA_WIN_C)).reshape(1, -1).astype(np.int32)
    rows = DEPTH * NA_HEADS * NA_DR
    rpb2 = jnp.pad(na_rpb.reshape(rows, NA_DC), ((0, LANE - rows), (0, LANE - NA_DC)))
    t = pl.pallas_call(
        _na_table_kernel, out_shape=jax.ShapeDtypeStruct((LANE, GRID_W * GRID_W), F32), name="na_table",
        compiler_params=pltpu.CompilerParams(vmem_limit_bytes=VMEM_LIMIT),
    )(rpb2, jnp.asarray(onehot, BF16), jnp.asarray(ok))
    t = t[:rows].reshape(DEPTH, NA_HEADS, NA_DR, GRID_W, GRID_W)
    t = jnp.pad(t, ((0, 0), (0, 0), (1, 1), (0, 0), (0, 0)), constant_values=NEG_INF)
    return jnp.concatenate([t[:, :, :-1], t[:, :, 1:]], -1)


def _pad_last(w, n):
    return jnp.pad(w, ((0, 0),) * (w.ndim - 1) + ((0, n - w.shape[-1]),))


def _prep_weights(g_attn_pre, g_attn_post, g_ffn_pre, g_ffn_post, w_in, w_gate, b_gate, mla_q_norm, mla_w_uq,
                  mla_kv_norm, mla_w_ukv, pool_w, pool_scale, w_branch, w_out, ffn_w_up, ffn_conv, ffn_w_down):
    q_scale = jnp.concatenate([jnp.full((256,), ATT_SCALE * LOG2E, F32), jnp.ones((512,), F32)])[:, None]
    b0, c0, d0 = 768, 1120, 1632
    wt = w_in.transpose(0, 2, 1)

    def rows(lo, hi, before=0, after=0):
        return jnp.pad(wt[:, lo:hi], ((0, 0), (before, after), (0, 0)))

    w_in_p = jnp.concatenate([wt[:, :b0] * q_scale, rows(b0, b0 + MLA_Q_RANK, 0, 256 - MLA_Q_RANK),
                              wt[:, b0 + MLA_Q_RANK:b0 + MLA_Q_RANK + MLA_KV_RANK],
                              rows(b0 + 320, b0 + 352, _KR_LANE, LANE - _KR_LANE - MLA_ROPE),
                              wt[:, c0:d0] * q_scale[:512], wt[:, d0:]], 1).astype(BF16)

    wuq = mla_w_uq.reshape(DEPTH, MLA_Q_RANK, MLA_HEADS, MLA_NOPE + MLA_ROPE)
    wuq = jnp.pad(wuq, ((0, 0), (0, 256 - MLA_Q_RANK), (0, 0), (0, LANE - MLA_NOPE - MLA_ROPE)))
    wukv = mla_w_ukv.reshape(DEPTH, MLA_KV_RANK, MLA_HEADS, MLA_NOPE + MLA_V)
    wuk = _pad_last(wukv[..., :MLA_NOPE], LANE)
    wuv = wukv[..., MLA_NOPE:].reshape(DEPTH, MLA_KV_RANK, MLA_HEADS * MLA_V).astype(BF16)

    eye = np.eye(len(POOL_WINDOWS), dtype=np.float32)
    w_bd = (pool_w[:, :, :, None, :] * eye[None, :, None, :, None]).reshape(DEPTH, POOL_WIDTH, POOL_WIDTH)

    return dict(
        g_attn_pre=g_attn_pre[:, None, :], g_attn_post=g_attn_post[:, None, :],
        g_ffn_pre=g_ffn_pre[:, None, :], g_ffn_post=g_ffn_post[:, None, :],
        w_in_p=w_in_p,
        mla_qn=_pad_last(mla_q_norm[:, None, :], 256),
        mla_wuq=wuq.reshape(DEPTH, 256, MLA_HEADS * LANE).astype(BF16),
        mla_kvn=mla_kv_norm[:, None, :],
        mla_wuk=wuk.reshape(DEPTH, MLA_KV_RANK, MLA_HEADS * LANE).astype(BF16),
        mla_wuv=wuv, mla_wuv_t=wuv.transpose(0, 2, 1),
        pool_w=w_bd.astype(BF16), pool_scale=pool_scale[:, None, :],
        w_gate=w_gate.astype(BF16), b_gate=b_gate[:, None, :],
        w_branch=w_branch.astype(BF16), w_out=w_out.astype(BF16),
        ffn_wa=_pad_last(ffn_w_up[:, :, :D_FF], D_FF_PAD).astype(BF16),
        ffn_wg=_pad_last(ffn_w_up[:, :, D_FF:], D_FF_PAD).astype(BF16),
        ffn_ca=_pad_last(ffn_conv[:, :, :D_FF], D_FF_PAD), ffn_cg=_pad_last(ffn_conv[:, :, D_FF:], D_FF_PAD),
        ffn_wd=jnp.pad(ffn_w_down, ((0, 0), (0, D_FF_PAD - D_FF), (0, 0))).astype(BF16))


def kernel(x_prompt, x_sample, cache_na_k, cache_na_v, cache_mla_ckv, cache_mla_krope, cache_swa_k, cache_swa_v, c, c_ctx, w_mod, b_mod, g_attn_pre, g_attn_post, g_ffn_pre, g_ffn_post, w_in, w_gate, b_gate, na_rpb, mla_q_norm, mla_w_uq, mla_kv_norm, mla_w_ukv, swa_sink, pool_w, pool_scale, w_branch, w_out, ffn_w_up, ffn_conv, ffn_w_down):
    x_p = x_prompt.reshape(BATCH * SEQ, D_MODEL)
    x_s = x_sample.reshape(DEC_BATCH * DEC_SEQ, D_MODEL)

    cv = jnp.concatenate([c_ctx[None, :], c, jnp.zeros((8 - 1 - DEC_BATCH, D_MODEL), F32)], 0)
    mod = _mod_call(cv, w_mod, b_mod).reshape(DEPTH, 8, 6, D_MODEL)
    w = _prep_weights(g_attn_pre, g_attn_post, g_ffn_pre, g_ffn_post, w_in, w_gate, b_gate, mla_q_norm, mla_w_uq,
                      mla_kv_norm, mla_w_ukv, pool_w, pool_scale, w_branch, w_out, ffn_w_up, ffn_conv, ffn_w_down)
    rope_tabs = _rope_tables()
    na_t2 = _na_bias_tables(na_rpb)
    cache_na_k = cache_na_k.reshape(DEC_BATCH, DEPTH, PAST_LEN, 256)
    cache_na_v = cache_na_v.reshape(DEC_BATCH, DEPTH, PAST_LEN, 256)
    cache_swa_k = cache_swa_k.reshape(DEC_BATCH, DEPTH, PAST_LEN, 128)
    cache_swa_v = cache_swa_v.reshape(DEC_BATCH, DEPTH, PAST_LEN, 128)
    cache_kr = jnp.pad(cache_mla_krope, ((0, 0), (0, 0), (0, 0), (_KR_LANE, LANE - _KR_LANE - MLA_ROPE)))

    states = []
    for l in range(DEPTH):
        mod_p = mod[l, 0:1]
        qa, ka_t, va_t, va, cq, ckv, kr, kr_t, qc, kc_t, vc_t, vc, yd = _inproj_call(x_p, mod_p, w, l)
        ys = (_ctx_attn_call(qa, ka_t, va), _ctx_mla_call(cq, ckv, kr, w, l),
              _ctx_swa_call(swa_sink[l], qc, kc_t, vc), yd)
        x_p = _merge_call(x_p, mod_p, ys, w, l)
        x_p = _ffn_call(x_p, mod_p, w, l, SEQ, 2 * SEQ)
        states.append((ka_t, va_t, ckv, kr_t[:, _KR_LANE:_KR_LANE + MLA_ROPE, :], kc_t, vc_t))

        mod_s = mod[l, 1:1 + DEC_BATCH]
        qa, ka, va, cq, ckv, kr, qc, kc, vc, pd = _inproj_call(x_s, mod_s, w, l, rope_tabs)
        ys = (_lat_na_call(qa, ka, va, cache_na_k, cache_na_v, na_t2, l),
              _lat_mla_call(cq, ckv, kr, cache_mla_ckv, cache_kr, rope_tabs[2], rope_tabs[3], w, l),
              _lat_swa_call(swa_sink[l], qc, kc, vc, cache_swa_k, cache_swa_v, l),
              _pool_call(pd, w, l, DEC_SEQ))
        x_s = _merge_call(x_s, mod_s, ys, w, l)
        x_s = _ffn_call(x_s, mod_s, w, l, DEC_SEQ, 1024)

    def stack(j, *tail):
        return jnp.stack([st[j].reshape(BATCH, SEQ, *tail) for st in states], 1)

    def stack_t(j, heads):
        a = jnp.stack([st[j] for st in states], 1).reshape(BATCH, DEPTH, heads, HEAD_DIM, SEQ)
        return a.transpose(0, 1, 4, 2, 3)

    return (x_p.reshape(BATCH, SEQ, D_MODEL), x_s.reshape(DEC_BATCH, DEC_SEQ, D_MODEL),
            stack_t(0, NA_HEADS), stack_t(1, NA_HEADS), stack(2, MLA_KV_RANK),
            jnp.stack([st[3] for st in states], 1).transpose(0, 1, 3, 2),
            stack_t(4, SWA_KV_HEADS), stack_t(5, SWA_KV_HEADS))
```

```python
import functools

import jax
import jax.numpy as jnp
import numpy as np
from jax import lax
from jax.experimental import pallas as pl
from jax.experimental.pallas import tpu as pltpu

F32 = jnp.float32
BF16 = jnp.bfloat16

D_MODEL = 1024
BATCH = 32
SEQ = 256
DEPTH = 2
DEC_BATCH = 2
DEC_SEQ = 2048
PAST_LEN = 256
GRID_W = 64
HEAD_DIM = 64
NA_HEADS = 4
NA_WIN_R = 8
NA_WIN_C = 16
MLA_HEADS = 4
MLA_NOPE = 64
MLA_ROPE = 32
MLA_V = 64
MLA_Q_RANK = 192
MLA_KV_RANK = 128
SWA_HEADS = 4
SWA_KV_HEADS = 2
SWA_WINDOW = 128
POOL_WINDOWS = (2, 4, 8, 16)
POOL_GROUP = 64
POOL_WIDTH = 256
BRANCH_W = 256
N_BRANCH = 4
D_FF = 2752
ROPE_BASE = 10000.0
EPS = 1e-6
NEG_INF = -1e30
ATT_SCALE = HEAD_DIM ** -0.5
MLA_SCALE = (MLA_NOPE + MLA_ROPE) ** -0.5
LOG2E = 1.4426950408889634

LANE = 128
D_FF_PAD = 2816
FF_CHUNK = 256
Q_BLOCK = 128
NA_Q_BLOCK = 256
NA_SPAN = 768
NA_DR = 2 * NA_WIN_R - 1
NA_DC = 2 * NA_WIN_C - 1
SWA_SPAN = 384
MLA_Q_BLOCK = 256
CTX_SEQS = 4
VMEM_LIMIT = 56 * 1024 * 1024

_QA, _KA, _VA, _CQ, _CKV, _KR, _QC, _KC, _VC, _PD = 0, 256, 512, 768, 1024, 1152, 1280, 1536, 1664, 1792
_KR_LANE = 64


def _dot(a, b):
    return jnp.dot(a, b, preferred_element_type=F32)


def _dot_nt(a, b):
    return lax.dot_general(a, b, (((1,), (1,)), ((), ())), preferred_element_type=F32)


def _sigmoid(x):
    return 1.0 / (1.0 + jnp.exp2(x * -LOG2E))


def _rms(x, g, n=None):
    n = x.shape[-1] if n is None else n
    ms = jnp.sum(x * x, -1, keepdims=True) * (1.0 / n)
    return x * lax.rsqrt(ms + EPS) * g


def _softmax_blocks(blocks, sink=None):
    m = None
    for s in blocks:
        mm = jnp.max(s, -1, keepdims=True)
        m = mm if m is None else jnp.maximum(m, mm)
    if sink is not None:
        m = jnp.maximum(m, sink)
    es = [jnp.exp2(s - m) for s in blocks]
    l = None
    for e in es:
        ll = jnp.sum(e, -1, keepdims=True)
        l = ll if l is None else l + ll
    if sink is not None:
        l = l + jnp.exp2(sink - m)
    return es, l


def _rope(x, cos, sin, q):
    w = x.shape[-1]
    lane = lax.broadcasted_iota(jnp.int32, x.shape, 1)
    up = pltpu.roll(x, w - q, axis=1)
    dn = pltpu.roll(x, q, axis=1)
    partner = jnp.where((lane & (2 * q - 1)) < q, up, dn)
    return x * cos + partner * sin


def _const_spec(shape):
    n = len(shape)
    return pl.BlockSpec(shape, lambda *_: (0,) * n, pipeline_mode=pl.Buffered(1))


def _layer_spec(arr, layer):
    n = arr.ndim - 1
    return pl.BlockSpec((None,) + arr.shape[1:], lambda *_: (layer,) + (0,) * n, pipeline_mode=pl.Buffered(1))


def _params(n_axes):
    return pltpu.CompilerParams(dimension_semantics=("arbitrary",) * n_axes, vmem_limit_bytes=VMEM_LIMIT)


def _mod_kernel(cv_ref, w_ref, b_ref, o_ref):
    cv = cv_ref[...]
    a = (cv * _sigmoid(cv)).astype(BF16)
    o_ref[0] = _dot(a, w_ref[0].astype(BF16)) + b_ref[0]


def _mod_call(cv, w_mod, b_mod):
    tn = 2048
    return pl.pallas_call(
        _mod_kernel,
        grid=(DEPTH, 6 * D_MODEL // tn),
        in_specs=[_const_spec((8, D_MODEL)),
                  pl.BlockSpec((1, D_MODEL, tn), lambda l, j: (l, 0, j)),
                  pl.BlockSpec((1, 1, tn), lambda l, j: (l, 0, j))],
        out_specs=pl.BlockSpec((1, 8, tn), lambda l, j: (l, 0, j)),
        out_shape=jax.ShapeDtypeStruct((DEPTH, 8, 6 * D_MODEL), F32),
        compiler_params=_params(2),
        name="mod",
    )(cv, w_mod, b_mod.reshape(DEPTH, 1, 6 * D_MODEL))


_IN_SLOTS = ((_QA, 256), (_KA, 256), (_VA, 256), (_CQ, 256), (_CKV, 128), (_KR, 128),
             (_QC, 256), (_KC, 128), (_VC, 128), (_PD, 256))
_CTX_SLOTS = ((_QA, 256, BF16, False), (_KA, 256, F32, True), (_VA, 256, F32, True), (_VA, 256, BF16, False),
              (_CQ, 256, BF16, False), (_CKV, 128, F32, False), (_KR, 128, F32, False), (_KR, 128, F32, True),
              (_QC, 256, BF16, False), (_KC, 128, F32, True), (_VC, 128, F32, True), (_VC, 128, BF16, False),
              (_PD, 256, BF16, False))
_LAT_SLOTS = tuple((off, wd, BF16, False) for off, wd in _IN_SLOTS)


def _inproj_kernel(*refs, latent, n_carried=0):
    if latent:
        x_ref, mod_ref, g_ref, w_ref, c64_ref, s64_ref, cm_ref, sm_ref = refs[:8]
        outs = refs[8:]
    else:
        x_ref, mod_ref, g_ref, w_ref, pw_ref, ps_ref = refs[:6]
        outs = refs[6 + n_carried:]
    x = x_ref[...]
    h = _rms(x, g_ref[...]) * (1.0 + mod_ref[0, 1:2, :]) + mod_ref[0, 0:1, :]
    p = _dot_nt(h.astype(BF16), w_ref[...])
    for (off, wd, _, transposed), o_ref in zip(_LAT_SLOTS if latent else _CTX_SLOTS, outs):
        v = p[:, off:off + wd]
        if latent:
            if off == _QC:
                v = _rope(v, c64_ref[...], s64_ref[...], 16)
            elif off == _KC:
                v = _rope(v, c64_ref[:, :128], s64_ref[:, :128], 16)
            elif off == _KR:
                v = _rope(v, cm_ref[:, :128], sm_ref[:, :128], 8)
        if transposed:
            for b in range(o_ref.shape[0]):
                slab = v[b * SEQ:(b + 1) * SEQ].T.astype(o_ref.dtype)
                if n_carried:
                    o_ref[b] = slab
                else:
                    o_ref[b, 0] = slab
                    o_ref[b, 1:] = jnp.zeros((DEPTH - 1,) + slab.shape, o_ref.dtype)
        elif off == _PD and not latent:
            ys = _pool_mix([v[b * SEQ:(b + 1) * SEQ] for b in range(v.shape[0] // SEQ)], pw_ref, ps_ref)
            for b, y in enumerate(ys):
                o_ref[b * SEQ:(b + 1) * SEQ, :] = y.astype(o_ref.dtype)
        else:
            o_ref[...] = v.astype(o_ref.dtype)


def _inproj_call(x2d, mod, w, layer, rope_tabs=None, carried=()):
    t = x2d.shape[0]
    tm = 512
    latent = rope_tabs is not None
    slots = _LAT_SLOTS if latent else _CTX_SLOTS
    tiles_per_mod = t // tm // mod.shape[0]
    in_specs = [pl.BlockSpec((tm, D_MODEL), lambda i: (i, 0)),
                pl.BlockSpec((1, 6, D_MODEL), lambda i: (i // tiles_per_mod, 0, 0)),
                _layer_spec(w["g_attn_pre"], layer), _layer_spec(w["w_in_p"], layer)]
    args = [x2d, mod, w["g_attn_pre"], w["w_in_p"]]
    if latent:
        tiles_per_seq = DEC_SEQ // tm
        c64, s64, cm, sm = rope_tabs
        in_specs += [pl.BlockSpec((tm, 256), lambda i: (i % tiles_per_seq, 0)),
                     pl.BlockSpec((tm, 256), lambda i: (i % tiles_per_seq, 0)),
                     pl.BlockSpec((tm, 512), lambda i: (i % tiles_per_seq, 0)),
                     pl.BlockSpec((tm, 512), lambda i: (i % tiles_per_seq, 0))]
        args += [c64, s64, cm, sm]
    else:
        in_specs += [_layer_spec(w["pool_w"], layer), _layer_spec(w["pool_scale"], layer)]
        args += [w["pool_w"], w["pool_scale"]]
    assert latent or bool(carried) == (layer > 0)
    state_outs = [j for j, slot in enumerate(slots) if slot[3]]
    aliases = {}
    if carried:
        aliases = {len(args) + k: j for k, j in enumerate(state_outs)}
        in_specs += [pl.BlockSpec(memory_space=pl.ANY)] * len(carried)
        args += list(carried)
    return pl.pallas_call(
        functools.partial(_inproj_kernel, latent=latent, n_carried=len(carried)),
        grid=(t // tm,),
        in_specs=in_specs,
        out_specs=[(pl.BlockSpec((tm // SEQ, None, wd, SEQ), lambda i: (i, layer, 0, 0)) if carried
                    else pl.BlockSpec((tm // SEQ, DEPTH, wd, SEQ), lambda i: (i, 0, 0, 0))) if tr
                   else pl.BlockSpec((tm, wd), lambda i: (i, 0)) for _, wd, _, tr in slots],
        out_shape=[jax.ShapeDtypeStruct((t // SEQ, DEPTH, wd, SEQ) if tr else (t, wd), dt)
                   for _, wd, dt, tr in slots],
        input_output_aliases=aliases,
        compiler_params=_params(1),
        name="inproj_lat" if latent else "inproj_ctx",
    )(*args)


def _ctx_attn_kernel(q_ref, kt_ref, v_ref, o_ref, *, seqs):
    q = q_ref[...]
    kt = kt_ref[...].astype(BF16)
    v = v_ref[...]
    scores = []
    for s in range(seqs):
        rows = slice(s * SEQ, (s + 1) * SEQ)
        for h in range(NA_HEADS):
            sl = slice(h * HEAD_DIM, (h + 1) * HEAD_DIM)
            scores.append(_dot(q[rows, sl], kt[s, sl, :]))
    probs = [_softmax_blocks([sc]) for sc in scores]
    outs = []
    for s in range(seqs):
        rows = slice(s * SEQ, (s + 1) * SEQ)
        ys = []
        for h in range(NA_HEADS):
            sl = slice(h * HEAD_DIM, (h + 1) * HEAD_DIM)
            (e,), l = probs[s * NA_HEADS + h]
            ys.append(_dot(e.astype(BF16), v[rows, sl]) / l)
        outs.append(jnp.concatenate(ys, -1))
    o_ref[...] = (outs[0] if seqs == 1 else jnp.concatenate(outs, 0)).astype(o_ref.dtype)


def _ctx_attn_call(q, k, v, layer):
    seqs = CTX_SEQS
    t = q.shape[0]
    spec = pl.BlockSpec((seqs * SEQ, 256), lambda b: (b, 0))
    kt_spec = pl.BlockSpec((seqs, None, 256, SEQ), lambda b: (b, layer, 0, 0))
    return pl.pallas_call(
        functools.partial(_ctx_attn_kernel, seqs=seqs), grid=(t // SEQ // seqs,),
        in_specs=[spec, kt_spec, spec], out_specs=spec,
        out_shape=jax.ShapeDtypeStruct((t, 256), BF16), compiler_params=_params(1), name="ctx_attn",
    )(q, k, v)


def _gqa_operands(q, k, rows_q, rows_k):
    group = SWA_HEADS // SWA_KV_HEADS
    out = []
    for kv in range(SWA_KV_HEADS):
        qs = jnp.concatenate([q[rows_q, (kv * group + g) * HEAD_DIM:(kv * group + g + 1) * HEAD_DIM]
                              for g in range(group)], 0)
        out.append((qs, k[rows_k, kv * HEAD_DIM:(kv + 1) * HEAD_DIM]))
    return out


def _gqa_sink(sink_ref, kv, m):
    group = SWA_HEADS // SWA_KV_HEADS
    row = lax.broadcasted_iota(jnp.int32, (group * m, 1), 0)
    col = jnp.full((group * m, 1), sink_ref[kv * group + group - 1] * LOG2E, F32)
    for g in range(group - 2, -1, -1):
        col = jnp.where(row < (g + 1) * m, sink_ref[kv * group + g] * LOG2E, col)
    return col


def _ctx_swa_kernel(sink_ref, q_ref, kt_ref, v_ref, o_ref, *, seqs):
    q = q_ref[...]
    kt = kt_ref[...].astype(BF16)
    v = v_ref[...]
    group = SWA_HEADS // SWA_KV_HEADS
    scores = []
    for s in range(seqs):
        rows = slice(s * SEQ, (s + 1) * SEQ)
        for kv, (qs, _) in enumerate(_gqa_operands(q, q, rows, rows)):
            scores.append(_dot(qs, kt[s, kv * HEAD_DIM:(kv + 1) * HEAD_DIM, :]))
    probs = [_softmax_blocks([sc], sink=_gqa_sink(sink_ref, i % SWA_KV_HEADS, SEQ)) for i, sc in enumerate(scores)]
    outs = []
    for s in range(seqs):
        rows = slice(s * SEQ, (s + 1) * SEQ)
        ys = []
        for kv in range(SWA_KV_HEADS):
            (e,), l = probs[s * SWA_KV_HEADS + kv]
            y = _dot(e.astype(BF16), v[rows, kv * HEAD_DIM:(kv + 1) * HEAD_DIM]) / l
            ys += [y[g * SEQ:(g + 1) * SEQ] for g in range(group)]
        outs.append(jnp.concatenate(ys, -1))
    o_ref[...] = (outs[0] if seqs == 1 else jnp.concatenate(outs, 0)).astype(o_ref.dtype)


def _ctx_swa_call(sink, q, k, v, layer):
    seqs = CTX_SEQS
    t = q.shape[0]
    rows = seqs * SEQ
    return pl.pallas_call(
        functools.partial(_ctx_swa_kernel, seqs=seqs), grid=(t // rows,),
        in_specs=[pl.BlockSpec(memory_space=pltpu.SMEM),
                  pl.BlockSpec((rows, 256), lambda b: (b, 0)),
                  pl.BlockSpec((seqs, None, 128, SEQ), lambda b: (b, layer, 0, 0)),
                  pl.BlockSpec((rows, 128), lambda b: (b, 0))],
        out_specs=pl.BlockSpec((rows, 256), lambda b: (b, 0)),
        out_shape=jax.ShapeDtypeStruct((t, 256), BF16), compiler_params=_params(1), name="ctx_swa",
    )(sink, q, k, v)


def _mla_q(cq, qn_ref, wuq_ref):
    return _dot(_rms(cq, qn_ref[...], MLA_Q_RANK).astype(BF16), wuq_ref[...])


def _mla_kv(ckv, kr, kvn_ref, wuk_ref, wuv_ref, values_t=False):
    cn = _rms(ckv, kvn_ref[...]).astype(BF16)
    kcat = _dot(cn, wuk_ref[...]) + jnp.concatenate([kr] * MLA_HEADS, -1)
    return kcat, (_dot_nt(wuv_ref[...], cn) if values_t else _dot(cn, wuv_ref[...]))


def _ctx_mla_kernel(cq_ref, ckv_ref, kr_ref, qn_ref, wuq_ref, kvn_ref, wuk_ref, wuv_ref, o_ref, *, seqs):
    q = (_mla_q(cq_ref[...].astype(F32), qn_ref, wuq_ref) * (MLA_SCALE * LOG2E)).astype(BF16)
    kcat, v = _mla_kv(ckv_ref[...], kr_ref[...], kvn_ref, wuk_ref, wuv_ref)
    kcat = kcat.astype(BF16)
    v = v.astype(BF16)
    scores = []
    for s in range(seqs):
        rows = slice(s * SEQ, (s + 1) * SEQ)
        for h in range(MLA_HEADS):
            sl = slice(h * LANE, (h + 1) * LANE)
            scores.append(_dot_nt(q[rows, sl], kcat[rows, sl]))
    probs = [_softmax_blocks([sc]) for sc in scores]
    outs = []
    for s in range(seqs):
        rows = slice(s * SEQ, (s + 1) * SEQ)
        ys = []
        for h in range(MLA_HEADS):
            (e,), l = probs[s * MLA_HEADS + h]
            ys.append(_dot(e.astype(BF16), v[rows, h * MLA_V:(h + 1) * MLA_V]) / l)
        outs.append(jnp.concatenate(ys, -1))
    o_ref[...] = (outs[0] if seqs == 1 else jnp.concatenate(outs, 0)).astype(o_ref.dtype)


_MLA_W = ("mla_qn", "mla_wuq", "mla_kvn", "mla_wuk", "mla_wuv")
_MLA_W_LAT = _MLA_W[:-1] + ("mla_wuv_t",)


def _ctx_mla_call(cq, ckv, kr, w, layer):
    seqs = CTX_SEQS
    t = cq.shape[0]
    rows = seqs * SEQ
    return pl.pallas_call(
        functools.partial(_ctx_mla_kernel, seqs=seqs), grid=(t // rows,),
        in_specs=[pl.BlockSpec((rows, 256), lambda b: (b, 0)),
                  pl.BlockSpec((rows, 128), lambda b: (b, 0)),
                  pl.BlockSpec((rows, 128), lambda b: (b, 0))] + [_layer_spec(w[k], layer) for k in _MLA_W],
        out_specs=pl.BlockSpec((rows, 256), lambda b: (b, 0)),
        out_shape=jax.ShapeDtypeStruct((t, 256), BF16), compiler_params=_params(1), name="ctx_mla",
    )(cq, ckv, kr, *[w[k] for k in _MLA_W])


_POOL_PAD = 8


def _pool_mix(xs, w_ref, sc_ref):
    n = xs[0].shape[0]
    ne = n + 2 * _POOL_PAD
    lo, hi = _POOL_PAD, _POOL_PAD + n
    z = jnp.zeros((_POOL_PAD, POOL_WIDTH), F32)
    grp = lax.broadcasted_iota(jnp.int32, (n, POOL_WIDTH), 1) >> 6
    t = lax.broadcasted_iota(jnp.int32, (n, POOL_WIDTH), 0)
    half = jnp.where(grp == 0, 1, jnp.where(grp == 1, 2, jnp.where(grp == 2, 4, 8)))
    cnt = (jnp.minimum(t + half, n) - jnp.maximum(t - half, 0)).astype(F32)

    def pair(a, s):
        return pltpu.roll(a, s, axis=0) + pltpu.roll(a, ne - s, axis=0)

    out = []
    for x in xs:
        xz = jnp.concatenate([z, x, z], 0)
        s2 = xz + pltpu.roll(xz, 1, axis=0)
        s4 = pair(s2, 1)
        s8 = pair(s4, 2)
        s16 = pair(s8, 4)
        tot = jnp.where(grp == 0, s2[lo:hi],
                        jnp.where(grp == 1, s4[lo:hi], jnp.where(grp == 2, s8[lo:hi], s16[lo:hi])))
        dlt = (tot / cnt - x).astype(BF16)
        out.append(_dot(dlt, w_ref[...]) * sc_ref[...])
    return out


def _pool_kernel(x_ref, w_ref, sc_ref, o_ref, *, n):
    xs = [x_ref[s * n:(s + 1) * n, :].astype(F32) for s in range(x_ref.shape[0] // n)]
    for s, y in enumerate(_pool_mix(xs, w_ref, sc_ref)):
        o_ref[s * n:(s + 1) * n, :] = y.astype(o_ref.dtype)


def _pool_call(pd, w, layer, seq):
    t = pd.shape[0]
    rows = seq
    return pl.pallas_call(
        functools.partial(_pool_kernel, n=seq), grid=(t // rows,),
        in_specs=[pl.BlockSpec((rows, POOL_WIDTH), lambda b: (b, 0)),
                  _layer_spec(w["pool_w"], layer), _layer_spec(w["pool_scale"], layer)],
        out_specs=pl.BlockSpec((rows, POOL_WIDTH), lambda b: (b, 0)),
        out_shape=jax.ShapeDtypeStruct((t, POOL_WIDTH), BF16), compiler_params=_params(1), name="pool",
    )(pd, w["pool_w"], w["pool_scale"])


def _lat_na_kernel(q_ref, k_ref, v_ref, kc_ref, vc_ref, t2_ref, o_ref):
    n = pl.program_id(1)
    rows = DEC_SEQ // GRID_W
    q_rows = NA_Q_BLOCK // GRID_W
    row0 = jnp.clip(q_rows * n - NA_WIN_R // 2, 0, rows - NA_SPAN // GRID_W)
    start = pl.multiple_of(row0 * GRID_W, LANE)
    q = q_ref[...]
    k = k_ref[pl.ds(start, NA_SPAN), :]
    v = v_ref[pl.ds(start, NA_SPAN), :]
    kc = kc_ref[...].astype(BF16)
    vc = vc_ref[...].astype(BF16)

    pairs = NA_SPAN // LANE
    low_half = lax.broadcasted_iota(jnp.int32, (GRID_W, LANE), 1) < GRID_W
    entries, masks = [], []
    for a in range(q_rows):
        r = q_rows * n + a
        r_start = jnp.clip(r - NA_WIN_R // 2, 0, rows - NA_WIN_R)
        for p in range(pairs):
            rk = row0 + 2 * p
            ok0 = ((rk >= r_start) & (rk < r_start + NA_WIN_R)).astype(jnp.int32)
            ok1 = ((rk + 1 >= r_start) & (rk + 1 < r_start + NA_WIN_R)).astype(jnp.int32)
            entries.append(jnp.clip(rk - r + NA_WIN_R, 0, NA_DR))
            masks.append(jnp.where(low_half, ok0, ok1) > 0)

    heads = [slice(h * HEAD_DIM, (h + 1) * HEAD_DIM) for h in range(NA_HEADS)]
    raw = [(_dot_nt(q[:, sl], k[:, sl]), _dot_nt(q[:, sl], kc[:, sl])) for sl in heads]
    probs = []
    for h, (s_loc, s_ctx) in enumerate(raw):
        cols = []
        for p in range(pairs):
            blk = [jnp.where(masks[a * pairs + p], t2_ref[h, entries[a * pairs + p]], NEG_INF)
                   for a in range(q_rows)]
            cols.append(jnp.concatenate(blk, 0))
        probs.append(_softmax_blocks([s_loc + jnp.concatenate(cols, 1), s_ctx]))
    ys = []
    for sl, ((e_loc, e_ctx), l) in zip(heads, probs):
        y = _dot(e_loc.astype(BF16), v[:, sl]) + _dot(e_ctx.astype(BF16), vc[:, sl])
        ys.append(y / l)
    o_ref[...] = jnp.concatenate(ys, -1).astype(o_ref.dtype)


def _lat_na_call(q, k, v, cache_k, cache_v, t2, layer):
    nq = DEC_SEQ // NA_Q_BLOCK
    seq_spec = pl.BlockSpec((DEC_SEQ, 256), lambda b, n: (b, 0))
    cache_spec = pl.BlockSpec((None, None, PAST_LEN, 256), lambda b, n: (b, layer, 0, 0))
    return pl.pallas_call(
        _lat_na_kernel, grid=(DEC_BATCH, nq),
        in_specs=[pl.BlockSpec((NA_Q_BLOCK, 256), lambda b, n: (b * nq + n, 0)), seq_spec, seq_spec,
                  cache_spec, cache_spec, _layer_spec(t2, layer)],
        out_specs=pl.BlockSpec((NA_Q_BLOCK, 256), lambda b, n: (b * nq + n, 0)),
        out_shape=jax.ShapeDtypeStruct((DEC_BATCH * DEC_SEQ, 256), BF16),
        compiler_params=_params(2), name="lat_na",
    )(q, k, v, cache_k, cache_v, t2)


def _lat_swa_kernel(sink_ref, q_ref, k_ref, v_ref, kc_ref, vc_ref, o_ref):
    n = pl.program_id(1)
    start = pl.multiple_of(jnp.clip(n - 1, 0, DEC_SEQ // Q_BLOCK - 3) * Q_BLOCK, LANE)
    q = q_ref[...]
    k = k_ref[pl.ds(start, SWA_SPAN), :]
    v = v_ref[pl.ds(start, SWA_SPAN), :]
    kc = kc_ref[...].astype(BF16)
    vc = vc_ref[...].astype(BF16)
    group = SWA_HEADS // SWA_KV_HEADS
    m = group * Q_BLOCK
    q_pos = n * Q_BLOCK + (lax.broadcasted_iota(jnp.int32, (m, SWA_SPAN), 0) & (Q_BLOCK - 1))
    k_pos = start + lax.broadcasted_iota(jnp.int32, (m, SWA_SPAN), 1)
    valid = jnp.abs(q_pos - k_pos) <= SWA_WINDOW
    everything = slice(None)
    raw = []
    for (qs, ks), (_, kcs) in zip(_gqa_operands(q, k, everything, everything),
                                  _gqa_operands(q, kc, everything, everything)):
        raw.append((_dot_nt(qs, ks), _dot_nt(qs, kcs)))
    probs = [_softmax_blocks([jnp.where(valid, s_loc, NEG_INF), s_ctx], sink=_gqa_sink(sink_ref, kv, Q_BLOCK))
             for kv, (s_loc, s_ctx) in enumerate(raw)]
    ys = []
    for kv, ((e_loc, e_ctx), l) in enumerate(probs):
        kvsl = slice(kv * HEAD_DIM, (kv + 1) * HEAD_DIM)
        y = (_dot(e_loc.astype(BF16), v[:, kvsl]) + _dot(e_ctx.astype(BF16), vc[:, kvsl])) / l
        ys += [y[g * Q_BLOCK:(g + 1) * Q_BLOCK] for g in range(group)]
    o_ref[...] = jnp.concatenate(ys, -1).astype(o_ref.dtype)


def _lat_swa_call(sink, q, k, v, cache_k, cache_v, layer):
    nq = DEC_SEQ // Q_BLOCK
    seq_spec = pl.BlockSpec((DEC_SEQ, 128), lambda b, n: (b, 0))
    cache_spec = pl.BlockSpec((None, None, PAST_LEN, 128), lambda b, n: (b, layer, 0, 0))
    return pl.pallas_call(
        _lat_swa_kernel, grid=(DEC_BATCH, nq),
        in_specs=[pl.BlockSpec(memory_space=pltpu.SMEM),
                  pl.BlockSpec((Q_BLOCK, 256), lambda b, n: (b * nq + n, 0)), seq_spec, seq_spec,
                  cache_spec, cache_spec],
        out_specs=pl.BlockSpec((Q_BLOCK, 256), lambda b, n: (b * nq + n, 0)),
        out_shape=jax.ShapeDtypeStruct((DEC_BATCH * DEC_SEQ, 256), BF16),
        compiler_params=_params(2), name="lat_swa",
    )(sink, q, k, v, cache_k, cache_v)


def _lat_mla_kernel(cq_ref, ckv_ref, kr_ref, cckv_ref, ckr_ref, cm_ref, sm_ref,
                    qn_ref, wuq_ref, kvn_ref, wuk_ref, wuvt_ref, o_ref, kcat_s, vt_s):
    @pl.when(pl.program_id(1) == 0)
    def _():
        kc, vc = _mla_kv(cckv_ref[...], ckr_ref[...], kvn_ref, wuk_ref, wuvt_ref, True)
        kcat_s[0:PAST_LEN, :] = kc.astype(BF16)
        vt_s[:, 0:PAST_LEN] = vc.astype(BF16)
        kl, vl = _mla_kv(ckv_ref[...].astype(F32), kr_ref[...].astype(F32), kvn_ref, wuk_ref, wuvt_ref, True)
        kcat_s[PAST_LEN:, :] = kl.astype(BF16)
        vt_s[:, PAST_LEN:] = vl.astype(BF16)

    q = _mla_q(cq_ref[...].astype(F32), qn_ref, wuq_ref)
    q = (_rope(q, cm_ref[...], sm_ref[...], 8) * (MLA_SCALE * LOG2E)).astype(BF16)
    scores = [_dot_nt(kcat_s[:, h * LANE:(h + 1) * LANE], q[:, h * LANE:(h + 1) * LANE]) for h in range(MLA_HEADS)]
    ys = []
    for h, st in enumerate(scores):
        e = jnp.exp2(st - jnp.max(st, 0, keepdims=True))
        l = jnp.sum(e, 0, keepdims=True)
        ys.append(_dot(vt_s[h * MLA_V:(h + 1) * MLA_V, :], e.astype(BF16)) / l)
    o_ref[...] = jnp.concatenate(ys, 0).T.astype(o_ref.dtype)


def _lat_mla_call(cq, ckv, kr, cache_ckv, cache_kr, cm, sm, w, layer):
    qb = MLA_Q_BLOCK
    nq = DEC_SEQ // qb
    seq_spec = pl.BlockSpec((DEC_SEQ, 128), lambda b, n: (b, 0))
    cache_spec = pl.BlockSpec((None, None, PAST_LEN, 128), lambda b, n: (b, layer, 0, 0))
    tab_spec = pl.BlockSpec((qb, 512), lambda b, n: (n, 0))
    return pl.pallas_call(
        _lat_mla_kernel, grid=(DEC_BATCH, nq),
        in_specs=[pl.BlockSpec((qb, 256), lambda b, n: (b * nq + n, 0)), seq_spec, seq_spec,
                  cache_spec, cache_spec, tab_spec, tab_spec] + [_layer_spec(w[k], layer) for k in _MLA_W_LAT],
        out_specs=pl.BlockSpec((qb, 256), lambda b, n: (b * nq + n, 0)),
        out_shape=jax.ShapeDtypeStruct((DEC_BATCH * DEC_SEQ, 256), BF16),
        scratch_shapes=[pltpu.VMEM((PAST_LEN + DEC_SEQ, MLA_HEADS * LANE), BF16),
                        pltpu.VMEM((MLA_HEADS * MLA_V, PAST_LEN + DEC_SEQ), BF16)],
        compiler_params=_params(2), name="lat_mla",
    )(cq, ckv, kr, cache_ckv, cache_kr, cm, sm, *[w[k] for k in _MLA_W_LAT])


_MERGE_PARTS = 2


def _merge_kernel(x_ref, mod_ref, gpre_ref, ya_ref, yb_ref, yc_ref, yd_ref,
                  wg_ref, bg_ref, wb_ref, wo_ref, gpost_ref, o_ref):
    tm = x_ref.shape[0] // _MERGE_PARTS
    for p in range(_MERGE_PARTS):
        rows = slice(p * tm, (p + 1) * tm)
        x = x_ref[rows, :]
        h = (_rms(x, gpre_ref[...]) * (1.0 + mod_ref[0, 1:2, :]) + mod_ref[0, 0:1, :]).astype(BF16)
        merged = None
        for k, y_ref in enumerate((ya_ref, yb_ref, yc_ref, yd_ref)):
            cols = slice(k * D_MODEL, (k + 1) * D_MODEL)
            gate = _sigmoid(_dot(h, wg_ref[:, cols]) + bg_ref[:, cols])
            term = gate * _dot(y_ref[rows, :], wb_ref[k])
            merged = term if merged is None else merged + term
        o = _dot(merged.astype(BF16), wo_ref[...])
        o_ref[rows, :] = x + mod_ref[0, 2:3, :] * _rms(o, gpost_ref[...])


_MERGE_W = ("w_gate", "b_gate", "w_branch", "w_out", "g_attn_post")


def _merge_call(x2d, mod, ys, w, layer):
    t = x2d.shape[0]
    tm = 512
    tiles_per_mod = t // tm // mod.shape[0]
    tile = pl.BlockSpec((tm, D_MODEL), lambda i: (i, 0))
    ytile = pl.BlockSpec((tm, BRANCH_W), lambda i: (i, 0))
    return pl.pallas_call(
        _merge_kernel, grid=(t // tm,),
        in_specs=[tile, pl.BlockSpec((1, 6, D_MODEL), lambda i: (i // tiles_per_mod, 0, 0)),
                  _layer_spec(w["g_attn_pre"], layer), ytile, ytile, ytile, ytile]
                 + [_layer_spec(w[k], layer) for k in _MERGE_W],
        out_specs=tile,
        out_shape=jax.ShapeDtypeStruct((t, D_MODEL), F32),
        compiler_params=_params(1), name="merge",
    )(x2d, mod, w["g_attn_pre"], *ys, *[w[k] for k in _MERGE_W])


_GAP = 8


def _ffn_kernel(*refs, seqs, halo):
    if halo:
        x_ref, xp_ref, xn_ref = refs[:3]
        refs = refs[3:]
    else:
        x_ref = refs[0]
        refs = refs[1:]
    mod_ref, gpre_ref, wa_ref, wg_ref, ca_ref, cg_ref, wd_ref, gpost_ref, o_ref = refs
    x = x_ref[...]
    tm = x.shape[0]
    shift, scale, gate = mod_ref[0, 3:4, :], mod_ref[0, 4:5, :], mod_ref[0, 5:6, :]

    def pre(xx):
        return _rms(xx, gpre_ref[...]) * (1.0 + scale) + shift

    h = pre(x)
    if halo:
        i = pl.program_id(0) % halo
        hp = jnp.where(i == 0, 0.0, pre(xp_ref[...]))
        hn = jnp.where(i == halo - 1, 0.0, pre(xn_ref[...]))
        pieces = [hp, h, hn]
        starts = [_GAP]
        seq_len = tm
    else:
        seq_len = tm // seqs
        pieces = [h]
        starts = [s * seq_len for s in range(seqs)]
    hb = jnp.concatenate(pieces, 0).astype(BF16)
    rows = hb.shape[0]
    edge_row = lax.broadcasted_iota(jnp.int32, (_GAP, FF_CHUNK), 0)

    def zero_edges(a, first):
        out = []
        for st in starts:
            seg = a[st:st + seq_len]
            if first:
                out += [jnp.where(edge_row == 0, 0.0, seg[:_GAP]), seg[_GAP:]]
            else:
                out += [seg[:seq_len - _GAP], jnp.where(edge_row == _GAP - 1, 0.0, seg[seq_len - _GAP:])]
        return jnp.concatenate(out, 0)

    def up(c):
        cols = slice(c * FF_CHUNK, (c + 1) * FF_CHUNK)
        return _dot(hb, wa_ref[:, cols]), _dot(hb, wg_ref[:, cols])

    def conv(u, c_ref, cols):
        prev = pltpu.roll(u, 1, axis=0)
        nxt = pltpu.roll(u, rows - 1, axis=0)
        if not halo:
            prev, nxt = zero_edges(prev, True), zero_edges(nxt, False)
        return prev * c_ref[0:1, cols] + u * c_ref[1:2, cols] + nxt * c_ref[2:3, cols]

    acts = []
    for c in range(D_FF_PAD // FF_CHUNK):
        ua, ug = up(c)
        cols = slice(c * FF_CHUNK, (c + 1) * FF_CHUNK)
        a = conv(ua, ca_ref, cols)
        g = conv(ug, cg_ref, cols)
        acts.append((g * _sigmoid(g) * a).astype(BF16))
    acc = _dot(jnp.concatenate(acts, 1), wd_ref[...])
    for s, st in enumerate(starts):
        ys = _rms(acc[st:st + seq_len], gpost_ref[...])
        o_ref[s * seq_len:(s + 1) * seq_len, :] = x[s * seq_len:(s + 1) * seq_len] + gate * ys


_FFN_W = ("g_ffn_pre", "ffn_wa", "ffn_wg", "ffn_ca", "ffn_cg", "ffn_wd", "g_ffn_post")


def _ffn_call(x2d, mod, w, layer, seq, tm):
    t = x2d.shape[0]
    tile = pl.BlockSpec((tm, D_MODEL), lambda i: (i, 0))
    in_specs = [tile]
    args = [x2d]
    if tm < seq:
        halo, seqs = seq // tm, 1
        r = tm // _GAP
        last = t // _GAP - 1
        in_specs += [pl.BlockSpec((_GAP, D_MODEL), lambda i: (jnp.maximum(i * r - 1, 0), 0)),
                     pl.BlockSpec((_GAP, D_MODEL), lambda i: (jnp.minimum((i + 1) * r, last), 0))]
        args += [x2d, x2d]
    else:
        halo, seqs = 0, tm // seq
    tiles_per_mod = t // tm // mod.shape[0]
    in_specs += [pl.BlockSpec((1, 6, D_MODEL), lambda i: (i // tiles_per_mod, 0, 0))]
    in_specs += [_layer_spec(w[k], layer) for k in _FFN_W]
    args += [mod] + [w[k] for k in _FFN_W]
    return pl.pallas_call(
        functools.partial(_ffn_kernel, seqs=seqs, halo=halo), grid=(t // tm,),
        in_specs=in_specs, out_specs=tile,
        out_shape=jax.ShapeDtypeStruct((t, D_MODEL), F32),
        compiler_params=_params(1), name="ffn",
    )(*args)


def _rope_tables():
    t = np.arange(DEC_SEQ)
    pos = (t // GRID_W, t % GRID_W)

    def tab(d):
        half = d // 4
        inv = np.float32(ROPE_BASE) ** (-np.arange(half, dtype=np.float32) / np.float32(half))
        cs, sn = [], []
        for p in pos:
            ang = p.astype(np.float32)[:, None] * inv[None, :]
            cs += [np.cos(ang), np.cos(ang)]
            sn += [-np.sin(ang), np.sin(ang)]
        return np.concatenate(cs, -1), np.concatenate(sn, -1)

    c64, s64 = tab(HEAD_DIM)
    c32, s32 = tab(MLA_ROPE)
    pad = LANE - MLA_NOPE - MLA_ROPE
    cm = np.concatenate([np.ones((DEC_SEQ, MLA_NOPE), np.float32), c32, np.ones((DEC_SEQ, pad), np.float32)], -1)
    sm = np.concatenate([np.zeros((DEC_SEQ, MLA_NOPE), np.float32), s32, np.zeros((DEC_SEQ, pad), np.float32)], -1)
    return tuple(jnp.asarray(np.tile(x, (1, 4)), F32) for x in (c64, s64, cm, sm))


def _na_table_kernel(rpb_ref, e_ref, ok_ref, o_ref):
    r = rpb_ref[...]
    r1 = r.astype(BF16)
    r2 = (r - r1.astype(F32)).astype(BF16)
    r3 = (r - r1.astype(F32) - r2.astype(F32)).astype(BF16)
    e = e_ref[...]
    t = _dot(r1, e) + _dot(r2, e) + _dot(r3, e)
    o_ref[...] = jnp.where(ok_ref[...] > 0, t * LOG2E, NEG_INF)


def _na_bias_tables(na_rpb):
    c = np.arange(GRID_W)[:, None]
    w = np.arange(GRID_W)[None, :]
    dc = (w - c + NA_WIN_C - 1).reshape(-1)
    onehot = (np.arange(LANE)[:, None] == dc[None, :]).astype(np.float32)
    c_start = np.clip(c - NA_WIN_C // 2, 0, GRID_W - NA_WIN_C)
    ok = ((w >= c_start) & (w < c_start + NA_WIN_C)).reshape(1, -1).astype(np.int32)
    rows = DEPTH * NA_HEADS * NA_DR
    rpb2 = jnp.pad(na_rpb.reshape(rows, NA_DC), ((0, LANE - rows), (0, LANE - NA_DC)))
    t = pl.pallas_call(
        _na_table_kernel, out_shape=jax.ShapeDtypeStruct((LANE, GRID_W * GRID_W), F32), name="na_table",
        compiler_params=pltpu.CompilerParams(vmem_limit_bytes=VMEM_LIMIT),
    )(rpb2, jnp.asarray(onehot, BF16), jnp.asarray(ok))
    t = t[:rows].reshape(DEPTH, NA_HEADS, NA_DR, GRID_W, GRID_W)
    t = jnp.pad(t, ((0, 0), (0, 0), (1, 1), (0, 0), (0, 0)), constant_values=NEG_INF)
    return jnp.concatenate([t[:, :, :-1], t[:, :, 1:]], -1)


def _pad_last(w, n):
    return jnp.pad(w, ((0, 0),) * (w.ndim - 1) + ((0, n - w.shape[-1]),))


def _prep_weights(g_attn_pre, g_attn_post, g_ffn_pre, g_ffn_post, w_in, w_gate, b_gate, mla_q_norm, mla_w_uq,
                  mla_kv_norm, mla_w_ukv, pool_w, pool_scale, w_branch, w_out, ffn_w_up, ffn_conv, ffn_w_down):
    q_scale = jnp.concatenate([jnp.full((256,), ATT_SCALE * LOG2E, F32), jnp.ones((512,), F32)])[:, None]
    b0, c0, d0 = 768, 1120, 1632
    wt = w_in.transpose(0, 2, 1)

    def rows(lo, hi, before=0, after=0):
        return jnp.pad(wt[:, lo:hi], ((0, 0), (before, after), (0, 0)))

    w_in_p = jnp.concatenate([wt[:, :b0] * q_scale, rows(b0, b0 + MLA_Q_RANK, 0, 256 - MLA_Q_RANK),
                              wt[:, b0 + MLA_Q_RANK:b0 + MLA_Q_RANK + MLA_KV_RANK],
                              rows(b0 + 320, b0 + 352, _KR_LANE, LANE - _KR_LANE - MLA_ROPE),
                              wt[:, c0:d0] * q_scale[:512], wt[:, d0:]], 1).astype(BF16)

    wuq = mla_w_uq.reshape(DEPTH, MLA_Q_RANK, MLA_HEADS, MLA_NOPE + MLA_ROPE)
    wuq = jnp.pad(wuq, ((0, 0), (0, 256 - MLA_Q_RANK), (0, 0), (0, LANE - MLA_NOPE - MLA_ROPE)))
    wukv = mla_w_ukv.reshape(DEPTH, MLA_KV_RANK, MLA_HEADS, MLA_NOPE + MLA_V)
    wuk = _pad_last(wukv[..., :MLA_NOPE], LANE)
    wuv = wukv[..., MLA_NOPE:].reshape(DEPTH, MLA_KV_RANK, MLA_HEADS * MLA_V).astype(BF16)

    eye = np.eye(len(POOL_WINDOWS), dtype=np.float32)
    w_bd = (pool_w[:, :, :, None, :] * eye[None, :, None, :, None]).reshape(DEPTH, POOL_WIDTH, POOL_WIDTH)

    return dict(
        g_attn_pre=g_attn_pre[:, None, :], g_attn_post=g_attn_post[:, None, :],
        g_ffn_pre=g_ffn_pre[:, None, :], g_ffn_post=g_ffn_post[:, None, :],
        w_in_p=w_in_p,
        mla_qn=_pad_last(mla_q_norm[:, None, :], 256),
        mla_wuq=wuq.reshape(DEPTH, 256, MLA_HEADS * LANE).astype(BF16),
        mla_kvn=mla_kv_norm[:, None, :],
        mla_wuk=wuk.reshape(DEPTH, MLA_KV_RANK, MLA_HEADS * LANE).astype(BF16),
        mla_wuv=wuv, mla_wuv_t=wuv.transpose(0, 2, 1),
        pool_w=w_bd.astype(BF16), pool_scale=pool_scale[:, None, :],
        w_gate=w_gate.astype(BF16), b_gate=b_gate[:, None, :],
        w_branch=w_branch.astype(BF16), w_out=w_out.astype(BF16),
        ffn_wa=_pad_last(ffn_w_up[:, :, :D_FF], D_FF_PAD).astype(BF16),
        ffn_wg=_pad_last(ffn_w_up[:, :, D_FF:], D_FF_PAD).astype(BF16),
        ffn_ca=_pad_last(ffn_conv[:, :, :D_FF], D_FF_PAD), ffn_cg=_pad_last(ffn_conv[:, :, D_FF:], D_FF_PAD),
        ffn_wd=jnp.pad(ffn_w_down, ((0, 0), (0, D_FF_PAD - D_FF), (0, 0))).astype(BF16))


def kernel(x_prompt, x_sample, cache_na_k, cache_na_v, cache_mla_ckv, cache_mla_krope, cache_swa_k, cache_swa_v, c, c_ctx, w_mod, b_mod, g_attn_pre, g_attn_post, g_ffn_pre, g_ffn_post, w_in, w_gate, b_gate, na_rpb, mla_q_norm, mla_w_uq, mla_kv_norm, mla_w_ukv, swa_sink, pool_w, pool_scale, w_branch, w_out, ffn_w_up, ffn_conv, ffn_w_down):
    x_p = x_prompt.reshape(BATCH * SEQ, D_MODEL)
    x_s = x_sample.reshape(DEC_BATCH * DEC_SEQ, D_MODEL)

    cv = jnp.concatenate([c_ctx[None, :], c, jnp.zeros((8 - 1 - DEC_BATCH, D_MODEL), F32)], 0)
    mod = _mod_call(cv, w_mod, b_mod).reshape(DEPTH, 8, 6, D_MODEL)
    w = _prep_weights(g_attn_pre, g_attn_post, g_ffn_pre, g_ffn_post, w_in, w_gate, b_gate, mla_q_norm, mla_w_uq,
                      mla_kv_norm, mla_w_ukv, pool_w, pool_scale, w_branch, w_out, ffn_w_up, ffn_conv, ffn_w_down)
    rope_tabs = _rope_tables()
    na_t2 = _na_bias_tables(na_rpb)
    cache_na_k = cache_na_k.reshape(DEC_BATCH, DEPTH, PAST_LEN, 256)
    cache_na_v = cache_na_v.reshape(DEC_BATCH, DEPTH, PAST_LEN, 256)
    cache_swa_k = cache_swa_k.reshape(DEC_BATCH, DEPTH, PAST_LEN, 128)
    cache_swa_v = cache_swa_v.reshape(DEC_BATCH, DEPTH, PAST_LEN, 128)
    cache_kr = jnp.pad(cache_mla_krope, ((0, 0), (0, 0), (0, 0), (_KR_LANE, LANE - _KR_LANE - MLA_ROPE)))

    carried, ckvs = (), []
    for l in range(DEPTH):
        mod_p = mod[l, 0:1]
        qa, ka_t, va_t, va, cq, ckv, kr, kr_t, qc, kc_t, vc_t, vc, yd = _inproj_call(x_p, mod_p, w, l,
                                                                                      carried=carried)
        carried = (ka_t, va_t, kr_t, kc_t, vc_t)
        ckvs.append(ckv)
        ys = (_ctx_attn_call(qa, ka_t, va, l), _ctx_mla_call(cq, ckv, kr, w, l),
              _ctx_swa_call(swa_sink[l], qc, kc_t, vc, l), yd)
        x_p = _merge_call(x_p, mod_p, ys, w, l)
        x_p = _ffn_call(x_p, mod_p, w, l, SEQ, 2 * SEQ)

        mod_s = mod[l, 1:1 + DEC_BATCH]
        qa, ka, va, cq, ckv, kr, qc, kc, vc, pd = _inproj_call(x_s, mod_s, w, l, rope_tabs)
        ys = (_lat_na_call(qa, ka, va, cache_na_k, cache_na_v, na_t2, l),
              _lat_mla_call(cq, ckv, kr, cache_mla_ckv, cache_kr, rope_tabs[2], rope_tabs[3], w, l),
              _lat_swa_call(swa_sink[l], qc, kc, vc, cache_swa_k, cache_swa_v, l),
              _pool_call(pd, w, l, DEC_SEQ))
        x_s = _merge_call(x_s, mod_s, ys, w, l)
        x_s = _ffn_call(x_s, mod_s, w, l, DEC_SEQ, 1024)

    ka_t, va_t, kr_t, kc_t, vc_t = carried

    def heads_last(a, heads):
        return a.reshape(BATCH, DEPTH, heads, HEAD_DIM, SEQ).transpose(0, 1, 4, 2, 3)

    return (x_p.reshape(BATCH, SEQ, D_MODEL), x_s.reshape(DEC_BATCH, DEC_SEQ, D_MODEL),
            heads_last(ka_t, NA_HEADS), heads_last(va_t, NA_HEADS),
            jnp.stack([c.reshape(BATCH, SEQ, MLA_KV_RANK) for c in ckvs], 1),
            kr_t[:, :, _KR_LANE:_KR_LANE + MLA_ROPE, :].transpose(0, 1, 3, 2),
            heads_last(kc_t, SWA_KV_HEADS), heads_last(vc_t, SWA_KV_HEADS))
```

```python
import functools

import jax
import jax.numpy as jnp
import numpy as np
from jax import lax
from jax.experimental import pallas as pl
from jax.experimental.pallas import tpu as pltpu

F32 = jnp.float32
BF16 = jnp.bfloat16

D_MODEL = 1024
BATCH = 32
SEQ = 256
DEPTH = 2
DEC_BATCH = 2
DEC_SEQ = 2048
PAST_LEN = 256
GRID_W = 64
HEAD_DIM = 64
NA_HEADS = 4
NA_WIN_R = 8
NA_WIN_C = 16
MLA_HEADS = 4
MLA_NOPE = 64
MLA_ROPE = 32
MLA_V = 64
MLA_Q_RANK = 192
MLA_KV_RANK = 128
SWA_HEADS = 4
SWA_KV_HEADS = 2
SWA_WINDOW = 128
POOL_WINDOWS = (2, 4, 8, 16)
POOL_GROUP = 64
POOL_WIDTH = 256
BRANCH_W = 256
N_BRANCH = 4
D_FF = 2752
ROPE_BASE = 10000.0
EPS = 1e-6
NEG_INF = -1e30
ATT_SCALE = HEAD_DIM ** -0.5
MLA_SCALE = (MLA_NOPE + MLA_ROPE) ** -0.5
LOG2E = 1.4426950408889634

LANE = 128
D_FF_PAD = 2816
FF_CHUNK = 256
SWA_Q_BLOCK = 256
NA_Q_BLOCK = 256
NA_SPAN = 768
NA_DR = 2 * NA_WIN_R - 1
NA_DC = 2 * NA_WIN_C - 1
SWA_SPAN = SWA_Q_BLOCK + 2 * SWA_WINDOW
MLA_Q_BLOCK = 256
CTX_SEQS = 4
VMEM_LIMIT = 56 * 1024 * 1024

_QA, _KA, _VA, _CQ, _CKV, _KR, _QC, _KC, _VC, _PD = 0, 256, 512, 768, 1024, 1152, 1280, 1536, 1664, 1792
_KR_LANE = 64


def _dot(a, b):
    return jnp.dot(a, b, preferred_element_type=F32)


def _dot_nt(a, b):
    return lax.dot_general(a, b, (((1,), (1,)), ((), ())), preferred_element_type=F32)


def _sigmoid(x):
    return 1.0 / (1.0 + jnp.exp2(x * -LOG2E))


def _rms(x, g, n=None):
    n = x.shape[-1] if n is None else n
    ms = jnp.sum(x * x, -1, keepdims=True) * (1.0 / n)
    return x * lax.rsqrt(ms + EPS) * g


def _softmax_blocks(blocks, sink=None):
    m = None
    for s in blocks:
        mm = jnp.max(s, -1, keepdims=True)
        m = mm if m is None else jnp.maximum(m, mm)
    if sink is not None:
        m = jnp.maximum(m, sink)
    es = [jnp.exp2(s - m) for s in blocks]
    l = None
    for e in es:
        ll = jnp.sum(e, -1, keepdims=True)
        l = ll if l is None else l + ll
    if sink is not None:
        l = l + jnp.exp2(sink - m)
    return es, l


def _rope(x, cos, sin, q):
    w = x.shape[-1]
    lane = lax.broadcasted_iota(jnp.int32, x.shape, 1)
    up = pltpu.roll(x, w - q, axis=1)
    dn = pltpu.roll(x, q, axis=1)
    partner = jnp.where((lane & (2 * q - 1)) < q, up, dn)
    return x * cos + partner * sin


def _const_spec(shape):
    n = len(shape)
    return pl.BlockSpec(shape, lambda *_: (0,) * n, pipeline_mode=pl.Buffered(1))


def _layer_spec(arr, layer):
    n = arr.ndim - 1
    return pl.BlockSpec((None,) + arr.shape[1:], lambda *_: (layer,) + (0,) * n, pipeline_mode=pl.Buffered(1))


def _params(n_axes):
    return pltpu.CompilerParams(dimension_semantics=("arbitrary",) * n_axes, vmem_limit_bytes=VMEM_LIMIT)


def _mod_kernel(cv_ref, w_ref, b_ref, o_ref):
    cv = cv_ref[...]
    a = (cv * _sigmoid(cv)).astype(BF16)
    o_ref[0] = _dot(a, w_ref[0].astype(BF16)) + b_ref[0]


def _mod_call(cv, w_mod, b_mod):
    tn = 2048
    return pl.pallas_call(
        _mod_kernel,
        grid=(DEPTH, 6 * D_MODEL // tn),
        in_specs=[_const_spec((8, D_MODEL)),
                  pl.BlockSpec((1, D_MODEL, tn), lambda l, j: (l, 0, j)),
                  pl.BlockSpec((1, 1, tn), lambda l, j: (l, 0, j))],
        out_specs=pl.BlockSpec((1, 8, tn), lambda l, j: (l, 0, j)),
        out_shape=jax.ShapeDtypeStruct((DEPTH, 8, 6 * D_MODEL), F32),
        compiler_params=_params(2),
        name="mod",
    )(cv, w_mod, b_mod.reshape(DEPTH, 1, 6 * D_MODEL))


_IN_SLOTS = ((_QA, 256), (_KA, 256), (_VA, 256), (_CQ, 256), (_CKV, 128), (_KR, 128),
             (_QC, 256), (_KC, 128), (_VC, 128), (_PD, 256))
_CTX_SLOTS = ((_QA, 256, BF16, False), (_KA, 256, F32, True), (_VA, 256, F32, True), (_VA, 256, BF16, False),
              (_CQ, 256, BF16, False), (_CKV, 128, F32, False), (_KR, 128, F32, False), (_KR, 128, F32, True),
              (_QC, 256, BF16, False), (_KC, 128, F32, True), (_VC, 128, F32, True), (_VC, 128, BF16, False),
              (_PD, 256, BF16, False))
_LAT_SLOTS = tuple((off, wd, BF16, False) for off, wd in _IN_SLOTS)


def _inproj_kernel(*refs, latent, n_carried=0):
    if latent:
        x_ref, mod_ref, g_ref, w_ref, c64_ref, s64_ref, cm_ref, sm_ref = refs[:8]
        outs = refs[8:]
    else:
        x_ref, mod_ref, g_ref, w_ref, pw_ref, ps_ref = refs[:6]
        outs = refs[6 + n_carried:]
    x = x_ref[...]
    h = _rms(x, g_ref[...]) * (1.0 + mod_ref[0, 1:2, :]) + mod_ref[0, 0:1, :]
    p = _dot_nt(h.astype(BF16), w_ref[...])
    for (off, wd, _, transposed), o_ref in zip(_LAT_SLOTS if latent else _CTX_SLOTS, outs):
        v = p[:, off:off + wd]
        if latent:
            if off == _QC:
                v = _rope(v, c64_ref[...], s64_ref[...], 16)
            elif off == _KC:
                v = _rope(v, c64_ref[:, :128], s64_ref[:, :128], 16)
            elif off == _KR:
                v = _rope(v, cm_ref[:, :128], sm_ref[:, :128], 8)
        if transposed:
            for b in range(o_ref.shape[0]):
                slab = v[b * SEQ:(b + 1) * SEQ].T.astype(o_ref.dtype)
                if n_carried:
                    o_ref[b] = slab
                else:
                    o_ref[b, 0] = slab
                    o_ref[b, 1:] = jnp.zeros((DEPTH - 1,) + slab.shape, o_ref.dtype)
        elif off == _PD and not latent:
            ys = _pool_mix([v[b * SEQ:(b + 1) * SEQ] for b in range(v.shape[0] // SEQ)], pw_ref, ps_ref)
            for b, y in enumerate(ys):
                o_ref[b * SEQ:(b + 1) * SEQ, :] = y.astype(o_ref.dtype)
        else:
            o_ref[...] = v.astype(o_ref.dtype)


def _inproj_call(x2d, mod, w, layer, rope_tabs=None, carried=()):
    t = x2d.shape[0]
    tm = 512
    latent = rope_tabs is not None
    slots = _LAT_SLOTS if latent else _CTX_SLOTS
    tiles_per_mod = t // tm // mod.shape[0]
    in_specs = [pl.BlockSpec((tm, D_MODEL), lambda i: (i, 0)),
                pl.BlockSpec((1, 6, D_MODEL), lambda i: (i // tiles_per_mod, 0, 0)),
                _layer_spec(w["g_attn_pre"], layer), _layer_spec(w["w_in_p"], layer)]
    args = [x2d, mod, w["g_attn_pre"], w["w_in_p"]]
    if latent:
        tiles_per_seq = DEC_SEQ // tm
        c64, s64, cm, sm = rope_tabs
        in_specs += [pl.BlockSpec((tm, 256), lambda i: (i % tiles_per_seq, 0)),
                     pl.BlockSpec((tm, 256), lambda i: (i % tiles_per_seq, 0)),
                     pl.BlockSpec((tm, 512), lambda i: (i % tiles_per_seq, 0)),
                     pl.BlockSpec((tm, 512), lambda i: (i % tiles_per_seq, 0))]
        args += [c64, s64, cm, sm]
    else:
        in_specs += [_layer_spec(w["pool_w"], layer), _layer_spec(w["pool_scale"], layer)]
        args += [w["pool_w"], w["pool_scale"]]
    assert latent or bool(carried) == (layer > 0)
    state_outs = [j for j, slot in enumerate(slots) if slot[3]]
    aliases = {}
    if carried:
        aliases = {len(args) + k: j for k, j in enumerate(state_outs)}
        in_specs += [pl.BlockSpec(memory_space=pl.ANY)] * len(carried)
        args += list(carried)
    return pl.pallas_call(
        functools.partial(_inproj_kernel, latent=latent, n_carried=len(carried)),
        grid=(t // tm,),
        in_specs=in_specs,
        out_specs=[(pl.BlockSpec((tm // SEQ, None, wd, SEQ), lambda i: (i, layer, 0, 0)) if carried
                    else pl.BlockSpec((tm // SEQ, DEPTH, wd, SEQ), lambda i: (i, 0, 0, 0))) if tr
                   else pl.BlockSpec((tm, wd), lambda i: (i, 0)) for _, wd, _, tr in slots],
        out_shape=[jax.ShapeDtypeStruct((t // SEQ, DEPTH, wd, SEQ) if tr else (t, wd), dt)
                   for _, wd, dt, tr in slots],
        input_output_aliases=aliases,
        compiler_params=_params(1),
        name="inproj_lat" if latent else "inproj_ctx",
    )(*args)


def _ctx_attn_kernel(q_ref, kt_ref, v_ref, o_ref, *, seqs):
    q = q_ref[...]
    kt = kt_ref[...].astype(BF16)
    v = v_ref[...]
    scores = []
    for s in range(seqs):
        rows = slice(s * SEQ, (s + 1) * SEQ)
        for h in range(NA_HEADS):
            sl = slice(h * HEAD_DIM, (h + 1) * HEAD_DIM)
            scores.append(_dot(q[rows, sl], kt[s, sl, :]))
    probs = [_softmax_blocks([sc]) for sc in scores]
    outs = []
    for s in range(seqs):
        rows = slice(s * SEQ, (s + 1) * SEQ)
        ys = []
        for h in range(NA_HEADS):
            sl = slice(h * HEAD_DIM, (h + 1) * HEAD_DIM)
            (e,), l = probs[s * NA_HEADS + h]
            ys.append(_dot(e.astype(BF16), v[rows, sl]) / l)
        outs.append(jnp.concatenate(ys, -1))
    o_ref[...] = (outs[0] if seqs == 1 else jnp.concatenate(outs, 0)).astype(o_ref.dtype)


def _ctx_attn_call(q, k, v, layer):
    seqs = CTX_SEQS
    t = q.shape[0]
    spec = pl.BlockSpec((seqs * SEQ, 256), lambda b: (b, 0))
    kt_spec = pl.BlockSpec((seqs, None, 256, SEQ), lambda b: (b, layer, 0, 0))
    return pl.pallas_call(
        functools.partial(_ctx_attn_kernel, seqs=seqs), grid=(t // SEQ // seqs,),
        in_specs=[spec, kt_spec, spec], out_specs=spec,
        out_shape=jax.ShapeDtypeStruct((t, 256), BF16), compiler_params=_params(1), name="ctx_attn",
    )(q, k, v)


def _gqa_operands(q, k, rows_q, rows_k):
    group = SWA_HEADS // SWA_KV_HEADS
    out = []
    for kv in range(SWA_KV_HEADS):
        qs = jnp.concatenate([q[rows_q, (kv * group + g) * HEAD_DIM:(kv * group + g + 1) * HEAD_DIM]
                              for g in range(group)], 0)
        out.append((qs, k[rows_k, kv * HEAD_DIM:(kv + 1) * HEAD_DIM]))
    return out


def _gqa_sink(sink_ref, kv, m):
    group = SWA_HEADS // SWA_KV_HEADS
    row = lax.broadcasted_iota(jnp.int32, (group * m, 1), 0)
    col = jnp.full((group * m, 1), sink_ref[kv * group + group - 1] * LOG2E, F32)
    for g in range(group - 2, -1, -1):
        col = jnp.where(row < (g + 1) * m, sink_ref[kv * group + g] * LOG2E, col)
    return col


def _ctx_swa_kernel(sink_ref, q_ref, kt_ref, v_ref, o_ref, *, seqs):
    q = q_ref[...]
    kt = kt_ref[...].astype(BF16)
    v = v_ref[...]
    group = SWA_HEADS // SWA_KV_HEADS
    scores = []
    for s in range(seqs):
        rows = slice(s * SEQ, (s + 1) * SEQ)
        for kv, (qs, _) in enumerate(_gqa_operands(q, q, rows, rows)):
            scores.append(_dot(qs, kt[s, kv * HEAD_DIM:(kv + 1) * HEAD_DIM, :]))
    probs = [_softmax_blocks([sc], sink=_gqa_sink(sink_ref, i % SWA_KV_HEADS, SEQ)) for i, sc in enumerate(scores)]
    outs = []
    for s in range(seqs):
        rows = slice(s * SEQ, (s + 1) * SEQ)
        ys = []
        for kv in range(SWA_KV_HEADS):
            (e,), l = probs[s * SWA_KV_HEADS + kv]
            y = _dot(e.astype(BF16), v[rows, kv * HEAD_DIM:(kv + 1) * HEAD_DIM]) / l
            ys += [y[g * SEQ:(g + 1) * SEQ] for g in range(group)]
        outs.append(jnp.concatenate(ys, -1))
    o_ref[...] = (outs[0] if seqs == 1 else jnp.concatenate(outs, 0)).astype(o_ref.dtype)


def _ctx_swa_call(sink, q, k, v, layer):
    seqs = CTX_SEQS
    t = q.shape[0]
    rows = seqs * SEQ
    return pl.pallas_call(
        functools.partial(_ctx_swa_kernel, seqs=seqs), grid=(t // rows,),
        in_specs=[pl.BlockSpec(memory_space=pltpu.SMEM),
                  pl.BlockSpec((rows, 256), lambda b: (b, 0)),
                  pl.BlockSpec((seqs, None, 128, SEQ), lambda b: (b, layer, 0, 0)),
                  pl.BlockSpec((rows, 128), lambda b: (b, 0))],
        out_specs=pl.BlockSpec((rows, 256), lambda b: (b, 0)),
        out_shape=jax.ShapeDtypeStruct((t, 256), BF16), compiler_params=_params(1), name="ctx_swa",
    )(sink, q, k, v)


def _mla_q(cq, qn_ref, wuq_ref):
    return _dot(_rms(cq, qn_ref[...], MLA_Q_RANK).astype(BF16), wuq_ref[...])


def _mla_kv(ckv, kr, kvn_ref, wuk_ref, wuv_ref, values_t=False):
    cn = _rms(ckv, kvn_ref[...]).astype(BF16)
    kcat = _dot(cn, wuk_ref[...]) + jnp.concatenate([kr] * MLA_HEADS, -1)
    return kcat, (_dot_nt(wuv_ref[...], cn) if values_t else _dot(cn, wuv_ref[...]))


def _ctx_mla_kernel(cq_ref, ckv_ref, kr_ref, qn_ref, wuq_ref, kvn_ref, wuk_ref, wuv_ref, o_ref, *, seqs):
    q = (_mla_q(cq_ref[...].astype(F32), qn_ref, wuq_ref) * (MLA_SCALE * LOG2E)).astype(BF16)
    kcat, v = _mla_kv(ckv_ref[...], kr_ref[...], kvn_ref, wuk_ref, wuv_ref)
    kcat = kcat.astype(BF16)
    v = v.astype(BF16)
    scores = []
    for s in range(seqs):
        rows = slice(s * SEQ, (s + 1) * SEQ)
        for h in range(MLA_HEADS):
            sl = slice(h * LANE, (h + 1) * LANE)
            scores.append(_dot_nt(q[rows, sl], kcat[rows, sl]))
    probs = [_softmax_blocks([sc]) for sc in scores]
    outs = []
    for s in range(seqs):
        rows = slice(s * SEQ, (s + 1) * SEQ)
        ys = []
        for h in range(MLA_HEADS):
            (e,), l = probs[s * MLA_HEADS + h]
            ys.append(_dot(e.astype(BF16), v[rows, h * MLA_V:(h + 1) * MLA_V]) / l)
        outs.append(jnp.concatenate(ys, -1))
    o_ref[...] = (outs[0] if seqs == 1 else jnp.concatenate(outs, 0)).astype(o_ref.dtype)


_MLA_W = ("mla_qn", "mla_wuq", "mla_kvn", "mla_wuk", "mla_wuv")
_MLA_W_LAT = _MLA_W[:-1] + ("mla_wuv_t",)


def _ctx_mla_call(cq, ckv, kr, w, layer):
    seqs = CTX_SEQS
    t = cq.shape[0]
    rows = seqs * SEQ
    return pl.pallas_call(
        functools.partial(_ctx_mla_kernel, seqs=seqs), grid=(t // rows,),
        in_specs=[pl.BlockSpec((rows, 256), lambda b: (b, 0)),
                  pl.BlockSpec((rows, 128), lambda b: (b, 0)),
                  pl.BlockSpec((rows, 128), lambda b: (b, 0))] + [_layer_spec(w[k], layer) for k in _MLA_W],
        out_specs=pl.BlockSpec((rows, 256), lambda b: (b, 0)),
        out_shape=jax.ShapeDtypeStruct((t, 256), BF16), compiler_params=_params(1), name="ctx_mla",
    )(cq, ckv, kr, *[w[k] for k in _MLA_W])


_POOL_PAD = 8


def _pool_mix(xs, w_ref, sc_ref):
    n = xs[0].shape[0]
    ne = n + 2 * _POOL_PAD
    lo, hi = _POOL_PAD, _POOL_PAD + n
    z = jnp.zeros((_POOL_PAD, POOL_WIDTH), F32)
    grp = lax.broadcasted_iota(jnp.int32, (n, POOL_WIDTH), 1) >> 6
    t = lax.broadcasted_iota(jnp.int32, (n, POOL_WIDTH), 0)
    half = jnp.where(grp == 0, 1, jnp.where(grp == 1, 2, jnp.where(grp == 2, 4, 8)))
    cnt = (jnp.minimum(t + half, n) - jnp.maximum(t - half, 0)).astype(F32)

    def pair(a, s):
        return pltpu.roll(a, s, axis=0) + pltpu.roll(a, ne - s, axis=0)

    out = []
    for x in xs:
        xz = jnp.concatenate([z, x, z], 0)
        s2 = xz + pltpu.roll(xz, 1, axis=0)
        s4 = pair(s2, 1)
        s8 = pair(s4, 2)
        s16 = pair(s8, 4)
        tot = jnp.where(grp == 0, s2[lo:hi],
                        jnp.where(grp == 1, s4[lo:hi], jnp.where(grp == 2, s8[lo:hi], s16[lo:hi])))
        dlt = (tot / cnt - x).astype(BF16)
        out.append(_dot(dlt, w_ref[...]) * sc_ref[...])
    return out


def _pool_kernel(x_ref, w_ref, sc_ref, o_ref, *, n):
    xs = [x_ref[s * n:(s + 1) * n, :].astype(F32) for s in range(x_ref.shape[0] // n)]
    for s, y in enumerate(_pool_mix(xs, w_ref, sc_ref)):
        o_ref[s * n:(s + 1) * n, :] = y.astype(o_ref.dtype)


def _pool_call(pd, w, layer, seq):
    t = pd.shape[0]
    rows = seq
    return pl.pallas_call(
        functools.partial(_pool_kernel, n=seq), grid=(t // rows,),
        in_specs=[pl.BlockSpec((rows, POOL_WIDTH), lambda b: (b, 0)),
                  _layer_spec(w["pool_w"], layer), _layer_spec(w["pool_scale"], layer)],
        out_specs=pl.BlockSpec((rows, POOL_WIDTH), lambda b: (b, 0)),
        out_shape=jax.ShapeDtypeStruct((t, POOL_WIDTH), BF16), compiler_params=_params(1), name="pool",
    )(pd, w["pool_w"], w["pool_scale"])


def _lat_na_kernel(q_ref, k_ref, v_ref, kc_ref, vc_ref, t2_ref, o_ref):
    n = pl.program_id(1)
    rows = DEC_SEQ // GRID_W
    q_rows = NA_Q_BLOCK // GRID_W
    row0 = jnp.clip(q_rows * n - NA_WIN_R // 2, 0, rows - NA_SPAN // GRID_W)
    start = pl.multiple_of(row0 * GRID_W, LANE)
    q = q_ref[...]
    k = k_ref[pl.ds(start, NA_SPAN), :]
    v = v_ref[pl.ds(start, NA_SPAN), :]
    kc = kc_ref[...].astype(BF16)
    vc = vc_ref[...].astype(BF16)

    pairs = NA_SPAN // LANE
    low_half = lax.broadcasted_iota(jnp.int32, (GRID_W, LANE), 1) < GRID_W
    entries, masks = [], []
    for a in range(q_rows):
        r = q_rows * n + a
        r_start = jnp.clip(r - NA_WIN_R // 2, 0, rows - NA_WIN_R)
        for p in range(pairs):
            rk = row0 + 2 * p
            ok0 = ((rk >= r_start) & (rk < r_start + NA_WIN_R)).astype(jnp.int32)
            ok1 = ((rk + 1 >= r_start) & (rk + 1 < r_start + NA_WIN_R)).astype(jnp.int32)
            entries.append(jnp.clip(rk - r + NA_WIN_R, 0, NA_DR))
            masks.append(jnp.where(low_half, ok0, ok1) > 0)

    heads = [slice(h * HEAD_DIM, (h + 1) * HEAD_DIM) for h in range(NA_HEADS)]
    raw = [(_dot_nt(q[:, sl], k[:, sl]), _dot_nt(q[:, sl], kc[:, sl])) for sl in heads]
    probs = []
    for h, (s_loc, s_ctx) in enumerate(raw):
        cols = []
        for p in range(pairs):
            blk = [jnp.where(masks[a * pairs + p], t2_ref[h, entries[a * pairs + p]], NEG_INF)
                   for a in range(q_rows)]
            cols.append(jnp.concatenate(blk, 0))
        probs.append(_softmax_blocks([s_loc + jnp.concatenate(cols, 1), s_ctx]))
    ys = []
    for sl, ((e_loc, e_ctx), l) in zip(heads, probs):
        y = _dot(e_loc.astype(BF16), v[:, sl]) + _dot(e_ctx.astype(BF16), vc[:, sl])
        ys.append(y / l)
    o_ref[...] = jnp.concatenate(ys, -1).astype(o_ref.dtype)


def _lat_na_call(q, k, v, cache_k, cache_v, t2, layer):
    nq = DEC_SEQ // NA_Q_BLOCK
    seq_spec = pl.BlockSpec((DEC_SEQ, 256), lambda b, n: (b, 0))
    cache_spec = pl.BlockSpec((None, None, PAST_LEN, 256), lambda b, n: (b, layer, 0, 0))
    return pl.pallas_call(
        _lat_na_kernel, grid=(DEC_BATCH, nq),
        in_specs=[pl.BlockSpec((NA_Q_BLOCK, 256), lambda b, n: (b * nq + n, 0)), seq_spec, seq_spec,
                  cache_spec, cache_spec, _layer_spec(t2, layer)],
        out_specs=pl.BlockSpec((NA_Q_BLOCK, 256), lambda b, n: (b * nq + n, 0)),
        out_shape=jax.ShapeDtypeStruct((DEC_BATCH * DEC_SEQ, 256), BF16),
        compiler_params=_params(2), name="lat_na",
    )(q, k, v, cache_k, cache_v, t2)


def _lat_swa_kernel(sink_ref, q_ref, k_ref, v_ref, kc_ref, vc_ref, o_ref):
    n = pl.program_id(1)
    start = jnp.clip(n * SWA_Q_BLOCK - SWA_WINDOW, 0, DEC_SEQ - SWA_SPAN)
    start = pl.multiple_of(start, LANE)
    q = q_ref[...]
    k = k_ref[pl.ds(start, SWA_SPAN), :]
    v = v_ref[pl.ds(start, SWA_SPAN), :]
    kc = kc_ref[...].astype(BF16)
    vc = vc_ref[...].astype(BF16)
    group = SWA_HEADS // SWA_KV_HEADS
    m = group * SWA_Q_BLOCK
    q_pos = n * SWA_Q_BLOCK + (lax.broadcasted_iota(jnp.int32, (m, SWA_SPAN), 0) & (SWA_Q_BLOCK - 1))
    k_pos = start + lax.broadcasted_iota(jnp.int32, (m, SWA_SPAN), 1)
    valid = jnp.abs(q_pos - k_pos) <= SWA_WINDOW
    everything = slice(None)
    raw = []
    for (qs, ks), (_, kcs) in zip(_gqa_operands(q, k, everything, everything),
                                  _gqa_operands(q, kc, everything, everything)):
        raw.append((_dot_nt(qs, ks), _dot_nt(qs, kcs)))
    probs = [_softmax_blocks([jnp.where(valid, s_loc, NEG_INF), s_ctx], sink=_gqa_sink(sink_ref, kv, SWA_Q_BLOCK))
             for kv, (s_loc, s_ctx) in enumerate(raw)]
    ys = []
    for kv, ((e_loc, e_ctx), l) in enumerate(probs):
        kvsl = slice(kv * HEAD_DIM, (kv + 1) * HEAD_DIM)
        y = (_dot(e_loc.astype(BF16), v[:, kvsl]) + _dot(e_ctx.astype(BF16), vc[:, kvsl])) / l
        ys += [y[g * SWA_Q_BLOCK:(g + 1) * SWA_Q_BLOCK] for g in range(group)]
    o_ref[...] = jnp.concatenate(ys, -1).astype(o_ref.dtype)


def _lat_swa_call(sink, q, k, v, cache_k, cache_v, layer):
    nq = DEC_SEQ // SWA_Q_BLOCK
    seq_spec = pl.BlockSpec((DEC_SEQ, 128), lambda b, n: (b, 0))
    cache_spec = pl.BlockSpec((None, None, PAST_LEN, 128), lambda b, n: (b, layer, 0, 0))
    return pl.pallas_call(
        _lat_swa_kernel, grid=(DEC_BATCH, nq),
        in_specs=[pl.BlockSpec(memory_space=pltpu.SMEM),
                  pl.BlockSpec((SWA_Q_BLOCK, 256), lambda b, n: (b * nq + n, 0)), seq_spec, seq_spec,
                  cache_spec, cache_spec],
        out_specs=pl.BlockSpec((SWA_Q_BLOCK, 256), lambda b, n: (b * nq + n, 0)),
        out_shape=jax.ShapeDtypeStruct((DEC_BATCH * DEC_SEQ, 256), BF16),
        compiler_params=_params(2), name="lat_swa",
    )(sink, q, k, v, cache_k, cache_v)


def _lat_mla_kernel(cq_ref, ckv_ref, kr_ref, cckv_ref, ckr_ref, cm_ref, sm_ref,
                    qn_ref, wuq_ref, kvn_ref, wuk_ref, wuvt_ref, o_ref, kcat_s, vt_s):
    @pl.when(pl.program_id(1) == 0)
    def _():
        kc, vc = _mla_kv(cckv_ref[...], ckr_ref[...], kvn_ref, wuk_ref, wuvt_ref, True)
        kcat_s[0:PAST_LEN, :] = kc.astype(BF16)
        vt_s[:, 0:PAST_LEN] = vc.astype(BF16)
        kl, vl = _mla_kv(ckv_ref[...].astype(F32), kr_ref[...].astype(F32), kvn_ref, wuk_ref, wuvt_ref, True)
        kcat_s[PAST_LEN:, :] = kl.astype(BF16)
        vt_s[:, PAST_LEN:] = vl.astype(BF16)

    q = _mla_q(cq_ref[...].astype(F32), qn_ref, wuq_ref)
    q = (_rope(q, cm_ref[...], sm_ref[...], 8) * (MLA_SCALE * LOG2E)).astype(BF16)
    scores = [_dot_nt(kcat_s[:, h * LANE:(h + 1) * LANE], q[:, h * LANE:(h + 1) * LANE]) for h in range(MLA_HEADS)]
    ys = []
    for h, st in enumerate(scores):
        e = jnp.exp2(st - jnp.max(st, 0, keepdims=True))
        l = jnp.sum(e, 0, keepdims=True)
        ys.append(_dot(vt_s[h * MLA_V:(h + 1) * MLA_V, :], e.astype(BF16)) / l)
    o_ref[...] = jnp.concatenate(ys, 0).T.astype(o_ref.dtype)


def _lat_mla_call(cq, ckv, kr, cache_ckv, cache_kr, cm, sm, w, layer):
    qb = MLA_Q_BLOCK
    nq = DEC_SEQ // qb
    seq_spec = pl.BlockSpec((DEC_SEQ, 128), lambda b, n: (b, 0))
    cache_spec = pl.BlockSpec((None, None, PAST_LEN, 128), lambda b, n: (b, layer, 0, 0))
    tab_spec = pl.BlockSpec((qb, 512), lambda b, n: (n, 0))
    return pl.pallas_call(
        _lat_mla_kernel, grid=(DEC_BATCH, nq),
        in_specs=[pl.BlockSpec((qb, 256), lambda b, n: (b * nq + n, 0)), seq_spec, seq_spec,
                  cache_spec, cache_spec, tab_spec, tab_spec] + [_layer_spec(w[k], layer) for k in _MLA_W_LAT],
        out_specs=pl.BlockSpec((qb, 256), lambda b, n: (b * nq + n, 0)),
        out_shape=jax.ShapeDtypeStruct((DEC_BATCH * DEC_SEQ, 256), BF16),
        scratch_shapes=[pltpu.VMEM((PAST_LEN + DEC_SEQ, MLA_HEADS * LANE), BF16),
                        pltpu.VMEM((MLA_HEADS * MLA_V, PAST_LEN + DEC_SEQ), BF16)],
        compiler_params=_params(2), name="lat_mla",
    )(cq, ckv, kr, cache_ckv, cache_kr, cm, sm, *[w[k] for k in _MLA_W_LAT])


_MERGE_PARTS = 2


def _merge_kernel(x_ref, mod_ref, gpre_ref, ya_ref, yb_ref, yc_ref, yd_ref,
                  wg_ref, bg_ref, wb_ref, wo_ref, gpost_ref, o_ref):
    tm = x_ref.shape[0] // _MERGE_PARTS
    for p in range(_MERGE_PARTS):
        rows = slice(p * tm, (p + 1) * tm)
        x = x_ref[rows, :]
        h = (_rms(x, gpre_ref[...]) * (1.0 + mod_ref[0, 1:2, :]) + mod_ref[0, 0:1, :]).astype(BF16)
        merged = None
        for k, y_ref in enumerate((ya_ref, yb_ref, yc_ref, yd_ref)):
            cols = slice(k * D_MODEL, (k + 1) * D_MODEL)
            gate = _sigmoid(_dot(h, wg_ref[:, cols]) + bg_ref[:, cols])
            term = gate * _dot(y_ref[rows, :], wb_ref[k])
            merged = term if merged is None else merged + term
        o = _dot(merged.astype(BF16), wo_ref[...])
        o_ref[rows, :] = x + mod_ref[0, 2:3, :] * _rms(o, gpost_ref[...])


_MERGE_W = ("w_gate", "b_gate", "w_branch", "w_out", "g_attn_post")


def _merge_call(x2d, mod, ys, w, layer):
    t = x2d.shape[0]
    tm = 512
    tiles_per_mod = t // tm // mod.shape[0]
    tile = pl.BlockSpec((tm, D_MODEL), lambda i: (i, 0))
    ytile = pl.BlockSpec((tm, BRANCH_W), lambda i: (i, 0))
    return pl.pallas_call(
        _merge_kernel, grid=(t // tm,),
        in_specs=[tile, pl.BlockSpec((1, 6, D_MODEL), lambda i: (i // tiles_per_mod, 0, 0)),
                  _layer_spec(w["g_attn_pre"], layer), ytile, ytile, ytile, ytile]
                 + [_layer_spec(w[k], layer) for k in _MERGE_W],
        out_specs=tile,
        out_shape=jax.ShapeDtypeStruct((t, D_MODEL), F32),
        compiler_params=_params(1), name="merge",
    )(x2d, mod, w["g_attn_pre"], *ys, *[w[k] for k in _MERGE_W])


_GAP = 8


def _ffn_kernel(*refs, seqs, halo):
    if halo:
        x_ref, xp_ref, xn_ref = refs[:3]
        refs = refs[3:]
    else:
        x_ref = refs[0]
        refs = refs[1:]
    mod_ref, gpre_ref, wa_ref, wg_ref, ca_ref, cg_ref, wd_ref, gpost_ref, o_ref = refs
    x = x_ref[...]
    tm = x.shape[0]
    shift, scale, gate = mod_ref[0, 3:4, :], mod_ref[0, 4:5, :], mod_ref[0, 5:6, :]

    def pre(xx):
        return _rms(xx, gpre_ref[...]) * (1.0 + scale) + shift

    h = pre(x)
    if halo:
        i = pl.program_id(0) % halo
        hp = jnp.where(i == 0, 0.0, pre(xp_ref[...]))
        hn = jnp.where(i == halo - 1, 0.0, pre(xn_ref[...]))
        pieces = [hp, h, hn]
        starts = [_GAP]
        seq_len = tm
    else:
        seq_len = tm // seqs
        pieces = [h]
        starts = [s * seq_len for s in range(seqs)]
    hb = jnp.concatenate(pieces, 0).astype(BF16)
    rows = hb.shape[0]
    edge_row = lax.broadcasted_iota(jnp.int32, (_GAP, FF_CHUNK), 0)

    def zero_edges(a, first):
        out = []
        for st in starts:
            seg = a[st:st + seq_len]
            if first:
                out += [jnp.where(edge_row == 0, 0.0, seg[:_GAP]), seg[_GAP:]]
            else:
                out += [seg[:seq_len - _GAP], jnp.where(edge_row == _GAP - 1, 0.0, seg[seq_len - _GAP:])]
        return jnp.concatenate(out, 0)

    def up(c):
        cols = slice(c * FF_CHUNK, (c + 1) * FF_CHUNK)
        return _dot(hb, wa_ref[:, cols]), _dot(hb, wg_ref[:, cols])

    def conv(u, c_ref, cols):
        prev = pltpu.roll(u, 1, axis=0)
        nxt = pltpu.roll(u, rows - 1, axis=0)
        if not halo:
            prev, nxt = zero_edges(prev, True), zero_edges(nxt, False)
        return prev * c_ref[0:1, cols] + u * c_ref[1:2, cols] + nxt * c_ref[2:3, cols]

    acts = []
    for c in range(D_FF_PAD // FF_CHUNK):
        ua, ug = up(c)
        cols = slice(c * FF_CHUNK, (c + 1) * FF_CHUNK)
        a = conv(ua, ca_ref, cols)
        g = conv(ug, cg_ref, cols)
        acts.append((g * _sigmoid(g) * a).astype(BF16))
    acc = _dot(jnp.concatenate(acts, 1), wd_ref[...])
    for s, st in enumerate(starts):
        ys = _rms(acc[st:st + seq_len], gpost_ref[...])
        o_ref[s * seq_len:(s + 1) * seq_len, :] = x[s * seq_len:(s + 1) * seq_len] + gate * ys


_FFN_W = ("g_ffn_pre", "ffn_wa", "ffn_wg", "ffn_ca", "ffn_cg", "ffn_wd", "g_ffn_post")


def _ffn_call(x2d, mod, w, layer, seq, tm):
    t = x2d.shape[0]
    tile = pl.BlockSpec((tm, D_MODEL), lambda i: (i, 0))
    in_specs = [tile]
    args = [x2d]
    if tm < seq:
        halo, seqs = seq // tm, 1
        r = tm // _GAP
        last = t // _GAP - 1
        in_specs += [pl.BlockSpec((_GAP, D_MODEL), lambda i: (jnp.maximum(i * r - 1, 0), 0)),
                     pl.BlockSpec((_GAP, D_MODEL), lambda i: (jnp.minimum((i + 1) * r, last), 0))]
        args += [x2d, x2d]
    else:
        halo, seqs = 0, tm // seq
    tiles_per_mod = t // tm // mod.shape[0]
    in_specs += [pl.BlockSpec((1, 6, D_MODEL), lambda i: (i // tiles_per_mod, 0, 0))]
    in_specs += [_layer_spec(w[k], layer) for k in _FFN_W]
    args += [mod] + [w[k] for k in _FFN_W]
    return pl.pallas_call(
        functools.partial(_ffn_kernel, seqs=seqs, halo=halo), grid=(t // tm,),
        in_specs=in_specs, out_specs=tile,
        out_shape=jax.ShapeDtypeStruct((t, D_MODEL), F32),
        compiler_params=_params(1), name="ffn",
    )(*args)


def _rope_tables():
    t = np.arange(DEC_SEQ)
    pos = (t // GRID_W, t % GRID_W)

    def tab(d):
        half = d // 4
        inv = np.float32(ROPE_BASE) ** (-np.arange(half, dtype=np.float32) / np.float32(half))
        cs, sn = [], []
        for p in pos:
            ang = p.astype(np.float32)[:, None] * inv[None, :]
            cs += [np.cos(ang), np.cos(ang)]
            sn += [-np.sin(ang), np.sin(ang)]
        return np.concatenate(cs, -1), np.concatenate(sn, -1)

    c64, s64 = tab(HEAD_DIM)
    c32, s32 = tab(MLA_ROPE)
    pad = LANE - MLA_NOPE - MLA_ROPE
    cm = np.concatenate([np.ones((DEC_SEQ, MLA_NOPE), np.float32), c32, np.ones((DEC_SEQ, pad), np.float32)], -1)
    sm = np.concatenate([np.zeros((DEC_SEQ, MLA_NOPE), np.float32), s32, np.zeros((DEC_SEQ, pad), np.float32)], -1)
    return tuple(jnp.asarray(np.tile(x, (1, 4)), F32) for x in (c64, s64, cm, sm))


def _na_table_kernel(rpb_ref, e_ref, ok_ref, o_ref):
    r = rpb_ref[...]
    r1 = r.astype(BF16)
    r2 = (r - r1.astype(F32)).astype(BF16)
    r3 = (r - r1.astype(F32) - r2.astype(F32)).astype(BF16)
    e = e_ref[...]
    t = _dot(r1, e) + _dot(r2, e) + _dot(r3, e)
    o_ref[...] = jnp.where(ok_ref[...] > 0, t * LOG2E, NEG_INF)


def _na_bias_tables(na_rpb):
    c = np.arange(GRID_W)[:, None]
    w = np.arange(GRID_W)[None, :]
    dc = (w - c + NA_WIN_C - 1).reshape(-1)
    onehot = (np.arange(LANE)[:, None] == dc[None, :]).astype(np.float32)
    c_start = np.clip(c - NA_WIN_C // 2, 0, GRID_W - NA_WIN_C)
    ok = ((w >= c_start) & (w < c_start + NA_WIN_C)).reshape(1, -1).astype(np.int32)
    rows = DEPTH * NA_HEADS * NA_DR
    rpb2 = jnp.pad(na_rpb.reshape(rows, NA_DC), ((0, LANE - rows), (0, LANE - NA_DC)))
    t = pl.pallas_call(
        _na_table_kernel, out_shape=jax.ShapeDtypeStruct((LANE, GRID_W * GRID_W), F32), name="na_table",
        compiler_params=pltpu.CompilerParams(vmem_limit_bytes=VMEM_LIMIT),
    )(rpb2, jnp.asarray(onehot, BF16), jnp.asarray(ok))
    t = t[:rows].reshape(DEPTH, NA_HEADS, NA_DR, GRID_W, GRID_W)
    t = jnp.pad(t, ((0, 0), (0, 0), (1, 1), (0, 0), (0, 0)), constant_values=NEG_INF)
    return jnp.concatenate([t[:, :, :-1], t[:, :, 1:]], -1)


def _pad_last(w, n):
    return jnp.pad(w, ((0, 0),) * (w.ndim - 1) + ((0, n - w.shape[-1]),))


def _prep_weights(g_attn_pre, g_attn_post, g_ffn_pre, g_ffn_post, w_in, w_gate, b_gate, mla_q_norm, mla_w_uq,
                  mla_kv_norm, mla_w_ukv, pool_w, pool_scale, w_branch, w_out, ffn_w_up, ffn_conv, ffn_w_down):
    q_scale = jnp.concatenate([jnp.full((256,), ATT_SCALE * LOG2E, F32), jnp.ones((512,), F32)])[:, None]
    b0, c0, d0 = 768, 1120, 1632
    wt = w_in.transpose(0, 2, 1)

    def rows(lo, hi, before=0, after=0):
        return jnp.pad(wt[:, lo:hi], ((0, 0), (before, after), (0, 0)))

    w_in_p = jnp.concatenate([wt[:, :b0] * q_scale, rows(b0, b0 + MLA_Q_RANK, 0, 256 - MLA_Q_RANK),
                              wt[:, b0 + MLA_Q_RANK:b0 + MLA_Q_RANK + MLA_KV_RANK],
                              rows(b0 + 320, b0 + 352, _KR_LANE, LANE - _KR_LANE - MLA_ROPE),
                              wt[:, c0:d0] * q_scale[:512], wt[:, d0:]], 1).astype(BF16)

    wuq = mla_w_uq.reshape(DEPTH, MLA_Q_RANK, MLA_HEADS, MLA_NOPE + MLA_ROPE)
    wuq = jnp.pad(wuq, ((0, 0), (0, 256 - MLA_Q_RANK), (0, 0), (0, LANE - MLA_NOPE - MLA_ROPE)))
    wukv = mla_w_ukv.reshape(DEPTH, MLA_KV_RANK, MLA_HEADS, MLA_NOPE + MLA_V)
    wuk = _pad_last(wukv[..., :MLA_NOPE], LANE)
    wuv = wukv[..., MLA_NOPE:].reshape(DEPTH, MLA_KV_RANK, MLA_HEADS * MLA_V).astype(BF16)

    eye = np.eye(len(POOL_WINDOWS), dtype=np.float32)
    w_bd = (pool_w[:, :, :, None, :] * eye[None, :, None, :, None]).reshape(DEPTH, POOL_WIDTH, POOL_WIDTH)

    return dict(
        g_attn_pre=g_attn_pre[:, None, :], g_attn_post=g_attn_post[:, None, :],
        g_ffn_pre=g_ffn_pre[:, None, :], g_ffn_post=g_ffn_post[:, None, :],
        w_in_p=w_in_p,
        mla_qn=_pad_last(mla_q_norm[:, None, :], 256),
        mla_wuq=wuq.reshape(DEPTH, 256, MLA_HEADS * LANE).astype(BF16),
        mla_kvn=mla_kv_norm[:, None, :],
        mla_wuk=wuk.reshape(DEPTH, MLA_KV_RANK, MLA_HEADS * LANE).astype(BF16),
        mla_wuv=wuv, mla_wuv_t=wuv.transpose(0, 2, 1),
        pool_w=w_bd.astype(BF16), pool_scale=pool_scale[:, None, :],
        w_gate=w_gate.astype(BF16), b_gate=b_gate[:, None, :],
        w_branch=w_branch.astype(BF16), w_out=w_out.astype(BF16),
        ffn_wa=_pad_last(ffn_w_up[:, :, :D_FF], D_FF_PAD).astype(BF16),
        ffn_wg=_pad_last(ffn_w_up[:, :, D_FF:], D_FF_PAD).astype(BF16),
        ffn_ca=_pad_last(ffn_conv[:, :, :D_FF], D_FF_PAD), ffn_cg=_pad_last(ffn_conv[:, :, D_FF:], D_FF_PAD),
        ffn_wd=jnp.pad(ffn_w_down, ((0, 0), (0, D_FF_PAD - D_FF), (0, 0))).astype(BF16))


def kernel(x_prompt, x_sample, cache_na_k, cache_na_v, cache_mla_ckv, cache_mla_krope, cache_swa_k, cache_swa_v, c, c_ctx, w_mod, b_mod, g_attn_pre, g_attn_post, g_ffn_pre, g_ffn_post, w_in, w_gate, b_gate, na_rpb, mla_q_norm, mla_w_uq, mla_kv_norm, mla_w_ukv, swa_sink, pool_w, pool_scale, w_branch, w_out, ffn_w_up, ffn_conv, ffn_w_down):
    x_p = x_prompt.reshape(BATCH * SEQ, D_MODEL)
    x_s = x_sample.reshape(DEC_BATCH * DEC_SEQ, D_MODEL)

    cv = jnp.concatenate([c_ctx[None, :], c, jnp.zeros((8 - 1 - DEC_BATCH, D_MODEL), F32)], 0)
    mod = _mod_call(cv, w_mod, b_mod).reshape(DEPTH, 8, 6, D_MODEL)
    w = _prep_weights(g_attn_pre, g_attn_post, g_ffn_pre, g_ffn_post, w_in, w_gate, b_gate, mla_q_norm, mla_w_uq,
                      mla_kv_norm, mla_w_ukv, pool_w, pool_scale, w_branch, w_out, ffn_w_up, ffn_conv, ffn_w_down)
    rope_tabs = _rope_tables()
    na_t2 = _na_bias_tables(na_rpb)
    cache_na_k = cache_na_k.reshape(DEC_BATCH, DEPTH, PAST_LEN, 256)
    cache_na_v = cache_na_v.reshape(DEC_BATCH, DEPTH, PAST_LEN, 256)
    cache_swa_k = cache_swa_k.reshape(DEC_BATCH, DEPTH, PAST_LEN, 128)
    cache_swa_v = cache_swa_v.reshape(DEC_BATCH, DEPTH, PAST_LEN, 128)
    cache_kr = jnp.pad(cache_mla_krope, ((0, 0), (0, 0), (0, 0), (_KR_LANE, LANE - _KR_LANE - MLA_ROPE)))

    carried, ckvs = (), []
    for l in range(DEPTH):
        mod_p = mod[l, 0:1]
        qa, ka_t, va_t, va, cq, ckv, kr, kr_t, qc, kc_t, vc_t, vc, yd = _inproj_call(x_p, mod_p, w, l,
                                                                                      carried=carried)
        carried = (ka_t, va_t, kr_t, kc_t, vc_t)
        ckvs.append(ckv)
        ys = (_ctx_attn_call(qa, ka_t, va, l), _ctx_mla_call(cq, ckv, kr, w, l),
              _ctx_swa_call(swa_sink[l], qc, kc_t, vc, l), yd)
        x_p = _merge_call(x_p, mod_p, ys, w, l)
        x_p = _ffn_call(x_p, mod_p, w, l, SEQ, 2 * SEQ)

        mod_s = mod[l, 1:1 + DEC_BATCH]
        qa, ka, va, cq, ckv, kr, qc, kc, vc, pd = _inproj_call(x_s, mod_s, w, l, rope_tabs)
        ys = (_lat_na_call(qa, ka, va, cache_na_k, cache_na_v, na_t2, l),
              _lat_mla_call(cq, ckv, kr, cache_mla_ckv, cache_kr, rope_tabs[2], rope_tabs[3], w, l),
              _lat_swa_call(swa_sink[l], qc, kc, vc, cache_swa_k, cache_swa_v, l),
              _pool_call(pd, w, l, DEC_SEQ))
        x_s = _merge_call(x_s, mod_s, ys, w, l)
        x_s = _ffn_call(x_s, mod_s, w, l, DEC_SEQ, 1024)

    ka_t, va_t, kr_t, kc_t, vc_t = carried

    def heads_last(a, heads):
        return a.reshape(BATCH, DEPTH, heads, HEAD_DIM, SEQ).transpose(0, 1, 4, 2, 3)

    return (x_p.reshape(BATCH, SEQ, D_MODEL), x_s.reshape(DEC_BATCH, DEC_SEQ, D_MODEL),
            heads_last(ka_t, NA_HEADS), heads_last(va_t, NA_HEADS),
            jnp.stack([c.reshape(BATCH, SEQ, MLA_KV_RANK) for c in ckvs], 1),
            kr_t[:, :, _KR_LANE:_KR_LANE + MLA_ROPE, :].transpose(0, 1, 3, 2),
            heads_last(kc_t, SWA_KV_HEADS), heads_last(vc_t, SWA_KV_HEADS))
```

```python
import functools

import jax
import jax.numpy as jnp
import numpy as np
from jax import lax
from jax.experimental import pallas as pl
from jax.experimental.pallas import tpu as pltpu

F32 = jnp.float32
BF16 = jnp.bfloat16

D_MODEL = 1024
BATCH = 32
SEQ = 256
DEPTH = 2
DEC_BATCH = 2
DEC_SEQ = 2048
PAST_LEN = 256
GRID_W = 64
HEAD_DIM = 64
NA_HEADS = 4
NA_WIN_R = 8
NA_WIN_C = 16
MLA_HEADS = 4
MLA_NOPE = 64
MLA_ROPE = 32
MLA_V = 64
MLA_Q_RANK = 192
MLA_KV_RANK = 128
SWA_HEADS = 4
SWA_KV_HEADS = 2
SWA_WINDOW = 128
POOL_WINDOWS = (2, 4, 8, 16)
POOL_GROUP = 64
POOL_WIDTH = 256
BRANCH_W = 256
N_BRANCH = 4
D_FF = 2752
ROPE_BASE = 10000.0
EPS = 1e-6
NEG_INF = -1e30
ATT_SCALE = HEAD_DIM ** -0.5
MLA_SCALE = (MLA_NOPE + MLA_ROPE) ** -0.5
LOG2E = 1.4426950408889634

LANE = 128
D_FF_PAD = 2816
FF_CHUNK = 256
SWA_Q_BLOCK = 256
NA_Q_BLOCK = 256
NA_SPAN = 768
NA_DR = 2 * NA_WIN_R - 1
NA_DC = 2 * NA_WIN_C - 1
SWA_SPAN = SWA_Q_BLOCK + 2 * SWA_WINDOW
MLA_Q_BLOCK = 256
CTX_SEQS = 4
VMEM_LIMIT = 56 * 1024 * 1024

_PD, _QC, _KA, _VA, _KR, _KC, _VC, _CKV, _QA, _CQ = 0, 256, 512, 768, 1024, 1152, 1280, 1408, 1536, 1792
_KR_LANE = 64


def _dot(a, b):
    return jnp.dot(a, b, preferred_element_type=F32)


def _dot_nt(a, b):
    return lax.dot_general(a, b, (((1,), (1,)), ((), ())), preferred_element_type=F32)


def _sigmoid(x):
    return 1.0 / (1.0 + jnp.exp2(x * -LOG2E))


def _rms(x, g, n=None):
    n = x.shape[-1] if n is None else n
    ms = jnp.sum(x * x, -1, keepdims=True) * (1.0 / n)
    return x * lax.rsqrt(ms + EPS) * g


def _softmax_blocks(blocks, sink=None):
    m = None
    for s in blocks:
        mm = jnp.max(s, -1, keepdims=True)
        m = mm if m is None else jnp.maximum(m, mm)
    if sink is not None:
        m = jnp.maximum(m, sink)
    es = [jnp.exp2(s - m) for s in blocks]
    l = None
    for e in es:
        ll = jnp.sum(e, -1, keepdims=True)
        l = ll if l is None else l + ll
    if sink is not None:
        l = l + jnp.exp2(sink - m)
    return es, l


def _rope(x, cos, sin, q):
    w = x.shape[-1]
    lane = lax.broadcasted_iota(jnp.int32, x.shape, 1)
    up = pltpu.roll(x, w - q, axis=1)
    dn = pltpu.roll(x, q, axis=1)
    partner = jnp.where((lane & (2 * q - 1)) < q, up, dn)
    return x * cos + partner * sin


def _const_spec(shape):
    n = len(shape)
    return pl.BlockSpec(shape, lambda *_: (0,) * n, pipeline_mode=pl.Buffered(1))


def _layer_spec(arr, layer):
    n = arr.ndim - 1
    return pl.BlockSpec((None,) + arr.shape[1:], lambda *_: (layer,) + (0,) * n, pipeline_mode=pl.Buffered(1))


def _params(n_axes):
    return pltpu.CompilerParams(dimension_semantics=("arbitrary",) * n_axes, vmem_limit_bytes=VMEM_LIMIT)


def _mod_kernel(cv_ref, w_ref, b_ref, o_ref):
    cv = cv_ref[...]
    a = (cv * _sigmoid(cv)).astype(BF16)
    o_ref[0] = _dot(a, w_ref[0].astype(BF16)) + b_ref[0]


def _mod_call(cv, w_mod, b_mod):
    tn = 2048
    return pl.pallas_call(
        _mod_kernel,
        grid=(DEPTH, 6 * D_MODEL // tn),
        in_specs=[_const_spec((8, D_MODEL)),
                  pl.BlockSpec((1, D_MODEL, tn), lambda l, j: (l, 0, j)),
                  pl.BlockSpec((1, 1, tn), lambda l, j: (l, 0, j))],
        out_specs=pl.BlockSpec((1, 8, tn), lambda l, j: (l, 0, j)),
        out_shape=jax.ShapeDtypeStruct((DEPTH, 8, 6 * D_MODEL), F32),
        compiler_params=_params(2),
        name="mod",
    )(cv, w_mod, b_mod.reshape(DEPTH, 1, 6 * D_MODEL))


_IN_SLOTS = ((_QA, 256), (_KA, 256), (_VA, 256), (_CQ, 256), (_CKV, 128), (_KR, 128),
             (_QC, 256), (_KC, 128), (_VC, 128), (_PD, 256))
_CTX_SLOTS = ((_QA, 256, BF16, False), (_KA, 256, F32, True), (_VA, 256, F32, True), (_VA, 256, BF16, False),
              (_CQ, 256, BF16, False), (_CKV, 128, F32, False), (_KR, 128, F32, False), (_KR, 128, F32, True),
              (_QC, 256, BF16, False), (_KC, 128, F32, True), (_VC, 128, F32, True), (_VC, 128, BF16, False),
              (_PD, 256, BF16, False))
_LAT_SLOTS = tuple((off, wd, BF16, False) for off, wd in _IN_SLOTS)


def _inproj_kernel(*refs, latent, n_carried=0):
    if latent:
        x_ref, mod_ref, g_ref, w_ref, c64_ref, s64_ref, cm_ref, sm_ref = refs[:8]
        outs = refs[8:]
    else:
        x_ref, mod_ref, g_ref, w_ref, pw_ref, ps_ref = refs[:6]
        outs = refs[6 + n_carried:]
    x = x_ref[...]
    h = _rms(x, g_ref[...]) * (1.0 + mod_ref[0, 1:2, :]) + mod_ref[0, 0:1, :]
    p = _dot_nt(h.astype(BF16), w_ref[...])
    slots = sorted(zip(_LAT_SLOTS if latent else _CTX_SLOTS, outs), key=lambda so: so[0][0])
    for (off, wd, _, transposed), o_ref in slots:
        v = p[:, off:off + wd]
        if latent:
            if off == _QC:
                v = _rope(v, c64_ref[...], s64_ref[...], 16)
            elif off == _KC:
                v = _rope(v, c64_ref[:, :128], s64_ref[:, :128], 16)
            elif off == _KR:
                v = _rope(v, cm_ref[:, :128], sm_ref[:, :128], 8)
        if transposed:
            for b in range(o_ref.shape[0]):
                slab = v[b * SEQ:(b + 1) * SEQ].T.astype(o_ref.dtype)
                if n_carried:
                    o_ref[b] = slab
                else:
                    o_ref[b, 0] = slab
                    o_ref[b, 1:] = jnp.zeros((DEPTH - 1,) + slab.shape, o_ref.dtype)
        elif off == _PD and not latent:
            ys = _pool_mix([v[b * SEQ:(b + 1) * SEQ] for b in range(v.shape[0] // SEQ)], pw_ref, ps_ref)
            for b, y in enumerate(ys):
                o_ref[b * SEQ:(b + 1) * SEQ, :] = y.astype(o_ref.dtype)
        else:
            o_ref[...] = v.astype(o_ref.dtype)


def _inproj_call(x2d, mod, w, layer, rope_tabs=None, carried=()):
    t = x2d.shape[0]
    tm = 512
    latent = rope_tabs is not None
    slots = _LAT_SLOTS if latent else _CTX_SLOTS
    tiles_per_mod = t // tm // mod.shape[0]
    in_specs = [pl.BlockSpec((tm, D_MODEL), lambda i: (i, 0)),
                pl.BlockSpec((1, 6, D_MODEL), lambda i: (i // tiles_per_mod, 0, 0)),
                _layer_spec(w["g_attn_pre"], layer), _layer_spec(w["w_in_p"], layer)]
    args = [x2d, mod, w["g_attn_pre"], w["w_in_p"]]
    if latent:
        tiles_per_seq = DEC_SEQ // tm
        c64, s64, cm, sm = rope_tabs
        in_specs += [pl.BlockSpec((tm, 256), lambda i: (i % tiles_per_seq, 0)),
                     pl.BlockSpec((tm, 256), lambda i: (i % tiles_per_seq, 0)),
                     pl.BlockSpec((tm, 512), lambda i: (i % tiles_per_seq, 0)),
                     pl.BlockSpec((tm, 512), lambda i: (i % tiles_per_seq, 0))]
        args += [c64, s64, cm, sm]
    else:
        in_specs += [_layer_spec(w["pool_w"], layer), _layer_spec(w["pool_scale"], layer)]
        args += [w["pool_w"], w["pool_scale"]]
    assert latent or bool(carried) == (layer > 0)
    state_outs = [j for j, slot in enumerate(slots) if slot[3]]
    aliases = {}
    if carried:
        aliases = {len(args) + k: j for k, j in enumerate(state_outs)}
        in_specs += [pl.BlockSpec(memory_space=pl.ANY)] * len(carried)
        args += list(carried)
    return pl.pallas_call(
        functools.partial(_inproj_kernel, latent=latent, n_carried=len(carried)),
        grid=(t // tm,),
        in_specs=in_specs,
        out_specs=[(pl.BlockSpec((tm // SEQ, None, wd, SEQ), lambda i: (i, layer, 0, 0)) if carried
                    else pl.BlockSpec((tm // SEQ, DEPTH, wd, SEQ), lambda i: (i, 0, 0, 0))) if tr
                   else pl.BlockSpec((tm, wd), lambda i: (i, 0)) for _, wd, _, tr in slots],
        out_shape=[jax.ShapeDtypeStruct((t // SEQ, DEPTH, wd, SEQ) if tr else (t, wd), dt)
                   for _, wd, dt, tr in slots],
        input_output_aliases=aliases,
        compiler_params=_params(1),
        name="inproj_lat" if latent else "inproj_ctx",
    )(*args)


def _ctx_attn_kernel(q_ref, kt_ref, v_ref, o_ref, *, seqs):
    q = q_ref[...]
    kt = kt_ref[...].astype(BF16)
    v = v_ref[...]
    scores = []
    for s in range(seqs):
        rows = slice(s * SEQ, (s + 1) * SEQ)
        for h in range(NA_HEADS):
            sl = slice(h * HEAD_DIM, (h + 1) * HEAD_DIM)
            scores.append(_dot(q[rows, sl], kt[s, sl, :]))
    probs = [_softmax_blocks([sc]) for sc in scores]
    outs = []
    for s in range(seqs):
        rows = slice(s * SEQ, (s + 1) * SEQ)
        ys = []
        for h in range(NA_HEADS):
            sl = slice(h * HEAD_DIM, (h + 1) * HEAD_DIM)
            (e,), l = probs[s * NA_HEADS + h]
            ys.append(_dot(e.astype(BF16), v[rows, sl]) / l)
        outs.append(jnp.concatenate(ys, -1))
    o_ref[...] = (outs[0] if seqs == 1 else jnp.concatenate(outs, 0)).astype(o_ref.dtype)


def _ctx_attn_call(q, k, v, layer):
    seqs = CTX_SEQS
    t = q.shape[0]
    spec = pl.BlockSpec((seqs * SEQ, 256), lambda b: (b, 0))
    kt_spec = pl.BlockSpec((seqs, None, 256, SEQ), lambda b: (b, layer, 0, 0))
    return pl.pallas_call(
        functools.partial(_ctx_attn_kernel, seqs=seqs), grid=(t // SEQ // seqs,),
        in_specs=[spec, kt_spec, spec], out_specs=spec,
        out_shape=jax.ShapeDtypeStruct((t, 256), BF16), compiler_params=_params(1), name="ctx_attn",
    )(q, k, v)


def _gqa_operands(q, k, rows_q, rows_k):
    group = SWA_HEADS // SWA_KV_HEADS
    out = []
    for kv in range(SWA_KV_HEADS):
        qs = jnp.concatenate([q[rows_q, (kv * group + g) * HEAD_DIM:(kv * group + g + 1) * HEAD_DIM]
                              for g in range(group)], 0)
        out.append((qs, k[rows_k, kv * HEAD_DIM:(kv + 1) * HEAD_DIM]))
    return out


def _gqa_sink(sink_ref, kv, m):
    group = SWA_HEADS // SWA_KV_HEADS
    row = lax.broadcasted_iota(jnp.int32, (group * m, 1), 0)
    col = jnp.full((group * m, 1), sink_ref[kv * group + group - 1] * LOG2E, F32)
    for g in range(group - 2, -1, -1):
        col = jnp.where(row < (g + 1) * m, sink_ref[kv * group + g] * LOG2E, col)
    return col


def _ctx_swa_kernel(sink_ref, q_ref, kt_ref, v_ref, o_ref, *, seqs):
    q = q_ref[...]
    kt = kt_ref[...].astype(BF16)
    v = v_ref[...]
    group = SWA_HEADS // SWA_KV_HEADS
    scores = []
    for s in range(seqs):
        rows = slice(s * SEQ, (s + 1) * SEQ)
        for kv, (qs, _) in enumerate(_gqa_operands(q, q, rows, rows)):
            scores.append(_dot(qs, kt[s, kv * HEAD_DIM:(kv + 1) * HEAD_DIM, :]))
    probs = [_softmax_blocks([sc], sink=_gqa_sink(sink_ref, i % SWA_KV_HEADS, SEQ)) for i, sc in enumerate(scores)]
    outs = []
    for s in range(seqs):
        rows = slice(s * SEQ, (s + 1) * SEQ)
        ys = []
        for kv in range(SWA_KV_HEADS):
            (e,), l = probs[s * SWA_KV_HEADS + kv]
            y = _dot(e.astype(BF16), v[rows, kv * HEAD_DIM:(kv + 1) * HEAD_DIM]) / l
            ys += [y[g * SEQ:(g + 1) * SEQ] for g in range(group)]
        outs.append(jnp.concatenate(ys, -1))
    o_ref[...] = (outs[0] if seqs == 1 else jnp.concatenate(outs, 0)).astype(o_ref.dtype)


def _ctx_swa_call(sink, q, k, v, layer):
    seqs = CTX_SEQS
    t = q.shape[0]
    rows = seqs * SEQ
    return pl.pallas_call(
        functools.partial(_ctx_swa_kernel, seqs=seqs), grid=(t // rows,),
        in_specs=[pl.BlockSpec(memory_space=pltpu.SMEM),
                  pl.BlockSpec((rows, 256), lambda b: (b, 0)),
                  pl.BlockSpec((seqs, None, 128, SEQ), lambda b: (b, layer, 0, 0)),
                  pl.BlockSpec((rows, 128), lambda b: (b, 0))],
        out_specs=pl.BlockSpec((rows, 256), lambda b: (b, 0)),
        out_shape=jax.ShapeDtypeStruct((t, 256), BF16), compiler_params=_params(1), name="ctx_swa",
    )(sink, q, k, v)


def _mla_q(cq, qn_ref, wuq_ref):
    return _dot(_rms(cq, qn_ref[...], MLA_Q_RANK).astype(BF16), wuq_ref[...])


def _mla_kv(ckv, kr, kvn_ref, wuk_ref, wuv_ref, values_t=False):
    cn = _rms(ckv, kvn_ref[...]).astype(BF16)
    kcat = _dot(cn, wuk_ref[...]) + jnp.concatenate([kr] * MLA_HEADS, -1)
    return kcat, (_dot_nt(wuv_ref[...], cn) if values_t else _dot(cn, wuv_ref[...]))


def _ctx_mla_kernel(cq_ref, ckv_ref, kr_ref, qn_ref, wuq_ref, kvn_ref, wuk_ref, wuv_ref, o_ref, *, seqs):
    q = (_mla_q(cq_ref[...].astype(F32), qn_ref, wuq_ref) * (MLA_SCALE * LOG2E)).astype(BF16)
    kcat, v = _mla_kv(ckv_ref[...], kr_ref[...], kvn_ref, wuk_ref, wuv_ref)
    kcat = kcat.astype(BF16)
    v = v.astype(BF16)
    scores = []
    for s in range(seqs):
        rows = slice(s * SEQ, (s + 1) * SEQ)
        for h in range(MLA_HEADS):
            sl = slice(h * LANE, (h + 1) * LANE)
            scores.append(_dot_nt(q[rows, sl], kcat[rows, sl]))
    probs = [_softmax_blocks([sc]) for sc in scores]
    outs = []
    for s in range(seqs):
        rows = slice(s * SEQ, (s + 1) * SEQ)
        ys = []
        for h in range(MLA_HEADS):
            (e,), l = probs[s * MLA_HEADS + h]
            ys.append(_dot(e.astype(BF16), v[rows, h * MLA_V:(h + 1) * MLA_V]) / l)
        outs.append(jnp.concatenate(ys, -1))
    o_ref[...] = (outs[0] if seqs == 1 else jnp.concatenate(outs, 0)).astype(o_ref.dtype)


_MLA_W = ("mla_qn", "mla_wuq", "mla_kvn", "mla_wuk", "mla_wuv")
_MLA_W_LAT = _MLA_W[:-1] + ("mla_wuv_t",)


def _ctx_mla_call(cq, ckv, kr, w, layer):
    seqs = CTX_SEQS
    t = cq.shape[0]
    rows = seqs * SEQ
    return pl.pallas_call(
        functools.partial(_ctx_mla_kernel, seqs=seqs), grid=(t // rows,),
        in_specs=[pl.BlockSpec((rows, 256), lambda b: (b, 0)),
                  pl.BlockSpec((rows, 128), lambda b: (b, 0)),
                  pl.BlockSpec((rows, 128), lambda b: (b, 0))] + [_layer_spec(w[k], layer) for k in _MLA_W],
        out_specs=pl.BlockSpec((rows, 256), lambda b: (b, 0)),
        out_shape=jax.ShapeDtypeStruct((t, 256), BF16), compiler_params=_params(1), name="ctx_mla",
    )(cq, ckv, kr, *[w[k] for k in _MLA_W])


_POOL_PAD = 8


def _pool_mix(xs, w_ref, sc_ref):
    n = xs[0].shape[0]
    ne = n + 2 * _POOL_PAD
    lo, hi = _POOL_PAD, _POOL_PAD + n
    z = jnp.zeros((_POOL_PAD, POOL_WIDTH), F32)
    grp = lax.broadcasted_iota(jnp.int32, (n, POOL_WIDTH), 1) >> 6
    t = lax.broadcasted_iota(jnp.int32, (n, POOL_WIDTH), 0)
    half = jnp.where(grp == 0, 1, jnp.where(grp == 1, 2, jnp.where(grp == 2, 4, 8)))
    cnt = (jnp.minimum(t + half, n) - jnp.maximum(t - half, 0)).astype(F32)

    def pair(a, s):
        return pltpu.roll(a, s, axis=0) + pltpu.roll(a, ne - s, axis=0)

    out = []
    for x in xs:
        xz = jnp.concatenate([z, x, z], 0)
        s2 = xz + pltpu.roll(xz, 1, axis=0)
        s4 = pair(s2, 1)
        s8 = pair(s4, 2)
        s16 = pair(s8, 4)
        tot = jnp.where(grp == 0, s2[lo:hi],
                        jnp.where(grp == 1, s4[lo:hi], jnp.where(grp == 2, s8[lo:hi], s16[lo:hi])))
        dlt = (tot / cnt - x).astype(BF16)
        out.append(_dot(dlt, w_ref[...]) * sc_ref[...])
    return out


def _pool_kernel(x_ref, w_ref, sc_ref, o_ref, *, n):
    xs = [x_ref[s * n:(s + 1) * n, :].astype(F32) for s in range(x_ref.shape[0] // n)]
    for s, y in enumerate(_pool_mix(xs, w_ref, sc_ref)):
        o_ref[s * n:(s + 1) * n, :] = y.astype(o_ref.dtype)


def _pool_call(pd, w, layer, seq):
    t = pd.shape[0]
    rows = seq
    return pl.pallas_call(
        functools.partial(_pool_kernel, n=seq), grid=(t // rows,),
        in_specs=[pl.BlockSpec((rows, POOL_WIDTH), lambda b: (b, 0)),
                  _layer_spec(w["pool_w"], layer), _layer_spec(w["pool_scale"], layer)],
        out_specs=pl.BlockSpec((rows, POOL_WIDTH), lambda b: (b, 0)),
        out_shape=jax.ShapeDtypeStruct((t, POOL_WIDTH), BF16), compiler_params=_params(1), name="pool",
    )(pd, w["pool_w"], w["pool_scale"])


def _lat_na_kernel(q_ref, k_ref, v_ref, kc_ref, vc_ref, t2_ref, o_ref):
    n = pl.program_id(1)
    rows = DEC_SEQ // GRID_W
    q_rows = NA_Q_BLOCK // GRID_W
    row0 = jnp.clip(q_rows * n - NA_WIN_R // 2, 0, rows - NA_SPAN // GRID_W)
    start = pl.multiple_of(row0 * GRID_W, LANE)
    q = q_ref[...]
    k = k_ref[pl.ds(start, NA_SPAN), :]
    v = v_ref[pl.ds(start, NA_SPAN), :]
    kc = kc_ref[...].astype(BF16)
    vc = vc_ref[...].astype(BF16)

    pairs = NA_SPAN // LANE
    low_half = lax.broadcasted_iota(jnp.int32, (GRID_W, LANE), 1) < GRID_W
    entries, masks = [], []
    for a in range(q_rows):
        r = q_rows * n + a
        r_start = jnp.clip(r - NA_WIN_R // 2, 0, rows - NA_WIN_R)
        for p in range(pairs):
            rk = row0 + 2 * p
            ok0 = ((rk >= r_start) & (rk < r_start + NA_WIN_R)).astype(jnp.int32)
            ok1 = ((rk + 1 >= r_start) & (rk + 1 < r_start + NA_WIN_R)).astype(jnp.int32)
            entries.append(jnp.clip(rk - r + NA_WIN_R, 0, NA_DR))
            masks.append(jnp.where(low_half, ok0, ok1) > 0)

    heads = [slice(h * HEAD_DIM, (h + 1) * HEAD_DIM) for h in range(NA_HEADS)]
    raw = [(_dot_nt(q[:, sl], k[:, sl]), _dot_nt(q[:, sl], kc[:, sl])) for sl in heads]
    probs = []
    for h, (s_loc, s_ctx) in enumerate(raw):
        cols = []
        for p in range(pairs):
            blk = [jnp.where(masks[a * pairs + p], t2_ref[h, entries[a * pairs + p]], NEG_INF)
                   for a in range(q_rows)]
            cols.append(jnp.concatenate(blk, 0))
        probs.append(_softmax_blocks([s_loc + jnp.concatenate(cols, 1), s_ctx]))
    ys = []
    for sl, ((e_loc, e_ctx), l) in zip(heads, probs):
        y = _dot(e_loc.astype(BF16), v[:, sl]) + _dot(e_ctx.astype(BF16), vc[:, sl])
        ys.append(y / l)
    o_ref[...] = jnp.concatenate(ys, -1).astype(o_ref.dtype)


def _lat_na_call(q, k, v, cache_k, cache_v, t2, layer):
    nq = DEC_SEQ // NA_Q_BLOCK
    seq_spec = pl.BlockSpec((DEC_SEQ, 256), lambda b, n: (b, 0))
    cache_spec = pl.BlockSpec((None, None, PAST_LEN, 256), lambda b, n: (b, layer, 0, 0))
    return pl.pallas_call(
        _lat_na_kernel, grid=(DEC_BATCH, nq),
        in_specs=[pl.BlockSpec((NA_Q_BLOCK, 256), lambda b, n: (b * nq + n, 0)), seq_spec, seq_spec,
                  cache_spec, cache_spec, _layer_spec(t2, layer)],
        out_specs=pl.BlockSpec((NA_Q_BLOCK, 256), lambda b, n: (b * nq + n, 0)),
        out_shape=jax.ShapeDtypeStruct((DEC_BATCH * DEC_SEQ, 256), BF16),
        compiler_params=_params(2), name="lat_na",
    )(q, k, v, cache_k, cache_v, t2)


def _lat_swa_kernel(sink_ref, q_ref, k_ref, v_ref, kc_ref, vc_ref, o_ref):
    n = pl.program_id(1)
    start = jnp.clip(n * SWA_Q_BLOCK - SWA_WINDOW, 0, DEC_SEQ - SWA_SPAN)
    start = pl.multiple_of(start, LANE)
    q = q_ref[...]
    k = k_ref[pl.ds(start, SWA_SPAN), :]
    v = v_ref[pl.ds(start, SWA_SPAN), :]
    kc = kc_ref[...].astype(BF16)
    vc = vc_ref[...].astype(BF16)
    group = SWA_HEADS // SWA_KV_HEADS
    m = group * SWA_Q_BLOCK
    q_pos = n * SWA_Q_BLOCK + (lax.broadcasted_iota(jnp.int32, (m, SWA_SPAN), 0) & (SWA_Q_BLOCK - 1))
    k_pos = start + lax.broadcasted_iota(jnp.int32, (m, SWA_SPAN), 1)
    valid = jnp.abs(q_pos - k_pos) <= SWA_WINDOW
    everything = slice(None)
    raw = []
    for (qs, ks), (_, kcs) in zip(_gqa_operands(q, k, everything, everything),
                                  _gqa_operands(q, kc, everything, everything)):
        raw.append((_dot_nt(qs, ks), _dot_nt(qs, kcs)))
    probs = [_softmax_blocks([jnp.where(valid, s_loc, NEG_INF), s_ctx], sink=_gqa_sink(sink_ref, kv, SWA_Q_BLOCK))
             for kv, (s_loc, s_ctx) in enumerate(raw)]
    ys = []
    for kv, ((e_loc, e_ctx), l) in enumerate(probs):
        kvsl = slice(kv * HEAD_DIM, (kv + 1) * HEAD_DIM)
        y = (_dot(e_loc.astype(BF16), v[:, kvsl]) + _dot(e_ctx.astype(BF16), vc[:, kvsl])) / l
        ys += [y[g * SWA_Q_BLOCK:(g + 1) * SWA_Q_BLOCK] for g in range(group)]
    o_ref[...] = jnp.concatenate(ys, -1).astype(o_ref.dtype)


def _lat_swa_call(sink, q, k, v, cache_k, cache_v, layer):
    nq = DEC_SEQ // SWA_Q_BLOCK
    seq_spec = pl.BlockSpec((DEC_SEQ, 128), lambda b, n: (b, 0))
    cache_spec = pl.BlockSpec((None, None, PAST_LEN, 128), lambda b, n: (b, layer, 0, 0))
    return pl.pallas_call(
        _lat_swa_kernel, grid=(DEC_BATCH, nq),
        in_specs=[pl.BlockSpec(memory_space=pltpu.SMEM),
                  pl.BlockSpec((SWA_Q_BLOCK, 256), lambda b, n: (b * nq + n, 0)), seq_spec, seq_spec,
                  cache_spec, cache_spec],
        out_specs=pl.BlockSpec((SWA_Q_BLOCK, 256), lambda b, n: (b * nq + n, 0)),
        out_shape=jax.ShapeDtypeStruct((DEC_BATCH * DEC_SEQ, 256), BF16),
        compiler_params=_params(2), name="lat_swa",
    )(sink, q, k, v, cache_k, cache_v)


def _lat_mla_kernel(cq_ref, ckv_ref, kr_ref, cckv_ref, ckr_ref, cm_ref, sm_ref,
                    qn_ref, wuq_ref, kvn_ref, wuk_ref, wuvt_ref, o_ref, kcat_s, vt_s):
    @pl.when(pl.program_id(1) == 0)
    def _():
        kc, vc = _mla_kv(cckv_ref[...], ckr_ref[...], kvn_ref, wuk_ref, wuvt_ref, True)
        kcat_s[0:PAST_LEN, :] = kc.astype(BF16)
        vt_s[:, 0:PAST_LEN] = vc.astype(BF16)
        kl, vl = _mla_kv(ckv_ref[...].astype(F32), kr_ref[...].astype(F32), kvn_ref, wuk_ref, wuvt_ref, True)
        kcat_s[PAST_LEN:, :] = kl.astype(BF16)
        vt_s[:, PAST_LEN:] = vl.astype(BF16)

    q = _mla_q(cq_ref[...].astype(F32), qn_ref, wuq_ref)
    q = (_rope(q, cm_ref[...], sm_ref[...], 8) * (MLA_SCALE * LOG2E)).astype(BF16)
    scores = [_dot_nt(kcat_s[:, h * LANE:(h + 1) * LANE], q[:, h * LANE:(h + 1) * LANE]) for h in range(MLA_HEADS)]
    ys = []
    for h, st in enumerate(scores):
        e = jnp.exp2(st - jnp.max(st, 0, keepdims=True))
        l = jnp.sum(e, 0, keepdims=True)
        ys.append(_dot(vt_s[h * MLA_V:(h + 1) * MLA_V, :], e.astype(BF16)) / l)
    o_ref[...] = jnp.concatenate(ys, 0).T.astype(o_ref.dtype)


def _lat_mla_call(cq, ckv, kr, cache_ckv, cache_kr, cm, sm, w, layer):
    qb = MLA_Q_BLOCK
    nq = DEC_SEQ // qb
    seq_spec = pl.BlockSpec((DEC_SEQ, 128), lambda b, n: (b, 0))
    cache_spec = pl.BlockSpec((None, None, PAST_LEN, 128), lambda b, n: (b, layer, 0, 0))
    tab_spec = pl.BlockSpec((qb, 512), lambda b, n: (n, 0))
    return pl.pallas_call(
        _lat_mla_kernel, grid=(DEC_BATCH, nq),
        in_specs=[pl.BlockSpec((qb, 256), lambda b, n: (b * nq + n, 0)), seq_spec, seq_spec,
                  cache_spec, cache_spec, tab_spec, tab_spec] + [_layer_spec(w[k], layer) for k in _MLA_W_LAT],
        out_specs=pl.BlockSpec((qb, 256), lambda b, n: (b * nq + n, 0)),
        out_shape=jax.ShapeDtypeStruct((DEC_BATCH * DEC_SEQ, 256), BF16),
        scratch_shapes=[pltpu.VMEM((PAST_LEN + DEC_SEQ, MLA_HEADS * LANE), BF16),
                        pltpu.VMEM((MLA_HEADS * MLA_V, PAST_LEN + DEC_SEQ), BF16)],
        compiler_params=_params(2), name="lat_mla",
    )(cq, ckv, kr, cache_ckv, cache_kr, cm, sm, *[w[k] for k in _MLA_W_LAT])


_MERGE_PARTS = 2


def _merge_kernel(x_ref, mod_ref, gpre_ref, ya_ref, yb_ref, yc_ref, yd_ref,
                  wg_ref, bg_ref, wb_ref, wo_ref, gpost_ref, o_ref):
    tm = x_ref.shape[0] // _MERGE_PARTS
    for p in range(_MERGE_PARTS):
        rows = slice(p * tm, (p + 1) * tm)
        x = x_ref[rows, :]
        h = (_rms(x, gpre_ref[...]) * (1.0 + mod_ref[0, 1:2, :]) + mod_ref[0, 0:1, :]).astype(BF16)
        merged = None
        for k, y_ref in enumerate((ya_ref, yb_ref, yc_ref, yd_ref)):
            cols = slice(k * D_MODEL, (k + 1) * D_MODEL)
            gate = _sigmoid(_dot(h, wg_ref[:, cols]) + bg_ref[:, cols])
            term = gate * _dot(y_ref[rows, :], wb_ref[k])
            merged = term if merged is None else merged + term
        o = _dot(merged.astype(BF16), wo_ref[...])
        o_ref[rows, :] = x + mod_ref[0, 2:3, :] * _rms(o, gpost_ref[...])


_MERGE_W = ("w_gate", "b_gate", "w_branch", "w_out", "g_attn_post")


def _merge_call(x2d, mod, ys, w, layer):
    t = x2d.shape[0]
    tm = 512
    tiles_per_mod = t // tm // mod.shape[0]
    tile = pl.BlockSpec((tm, D_MODEL), lambda i: (i, 0))
    ytile = pl.BlockSpec((tm, BRANCH_W), lambda i: (i, 0))
    return pl.pallas_call(
        _merge_kernel, grid=(t // tm,),
        in_specs=[tile, pl.BlockSpec((1, 6, D_MODEL), lambda i: (i // tiles_per_mod, 0, 0)),
                  _layer_spec(w["g_attn_pre"], layer), ytile, ytile, ytile, ytile]
                 + [_layer_spec(w[k], layer) for k in _MERGE_W],
        out_specs=tile,
        out_shape=jax.ShapeDtypeStruct((t, D_MODEL), F32),
        compiler_params=_params(1), name="merge",
    )(x2d, mod, w["g_attn_pre"], *ys, *[w[k] for k in _MERGE_W])


_GAP = 8


def _ffn_kernel(*refs, seqs, halo):
    if halo:
        x_ref, xp_ref, xn_ref = refs[:3]
        refs = refs[3:]
    else:
        x_ref = refs[0]
        refs = refs[1:]
    mod_ref, gpre_ref, wa_ref, wg_ref, ca_ref, cg_ref, wd_ref, gpost_ref, o_ref = refs
    x = x_ref[...]
    tm = x.shape[0]
    shift, scale, gate = mod_ref[0, 3:4, :], mod_ref[0, 4:5, :], mod_ref[0, 5:6, :]

    def pre(xx):
        return _rms(xx, gpre_ref[...]) * (1.0 + scale) + shift

    h = pre(x)
    if halo:
        i = pl.program_id(0) % halo
        hp = jnp.where(i == 0, 0.0, pre(xp_ref[...]))
        hn = jnp.where(i == halo - 1, 0.0, pre(xn_ref[...]))
        pieces = [hp, h, hn]
        starts = [_GAP]
        seq_len = tm
    else:
        seq_len = tm // seqs
        pieces = [h]
        starts = [s * seq_len for s in range(seqs)]
    hb = jnp.concatenate(pieces, 0).astype(BF16)
    rows = hb.shape[0]
    edge_row = lax.broadcasted_iota(jnp.int32, (_GAP, FF_CHUNK), 0)

    def zero_edges(a, first):
        out = []
        for st in starts:
            seg = a[st:st + seq_len]
            if first:
                out += [jnp.where(edge_row == 0, 0.0, seg[:_GAP]), seg[_GAP:]]
            else:
                out += [seg[:seq_len - _GAP], jnp.where(edge_row == _GAP - 1, 0.0, seg[seq_len - _GAP:])]
        return jnp.concatenate(out, 0)

    def up(c):
        cols = slice(c * FF_CHUNK, (c + 1) * FF_CHUNK)
        return _dot(hb, wa_ref[:, cols]), _dot(hb, wg_ref[:, cols])

    def conv(u, c_ref, cols):
        prev = pltpu.roll(u, 1, axis=0)
        nxt = pltpu.roll(u, rows - 1, axis=0)
        if not halo:
            prev, nxt = zero_edges(prev, True), zero_edges(nxt, False)
        return prev * c_ref[0:1, cols] + u * c_ref[1:2, cols] + nxt * c_ref[2:3, cols]

    acts = []
    for c in range(D_FF_PAD // FF_CHUNK):
        ua, ug = up(c)
        cols = slice(c * FF_CHUNK, (c + 1) * FF_CHUNK)
        a = conv(ua, ca_ref, cols)
        g = conv(ug, cg_ref, cols)
        acts.append((g * _sigmoid(g) * a).astype(BF16))
    acc = _dot(jnp.concatenate(acts, 1), wd_ref[...])
    for s, st in enumerate(starts):
        ys = _rms(acc[st:st + seq_len], gpost_ref[...])
        o_ref[s * seq_len:(s + 1) * seq_len, :] = x[s * seq_len:(s + 1) * seq_len] + gate * ys


_FFN_W = ("g_ffn_pre", "ffn_wa", "ffn_wg", "ffn_ca", "ffn_cg", "ffn_wd", "g_ffn_post")


def _ffn_call(x2d, mod, w, layer, seq, tm):
    t = x2d.shape[0]
    tile = pl.BlockSpec((tm, D_MODEL), lambda i: (i, 0))
    in_specs = [tile]
    args = [x2d]
    if tm < seq:
        halo, seqs = seq // tm, 1
        r = tm // _GAP
        last = t // _GAP - 1
        in_specs += [pl.BlockSpec((_GAP, D_MODEL), lambda i: (jnp.maximum(i * r - 1, 0), 0)),
                     pl.BlockSpec((_GAP, D_MODEL), lambda i: (jnp.minimum((i + 1) * r, last), 0))]
        args += [x2d, x2d]
    else:
        halo, seqs = 0, tm // seq
    tiles_per_mod = t // tm // mod.shape[0]
    in_specs += [pl.BlockSpec((1, 6, D_MODEL), lambda i: (i // tiles_per_mod, 0, 0))]
    in_specs += [_layer_spec(w[k], layer) for k in _FFN_W]
    args += [mod] + [w[k] for k in _FFN_W]
    return pl.pallas_call(
        functools.partial(_ffn_kernel, seqs=seqs, halo=halo), grid=(t // tm,),
        in_specs=in_specs, out_specs=tile,
        out_shape=jax.ShapeDtypeStruct((t, D_MODEL), F32),
        compiler_params=_params(1), name="ffn",
    )(*args)


def _rope_tables():
    t = np.arange(DEC_SEQ)
    pos = (t // GRID_W, t % GRID_W)

    def tab(d):
        half = d // 4
        inv = np.float32(ROPE_BASE) ** (-np.arange(half, dtype=np.float32) / np.float32(half))
        cs, sn = [], []
        for p in pos:
            ang = p.astype(np.float32)[:, None] * inv[None, :]
            cs += [np.cos(ang), np.cos(ang)]
            sn += [-np.sin(ang), np.sin(ang)]
        return np.concatenate(cs, -1), np.concatenate(sn, -1)

    c64, s64 = tab(HEAD_DIM)
    c32, s32 = tab(MLA_ROPE)
    pad = LANE - MLA_NOPE - MLA_ROPE
    cm = np.concatenate([np.ones((DEC_SEQ, MLA_NOPE), np.float32), c32, np.ones((DEC_SEQ, pad), np.float32)], -1)
    sm = np.concatenate([np.zeros((DEC_SEQ, MLA_NOPE), np.float32), s32, np.zeros((DEC_SEQ, pad), np.float32)], -1)
    return tuple(jnp.asarray(np.tile(x, (1, 4)), F32) for x in (c64, s64, cm, sm))


def _na_table_kernel(rpb_ref, e_ref, ok_ref, o_ref):
    r = rpb_ref[...]
    r1 = r.astype(BF16)
    r2 = (r - r1.astype(F32)).astype(BF16)
    r3 = (r - r1.astype(F32) - r2.astype(F32)).astype(BF16)
    e = e_ref[...]
    t = _dot(r1, e) + _dot(r2, e) + _dot(r3, e)
    o_ref[...] = jnp.where(ok_ref[...] > 0, t * LOG2E, NEG_INF)


def _na_bias_tables(na_rpb):
    c = np.arange(GRID_W)[:, None]
    w = np.arange(GRID_W)[None, :]
    dc = (w - c + NA_WIN_C - 1).reshape(-1)
    onehot = (np.arange(LANE)[:, None] == dc[None, :]).astype(np.float32)
    c_start = np.clip(c - NA_WIN_C // 2, 0, GRID_W - NA_WIN_C)
    ok = ((w >= c_start) & (w < c_start + NA_WIN_C)).reshape(1, -1).astype(np.int32)
    rows = DEPTH * NA_HEADS * NA_DR
    rpb2 = jnp.pad(na_rpb.reshape(rows, NA_DC), ((0, LANE - rows), (0, LANE - NA_DC)))
    t = pl.pallas_call(
        _na_table_kernel, out_shape=jax.ShapeDtypeStruct((LANE, GRID_W * GRID_W), F32), name="na_table",
        compiler_params=pltpu.CompilerParams(vmem_limit_bytes=VMEM_LIMIT),
    )(rpb2, jnp.asarray(onehot, BF16), jnp.asarray(ok))
    t = t[:rows].reshape(DEPTH, NA_HEADS, NA_DR, GRID_W, GRID_W)
    t = jnp.pad(t, ((0, 0), (0, 0), (1, 1), (0, 0), (0, 0)), constant_values=NEG_INF)
    return jnp.concatenate([t[:, :, :-1], t[:, :, 1:]], -1)


def _pad_last(w, n):
    return jnp.pad(w, ((0, 0),) * (w.ndim - 1) + ((0, n - w.shape[-1]),))


def _prep_weights(g_attn_pre, g_attn_post, g_ffn_pre, g_ffn_post, w_in, w_gate, b_gate, mla_q_norm, mla_w_uq,
                  mla_kv_norm, mla_w_ukv, pool_w, pool_scale, w_branch, w_out, ffn_w_up, ffn_conv, ffn_w_down):
    q_scale = ATT_SCALE * LOG2E
    b0, c0, d0 = 768, 1120, 1632
    wt = w_in.transpose(0, 2, 1)

    def rows(lo, hi, before=0, after=0):
        return jnp.pad(wt[:, lo:hi], ((0, 0), (before, after), (0, 0)))

    w_in_p = jnp.concatenate([
        wt[:, d0:],
        wt[:, c0:c0 + 256] * q_scale,
        wt[:, 256:768],
        rows(b0 + 320, b0 + 352, _KR_LANE, LANE - _KR_LANE - MLA_ROPE),
        wt[:, c0 + 256:d0],
        wt[:, b0 + MLA_Q_RANK:b0 + MLA_Q_RANK + MLA_KV_RANK],
        wt[:, :256] * q_scale,
        rows(b0, b0 + MLA_Q_RANK, 0, 256 - MLA_Q_RANK)], 1).astype(BF16)

    wuq = mla_w_uq.reshape(DEPTH, MLA_Q_RANK, MLA_HEADS, MLA_NOPE + MLA_ROPE)
    wuq = jnp.pad(wuq, ((0, 0), (0, 256 - MLA_Q_RANK), (0, 0), (0, LANE - MLA_NOPE - MLA_ROPE)))
    wukv = mla_w_ukv.reshape(DEPTH, MLA_KV_RANK, MLA_HEADS, MLA_NOPE + MLA_V)
    wuk = _pad_last(wukv[..., :MLA_NOPE], LANE)
    wuv = wukv[..., MLA_NOPE:].reshape(DEPTH, MLA_KV_RANK, MLA_HEADS * MLA_V).astype(BF16)

    eye = np.eye(len(POOL_WINDOWS), dtype=np.float32)
    w_bd = (pool_w[:, :, :, None, :] * eye[None, :, None, :, None]).reshape(DEPTH, POOL_WIDTH, POOL_WIDTH)

    return dict(
        g_attn_pre=g_attn_pre[:, None, :], g_attn_post=g_attn_post[:, None, :],
        g_ffn_pre=g_ffn_pre[:, None, :], g_ffn_post=g_ffn_post[:, None, :],
        w_in_p=w_in_p,
        mla_qn=_pad_last(mla_q_norm[:, None, :], 256),
        mla_wuq=wuq.reshape(DEPTH, 256, MLA_HEADS * LANE).astype(BF16),
        mla_kvn=mla_kv_norm[:, None, :],
        mla_wuk=wuk.reshape(DEPTH, MLA_KV_RANK, MLA_HEADS * LANE).astype(BF16),
        mla_wuv=wuv, mla_wuv_t=wuv.transpose(0, 2, 1),
        pool_w=w_bd.astype(BF16), pool_scale=pool_scale[:, None, :],
        w_gate=w_gate.astype(BF16), b_gate=b_gate[:, None, :],
        w_branch=w_branch.astype(BF16), w_out=w_out.astype(BF16),
        ffn_wa=_pad_last(ffn_w_up[:, :, :D_FF], D_FF_PAD).astype(BF16),
        ffn_wg=_pad_last(ffn_w_up[:, :, D_FF:], D_FF_PAD).astype(BF16),
        ffn_ca=_pad_last(ffn_conv[:, :, :D_FF], D_FF_PAD), ffn_cg=_pad_last(ffn_conv[:, :, D_FF:], D_FF_PAD),
        ffn_wd=jnp.pad(ffn_w_down, ((0, 0), (0, D_FF_PAD - D_FF), (0, 0))).astype(BF16))


def kernel(x_prompt, x_sample, cache_na_k, cache_na_v, cache_mla_ckv, cache_mla_krope, cache_swa_k, cache_swa_v, c, c_ctx, w_mod, b_mod, g_attn_pre, g_attn_post, g_ffn_pre, g_ffn_post, w_in, w_gate, b_gate, na_rpb, mla_q_norm, mla_w_uq, mla_kv_norm, mla_w_ukv, swa_sink, pool_w, pool_scale, w_branch, w_out, ffn_w_up, ffn_conv, ffn_w_down):
    x_p = x_prompt.reshape(BATCH * SEQ, D_MODEL)
    x_s = x_sample.reshape(DEC_BATCH * DEC_SEQ, D_MODEL)

    cv = jnp.concatenate([c_ctx[None, :], c, jnp.zeros((8 - 1 - DEC_BATCH, D_MODEL), F32)], 0)
    mod = _mod_call(cv, w_mod, b_mod).reshape(DEPTH, 8, 6, D_MODEL)
    w = _prep_weights(g_attn_pre, g_attn_post, g_ffn_pre, g_ffn_post, w_in, w_gate, b_gate, mla_q_norm, mla_w_uq,
                      mla_kv_norm, mla_w_ukv, pool_w, pool_scale, w_branch, w_out, ffn_w_up, ffn_conv, ffn_w_down)
    rope_tabs = _rope_tables()
    na_t2 = _na_bias_tables(na_rpb)
    cache_na_k = cache_na_k.reshape(DEC_BATCH, DEPTH, PAST_LEN, 256)
    cache_na_v = cache_na_v.reshape(DEC_BATCH, DEPTH, PAST_LEN, 256)
    cache_swa_k = cache_swa_k.reshape(DEC_BATCH, DEPTH, PAST_LEN, 128)
    cache_swa_v = cache_swa_v.reshape(DEC_BATCH, DEPTH, PAST_LEN, 128)
    cache_kr = jnp.pad(cache_mla_krope, ((0, 0), (0, 0), (0, 0), (_KR_LANE, LANE - _KR_LANE - MLA_ROPE)))

    carried, ckvs = (), []
    for l in range(DEPTH):
        mod_p = mod[l, 0:1]
        qa, ka_t, va_t, va, cq, ckv, kr, kr_t, qc, kc_t, vc_t, vc, yd = _inproj_call(x_p, mod_p, w, l,
                                                                                      carried=carried)
        carried = (ka_t, va_t, kr_t, kc_t, vc_t)
        ckvs.append(ckv)
        ys = (_ctx_attn_call(qa, ka_t, va, l), _ctx_mla_call(cq, ckv, kr, w, l),
              _ctx_swa_call(swa_sink[l], qc, kc_t, vc, l), yd)
        x_p = _merge_call(x_p, mod_p, ys, w, l)
        x_p = _ffn_call(x_p, mod_p, w, l, SEQ, 2 * SEQ)

        mod_s = mod[l, 1:1 + DEC_BATCH]
        qa, ka, va, cq, ckv, kr, qc, kc, vc, pd = _inproj_call(x_s, mod_s, w, l, rope_tabs)
        ys = (_lat_na_call(qa, ka, va, cache_na_k, cache_na_v, na_t2, l),
              _lat_mla_call(cq, ckv, kr, cache_mla_ckv, cache_kr, rope_tabs[2], rope_tabs[3], w, l),
              _lat_swa_call(swa_sink[l], qc, kc, vc, cache_swa_k, cache_swa_v, l),
              _pool_call(pd, w, l, DEC_SEQ))
        x_s = _merge_call(x_s, mod_s, ys, w, l)
        x_s = _ffn_call(x_s, mod_s, w, l, DEC_SEQ, 1024)

    ka_t, va_t, kr_t, kc_t, vc_t = carried

    def heads_last(a, heads):
        return a.reshape(BATCH, DEPTH, heads, HEAD_DIM, SEQ).transpose(0, 1, 4, 2, 3)

    return (x_p.reshape(BATCH, SEQ, D_MODEL), x_s.reshape(DEC_BATCH, DEC_SEQ, D_MODEL),
            heads_last(ka_t, NA_HEADS), heads_last(va_t, NA_HEADS),
            jnp.stack([c.reshape(BATCH, SEQ, MLA_KV_RANK) for c in ckvs], 1),
            kr_t[:, :, _KR_LANE:_KR_LANE + MLA_ROPE, :].transpose(0, 1, 3, 2),
            heads_last(kc_t, SWA_KV_HEADS), heads_last(vc_t, SWA_KV_HEADS))
```

```python
import functools

import jax
import jax.numpy as jnp
import numpy as np
from jax import lax
from jax.experimental import pallas as pl
from jax.experimental.pallas import tpu as pltpu

F32 = jnp.float32
BF16 = jnp.bfloat16

D_MODEL = 1024
BATCH = 32
SEQ = 256
DEPTH = 2
DEC_BATCH = 2
DEC_SEQ = 2048
PAST_LEN = 256
GRID_W = 64
HEAD_DIM = 64
NA_HEADS = 4
NA_WIN_R = 8
NA_WIN_C = 16
MLA_HEADS = 4
MLA_NOPE = 64
MLA_ROPE = 32
MLA_V = 64
MLA_Q_RANK = 192
MLA_KV_RANK = 128
SWA_HEADS = 4
SWA_KV_HEADS = 2
SWA_WINDOW = 128
POOL_WINDOWS = (2, 4, 8, 16)
POOL_GROUP = 64
POOL_WIDTH = 256
BRANCH_W = 256
N_BRANCH = 4
D_FF = 2752
ROPE_BASE = 10000.0
EPS = 1e-6
NEG_INF = -1e30
ATT_SCALE = HEAD_DIM ** -0.5
MLA_SCALE = (MLA_NOPE + MLA_ROPE) ** -0.5
LOG2E = 1.4426950408889634

LANE = 128
D_FF_PAD = 2816
FF_CHUNK = 256
SWA_Q_BLOCK = 256
NA_Q_BLOCK = 256
NA_SPAN = 768
NA_DR = 2 * NA_WIN_R - 1
NA_DC = 2 * NA_WIN_C - 1
SWA_SPAN = SWA_Q_BLOCK + 2 * SWA_WINDOW
MLA_Q_BLOCK = 256
CTX_SEQS = 4
CTX_MLA_SEQS = 8
VMEM_LIMIT = 56 * 1024 * 1024

_PD, _QC, _KA, _VA, _KR, _KC, _VC, _CKV, _QA, _CQ = 0, 256, 512, 768, 1024, 1152, 1280, 1408, 1536, 1792
_KR_LANE = 64


def _dot(a, b):
    return jnp.dot(a, b, preferred_element_type=F32)


def _dot_nt(a, b):
    return lax.dot_general(a, b, (((1,), (1,)), ((), ())), preferred_element_type=F32)


def _sigmoid(x):
    return 1.0 / (1.0 + jnp.exp2(x * -LOG2E))


def _rms(x, g, n=None):
    n = x.shape[-1] if n is None else n
    ms = jnp.sum(x * x, -1, keepdims=True) * (1.0 / n)
    return x * lax.rsqrt(ms + EPS) * g


def _softmax_blocks(blocks, sink=None):
    m = None
    for s in blocks:
        mm = jnp.max(s, -1, keepdims=True)
        m = mm if m is None else jnp.maximum(m, mm)
    if sink is not None:
        m = jnp.maximum(m, sink)
    es = [jnp.exp2(s - m) for s in blocks]
    l = None
    for e in es:
        ll = jnp.sum(e, -1, keepdims=True)
        l = ll if l is None else l + ll
    if sink is not None:
        l = l + jnp.exp2(sink - m)
    return es, l


def _rope(x, cos, sin, q):
    w = x.shape[-1]
    lane = lax.broadcasted_iota(jnp.int32, x.shape, 1)
    up = pltpu.roll(x, w - q, axis=1)
    dn = pltpu.roll(x, q, axis=1)
    partner = jnp.where((lane & (2 * q - 1)) < q, up, dn)
    return x * cos + partner * sin


def _const_spec(shape):
    n = len(shape)
    return pl.BlockSpec(shape, lambda *_: (0,) * n, pipeline_mode=pl.Buffered(1))


def _layer_spec(arr, layer):
    n = arr.ndim - 1
    return pl.BlockSpec((None,) + arr.shape[1:], lambda *_: (layer,) + (0,) * n, pipeline_mode=pl.Buffered(1))


def _params(n_axes):
    return pltpu.CompilerParams(dimension_semantics=("arbitrary",) * n_axes, vmem_limit_bytes=VMEM_LIMIT)


def _mod_kernel(cv_ref, w_ref, b_ref, o_ref):
    cv = cv_ref[...]
    a = (cv * _sigmoid(cv)).astype(BF16)
    o_ref[0] = _dot(a, w_ref[0].astype(BF16)) + b_ref[0]


def _mod_call(cv, w_mod, b_mod):
    tn = 2048
    return pl.pallas_call(
        _mod_kernel,
        grid=(DEPTH, 6 * D_MODEL // tn),
        in_specs=[_const_spec((8, D_MODEL)),
                  pl.BlockSpec((1, D_MODEL, tn), lambda l, j: (l, 0, j)),
                  pl.BlockSpec((1, 1, tn), lambda l, j: (l, 0, j))],
        out_specs=pl.BlockSpec((1, 8, tn), lambda l, j: (l, 0, j)),
        out_shape=jax.ShapeDtypeStruct((DEPTH, 8, 6 * D_MODEL), F32),
        compiler_params=_params(2),
        name="mod",
    )(cv, w_mod, b_mod.reshape(DEPTH, 1, 6 * D_MODEL))


_IN_SLOTS = ((_QA, 256), (_KA, 256), (_VA, 256), (_CQ, 256), (_CKV, 128), (_KR, 128),
             (_QC, 256), (_KC, 128), (_VC, 128), (_PD, 256))
_CTX_SLOTS = ((_QA, 256, BF16, False), (_KA, 256, F32, True), (_VA, 256, F32, True), (_VA, 256, BF16, False),
              (_CQ, 256, BF16, False), (_CKV, 128, F32, False), (_KR, 128, F32, False), (_KR, 128, F32, True),
              (_QC, 256, BF16, False), (_KC, 128, F32, True), (_VC, 128, F32, True), (_VC, 128, BF16, False),
              (_PD, 256, BF16, False))
_LAT_SLOTS = tuple((off, wd, BF16, False) for off, wd in _IN_SLOTS)


def _inproj_kernel(*refs, latent, n_carried=0):
    if latent:
        x_ref, mod_ref, g_ref, w_ref, c64_ref, s64_ref, cm_ref, sm_ref = refs[:8]
        outs = refs[8:]
    else:
        x_ref, mod_ref, g_ref, w_ref, pw_ref, ps_ref = refs[:6]
        outs = refs[6 + n_carried:]
    x = x_ref[...]
    h = _rms(x, g_ref[...]) * (1.0 + mod_ref[0, 1:2, :]) + mod_ref[0, 0:1, :]
    p = _dot_nt(h.astype(BF16), w_ref[...])
    slots = sorted(zip(_LAT_SLOTS if latent else _CTX_SLOTS, outs), key=lambda so: so[0][0])
    for (off, wd, _, transposed), o_ref in slots:
        v = p[:, off:off + wd]
        if latent:
            if off == _QC:
                v = _rope(v, c64_ref[...], s64_ref[...], 16)
            elif off == _KC:
                v = _rope(v, c64_ref[:, :128], s64_ref[:, :128], 16)
            elif off == _KR:
                v = _rope(v, cm_ref[:, :128], sm_ref[:, :128], 8)
        if transposed:
            for b in range(o_ref.shape[0]):
                slab = v[b * SEQ:(b + 1) * SEQ].T.astype(o_ref.dtype)
                if n_carried:
                    o_ref[b] = slab
                else:
                    o_ref[b, 0] = slab
                    o_ref[b, 1:] = jnp.zeros((DEPTH - 1,) + slab.shape, o_ref.dtype)
        elif off == _PD and not latent:
            ys = _pool_mix([v[b * SEQ:(b + 1) * SEQ] for b in range(v.shape[0] // SEQ)], pw_ref, ps_ref)
            for b, y in enumerate(ys):
                o_ref[b * SEQ:(b + 1) * SEQ, :] = y.astype(o_ref.dtype)
        else:
            o_ref[...] = v.astype(o_ref.dtype)


def _inproj_call(x2d, mod, w, layer, rope_tabs=None, carried=()):
    t = x2d.shape[0]
    latent = rope_tabs is not None
    tm = 512 if latent else 1024
    slots = _LAT_SLOTS if latent else _CTX_SLOTS
    tiles_per_mod = t // tm // mod.shape[0]
    in_specs = [pl.BlockSpec((tm, D_MODEL), lambda i: (i, 0)),
                pl.BlockSpec((1, 6, D_MODEL), lambda i: (i // tiles_per_mod, 0, 0)),
                _layer_spec(w["g_attn_pre"], layer), _layer_spec(w["w_in_p"], layer)]
    args = [x2d, mod, w["g_attn_pre"], w["w_in_p"]]
    if latent:
        tiles_per_seq = DEC_SEQ // tm
        c64, s64, cm, sm = rope_tabs
        in_specs += [pl.BlockSpec((tm, 256), lambda i: (i % tiles_per_seq, 0)),
                     pl.BlockSpec((tm, 256), lambda i: (i % tiles_per_seq, 0)),
                     pl.BlockSpec((tm, 512), lambda i: (i % tiles_per_seq, 0)),
                     pl.BlockSpec((tm, 512), lambda i: (i % tiles_per_seq, 0))]
        args += [c64, s64, cm, sm]
    else:
        in_specs += [_layer_spec(w["pool_w"], layer), _layer_spec(w["pool_scale"], layer)]
        args += [w["pool_w"], w["pool_scale"]]
    assert latent or bool(carried) == (layer > 0)
    state_outs = [j for j, slot in enumerate(slots) if slot[3]]
    aliases = {}
    if carried:
        aliases = {len(args) + k: j for k, j in enumerate(state_outs)}
        in_specs += [pl.BlockSpec(memory_space=pl.ANY)] * len(carried)
        args += list(carried)
    return pl.pallas_call(
        functools.partial(_inproj_kernel, latent=latent, n_carried=len(carried)),
        grid=(t // tm,),
        in_specs=in_specs,
        out_specs=[(pl.BlockSpec((tm // SEQ, None, wd, SEQ), lambda i: (i, layer, 0, 0)) if carried
                    else pl.BlockSpec((tm // SEQ, DEPTH, wd, SEQ), lambda i: (i, 0, 0, 0))) if tr
                   else pl.BlockSpec((tm, wd), lambda i: (i, 0)) for _, wd, _, tr in slots],
        out_shape=[jax.ShapeDtypeStruct((t // SEQ, DEPTH, wd, SEQ) if tr else (t, wd), dt)
                   for _, wd, dt, tr in slots],
        input_output_aliases=aliases,
        compiler_params=_params(1),
        name="inproj_lat" if latent else "inproj_ctx",
    )(*args)


def _ctx_attn_kernel(q_ref, kt_ref, v_ref, o_ref, *, seqs):
    q = q_ref[...]
    kt = kt_ref[...].astype(BF16)
    v = v_ref[...]
    scores = []
    for s in range(seqs):
        rows = slice(s * SEQ, (s + 1) * SEQ)
        for h in range(NA_HEADS):
            sl = slice(h * HEAD_DIM, (h + 1) * HEAD_DIM)
            scores.append(_dot(q[rows, sl], kt[s, sl, :]))
    probs = [_softmax_blocks([sc]) for sc in scores]
    outs = []
    for s in range(seqs):
        rows = slice(s * SEQ, (s + 1) * SEQ)
        ys = []
        for h in range(NA_HEADS):
            sl = slice(h * HEAD_DIM, (h + 1) * HEAD_DIM)
            (e,), l = probs[s * NA_HEADS + h]
            ys.append(_dot(e.astype(BF16), v[rows, sl]) / l)
        outs.append(jnp.concatenate(ys, -1))
    o_ref[...] = (outs[0] if seqs == 1 else jnp.concatenate(outs, 0)).astype(o_ref.dtype)


def _ctx_attn_call(q, k, v, layer):
    seqs = CTX_SEQS
    t = q.shape[0]
    spec = pl.BlockSpec((seqs * SEQ, 256), lambda b: (b, 0))
    kt_spec = pl.BlockSpec((seqs, None, 256, SEQ), lambda b: (b, layer, 0, 0))
    return pl.pallas_call(
        functools.partial(_ctx_attn_kernel, seqs=seqs), grid=(t // SEQ // seqs,),
        in_specs=[spec, kt_spec, spec], out_specs=spec,
        out_shape=jax.ShapeDtypeStruct((t, 256), BF16), compiler_params=_params(1), name="ctx_attn",
    )(q, k, v)


def _gqa_operands(q, k, rows_q, rows_k):
    group = SWA_HEADS // SWA_KV_HEADS
    out = []
    for kv in range(SWA_KV_HEADS):
        qs = jnp.concatenate([q[rows_q, (kv * group + g) * HEAD_DIM:(kv * group + g + 1) * HEAD_DIM]
                              for g in range(group)], 0)
        out.append((qs, k[rows_k, kv * HEAD_DIM:(kv + 1) * HEAD_DIM]))
    return out


def _gqa_sink(sink_ref, kv, m):
    group = SWA_HEADS // SWA_KV_HEADS
    row = lax.broadcasted_iota(jnp.int32, (group * m, 1), 0)
    col = jnp.full((group * m, 1), sink_ref[kv * group + group - 1] * LOG2E, F32)
    for g in range(group - 2, -1, -1):
        col = jnp.where(row < (g + 1) * m, sink_ref[kv * group + g] * LOG2E, col)
    return col


def _ctx_swa_kernel(sink_ref, q_ref, kt_ref, v_ref, o_ref, *, seqs):
    q = q_ref[...]
    kt = kt_ref[...].astype(BF16)
    v = v_ref[...]
    group = SWA_HEADS // SWA_KV_HEADS
    scores = []
    for s in range(seqs):
        rows = slice(s * SEQ, (s + 1) * SEQ)
        for kv, (qs, _) in enumerate(_gqa_operands(q, q, rows, rows)):
            scores.append(_dot(qs, kt[s, kv * HEAD_DIM:(kv + 1) * HEAD_DIM, :]))
    probs = [_softmax_blocks([sc], sink=_gqa_sink(sink_ref, i % SWA_KV_HEADS, SEQ)) for i, sc in enumerate(scores)]
    outs = []
    for s in range(seqs):
        rows = slice(s * SEQ, (s + 1) * SEQ)
        ys = []
        for kv in range(SWA_KV_HEADS):
            (e,), l = probs[s * SWA_KV_HEADS + kv]
            y = _dot(e.astype(BF16), v[rows, kv * HEAD_DIM:(kv + 1) * HEAD_DIM]) / l
            ys += [y[g * SEQ:(g + 1) * SEQ] for g in range(group)]
        outs.append(jnp.concatenate(ys, -1))
    o_ref[...] = (outs[0] if seqs == 1 else jnp.concatenate(outs, 0)).astype(o_ref.dtype)


def _ctx_swa_call(sink, q, k, v, layer):
    seqs = CTX_SEQS
    t = q.shape[0]
    rows = seqs * SEQ
    return pl.pallas_call(
        functools.partial(_ctx_swa_kernel, seqs=seqs), grid=(t // rows,),
        in_specs=[pl.BlockSpec(memory_space=pltpu.SMEM),
                  pl.BlockSpec((rows, 256), lambda b: (b, 0)),
                  pl.BlockSpec((seqs, None, 128, SEQ), lambda b: (b, layer, 0, 0)),
                  pl.BlockSpec((rows, 128), lambda b: (b, 0))],
        out_specs=pl.BlockSpec((rows, 256), lambda b: (b, 0)),
        out_shape=jax.ShapeDtypeStruct((t, 256), BF16), compiler_params=_params(1), name="ctx_swa",
    )(sink, q, k, v)


def _mla_q(cq, qn_ref, wuq_ref):
    return _dot(_rms(cq, qn_ref[...], MLA_Q_RANK).astype(BF16), wuq_ref[...])


def _mla_kv(ckv, kr, kvn_ref, wuk_ref, wuv_ref, values_t=False):
    cn = _rms(ckv, kvn_ref[...]).astype(BF16)
    kcat = _dot(cn, wuk_ref[...]) + jnp.concatenate([kr] * MLA_HEADS, -1)
    return kcat, (_dot_nt(wuv_ref[...], cn) if values_t else _dot(cn, wuv_ref[...]))


def _ctx_mla_kernel(cq_ref, ckv_ref, kr_ref, qn_ref, wuq_ref, kvn_ref, wuk_ref, wuv_ref, o_ref, *, seqs):
    q = (_mla_q(cq_ref[...].astype(F32), qn_ref, wuq_ref) * (MLA_SCALE * LOG2E)).astype(BF16)
    kcat, v = _mla_kv(ckv_ref[...], kr_ref[...], kvn_ref, wuk_ref, wuv_ref)
    kcat = kcat.astype(BF16)
    v = v.astype(BF16)
    scores = []
    for s in range(seqs):
        rows = slice(s * SEQ, (s + 1) * SEQ)
        for h in range(MLA_HEADS):
            sl = slice(h * LANE, (h + 1) * LANE)
            scores.append(_dot_nt(q[rows, sl], kcat[rows, sl]))
    probs = [_softmax_blocks([sc]) for sc in scores]
    outs = []
    for s in range(seqs):
        rows = slice(s * SEQ, (s + 1) * SEQ)
        ys = []
        for h in range(MLA_HEADS):
            (e,), l = probs[s * MLA_HEADS + h]
            ys.append(_dot(e.astype(BF16), v[rows, h * MLA_V:(h + 1) * MLA_V]) / l)
        outs.append(jnp.concatenate(ys, -1))
    o_ref[...] = (outs[0] if seqs == 1 else jnp.concatenate(outs, 0)).astype(o_ref.dtype)


_MLA_W = ("mla_qn", "mla_wuq", "mla_kvn", "mla_wuk", "mla_wuv")
_MLA_W_LAT = _MLA_W[:-1] + ("mla_wuv_t",)


def _ctx_mla_call(cq, ckv, kr, w, layer):
    seqs = CTX_MLA_SEQS
    t = cq.shape[0]
    rows = seqs * SEQ
    return pl.pallas_call(
        functools.partial(_ctx_mla_kernel, seqs=seqs), grid=(t // rows,),
        in_specs=[pl.BlockSpec((rows, 256), lambda b: (b, 0)),
                  pl.BlockSpec((rows, 128), lambda b: (b, 0)),
                  pl.BlockSpec((rows, 128), lambda b: (b, 0))] + [_layer_spec(w[k], layer) for k in _MLA_W],
        out_specs=pl.BlockSpec((rows, 256), lambda b: (b, 0)),
        out_shape=jax.ShapeDtypeStruct((t, 256), BF16), compiler_params=_params(1), name="ctx_mla",
    )(cq, ckv, kr, *[w[k] for k in _MLA_W])


_POOL_PAD = 8


def _pool_mix(xs, w_ref, sc_ref):
    n = xs[0].shape[0]
    ne = n + 2 * _POOL_PAD
    lo, hi = _POOL_PAD, _POOL_PAD + n
    z = jnp.zeros((_POOL_PAD, POOL_WIDTH), F32)
    grp = lax.broadcasted_iota(jnp.int32, (n, POOL_WIDTH), 1) >> 6
    t = lax.broadcasted_iota(jnp.int32, (n, POOL_WIDTH), 0)
    half = jnp.where(grp == 0, 1, jnp.where(grp == 1, 2, jnp.where(grp == 2, 4, 8)))
    cnt = (jnp.minimum(t + half, n) - jnp.maximum(t - half, 0)).astype(F32)

    def pair(a, s):
        return pltpu.roll(a, s, axis=0) + pltpu.roll(a, ne - s, axis=0)

    out = []
    for x in xs:
        xz = jnp.concatenate([z, x, z], 0)
        s2 = xz + pltpu.roll(xz, 1, axis=0)
        s4 = pair(s2, 1)
        s8 = pair(s4, 2)
        s16 = pair(s8, 4)
        tot = jnp.where(grp == 0, s2[lo:hi],
                        jnp.where(grp == 1, s4[lo:hi], jnp.where(grp == 2, s8[lo:hi], s16[lo:hi])))
        dlt = (tot / cnt - x).astype(BF16)
        out.append(_dot(dlt, w_ref[...]) * sc_ref[...])
    return out


def _pool_kernel(x_ref, w_ref, sc_ref, o_ref, *, n):
    xs = [x_ref[s * n:(s + 1) * n, :].astype(F32) for s in range(x_ref.shape[0] // n)]
    for s, y in enumerate(_pool_mix(xs, w_ref, sc_ref)):
        o_ref[s * n:(s + 1) * n, :] = y.astype(o_ref.dtype)


def _pool_call(pd, w, layer, seq):
    t = pd.shape[0]
    rows = seq
    return pl.pallas_call(
        functools.partial(_pool_kernel, n=seq), grid=(t // rows,),
        in_specs=[pl.BlockSpec((rows, POOL_WIDTH), lambda b: (b, 0)),
                  _layer_spec(w["pool_w"], layer), _layer_spec(w["pool_scale"], layer)],
        out_specs=pl.BlockSpec((rows, POOL_WIDTH), lambda b: (b, 0)),
        out_shape=jax.ShapeDtypeStruct((t, POOL_WIDTH), BF16), compiler_params=_params(1), name="pool",
    )(pd, w["pool_w"], w["pool_scale"])


def _lat_na_kernel(q_ref, k_ref, v_ref, kc_ref, vc_ref, t2_ref, o_ref):
    n = pl.program_id(1)
    rows = DEC_SEQ // GRID_W
    q_rows = NA_Q_BLOCK // GRID_W
    row0 = jnp.clip(q_rows * n - NA_WIN_R // 2, 0, rows - NA_SPAN // GRID_W)
    start = pl.multiple_of(row0 * GRID_W, LANE)
    q = q_ref[...]
    k = k_ref[pl.ds(start, NA_SPAN), :]
    v = v_ref[pl.ds(start, NA_SPAN), :]
    kc = kc_ref[...].astype(BF16)
    vc = vc_ref[...].astype(BF16)

    pairs = NA_SPAN // LANE
    low_half = lax.broadcasted_iota(jnp.int32, (GRID_W, LANE), 1) < GRID_W
    entries, masks = [], []
    for a in range(q_rows):
        r = q_rows * n + a
        r_start = jnp.clip(r - NA_WIN_R // 2, 0, rows - NA_WIN_R)
        for p in range(pairs):
            rk = row0 + 2 * p
            ok0 = ((rk >= r_start) & (rk < r_start + NA_WIN_R)).astype(jnp.int32)
            ok1 = ((rk + 1 >= r_start) & (rk + 1 < r_start + NA_WIN_R)).astype(jnp.int32)
            entries.append(jnp.clip(rk - r + NA_WIN_R, 0, NA_DR))
            masks.append(jnp.where(low_half, ok0, ok1) > 0)

    heads = [slice(h * HEAD_DIM, (h + 1) * HEAD_DIM) for h in range(NA_HEADS)]
    raw = [(_dot_nt(q[:, sl], k[:, sl]), _dot_nt(q[:, sl], kc[:, sl])) for sl in heads]
    probs = []
    for h, (s_loc, s_ctx) in enumerate(raw):
        cols = []
        for p in range(pairs):
            blk = [jnp.where(masks[a * pairs + p], t2_ref[h, entries[a * pairs + p]], NEG_INF)
                   for a in range(q_rows)]
            cols.append(jnp.concatenate(blk, 0))
        probs.append(_softmax_blocks([s_loc + jnp.concatenate(cols, 1), s_ctx]))
    ys = []
    for sl, ((e_loc, e_ctx), l) in zip(heads, probs):
        y = _dot(e_loc.astype(BF16), v[:, sl]) + _dot(e_ctx.astype(BF16), vc[:, sl])
        ys.append(y / l)
    o_ref[...] = jnp.concatenate(ys, -1).astype(o_ref.dtype)


def _lat_na_call(q, k, v, cache_k, cache_v, t2, layer):
    nq = DEC_SEQ // NA_Q_BLOCK
    seq_spec = pl.BlockSpec((DEC_SEQ, 256), lambda b, n: (b, 0))
    cache_spec = pl.BlockSpec((None, None, PAST_LEN, 256), lambda b, n: (b, layer, 0, 0))
    return pl.pallas_call(
        _lat_na_kernel, grid=(DEC_BATCH, nq),
        in_specs=[pl.BlockSpec((NA_Q_BLOCK, 256), lambda b, n: (b * nq + n, 0)), seq_spec, seq_spec,
                  cache_spec, cache_spec, _layer_spec(t2, layer)],
        out_specs=pl.BlockSpec((NA_Q_BLOCK, 256), lambda b, n: (b * nq + n, 0)),
        out_shape=jax.ShapeDtypeStruct((DEC_BATCH * DEC_SEQ, 256), BF16),
        compiler_params=_params(2), name="lat_na",
    )(q, k, v, cache_k, cache_v, t2)


def _lat_swa_kernel(sink_ref, q_ref, k_ref, v_ref, kc_ref, vc_ref, o_ref):
    n = pl.program_id(1)
    start = jnp.clip(n * SWA_Q_BLOCK - SWA_WINDOW, 0, DEC_SEQ - SWA_SPAN)
    start = pl.multiple_of(start, LANE)
    q = q_ref[...]
    k = k_ref[pl.ds(start, SWA_SPAN), :]
    v = v_ref[pl.ds(start, SWA_SPAN), :]
    kc = kc_ref[...].astype(BF16)
    vc = vc_ref[...].astype(BF16)
    group = SWA_HEADS // SWA_KV_HEADS
    m = group * SWA_Q_BLOCK
    q_pos = n * SWA_Q_BLOCK + (lax.broadcasted_iota(jnp.int32, (m, SWA_SPAN), 0) & (SWA_Q_BLOCK - 1))
    k_pos = start + lax.broadcasted_iota(jnp.int32, (m, SWA_SPAN), 1)
    valid = jnp.abs(q_pos - k_pos) <= SWA_WINDOW
    everything = slice(None)
    raw = []
    for (qs, ks), (_, kcs) in zip(_gqa_operands(q, k, everything, everything),
                                  _gqa_operands(q, kc, everything, everything)):
        raw.append((_dot_nt(qs, ks), _dot_nt(qs, kcs)))
    probs = [_softmax_blocks([jnp.where(valid, s_loc, NEG_INF), s_ctx], sink=_gqa_sink(sink_ref, kv, SWA_Q_BLOCK))
             for kv, (s_loc, s_ctx) in enumerate(raw)]
    ys = []
    for kv, ((e_loc, e_ctx), l) in enumerate(probs):
        kvsl = slice(kv * HEAD_DIM, (kv + 1) * HEAD_DIM)
        y = (_dot(e_loc.astype(BF16), v[:, kvsl]) + _dot(e_ctx.astype(BF16), vc[:, kvsl])) / l
        ys += [y[g * SWA_Q_BLOCK:(g + 1) * SWA_Q_BLOCK] for g in range(group)]
    o_ref[...] = jnp.concatenate(ys, -1).astype(o_ref.dtype)


def _lat_swa_call(sink, q, k, v, cache_k, cache_v, layer):
    nq = DEC_SEQ // SWA_Q_BLOCK
    seq_spec = pl.BlockSpec((DEC_SEQ, 128), lambda b, n: (b, 0))
    cache_spec = pl.BlockSpec((None, None, PAST_LEN, 128), lambda b, n: (b, layer, 0, 0))
    return pl.pallas_call(
        _lat_swa_kernel, grid=(DEC_BATCH, nq),
        in_specs=[pl.BlockSpec(memory_space=pltpu.SMEM),
                  pl.BlockSpec((SWA_Q_BLOCK, 256), lambda b, n: (b * nq + n, 0)), seq_spec, seq_spec,
                  cache_spec, cache_spec],
        out_specs=pl.BlockSpec((SWA_Q_BLOCK, 256), lambda b, n: (b * nq + n, 0)),
        out_shape=jax.ShapeDtypeStruct((DEC_BATCH * DEC_SEQ, 256), BF16),
        compiler_params=_params(2), name="lat_swa",
    )(sink, q, k, v, cache_k, cache_v)


def _lat_mla_kernel(cq_ref, ckv_ref, kr_ref, cckv_ref, ckr_ref, cm_ref, sm_ref,
                    qn_ref, wuq_ref, kvn_ref, wuk_ref, wuvt_ref, o_ref, kcat_s, vt_s):
    @pl.when(pl.program_id(1) == 0)
    def _():
        kc, vc = _mla_kv(cckv_ref[...], ckr_ref[...], kvn_ref, wuk_ref, wuvt_ref, True)
        kcat_s[0:PAST_LEN, :] = kc.astype(BF16)
        vt_s[:, 0:PAST_LEN] = vc.astype(BF16)
        kl, vl = _mla_kv(ckv_ref[...].astype(F32), kr_ref[...].astype(F32), kvn_ref, wuk_ref, wuvt_ref, True)
        kcat_s[PAST_LEN:, :] = kl.astype(BF16)
        vt_s[:, PAST_LEN:] = vl.astype(BF16)

    q = _mla_q(cq_ref[...].astype(F32), qn_ref, wuq_ref)
    q = (_rope(q, cm_ref[...], sm_ref[...], 8) * (MLA_SCALE * LOG2E)).astype(BF16)
    scores = [_dot_nt(kcat_s[:, h * LANE:(h + 1) * LANE], q[:, h * LANE:(h + 1) * LANE]) for h in range(MLA_HEADS)]
    ys = []
    for h, st in enumerate(scores):
        e = jnp.exp2(st - jnp.max(st, 0, keepdims=True))
        l = jnp.sum(e, 0, keepdims=True)
        ys.append(_dot(vt_s[h * MLA_V:(h + 1) * MLA_V, :], e.astype(BF16)) / l)
    o_ref[...] = jnp.concatenate(ys, 0).T.astype(o_ref.dtype)


def _lat_mla_call(cq, ckv, kr, cache_ckv, cache_kr, cm, sm, w, layer):
    qb = MLA_Q_BLOCK
    nq = DEC_SEQ // qb
    seq_spec = pl.BlockSpec((DEC_SEQ, 128), lambda b, n: (b, 0))
    cache_spec = pl.BlockSpec((None, None, PAST_LEN, 128), lambda b, n: (b, layer, 0, 0))
    tab_spec = pl.BlockSpec((qb, 512), lambda b, n: (n, 0))
    return pl.pallas_call(
        _lat_mla_kernel, grid=(DEC_BATCH, nq),
        in_specs=[pl.BlockSpec((qb, 256), lambda b, n: (b * nq + n, 0)), seq_spec, seq_spec,
                  cache_spec, cache_spec, tab_spec, tab_spec] + [_layer_spec(w[k], layer) for k in _MLA_W_LAT],
        out_specs=pl.BlockSpec((qb, 256), lambda b, n: (b * nq + n, 0)),
        out_shape=jax.ShapeDtypeStruct((DEC_BATCH * DEC_SEQ, 256), BF16),
        scratch_shapes=[pltpu.VMEM((PAST_LEN + DEC_SEQ, MLA_HEADS * LANE), BF16),
                        pltpu.VMEM((MLA_HEADS * MLA_V, PAST_LEN + DEC_SEQ), BF16)],
        compiler_params=_params(2), name="lat_mla",
    )(cq, ckv, kr, cache_ckv, cache_kr, cm, sm, *[w[k] for k in _MLA_W_LAT])


_MERGE_PARTS = 4


def _merge_kernel(x_ref, mod_ref, gpre_ref, ya_ref, yb_ref, yc_ref, yd_ref,
                  wg_ref, bg_ref, wb_ref, wo_ref, gpost_ref, o_ref):
    tm = x_ref.shape[0] // _MERGE_PARTS
    for p in range(_MERGE_PARTS):
        rows = slice(p * tm, (p + 1) * tm)
        x = x_ref[rows, :]
        h = (_rms(x, gpre_ref[...]) * (1.0 + mod_ref[0, 1:2, :]) + mod_ref[0, 0:1, :]).astype(BF16)
        merged = None
        for k, y_ref in enumerate((ya_ref, yb_ref, yc_ref, yd_ref)):
            cols = slice(k * D_MODEL, (k + 1) * D_MODEL)
            gate = _sigmoid(_dot(h, wg_ref[:, cols]) + bg_ref[:, cols])
            term = gate * _dot(y_ref[rows, :], wb_ref[k])
            merged = term if merged is None else merged + term
        o = _dot(merged.astype(BF16), wo_ref[...])
        o_ref[rows, :] = x + mod_ref[0, 2:3, :] * _rms(o, gpost_ref[...])


_MERGE_W = ("w_gate", "b_gate", "w_branch", "w_out", "g_attn_post")


def _merge_call(x2d, mod, ys, w, layer):
    t = x2d.shape[0]
    tm = 1024
    tiles_per_mod = t // tm // mod.shape[0]
    tile = pl.BlockSpec((tm, D_MODEL), lambda i: (i, 0))
    ytile = pl.BlockSpec((tm, BRANCH_W), lambda i: (i, 0))
    return pl.pallas_call(
        _merge_kernel, grid=(t // tm,),
        in_specs=[tile, pl.BlockSpec((1, 6, D_MODEL), lambda i: (i // tiles_per_mod, 0, 0)),
                  _layer_spec(w["g_attn_pre"], layer), ytile, ytile, ytile, ytile]
                 + [_layer_spec(w[k], layer) for k in _MERGE_W],
        out_specs=tile,
        out_shape=jax.ShapeDtypeStruct((t, D_MODEL), F32),
        compiler_params=_params(1), name="merge",
    )(x2d, mod, w["g_attn_pre"], *ys, *[w[k] for k in _MERGE_W])


_GAP = 8


def _ffn_kernel(*refs, seqs, halo):
    if halo:
        x_ref, xp_ref, xn_ref = refs[:3]
        refs = refs[3:]
    else:
        x_ref = refs[0]
        refs = refs[1:]
    mod_ref, gpre_ref, wa_ref, wg_ref, ca_ref, cg_ref, wd_ref, gpost_ref, o_ref = refs
    tm = x_ref.shape[0]
    shift, scale, gate = mod_ref[0, 3:4, :], mod_ref[0, 4:5, :], mod_ref[0, 5:6, :]

    def pre(xx):
        return _rms(xx, gpre_ref[...]) * (1.0 + scale) + shift

    if halo:
        i = pl.program_id(0) % halo
        x = x_ref[...]
        hp = jnp.where(i == 0, 0.0, pre(xp_ref[...]))
        hn = jnp.where(i == halo - 1, 0.0, pre(xn_ref[...]))
        chains = [(x, jnp.concatenate([hp, pre(x), hn], 0).astype(BF16), _GAP)]
        seq_len = tm
    else:
        seq_len = tm // seqs
        chains = []
        for s in range(seqs):
            x = x_ref[s * seq_len:(s + 1) * seq_len, :]
            chains.append((x, pre(x).astype(BF16), 0))
    edge_row = lax.broadcasted_iota(jnp.int32, (_GAP, FF_CHUNK), 0)

    def conv(u, c_ref, cols):
        rows = u.shape[0]
        prev = pltpu.roll(u, 1, axis=0)
        nxt = pltpu.roll(u, rows - 1, axis=0)
        if not halo:
            prev = jnp.concatenate([jnp.where(edge_row == 0, 0.0, prev[:_GAP]), prev[_GAP:]], 0)
            nxt = jnp.concatenate([nxt[:rows - _GAP], jnp.where(edge_row == _GAP - 1, 0.0, nxt[rows - _GAP:])], 0)
        return prev * c_ref[0:1, cols] + u * c_ref[1:2, cols] + nxt * c_ref[2:3, cols]

    all_acts = []
    for _, hb, _ in chains:
        acts = []
        for c in range(D_FF_PAD // FF_CHUNK):
            cols = slice(c * FF_CHUNK, (c + 1) * FF_CHUNK)
            a = conv(_dot(hb, wa_ref[:, cols]), ca_ref, cols)
            g = conv(_dot(hb, wg_ref[:, cols]), cg_ref, cols)
            acts.append((g * _sigmoid(g) * a).astype(BF16))
        all_acts.append(jnp.concatenate(acts, 1))
    accs = [_dot(acts, wd_ref[...]) for acts in all_acts]
    for s, ((x, _, first), acc) in enumerate(zip(chains, accs)):
        ys = _rms(acc[first:first + seq_len], gpost_ref[...])
        o_ref[s * seq_len:(s + 1) * seq_len, :] = x + gate * ys


_FFN_W = ("g_ffn_pre", "ffn_wa", "ffn_wg", "ffn_ca", "ffn_cg", "ffn_wd", "g_ffn_post")


def _ffn_call(x2d, mod, w, layer, seq, tm):
    t = x2d.shape[0]
    tile = pl.BlockSpec((tm, D_MODEL), lambda i: (i, 0))
    in_specs = [tile]
    args = [x2d]
    if tm < seq:
        halo, seqs = seq // tm, 1
        r = tm // _GAP
        last = t // _GAP - 1
        in_specs += [pl.BlockSpec((_GAP, D_MODEL), lambda i: (jnp.maximum(i * r - 1, 0), 0)),
                     pl.BlockSpec((_GAP, D_MODEL), lambda i: (jnp.minimum((i + 1) * r, last), 0))]
        args += [x2d, x2d]
    else:
        halo, seqs = 0, tm // seq
    tiles_per_mod = t // tm // mod.shape[0]
    in_specs += [pl.BlockSpec((1, 6, D_MODEL), lambda i: (i // tiles_per_mod, 0, 0))]
    in_specs += [_layer_spec(w[k], layer) for k in _FFN_W]
    args += [mod] + [w[k] for k in _FFN_W]
    return pl.pallas_call(
        functools.partial(_ffn_kernel, seqs=seqs, halo=halo), grid=(t // tm,),
        in_specs=in_specs, out_specs=tile,
        out_shape=jax.ShapeDtypeStruct((t, D_MODEL), F32),
        compiler_params=_params(1), name="ffn",
    )(*args)


def _rope_tables():
    t = np.arange(DEC_SEQ)
    pos = (t // GRID_W, t % GRID_W)

    def tab(d):
        half = d // 4
        inv = np.float32(ROPE_BASE) ** (-np.arange(half, dtype=np.float32) / np.float32(half))
        cs, sn = [], []
        for p in pos:
            ang = p.astype(np.float32)[:, None] * inv[None, :]
            cs += [np.cos(ang), np.cos(ang)]
            sn += [-np.sin(ang), np.sin(ang)]
        return np.concatenate(cs, -1), np.concatenate(sn, -1)

    c64, s64 = tab(HEAD_DIM)
    c32, s32 = tab(MLA_ROPE)
    pad = LANE - MLA_NOPE - MLA_ROPE
    cm = np.concatenate([np.ones((DEC_SEQ, MLA_NOPE), np.float32), c32, np.ones((DEC_SEQ, pad), np.float32)], -1)
    sm = np.concatenate([np.zeros((DEC_SEQ, MLA_NOPE), np.float32), s32, np.zeros((DEC_SEQ, pad), np.float32)], -1)
    return tuple(jnp.asarray(np.tile(x, (1, 4)), F32) for x in (c64, s64, cm, sm))


def _na_table_kernel(rpb_ref, e_ref, ok_ref, o_ref):
    r = rpb_ref[...]
    r1 = r.astype(BF16)
    r2 = (r - r1.astype(F32)).astype(BF16)
    r3 = (r - r1.astype(F32) - r2.astype(F32)).astype(BF16)
    e = e_ref[...]
    t = _dot(r1, e) + _dot(r2, e) + _dot(r3, e)
    o_ref[...] = jnp.where(ok_ref[...] > 0, t * LOG2E, NEG_INF)


def _na_bias_tables(na_rpb):
    c = np.arange(GRID_W)[:, None]
    w = np.arange(GRID_W)[None, :]
    dc = (w - c + NA_WIN_C - 1).reshape(-1)
    onehot = (np.arange(LANE)[:, None] == dc[None, :]).astype(np.float32)
    c_start = np.clip(c - NA_WIN_C // 2, 0, GRID_W - NA_WIN_C)
    ok = ((w >= c_start) & (w < c_start + NA_WIN_C)).reshape(1, -1).astype(np.int32)
    rows = DEPTH * NA_HEADS * NA_DR
    rpb2 = jnp.pad(na_rpb.reshape(rows, NA_DC), ((0, LANE - rows), (0, LANE - NA_DC)))
    t = pl.pallas_call(
        _na_table_kernel, out_shape=jax.ShapeDtypeStruct((LANE, GRID_W * GRID_W), F32), name="na_table",
        compiler_params=pltpu.CompilerParams(vmem_limit_bytes=VMEM_LIMIT),
    )(rpb2, jnp.asarray(onehot, BF16), jnp.asarray(ok))
    t = t[:rows].reshape(DEPTH, NA_HEADS, NA_DR, GRID_W, GRID_W)
    t = jnp.pad(t, ((0, 0), (0, 0), (1, 1), (0, 0), (0, 0)), constant_values=NEG_INF)
    return jnp.concatenate([t[:, :, :-1], t[:, :, 1:]], -1)


def _pad_last(w, n):
    return jnp.pad(w, ((0, 0),) * (w.ndim - 1) + ((0, n - w.shape[-1]),))


def _prep_weights(g_attn_pre, g_attn_post, g_ffn_pre, g_ffn_post, w_in, w_gate, b_gate, mla_q_norm, mla_w_uq,
                  mla_kv_norm, mla_w_ukv, pool_w, pool_scale, w_branch, w_out, ffn_w_up, ffn_conv, ffn_w_down):
    q_scale = ATT_SCALE * LOG2E
    b0, c0, d0 = 768, 1120, 1632
    wt = w_in.transpose(0, 2, 1)

    def rows(lo, hi, before=0, after=0):
        return jnp.pad(wt[:, lo:hi], ((0, 0), (before, after), (0, 0)))

    w_in_p = jnp.concatenate([
        wt[:, d0:],
        wt[:, c0:c0 + 256] * q_scale,
        wt[:, 256:768],
        rows(b0 + 320, b0 + 352, _KR_LANE, LANE - _KR_LANE - MLA_ROPE),
        wt[:, c0 + 256:d0],
        wt[:, b0 + MLA_Q_RANK:b0 + MLA_Q_RANK + MLA_KV_RANK],
        wt[:, :256] * q_scale,
        rows(b0, b0 + MLA_Q_RANK, 0, 256 - MLA_Q_RANK)], 1).astype(BF16)

    wuq = mla_w_uq.reshape(DEPTH, MLA_Q_RANK, MLA_HEADS, MLA_NOPE + MLA_ROPE)
    wuq = jnp.pad(wuq, ((0, 0), (0, 256 - MLA_Q_RANK), (0, 0), (0, LANE - MLA_NOPE - MLA_ROPE)))
    wukv = mla_w_ukv.reshape(DEPTH, MLA_KV_RANK, MLA_HEADS, MLA_NOPE + MLA_V)
    wuk = _pad_last(wukv[..., :MLA_NOPE], LANE)
    wuv = wukv[..., MLA_NOPE:].reshape(DEPTH, MLA_KV_RANK, MLA_HEADS * MLA_V).astype(BF16)

    eye = np.eye(len(POOL_WINDOWS), dtype=np.float32)
    w_bd = (pool_w[:, :, :, None, :] * eye[None, :, None, :, None]).reshape(DEPTH, POOL_WIDTH, POOL_WIDTH)

    return dict(
        g_attn_pre=g_attn_pre[:, None, :], g_attn_post=g_attn_post[:, None, :],
        g_ffn_pre=g_ffn_pre[:, None, :], g_ffn_post=g_ffn_post[:, None, :],
        w_in_p=w_in_p,
        mla_qn=_pad_last(mla_q_norm[:, None, :], 256),
        mla_wuq=wuq.reshape(DEPTH, 256, MLA_HEADS * LANE).astype(BF16),
        mla_kvn=mla_kv_norm[:, None, :],
        mla_wuk=wuk.reshape(DEPTH, MLA_KV_RANK, MLA_HEADS * LANE).astype(BF16),
        mla_wuv=wuv, mla_wuv_t=wuv.transpose(0, 2, 1),
        pool_w=w_bd.astype(BF16), pool_scale=pool_scale[:, None, :],
        w_gate=w_gate.astype(BF16), b_gate=b_gate[:, None, :],
        w_branch=w_branch.astype(BF16), w_out=w_out.astype(BF16),
        ffn_wa=_pad_last(ffn_w_up[:, :, :D_FF], D_FF_PAD).astype(BF16),
        ffn_wg=_pad_last(ffn_w_up[:, :, D_FF:], D_FF_PAD).astype(BF16),
        ffn_ca=_pad_last(ffn_conv[:, :, :D_FF], D_FF_PAD), ffn_cg=_pad_last(ffn_conv[:, :, D_FF:], D_FF_PAD),
        ffn_wd=jnp.pad(ffn_w_down, ((0, 0), (0, D_FF_PAD - D_FF), (0, 0))).astype(BF16))


def kernel(x_prompt, x_sample, cache_na_k, cache_na_v, cache_mla_ckv, cache_mla_krope, cache_swa_k, cache_swa_v, c, c_ctx, w_mod, b_mod, g_attn_pre, g_attn_post, g_ffn_pre, g_ffn_post, w_in, w_gate, b_gate, na_rpb, mla_q_norm, mla_w_uq, mla_kv_norm, mla_w_ukv, swa_sink, pool_w, pool_scale, w_branch, w_out, ffn_w_up, ffn_conv, ffn_w_down):
    x_p = x_prompt.reshape(BATCH * SEQ, D_MODEL)
    x_s = x_sample.reshape(DEC_BATCH * DEC_SEQ, D_MODEL)

    cv = jnp.concatenate([c_ctx[None, :], c, jnp.zeros((8 - 1 - DEC_BATCH, D_MODEL), F32)], 0)
    mod = _mod_call(cv, w_mod, b_mod).reshape(DEPTH, 8, 6, D_MODEL)
    w = _prep_weights(g_attn_pre, g_attn_post, g_ffn_pre, g_ffn_post, w_in, w_gate, b_gate, mla_q_norm, mla_w_uq,
                      mla_kv_norm, mla_w_ukv, pool_w, pool_scale, w_branch, w_out, ffn_w_up, ffn_conv, ffn_w_down)
    rope_tabs = _rope_tables()
    na_t2 = _na_bias_tables(na_rpb)
    cache_na_k = cache_na_k.reshape(DEC_BATCH, DEPTH, PAST_LEN, 256)
    cache_na_v = cache_na_v.reshape(DEC_BATCH, DEPTH, PAST_LEN, 256)
    cache_swa_k = cache_swa_k.reshape(DEC_BATCH, DEPTH, PAST_LEN, 128)
    cache_swa_v = cache_swa_v.reshape(DEC_BATCH, DEPTH, PAST_LEN, 128)
    cache_kr = jnp.pad(cache_mla_krope, ((0, 0), (0, 0), (0, 0), (_KR_LANE, LANE - _KR_LANE - MLA_ROPE)))

    carried, ckvs = (), []
    for l in range(DEPTH):
        mod_p = mod[l, 0:1]
        qa, ka_t, va_t, va, cq, ckv, kr, kr_t, qc, kc_t, vc_t, vc, yd = _inproj_call(x_p, mod_p, w, l,
                                                                                      carried=carried)
        carried = (ka_t, va_t, kr_t, kc_t, vc_t)
        ckvs.append(ckv)
        ys = (_ctx_attn_call(qa, ka_t, va, l), _ctx_mla_call(cq, ckv, kr, w, l),
              _ctx_swa_call(swa_sink[l], qc, kc_t, vc, l), yd)
        x_p = _merge_call(x_p, mod_p, ys, w, l)
        x_p = _ffn_call(x_p, mod_p, w, l, SEQ, 2 * SEQ)

        mod_s = mod[l, 1:1 + DEC_BATCH]
        qa, ka, va, cq, ckv, kr, qc, kc, vc, pd = _inproj_call(x_s, mod_s, w, l, rope_tabs)
        ys = (_lat_na_call(qa, ka, va, cache_na_k, cache_na_v, na_t2, l),
              _lat_mla_call(cq, ckv, kr, cache_mla_ckv, cache_kr, rope_tabs[2], rope_tabs[3], w, l),
              _lat_swa_call(swa_sink[l], qc, kc, vc, cache_swa_k, cache_swa_v, l),
              _pool_call(pd, w, l, DEC_SEQ))
        x_s = _merge_call(x_s, mod_s, ys, w, l)
        x_s = _ffn_call(x_s, mod_s, w, l, DEC_SEQ, 1024)

    ka_t, va_t, kr_t, kc_t, vc_t = carried

    def heads_last(a, heads):
        return a.reshape(BATCH, DEPTH, heads, HEAD_DIM, SEQ).transpose(0, 1, 4, 2, 3)

    return (x_p.reshape(BATCH, SEQ, D_MODEL), x_s.reshape(DEC_BATCH, DEC_SEQ, D_MODEL),
            heads_last(ka_t, NA_HEADS), heads_last(va_t, NA_HEADS),
            jnp.stack([c.reshape(BATCH, SEQ, MLA_KV_RANK) for c in ckvs], 1),
            kr_t[:, :, _KR_LANE:_KR_LANE + MLA_ROPE, :].transpose(0, 1, 3, 2),
            heads_last(kc_t, SWA_KV_HEADS), heads_last(vc_t, SWA_KV_HEADS))
```

```python
import functools

import jax
import jax.numpy as jnp
import numpy as np
from jax import lax
from jax.experimental import pallas as pl
from jax.experimental.pallas import tpu as pltpu

F32 = jnp.float32
BF16 = jnp.bfloat16

D_MODEL = 1024
BATCH = 32
SEQ = 256
DEPTH = 2
DEC_BATCH = 2
DEC_SEQ = 2048
PAST_LEN = 256
GRID_W = 64
HEAD_DIM = 64
NA_HEADS = 4
NA_WIN_R = 8
NA_WIN_C = 16
MLA_HEADS = 4
MLA_NOPE = 64
MLA_ROPE = 32
MLA_V = 64
MLA_Q_RANK = 192
MLA_KV_RANK = 128
SWA_HEADS = 4
SWA_KV_HEADS = 2
SWA_WINDOW = 128
POOL_WINDOWS = (2, 4, 8, 16)
POOL_GROUP = 64
POOL_WIDTH = 256
BRANCH_W = 256
N_BRANCH = 4
D_FF = 2752
ROPE_BASE = 10000.0
EPS = 1e-6
NEG_INF = -1e30
ATT_SCALE = HEAD_DIM ** -0.5
MLA_SCALE = (MLA_NOPE + MLA_ROPE) ** -0.5
LOG2E = 1.4426950408889634

LANE = 128
D_FF_PAD = 2816
FF_CHUNK = 256
SWA_Q_BLOCK = 256
NA_Q_BLOCK = 256
NA_SPAN = 768
NA_DR = 2 * NA_WIN_R - 1
NA_DC = 2 * NA_WIN_C - 1
SWA_SPAN = SWA_Q_BLOCK + 2 * SWA_WINDOW
MLA_Q_BLOCK = 256
CTX_SEQS = 4
CTX_MLA_SEQS = 8
VMEM_LIMIT = 56 * 1024 * 1024

_PD, _QC, _KA, _VA, _KR, _KC, _VC, _CKV, _QA, _CQ = 0, 256, 512, 768, 1024, 1152, 1280, 1408, 1536, 1792
_KR_LANE = 64


def _dot(a, b):
    return jnp.dot(a, b, preferred_element_type=F32)


def _dot_nt(a, b):
    return lax.dot_general(a, b, (((1,), (1,)), ((), ())), preferred_element_type=F32)


def _sigmoid(x):
    return 1.0 / (1.0 + jnp.exp2(x * -LOG2E))


def _rms(x, g, n=None):
    n = x.shape[-1] if n is None else n
    ms = jnp.sum(x * x, -1, keepdims=True) * (1.0 / n)
    return x * lax.rsqrt(ms + EPS) * g


def _softmax_blocks(blocks, sink=None):
    m = None
    for s in blocks:
        mm = jnp.max(s, -1, keepdims=True)
        m = mm if m is None else jnp.maximum(m, mm)
    if sink is not None:
        m = jnp.maximum(m, sink)
    es = [jnp.exp2(s - m) for s in blocks]
    l = None
    for e in es:
        ll = jnp.sum(e, -1, keepdims=True)
        l = ll if l is None else l + ll
    if sink is not None:
        l = l + jnp.exp2(sink - m)
    return es, l


def _rope(x, cos, sin, q):
    w = x.shape[-1]
    lane = lax.broadcasted_iota(jnp.int32, x.shape, 1)
    up = pltpu.roll(x, w - q, axis=1)
    dn = pltpu.roll(x, q, axis=1)
    partner = jnp.where((lane & (2 * q - 1)) < q, up, dn)
    return x * cos + partner * sin


def _const_spec(shape):
    n = len(shape)
    return pl.BlockSpec(shape, lambda *_: (0,) * n, pipeline_mode=pl.Buffered(1))


def _layer_spec(arr, layer):
    n = arr.ndim - 1
    return pl.BlockSpec((None,) + arr.shape[1:], lambda *_: (layer,) + (0,) * n, pipeline_mode=pl.Buffered(1))


def _params(n_axes):
    return pltpu.CompilerParams(dimension_semantics=("arbitrary",) * n_axes, vmem_limit_bytes=VMEM_LIMIT)


def _mod_kernel(cv_ref, w_ref, b_ref, o_ref):
    cv = cv_ref[...]
    a = (cv * _sigmoid(cv)).astype(BF16)
    o_ref[0] = _dot(a, w_ref[0].astype(BF16)) + b_ref[0]


def _mod_call(cv, w_mod, b_mod):
    tn = 2048
    return pl.pallas_call(
        _mod_kernel,
        grid=(DEPTH, 6 * D_MODEL // tn),
        in_specs=[_const_spec((8, D_MODEL)),
                  pl.BlockSpec((1, D_MODEL, tn), lambda l, j: (l, 0, j)),
                  pl.BlockSpec((1, 1, tn), lambda l, j: (l, 0, j))],
        out_specs=pl.BlockSpec((1, 8, tn), lambda l, j: (l, 0, j)),
        out_shape=jax.ShapeDtypeStruct((DEPTH, 8, 6 * D_MODEL), F32),
        compiler_params=_params(2),
        name="mod",
    )(cv, w_mod, b_mod.reshape(DEPTH, 1, 6 * D_MODEL))


_IN_SLOTS = ((_QA, 256), (_KA, 256), (_VA, 256), (_CQ, 256), (_CKV, 128), (_KR, 128),
             (_QC, 256), (_KC, 128), (_VC, 128), (_PD, 256))
_CTX_SLOTS = ((_QA, 256, BF16, False), (_KA, 256, F32, True), (_VA, 256, F32, True), (_VA, 256, BF16, False),
              (_CQ, 256, BF16, False), (_CKV, 128, F32, False), (_KR, 128, F32, False), (_KR, 128, F32, True),
              (_QC, 256, BF16, False), (_KC, 128, F32, True), (_VC, 128, F32, True), (_VC, 128, BF16, False),
              (_PD, 256, BF16, False))
_LAT_SLOTS = tuple((off, wd, BF16, False) for off, wd in _IN_SLOTS)


def _inproj_kernel(*refs, latent, n_carried=0):
    if latent:
        x_ref, mod_ref, g_ref, w_ref, c64_ref, s64_ref, cm_ref, sm_ref = refs[:8]
        outs = refs[8:]
    else:
        x_ref, mod_ref, g_ref, w_ref, pw_ref, ps_ref = refs[:6]
        outs = refs[6 + n_carried:]
    x = x_ref[...]
    h = _rms(x, g_ref[...]) * (1.0 + mod_ref[0, 1:2, :]) + mod_ref[0, 0:1, :]
    p = _dot_nt(h.astype(BF16), w_ref[...])
    slots = sorted(zip(_LAT_SLOTS if latent else _CTX_SLOTS, outs), key=lambda so: so[0][0])
    for (off, wd, _, transposed), o_ref in slots:
        v = p[:, off:off + wd]
        if latent:
            if off == _QC:
                v = _rope(v, c64_ref[...], s64_ref[...], 16)
            elif off == _KC:
                v = _rope(v, c64_ref[:, :128], s64_ref[:, :128], 16)
            elif off == _KR:
                v = _rope(v, cm_ref[:, :128], sm_ref[:, :128], 8)
        if transposed:
            for b in range(o_ref.shape[0]):
                slab = v[b * SEQ:(b + 1) * SEQ].T.astype(o_ref.dtype)
                if n_carried:
                    o_ref[b] = slab
                else:
                    o_ref[b, 0] = slab
                    o_ref[b, 1:] = jnp.zeros((DEPTH - 1,) + slab.shape, o_ref.dtype)
        elif off == _PD and not latent:
            ys = _pool_mix([v[b * SEQ:(b + 1) * SEQ] for b in range(v.shape[0] // SEQ)], pw_ref, ps_ref)
            for b, y in enumerate(ys):
                o_ref[b * SEQ:(b + 1) * SEQ, :] = y.astype(o_ref.dtype)
        else:
            o_ref[...] = v.astype(o_ref.dtype)


def _inproj_call(x2d, mod, w, layer, rope_tabs=None, carried=()):
    t = x2d.shape[0]
    latent = rope_tabs is not None
    tm = 512 if latent else 1024
    slots = _LAT_SLOTS if latent else _CTX_SLOTS
    tiles_per_mod = t // tm // mod.shape[0]
    in_specs = [pl.BlockSpec((tm, D_MODEL), lambda i: (i, 0)),
                pl.BlockSpec((1, 6, D_MODEL), lambda i: (i // tiles_per_mod, 0, 0)),
                _layer_spec(w["g_attn_pre"], layer), _layer_spec(w["w_in_p"], layer)]
    args = [x2d, mod, w["g_attn_pre"], w["w_in_p"]]
    if latent:
        tiles_per_seq = DEC_SEQ // tm
        c64, s64, cm, sm = rope_tabs
        in_specs += [pl.BlockSpec((tm, 256), lambda i: (i % tiles_per_seq, 0)),
                     pl.BlockSpec((tm, 256), lambda i: (i % tiles_per_seq, 0)),
                     pl.BlockSpec((tm, 512), lambda i: (i % tiles_per_seq, 0)),
                     pl.BlockSpec((tm, 512), lambda i: (i % tiles_per_seq, 0))]
        args += [c64, s64, cm, sm]
    else:
        in_specs += [_layer_spec(w["pool_w"], layer), _layer_spec(w["pool_scale"], layer)]
        args += [w["pool_w"], w["pool_scale"]]
    assert latent or bool(carried) == (layer > 0)
    state_outs = [j for j, slot in enumerate(slots) if slot[3]]
    aliases = {}
    if carried:
        aliases = {len(args) + k: j for k, j in enumerate(state_outs)}
        in_specs += [pl.BlockSpec(memory_space=pl.ANY)] * len(carried)
        args += list(carried)
    return pl.pallas_call(
        functools.partial(_inproj_kernel, latent=latent, n_carried=len(carried)),
        grid=(t // tm,),
        in_specs=in_specs,
        out_specs=[(pl.BlockSpec((tm // SEQ, None, wd, SEQ), lambda i: (i, layer, 0, 0)) if carried
                    else pl.BlockSpec((tm // SEQ, DEPTH, wd, SEQ), lambda i: (i, 0, 0, 0))) if tr
                   else pl.BlockSpec((tm, wd), lambda i: (i, 0)) for _, wd, _, tr in slots],
        out_shape=[jax.ShapeDtypeStruct((t // SEQ, DEPTH, wd, SEQ) if tr else (t, wd), dt)
                   for _, wd, dt, tr in slots],
        input_output_aliases=aliases,
        compiler_params=_params(1),
        name="inproj_lat" if latent else "inproj_ctx",
    )(*args)


def _ctx_attn_kernel(q_ref, kt_ref, v_ref, o_ref, *, seqs):
    q = q_ref[...]
    kt = kt_ref[...].astype(BF16)
    v = v_ref[...]
    scores = []
    for s in range(seqs):
        rows = slice(s * SEQ, (s + 1) * SEQ)
        for h in range(NA_HEADS):
            sl = slice(h * HEAD_DIM, (h + 1) * HEAD_DIM)
            scores.append(_dot(q[rows, sl], kt[s, sl, :]))
    probs = [_softmax_blocks([sc]) for sc in scores]
    outs = []
    for s in range(seqs):
        rows = slice(s * SEQ, (s + 1) * SEQ)
        ys = []
        for h in range(NA_HEADS):
            sl = slice(h * HEAD_DIM, (h + 1) * HEAD_DIM)
            (e,), l = probs[s * NA_HEADS + h]
            ys.append(_dot(e.astype(BF16), v[rows, sl]) / l)
        outs.append(jnp.concatenate(ys, -1))
    o_ref[...] = (outs[0] if seqs == 1 else jnp.concatenate(outs, 0)).astype(o_ref.dtype)


def _ctx_attn_call(q, k, v, layer):
    seqs = CTX_SEQS
    t = q.shape[0]
    spec = pl.BlockSpec((seqs * SEQ, 256), lambda b: (b, 0))
    kt_spec = pl.BlockSpec((seqs, None, 256, SEQ), lambda b: (b, layer, 0, 0))
    return pl.pallas_call(
        functools.partial(_ctx_attn_kernel, seqs=seqs), grid=(t // SEQ // seqs,),
        in_specs=[spec, kt_spec, spec], out_specs=spec,
        out_shape=jax.ShapeDtypeStruct((t, 256), BF16), compiler_params=_params(1), name="ctx_attn",
    )(q, k, v)


def _gqa_operands(q, k, rows_q, rows_k):
    group = SWA_HEADS // SWA_KV_HEADS
    out = []
    for kv in range(SWA_KV_HEADS):
        qs = jnp.concatenate([q[rows_q, (kv * group + g) * HEAD_DIM:(kv * group + g + 1) * HEAD_DIM]
                              for g in range(group)], 0)
        out.append((qs, k[rows_k, kv * HEAD_DIM:(kv + 1) * HEAD_DIM]))
    return out


def _gqa_sink(sink_ref, kv, m):
    group = SWA_HEADS // SWA_KV_HEADS
    row = lax.broadcasted_iota(jnp.int32, (group * m, 1), 0)
    col = jnp.full((group * m, 1), sink_ref[kv * group + group - 1] * LOG2E, F32)
    for g in range(group - 2, -1, -1):
        col = jnp.where(row < (g + 1) * m, sink_ref[kv * group + g] * LOG2E, col)
    return col


def _ctx_swa_kernel(sink_ref, q_ref, kt_ref, v_ref, o_ref, *, seqs):
    q = q_ref[...]
    kt = kt_ref[...].astype(BF16)
    v = v_ref[...]
    group = SWA_HEADS // SWA_KV_HEADS
    scores = []
    for s in range(seqs):
        rows = slice(s * SEQ, (s + 1) * SEQ)
        for kv, (qs, _) in enumerate(_gqa_operands(q, q, rows, rows)):
            scores.append(_dot(qs, kt[s, kv * HEAD_DIM:(kv + 1) * HEAD_DIM, :]))
    probs = [_softmax_blocks([sc], sink=_gqa_sink(sink_ref, i % SWA_KV_HEADS, SEQ)) for i, sc in enumerate(scores)]
    outs = []
    for s in range(seqs):
        rows = slice(s * SEQ, (s + 1) * SEQ)
        ys = []
        for kv in range(SWA_KV_HEADS):
            (e,), l = probs[s * SWA_KV_HEADS + kv]
            y = _dot(e.astype(BF16), v[rows, kv * HEAD_DIM:(kv + 1) * HEAD_DIM]) / l
            ys += [y[g * SEQ:(g + 1) * SEQ] for g in range(group)]
        outs.append(jnp.concatenate(ys, -1))
    o_ref[...] = (outs[0] if seqs == 1 else jnp.concatenate(outs, 0)).astype(o_ref.dtype)


def _ctx_swa_call(sink, q, k, v, layer):
    seqs = CTX_SEQS
    t = q.shape[0]
    rows = seqs * SEQ
    return pl.pallas_call(
        functools.partial(_ctx_swa_kernel, seqs=seqs), grid=(t // rows,),
        in_specs=[pl.BlockSpec(memory_space=pltpu.SMEM),
                  pl.BlockSpec((rows, 256), lambda b: (b, 0)),
                  pl.BlockSpec((seqs, None, 128, SEQ), lambda b: (b, layer, 0, 0)),
                  pl.BlockSpec((rows, 128), lambda b: (b, 0))],
        out_specs=pl.BlockSpec((rows, 256), lambda b: (b, 0)),
        out_shape=jax.ShapeDtypeStruct((t, 256), BF16), compiler_params=_params(1), name="ctx_swa",
    )(sink, q, k, v)


def _mla_q(cq, qn_ref, wuq_ref):
    return _dot(_rms(cq, qn_ref[...], MLA_Q_RANK).astype(BF16), wuq_ref[...])


def _mla_kv(ckv, kr, kvn_ref, wuk_ref, wuv_ref, values_t=False):
    cn = _rms(ckv, kvn_ref[...]).astype(BF16)
    kcat = _dot(cn, wuk_ref[...]) + jnp.concatenate([kr] * MLA_HEADS, -1)
    return kcat, (_dot_nt(wuv_ref[...], cn) if values_t else _dot(cn, wuv_ref[...]))


def _ctx_mla_kernel(cq_ref, ckv_ref, kr_ref, qn_ref, wuq_ref, kvn_ref, wuk_ref, wuv_ref, o_ref, *, seqs):
    q = (_mla_q(cq_ref[...].astype(F32), qn_ref, wuq_ref) * (MLA_SCALE * LOG2E)).astype(BF16)
    kcat, v = _mla_kv(ckv_ref[...], kr_ref[...], kvn_ref, wuk_ref, wuv_ref)
    kcat = kcat.astype(BF16)
    v = v.astype(BF16)
    scores = []
    for s in range(seqs):
        rows = slice(s * SEQ, (s + 1) * SEQ)
        for h in range(MLA_HEADS):
            sl = slice(h * LANE, (h + 1) * LANE)
            scores.append(_dot_nt(q[rows, sl], kcat[rows, sl]))
    probs = [_softmax_blocks([sc]) for sc in scores]
    outs = []
    for s in range(seqs):
        rows = slice(s * SEQ, (s + 1) * SEQ)
        ys = []
        for h in range(MLA_HEADS):
            (e,), l = probs[s * MLA_HEADS + h]
            ys.append(_dot(e.astype(BF16), v[rows, h * MLA_V:(h + 1) * MLA_V]) / l)
        outs.append(jnp.concatenate(ys, -1))
    o_ref[...] = (outs[0] if seqs == 1 else jnp.concatenate(outs, 0)).astype(o_ref.dtype)


_MLA_W = ("mla_qn", "mla_wuq", "mla_kvn", "mla_wuk", "mla_wuv")
_MLA_W_LAT = _MLA_W[:-1] + ("mla_wuv_t",)


def _ctx_mla_call(cq, ckv, kr, w, layer):
    seqs = CTX_MLA_SEQS
    t = cq.shape[0]
    rows = seqs * SEQ
    return pl.pallas_call(
        functools.partial(_ctx_mla_kernel, seqs=seqs), grid=(t // rows,),
        in_specs=[pl.BlockSpec((rows, 256), lambda b: (b, 0)),
                  pl.BlockSpec((rows, 128), lambda b: (b, 0)),
                  pl.BlockSpec((rows, 128), lambda b: (b, 0))] + [_layer_spec(w[k], layer) for k in _MLA_W],
        out_specs=pl.BlockSpec((rows, 256), lambda b: (b, 0)),
        out_shape=jax.ShapeDtypeStruct((t, 256), BF16), compiler_params=_params(1), name="ctx_mla",
    )(cq, ckv, kr, *[w[k] for k in _MLA_W])


_POOL_PAD = 8


def _pool_mix(xs, w_ref, sc_ref):
    n = xs[0].shape[0]
    ne = n + 2 * _POOL_PAD
    lo, hi = _POOL_PAD, _POOL_PAD + n
    z = jnp.zeros((_POOL_PAD, POOL_WIDTH), F32)
    grp = lax.broadcasted_iota(jnp.int32, (n, POOL_WIDTH), 1) >> 6
    t = lax.broadcasted_iota(jnp.int32, (n, POOL_WIDTH), 0)
    half = jnp.where(grp == 0, 1, jnp.where(grp == 1, 2, jnp.where(grp == 2, 4, 8)))
    cnt = (jnp.minimum(t + half, n) - jnp.maximum(t - half, 0)).astype(F32)

    def pair(a, s):
        return pltpu.roll(a, s, axis=0) + pltpu.roll(a, ne - s, axis=0)

    out = []
    for x in xs:
        xz = jnp.concatenate([z, x, z], 0)
        s2 = xz + pltpu.roll(xz, 1, axis=0)
        s4 = pair(s2, 1)
        s8 = pair(s4, 2)
        s16 = pair(s8, 4)
        tot = jnp.where(grp == 0, s2[lo:hi],
                        jnp.where(grp == 1, s4[lo:hi], jnp.where(grp == 2, s8[lo:hi], s16[lo:hi])))
        dlt = (tot / cnt - x).astype(BF16)
        out.append(_dot(dlt, w_ref[...]) * sc_ref[...])
    return out


def _pool_kernel(x_ref, w_ref, sc_ref, o_ref, *, n):
    xs = [x_ref[s * n:(s + 1) * n, :].astype(F32) for s in range(x_ref.shape[0] // n)]
    for s, y in enumerate(_pool_mix(xs, w_ref, sc_ref)):
        o_ref[s * n:(s + 1) * n, :] = y.astype(o_ref.dtype)


def _pool_call(pd, w, layer, seq):
    t = pd.shape[0]
    rows = seq
    return pl.pallas_call(
        functools.partial(_pool_kernel, n=seq), grid=(t // rows,),
        in_specs=[pl.BlockSpec((rows, POOL_WIDTH), lambda b: (b, 0)),
                  _layer_spec(w["pool_w"], layer), _layer_spec(w["pool_scale"], layer)],
        out_specs=pl.BlockSpec((rows, POOL_WIDTH), lambda b: (b, 0)),
        out_shape=jax.ShapeDtypeStruct((t, POOL_WIDTH), BF16), compiler_params=_params(1), name="pool",
    )(pd, w["pool_w"], w["pool_scale"])


def _lat_na_kernel(q_ref, k_ref, v_ref, kc_ref, vc_ref, t2_ref, o_ref):
    n = pl.program_id(1)
    rows = DEC_SEQ // GRID_W
    q_rows = NA_Q_BLOCK // GRID_W
    row0 = jnp.clip(q_rows * n - NA_WIN_R // 2, 0, rows - NA_SPAN // GRID_W)
    start = pl.multiple_of(row0 * GRID_W, LANE)
    q = q_ref[...]
    k = k_ref[pl.ds(start, NA_SPAN), :]
    v = v_ref[pl.ds(start, NA_SPAN), :]
    kc = kc_ref[...].astype(BF16)
    vc = vc_ref[...].astype(BF16)

    pairs = NA_SPAN // LANE
    low_half = lax.broadcasted_iota(jnp.int32, (GRID_W, LANE), 1) < GRID_W
    entries, masks = [], []
    for a in range(q_rows):
        r = q_rows * n + a
        r_start = jnp.clip(r - NA_WIN_R // 2, 0, rows - NA_WIN_R)
        for p in range(pairs):
            rk = row0 + 2 * p
            ok0 = ((rk >= r_start) & (rk < r_start + NA_WIN_R)).astype(jnp.int32)
            ok1 = ((rk + 1 >= r_start) & (rk + 1 < r_start + NA_WIN_R)).astype(jnp.int32)
            entries.append(jnp.clip(rk - r + NA_WIN_R, 0, NA_DR))
            masks.append(jnp.where(low_half, ok0, ok1) > 0)

    heads = [slice(h * HEAD_DIM, (h + 1) * HEAD_DIM) for h in range(NA_HEADS)]
    raw = [(_dot_nt(q[:, sl], k[:, sl]), _dot_nt(q[:, sl], kc[:, sl])) for sl in heads]
    probs = []
    for h, (s_loc, s_ctx) in enumerate(raw):
        cols = []
        for p in range(pairs):
            blk = [jnp.where(masks[a * pairs + p], t2_ref[h, entries[a * pairs + p]], NEG_INF)
                   for a in range(q_rows)]
            cols.append(jnp.concatenate(blk, 0))
        probs.append(_softmax_blocks([s_loc + jnp.concatenate(cols, 1), s_ctx]))
    ys = []
    for sl, ((e_loc, e_ctx), l) in zip(heads, probs):
        y = _dot(e_loc.astype(BF16), v[:, sl]) + _dot(e_ctx.astype(BF16), vc[:, sl])
        ys.append(y / l)
    o_ref[...] = jnp.concatenate(ys, -1).astype(o_ref.dtype)


def _lat_na_call(q, k, v, cache_k, cache_v, t2, layer):
    nq = DEC_SEQ // NA_Q_BLOCK
    seq_spec = pl.BlockSpec((DEC_SEQ, 256), lambda b, n: (b, 0))
    cache_spec = pl.BlockSpec((None, None, PAST_LEN, 256), lambda b, n: (b, layer, 0, 0))
    return pl.pallas_call(
        _lat_na_kernel, grid=(DEC_BATCH, nq),
        in_specs=[pl.BlockSpec((NA_Q_BLOCK, 256), lambda b, n: (b * nq + n, 0)), seq_spec, seq_spec,
                  cache_spec, cache_spec, _layer_spec(t2, layer)],
        out_specs=pl.BlockSpec((NA_Q_BLOCK, 256), lambda b, n: (b * nq + n, 0)),
        out_shape=jax.ShapeDtypeStruct((DEC_BATCH * DEC_SEQ, 256), BF16),
        compiler_params=_params(2), name="lat_na",
    )(q, k, v, cache_k, cache_v, t2)


def _lat_swa_kernel(sink_ref, q_ref, k_ref, v_ref, kc_ref, vc_ref, o_ref):
    n = pl.program_id(1)
    start = jnp.clip(n * SWA_Q_BLOCK - SWA_WINDOW, 0, DEC_SEQ - SWA_SPAN)
    start = pl.multiple_of(start, LANE)
    q = q_ref[...]
    k = k_ref[pl.ds(start, SWA_SPAN), :]
    v = v_ref[pl.ds(start, SWA_SPAN), :]
    kc = kc_ref[...].astype(BF16)
    vc = vc_ref[...].astype(BF16)
    group = SWA_HEADS // SWA_KV_HEADS
    m = group * SWA_Q_BLOCK
    q_pos = n * SWA_Q_BLOCK + (lax.broadcasted_iota(jnp.int32, (m, SWA_SPAN), 0) & (SWA_Q_BLOCK - 1))
    k_pos = start + lax.broadcasted_iota(jnp.int32, (m, SWA_SPAN), 1)
    valid = jnp.abs(q_pos - k_pos) <= SWA_WINDOW
    everything = slice(None)
    raw = []
    for (qs, ks), (_, kcs) in zip(_gqa_operands(q, k, everything, everything),
                                  _gqa_operands(q, kc, everything, everything)):
        raw.append((_dot_nt(qs, ks), _dot_nt(qs, kcs)))
    probs = [_softmax_blocks([jnp.where(valid, s_loc, NEG_INF), s_ctx], sink=_gqa_sink(sink_ref, kv, SWA_Q_BLOCK))
             for kv, (s_loc, s_ctx) in enumerate(raw)]
    ys = []
    for kv, ((e_loc, e_ctx), l) in enumerate(probs):
        kvsl = slice(kv * HEAD_DIM, (kv + 1) * HEAD_DIM)
        y = (_dot(e_loc.astype(BF16), v[:, kvsl]) + _dot(e_ctx.astype(BF16), vc[:, kvsl])) / l
        ys += [y[g * SWA_Q_BLOCK:(g + 1) * SWA_Q_BLOCK] for g in range(group)]
    o_ref[...] = jnp.concatenate(ys, -1).astype(o_ref.dtype)


def _lat_swa_call(sink, q, k, v, cache_k, cache_v, layer):
    nq = DEC_SEQ // SWA_Q_BLOCK
    seq_spec = pl.BlockSpec((DEC_SEQ, 128), lambda b, n: (b, 0))
    cache_spec = pl.BlockSpec((None, None, PAST_LEN, 128), lambda b, n: (b, layer, 0, 0))
    return pl.pallas_call(
        _lat_swa_kernel, grid=(DEC_BATCH, nq),
        in_specs=[pl.BlockSpec(memory_space=pltpu.SMEM),
                  pl.BlockSpec((SWA_Q_BLOCK, 256), lambda b, n: (b * nq + n, 0)), seq_spec, seq_spec,
                  cache_spec, cache_spec],
        out_specs=pl.BlockSpec((SWA_Q_BLOCK, 256), lambda b, n: (b * nq + n, 0)),
        out_shape=jax.ShapeDtypeStruct((DEC_BATCH * DEC_SEQ, 256), BF16),
        compiler_params=_params(2), name="lat_swa",
    )(sink, q, k, v, cache_k, cache_v)


def _lat_mla_kernel(cq_ref, ckv_ref, kr_ref, cckv_ref, ckr_ref, cm_ref, sm_ref,
                    qn_ref, wuq_ref, kvn_ref, wuk_ref, wuvt_ref, o_ref, kcat_s, vt_s):
    @pl.when(pl.program_id(1) == 0)
    def _():
        kc, vc = _mla_kv(cckv_ref[...], ckr_ref[...], kvn_ref, wuk_ref, wuvt_ref, True)
        kcat_s[0:PAST_LEN, :] = kc.astype(BF16)
        vt_s[:, 0:PAST_LEN] = vc.astype(BF16)
        kl, vl = _mla_kv(ckv_ref[...].astype(F32), kr_ref[...].astype(F32), kvn_ref, wuk_ref, wuvt_ref, True)
        kcat_s[PAST_LEN:, :] = kl.astype(BF16)
        vt_s[:, PAST_LEN:] = vl.astype(BF16)

    q = _mla_q(cq_ref[...].astype(F32), qn_ref, wuq_ref)
    q = (_rope(q, cm_ref[...], sm_ref[...], 8) * (MLA_SCALE * LOG2E)).astype(BF16)
    scores = [_dot_nt(kcat_s[:, h * LANE:(h + 1) * LANE], q[:, h * LANE:(h + 1) * LANE]) for h in range(MLA_HEADS)]
    ys = []
    for h, st in enumerate(scores):
        e = jnp.exp2(st - jnp.max(st, 0, keepdims=True))
        l = jnp.sum(e, 0, keepdims=True)
        ys.append(_dot(vt_s[h * MLA_V:(h + 1) * MLA_V, :], e.astype(BF16)) / l)
    o_ref[...] = jnp.concatenate(ys, 0).T.astype(o_ref.dtype)


def _lat_mla_call(cq, ckv, kr, cache_ckv, cache_kr, cm, sm, w, layer):
    qb = MLA_Q_BLOCK
    nq = DEC_SEQ // qb
    seq_spec = pl.BlockSpec((DEC_SEQ, 128), lambda b, n: (b, 0))
    cache_spec = pl.BlockSpec((None, None, PAST_LEN, 128), lambda b, n: (b, layer, 0, 0))
    tab_spec = pl.BlockSpec((qb, 512), lambda b, n: (n, 0))
    return pl.pallas_call(
        _lat_mla_kernel, grid=(DEC_BATCH, nq),
        in_specs=[pl.BlockSpec((qb, 256), lambda b, n: (b * nq + n, 0)), seq_spec, seq_spec,
                  cache_spec, cache_spec, tab_spec, tab_spec] + [_layer_spec(w[k], layer) for k in _MLA_W_LAT],
        out_specs=pl.BlockSpec((qb, 256), lambda b, n: (b * nq + n, 0)),
        out_shape=jax.ShapeDtypeStruct((DEC_BATCH * DEC_SEQ, 256), BF16),
        scratch_shapes=[pltpu.VMEM((PAST_LEN + DEC_SEQ, MLA_HEADS * LANE), BF16),
                        pltpu.VMEM((MLA_HEADS * MLA_V, PAST_LEN + DEC_SEQ), BF16)],
        compiler_params=_params(2), name="lat_mla",
    )(cq, ckv, kr, cache_ckv, cache_kr, cm, sm, *[w[k] for k in _MLA_W_LAT])


_MERGE_PARTS = 4


def _merge_kernel(x_ref, mod_ref, gpre_ref, ya_ref, yb_ref, yc_ref, yd_ref,
                  wg_ref, bg_ref, wb_ref, wo_ref, gpost_ref, o_ref):
    tm = x_ref.shape[0] // _MERGE_PARTS
    for p in range(_MERGE_PARTS):
        rows = slice(p * tm, (p + 1) * tm)
        x = x_ref[rows, :]
        h = (_rms(x, gpre_ref[...]) * (1.0 + mod_ref[0, 1:2, :]) + mod_ref[0, 0:1, :]).astype(BF16)
        merged = None
        for k, y_ref in enumerate((ya_ref, yb_ref, yc_ref, yd_ref)):
            cols = slice(k * D_MODEL, (k + 1) * D_MODEL)
            gate = _sigmoid(_dot(h, wg_ref[:, cols]) + bg_ref[:, cols])
            term = gate * _dot(y_ref[rows, :], wb_ref[k])
            merged = term if merged is None else merged + term
        o = _dot(merged.astype(BF16), wo_ref[...])
        o_ref[rows, :] = x + mod_ref[0, 2:3, :] * _rms(o, gpost_ref[...])


_MERGE_W = ("w_gate", "b_gate", "w_branch", "w_out", "g_attn_post")


def _merge_call(x2d, mod, ys, w, layer):
    t = x2d.shape[0]
    tm = 1024
    tiles_per_mod = t // tm // mod.shape[0]
    tile = pl.BlockSpec((tm, D_MODEL), lambda i: (i, 0))
    ytile = pl.BlockSpec((tm, BRANCH_W), lambda i: (i, 0))
    return pl.pallas_call(
        _merge_kernel, grid=(t // tm,),
        in_specs=[tile, pl.BlockSpec((1, 6, D_MODEL), lambda i: (i // tiles_per_mod, 0, 0)),
                  _layer_spec(w["g_attn_pre"], layer), ytile, ytile, ytile, ytile]
                 + [_layer_spec(w[k], layer) for k in _MERGE_W],
        out_specs=tile,
        out_shape=jax.ShapeDtypeStruct((t, D_MODEL), F32),
        compiler_params=_params(1), name="merge",
    )(x2d, mod, w["g_attn_pre"], *ys, *[w[k] for k in _MERGE_W])


_GAP = 8


def _ffn_kernel(*refs, seqs, halo):
    if halo:
        x_ref, xp_ref, xn_ref = refs[:3]
        refs = refs[3:]
    else:
        x_ref = refs[0]
        refs = refs[1:]
    mod_ref, gpre_ref, wa_ref, wg_ref, ca_ref, cg_ref, wd_ref, gpost_ref, o_ref = refs
    tm = x_ref.shape[0]
    shift, scale, gate = mod_ref[0, 3:4, :], mod_ref[0, 4:5, :], mod_ref[0, 5:6, :]

    def pre(xx):
        return _rms(xx, gpre_ref[...]) * (1.0 + scale) + shift

    if halo:
        i = pl.program_id(0) % halo
        x = x_ref[...]
        hp = jnp.where(i == 0, 0.0, pre(xp_ref[...]))
        hn = jnp.where(i == halo - 1, 0.0, pre(xn_ref[...]))
        chains = [(x, jnp.concatenate([hp, pre(x), hn], 0).astype(BF16), _GAP)]
        seq_len = tm
    else:
        seq_len = tm // seqs
        chains = []
        for s in range(seqs):
            x = x_ref[s * seq_len:(s + 1) * seq_len, :]
            chains.append((x, pre(x).astype(BF16), 0))
    edge_row = lax.broadcasted_iota(jnp.int32, (_GAP, FF_CHUNK), 0)

    def conv(u, c_ref, cols):
        rows = u.shape[0]
        prev = pltpu.roll(u, 1, axis=0)
        nxt = pltpu.roll(u, rows - 1, axis=0)
        if not halo:
            prev = jnp.concatenate([jnp.where(edge_row == 0, 0.0, prev[:_GAP]), prev[_GAP:]], 0)
            nxt = jnp.concatenate([nxt[:rows - _GAP], jnp.where(edge_row == _GAP - 1, 0.0, nxt[rows - _GAP:])], 0)
        return prev * c_ref[0:1, cols] + u * c_ref[1:2, cols] + nxt * c_ref[2:3, cols]

    all_acts = []
    for _, hb, _ in chains:
        acts = []
        for c in range(D_FF_PAD // FF_CHUNK):
            cols = slice(c * FF_CHUNK, (c + 1) * FF_CHUNK)
            a = conv(_dot(hb, wa_ref[:, cols]), ca_ref, cols)
            g = conv(_dot(hb, wg_ref[:, cols]), cg_ref, cols)
            acts.append((g * _sigmoid(g) * a).astype(BF16))
        all_acts.append(jnp.concatenate(acts, 1))
    accs = [_dot(acts, wd_ref[...]) for acts in all_acts]
    for s, ((x, _, first), acc) in enumerate(zip(chains, accs)):
        ys = _rms(acc[first:first + seq_len], gpost_ref[...])
        o_ref[s * seq_len:(s + 1) * seq_len, :] = x + gate * ys


_FFN_W = ("g_ffn_pre", "ffn_wa", "ffn_wg", "ffn_ca", "ffn_cg", "ffn_wd", "g_ffn_post")


def _ffn_call(x2d, mod, w, layer, seq, tm):
    t = x2d.shape[0]
    tile = pl.BlockSpec((tm, D_MODEL), lambda i: (i, 0))
    in_specs = [tile]
    args = [x2d]
    if tm < seq:
        halo, seqs = seq // tm, 1
        r = tm // _GAP
        last = t // _GAP - 1
        in_specs += [pl.BlockSpec((_GAP, D_MODEL), lambda i: (jnp.maximum(i * r - 1, 0), 0)),
                     pl.BlockSpec((_GAP, D_MODEL), lambda i: (jnp.minimum((i + 1) * r, last), 0))]
        args += [x2d, x2d]
    else:
        halo, seqs = 0, tm // seq
    tiles_per_mod = t // tm // mod.shape[0]
    in_specs += [pl.BlockSpec((1, 6, D_MODEL), lambda i: (i // tiles_per_mod, 0, 0))]
    in_specs += [_layer_spec(w[k], layer) for k in _FFN_W]
    args += [mod] + [w[k] for k in _FFN_W]
    return pl.pallas_call(
        functools.partial(_ffn_kernel, seqs=seqs, halo=halo), grid=(t // tm,),
        in_specs=in_specs, out_specs=tile,
        out_shape=jax.ShapeDtypeStruct((t, D_MODEL), F32),
        compiler_params=_params(1), name="ffn",
    )(*args)


def _rope_tables():
    t = np.arange(DEC_SEQ)
    pos = (t // GRID_W, t % GRID_W)

    def tab(d):
        half = d // 4
        inv = np.float32(ROPE_BASE) ** (-np.arange(half, dtype=np.float32) / np.float32(half))
        cs, sn = [], []
        for p in pos:
            ang = p.astype(np.float32)[:, None] * inv[None, :]
            cs += [np.cos(ang), np.cos(ang)]
            sn += [-np.sin(ang), np.sin(ang)]
        return np.concatenate(cs, -1), np.concatenate(sn, -1)

    c64, s64 = tab(HEAD_DIM)
    c32, s32 = tab(MLA_ROPE)
    pad = LANE - MLA_NOPE - MLA_ROPE
    cm = np.concatenate([np.ones((DEC_SEQ, MLA_NOPE), np.float32), c32, np.ones((DEC_SEQ, pad), np.float32)], -1)
    sm = np.concatenate([np.zeros((DEC_SEQ, MLA_NOPE), np.float32), s32, np.zeros((DEC_SEQ, pad), np.float32)], -1)
    return tuple(jnp.asarray(np.tile(x, (1, 4)), F32) for x in (c64, s64, cm, sm))


def _na_table_kernel(rpb_ref, e_ref, ok_ref, o_ref):
    r = rpb_ref[...]
    r1 = r.astype(BF16)
    r2 = (r - r1.astype(F32)).astype(BF16)
    r3 = (r - r1.astype(F32) - r2.astype(F32)).astype(BF16)
    e = e_ref[...]
    t = _dot(r1, e) + _dot(r2, e) + _dot(r3, e)
    o_ref[...] = jnp.where(ok_ref[...] > 0, t * LOG2E, NEG_INF)


def _na_bias_tables(na_rpb):
    c = np.arange(GRID_W)[:, None]
    w = np.arange(GRID_W)[None, :]
    dc = (w - c + NA_WIN_C - 1).reshape(-1)
    onehot = (np.arange(LANE)[:, None] == dc[None, :]).astype(np.float32)
    c_start = np.clip(c - NA_WIN_C // 2, 0, GRID_W - NA_WIN_C)
    ok = ((w >= c_start) & (w < c_start + NA_WIN_C)).reshape(1, -1).astype(np.int32)
    rows = DEPTH * NA_HEADS * NA_DR
    rpb2 = jnp.pad(na_rpb.reshape(rows, NA_DC), ((0, LANE - rows), (0, LANE - NA_DC)))
    t = pl.pallas_call(
        _na_table_kernel, out_shape=jax.ShapeDtypeStruct((LANE, GRID_W * GRID_W), F32), name="na_table",
        compiler_params=pltpu.CompilerParams(vmem_limit_bytes=VMEM_LIMIT),
    )(rpb2, jnp.asarray(onehot, BF16), jnp.asarray(ok))
    t = t[:rows].reshape(DEPTH, NA_HEADS, NA_DR, GRID_W, GRID_W)
    t = jnp.pad(t, ((0, 0), (0, 0), (1, 1), (0, 0), (0, 0)), constant_values=NEG_INF)
    return jnp.concatenate([t[:, :, :-1], t[:, :, 1:]], -1)


def _pad_last(w, n):
    return jnp.pad(w, ((0, 0),) * (w.ndim - 1) + ((0, n - w.shape[-1]),))


def _prep_weights(g_attn_pre, g_attn_post, g_ffn_pre, g_ffn_post, w_in, w_gate, b_gate, mla_q_norm, mla_w_uq,
                  mla_kv_norm, mla_w_ukv, pool_w, pool_scale, w_branch, w_out, ffn_w_up, ffn_conv, ffn_w_down):
    q_scale = ATT_SCALE * LOG2E
    b0, c0, d0 = 768, 1120, 1632
    wt = w_in.transpose(0, 2, 1)

    def rows(lo, hi, before=0, after=0):
        return jnp.pad(wt[:, lo:hi], ((0, 0), (before, after), (0, 0)))

    w_in_p = jnp.concatenate([
        wt[:, d0:],
        wt[:, c0:c0 + 256] * q_scale,
        wt[:, 256:768],
        rows(b0 + 320, b0 + 352, _KR_LANE, LANE - _KR_LANE - MLA_ROPE),
        wt[:, c0 + 256:d0],
        wt[:, b0 + MLA_Q_RANK:b0 + MLA_Q_RANK + MLA_KV_RANK],
        wt[:, :256] * q_scale,
        rows(b0, b0 + MLA_Q_RANK, 0, 256 - MLA_Q_RANK)], 1).astype(BF16)

    wuq = mla_w_uq.reshape(DEPTH, MLA_Q_RANK, MLA_HEADS, MLA_NOPE + MLA_ROPE)
    wuq = jnp.pad(wuq, ((0, 0), (0, 256 - MLA_Q_RANK), (0, 0), (0, LANE - MLA_NOPE - MLA_ROPE)))
    wukv = mla_w_ukv.reshape(DEPTH, MLA_KV_RANK, MLA_HEADS, MLA_NOPE + MLA_V)
    wuk = _pad_last(wukv[..., :MLA_NOPE], LANE)
    wuv = wukv[..., MLA_NOPE:].reshape(DEPTH, MLA_KV_RANK, MLA_HEADS * MLA_V).astype(BF16)

    eye = np.eye(len(POOL_WINDOWS), dtype=np.float32)
    w_bd = (pool_w[:, :, :, None, :] * eye[None, :, None, :, None]).reshape(DEPTH, POOL_WIDTH, POOL_WIDTH)

    return dict(
        g_attn_pre=g_attn_pre[:, None, :], g_attn_post=g_attn_post[:, None, :],
        g_ffn_pre=g_ffn_pre[:, None, :], g_ffn_post=g_ffn_post[:, None, :],
        w_in_p=w_in_p,
        mla_qn=_pad_last(mla_q_norm[:, None, :], 256),
        mla_wuq=wuq.reshape(DEPTH, 256, MLA_HEADS * LANE).astype(BF16),
        mla_kvn=mla_kv_norm[:, None, :],
        mla_wuk=wuk.reshape(DEPTH, MLA_KV_RANK, MLA_HEADS * LANE).astype(BF16),
        mla_wuv=wuv, mla_wuv_t=wuv.transpose(0, 2, 1),
        pool_w=w_bd.astype(BF16), pool_scale=pool_scale[:, None, :],
        w_gate=w_gate.astype(BF16), b_gate=b_gate[:, None, :],
        w_branch=w_branch.astype(BF16), w_out=w_out.astype(BF16),
        ffn_wa=_pad_last(ffn_w_up[:, :, :D_FF], D_FF_PAD).astype(BF16),
        ffn_wg=_pad_last(ffn_w_up[:, :, D_FF:], D_FF_PAD).astype(BF16),
        ffn_ca=_pad_last(ffn_conv[:, :, :D_FF], D_FF_PAD), ffn_cg=_pad_last(ffn_conv[:, :, D_FF:], D_FF_PAD),
        ffn_wd=jnp.pad(ffn_w_down, ((0, 0), (0, D_FF_PAD - D_FF), (0, 0))).astype(BF16))


def kernel(x_prompt, x_sample, cache_na_k, cache_na_v, cache_mla_ckv, cache_mla_krope, cache_swa_k, cache_swa_v, c, c_ctx, w_mod, b_mod, g_attn_pre, g_attn_post, g_ffn_pre, g_ffn_post, w_in, w_gate, b_gate, na_rpb, mla_q_norm, mla_w_uq, mla_kv_norm, mla_w_ukv, swa_sink, pool_w, pool_scale, w_branch, w_out, ffn_w_up, ffn_conv, ffn_w_down):
    x_p = x_prompt.reshape(BATCH * SEQ, D_MODEL)
    x_s = x_sample.reshape(DEC_BATCH * DEC_SEQ, D_MODEL)

    cv = jnp.concatenate([c_ctx[None, :], c, jnp.zeros((8 - 1 - DEC_BATCH, D_MODEL), F32)], 0)
    mod = _mod_call(cv, w_mod, b_mod).reshape(DEPTH, 8, 6, D_MODEL)
    w = _prep_weights(g_attn_pre, g_attn_post, g_ffn_pre, g_ffn_post, w_in, w_gate, b_gate, mla_q_norm, mla_w_uq,
                      mla_kv_norm, mla_w_ukv, pool_w, pool_scale, w_branch, w_out, ffn_w_up, ffn_conv, ffn_w_down)
    rope_tabs = _rope_tables()
    na_t2 = _na_bias_tables(na_rpb)
    cache_na_k = cache_na_k.reshape(DEC_BATCH, DEPTH, PAST_LEN, 256)
    cache_na_v = cache_na_v.reshape(DEC_BATCH, DEPTH, PAST_LEN, 256)
    cache_swa_k = cache_swa_k.reshape(DEC_BATCH, DEPTH, PAST_LEN, 128)
    cache_swa_v = cache_swa_v.reshape(DEC_BATCH, DEPTH, PAST_LEN, 128)
    cache_kr = jnp.pad(cache_mla_krope, ((0, 0), (0, 0), (0, 0), (_KR_LANE, LANE - _KR_LANE - MLA_ROPE)))

    carried, ckvs = (), []
    for l in range(DEPTH):
        mod_p = mod[l, 0:1]
        qa, ka_t, va_t, va, cq, ckv, kr, kr_t, qc, kc_t, vc_t, vc, yd = _inproj_call(x_p, mod_p, w, l,
                                                                                      carried=carried)
        carried = (ka_t, va_t, kr_t, kc_t, vc_t)
        ckvs.append(ckv)
        ys = (_ctx_attn_call(qa, ka_t, va, l), _ctx_mla_call(cq, ckv, kr, w, l),
              _ctx_swa_call(swa_sink[l], qc, kc_t, vc, l), yd)
        x_p = _merge_call(x_p, mod_p, ys, w, l)
        x_p = _ffn_call(x_p, mod_p, w, l, SEQ, 4 * SEQ)

        mod_s = mod[l, 1:1 + DEC_BATCH]
        qa, ka, va, cq, ckv, kr, qc, kc, vc, pd = _inproj_call(x_s, mod_s, w, l, rope_tabs)
        ys = (_lat_na_call(qa, ka, va, cache_na_k, cache_na_v, na_t2, l),
              _lat_mla_call(cq, ckv, kr, cache_mla_ckv, cache_kr, rope_tabs[2], rope_tabs[3], w, l),
              _lat_swa_call(swa_sink[l], qc, kc, vc, cache_swa_k, cache_swa_v, l),
              _pool_call(pd, w, l, DEC_SEQ))
        x_s = _merge_call(x_s, mod_s, ys, w, l)
        x_s = _ffn_call(x_s, mod_s, w, l, DEC_SEQ, 1024)

    ka_t, va_t, kr_t, kc_t, vc_t = carried

    def heads_last(a, heads):
        return a.reshape(BATCH, DEPTH, heads, HEAD_DIM, SEQ).transpose(0, 1, 4, 2, 3)

    return (x_p.reshape(BATCH, SEQ, D_MODEL), x_s.reshape(DEC_BATCH, DEC_SEQ, D_MODEL),
            heads_last(ka_t, NA_HEADS), heads_last(va_t, NA_HEADS),
            jnp.stack([c.reshape(BATCH, SEQ, MLA_KV_RANK) for c in ckvs], 1),
            kr_t[:, :, _KR_LANE:_KR_LANE + MLA_ROPE, :].transpose(0, 1, 3, 2),
            heads_last(kc_t, SWA_KV_HEADS), heads_last(vc_t, SWA_KV_HEADS))
```

```python
import functools

import jax
import jax.numpy as jnp
import numpy as np
from jax import lax
from jax.experimental import pallas as pl
from jax.experimental.pallas import tpu as pltpu

F32 = jnp.float32
BF16 = jnp.bfloat16

D_MODEL = 1024
BATCH = 32
SEQ = 256
DEPTH = 2
DEC_BATCH = 2
DEC_SEQ = 2048
PAST_LEN = 256
GRID_W = 64
HEAD_DIM = 64
NA_HEADS = 4
NA_WIN_R = 8
NA_WIN_C = 16
MLA_HEADS = 4
MLA_NOPE = 64
MLA_ROPE = 32
MLA_V = 64
MLA_Q_RANK = 192
MLA_KV_RANK = 128
SWA_HEADS = 4
SWA_KV_HEADS = 2
SWA_WINDOW = 128
POOL_WINDOWS = (2, 4, 8, 16)
POOL_GROUP = 64
POOL_WIDTH = 256
BRANCH_W = 256
N_BRANCH = 4
D_FF = 2752
ROPE_BASE = 10000.0
EPS = 1e-6
NEG_INF = -1e30
ATT_SCALE = HEAD_DIM ** -0.5
MLA_SCALE = (MLA_NOPE + MLA_ROPE) ** -0.5
LOG2E = 1.4426950408889634

LANE = 128
D_FF_PAD = 2816
FF_CHUNK = 256
SWA_Q_BLOCK = 256
NA_Q_BLOCK = 256
NA_SPAN = 768
NA_DR = 2 * NA_WIN_R - 1
NA_DC = 2 * NA_WIN_C - 1
SWA_SPAN = SWA_Q_BLOCK + 2 * SWA_WINDOW
MLA_Q_BLOCK = 256
CTX_SEQS = 4
CTX_MLA_SEQS = 8
VMEM_LIMIT = 56 * 1024 * 1024

_PD, _QC, _KA, _VA, _KR, _KC, _VC, _CKV, _QA, _CQ = 0, 256, 512, 768, 1024, 1152, 1280, 1408, 1536, 1792
_KR_LANE = 64


def _dot(a, b):
    return jnp.dot(a, b, preferred_element_type=F32)


def _dot_nt(a, b):
    return lax.dot_general(a, b, (((1,), (1,)), ((), ())), preferred_element_type=F32)


def _sigmoid(x):
    return 1.0 / (1.0 + jnp.exp2(x * -LOG2E))


def _rms(x, g, n=None):
    n = x.shape[-1] if n is None else n
    ms = jnp.sum(x * x, -1, keepdims=True) * (1.0 / n)
    return x * lax.rsqrt(ms + EPS) * g


def _softmax_blocks(blocks, sink=None):
    m = None
    for s in blocks:
        mm = jnp.max(s, -1, keepdims=True)
        m = mm if m is None else jnp.maximum(m, mm)
    if sink is not None:
        m = jnp.maximum(m, sink)
    es = [jnp.exp2(s - m) for s in blocks]
    l = None
    for e in es:
        ll = jnp.sum(e, -1, keepdims=True)
        l = ll if l is None else l + ll
    if sink is not None:
        l = l + jnp.exp2(sink - m)
    return es, l


def _rope(x, cos, sin, q):
    w = x.shape[-1]
    lane = lax.broadcasted_iota(jnp.int32, x.shape, 1)
    up = pltpu.roll(x, w - q, axis=1)
    dn = pltpu.roll(x, q, axis=1)
    partner = jnp.where((lane & (2 * q - 1)) < q, up, dn)
    return x * cos + partner * sin


def _const_spec(shape):
    n = len(shape)
    return pl.BlockSpec(shape, lambda *_: (0,) * n, pipeline_mode=pl.Buffered(1))


def _layer_spec(arr, layer):
    n = arr.ndim - 1
    return pl.BlockSpec((None,) + arr.shape[1:], lambda *_: (layer,) + (0,) * n, pipeline_mode=pl.Buffered(1))


def _params(n_axes):
    return pltpu.CompilerParams(dimension_semantics=("arbitrary",) * n_axes, vmem_limit_bytes=VMEM_LIMIT)


def _mod_kernel(cv_ref, w_ref, b_ref, o_ref):
    cv = cv_ref[...]
    a = (cv * _sigmoid(cv)).astype(BF16)
    o_ref[0] = _dot(a, w_ref[0].astype(BF16)) + b_ref[0]


def _mod_call(cv, w_mod, b_mod):
    tn = 2048
    return pl.pallas_call(
        _mod_kernel,
        grid=(DEPTH, 6 * D_MODEL // tn),
        in_specs=[_const_spec((8, D_MODEL)),
                  pl.BlockSpec((1, D_MODEL, tn), lambda l, j: (l, 0, j)),
                  pl.BlockSpec((1, 1, tn), lambda l, j: (l, 0, j))],
        out_specs=pl.BlockSpec((1, 8, tn), lambda l, j: (l, 0, j)),
        out_shape=jax.ShapeDtypeStruct((DEPTH, 8, 6 * D_MODEL), F32),
        compiler_params=_params(2),
        name="mod",
    )(cv, w_mod, b_mod.reshape(DEPTH, 1, 6 * D_MODEL))


_IN_SLOTS = ((_QA, 256), (_KA, 256), (_VA, 256), (_CQ, 256), (_CKV, 128), (_KR, 128),
             (_QC, 256), (_KC, 128), (_VC, 128), (_PD, 256))
_CTX_SLOTS = ((_QA, 256, BF16, False), (_KA, 256, F32, True), (_VA, 256, F32, True), (_VA, 256, BF16, False),
              (_CQ, 256, BF16, False), (_CKV, 128, F32, False), (_KR, 128, F32, False), (_KR, 128, F32, True),
              (_QC, 256, BF16, False), (_KC, 128, F32, True), (_VC, 128, F32, True), (_VC, 128, BF16, False),
              (_PD, 256, BF16, False))
_LAT_SLOTS = tuple((off, wd, BF16, False) for off, wd in _IN_SLOTS)


def _inproj_kernel(*refs, latent, n_carried=0):
    if latent:
        x_ref, mod_ref, g_ref, w_ref, c64_ref, s64_ref, cm_ref, sm_ref = refs[:8]
        outs = refs[8:]
    else:
        x_ref, mod_ref, g_ref, w_ref, pw_ref, ps_ref = refs[:6]
        outs = refs[6 + n_carried:]
    x = x_ref[...]
    h = _rms(x, g_ref[...]) * (1.0 + mod_ref[0, 1:2, :]) + mod_ref[0, 0:1, :]
    p = _dot_nt(h.astype(BF16), w_ref[...])
    slots = sorted(zip(_LAT_SLOTS if latent else _CTX_SLOTS, outs), key=lambda so: so[0][0])
    for (off, wd, _, transposed), o_ref in slots:
        v = p[:, off:off + wd]
        if latent:
            if off == _QC:
                v = _rope(v, c64_ref[...], s64_ref[...], 16)
            elif off == _KC:
                v = _rope(v, c64_ref[:, :128], s64_ref[:, :128], 16)
            elif off == _KR:
                v = _rope(v, cm_ref[:, :128], sm_ref[:, :128], 8)
        if transposed:
            for b in range(o_ref.shape[0]):
                slab = v[b * SEQ:(b + 1) * SEQ].T.astype(o_ref.dtype)
                if n_carried:
                    o_ref[b] = slab
                else:
                    o_ref[b, 0] = slab
                    o_ref[b, 1:] = jnp.zeros((DEPTH - 1,) + slab.shape, o_ref.dtype)
        elif off == _PD and not latent:
            ys = _pool_mix([v[b * SEQ:(b + 1) * SEQ] for b in range(v.shape[0] // SEQ)], pw_ref, ps_ref)
            for b, y in enumerate(ys):
                o_ref[b * SEQ:(b + 1) * SEQ, :] = y.astype(o_ref.dtype)
        else:
            o_ref[...] = v.astype(o_ref.dtype)


def _inproj_call(x2d, mod, w, layer, rope_tabs=None, carried=()):
    t = x2d.shape[0]
    latent = rope_tabs is not None
    tm = 512 if latent else 1024
    slots = _LAT_SLOTS if latent else _CTX_SLOTS
    tiles_per_mod = t // tm // mod.shape[0]
    in_specs = [pl.BlockSpec((tm, D_MODEL), lambda i: (i, 0)),
                pl.BlockSpec((1, 6, D_MODEL), lambda i: (i // tiles_per_mod, 0, 0)),
                _layer_spec(w["g_attn_pre"], layer), _layer_spec(w["w_in_p"], layer)]
    args = [x2d, mod, w["g_attn_pre"], w["w_in_p"]]
    if latent:
        tiles_per_seq = DEC_SEQ // tm
        c64, s64, cm, sm = rope_tabs
        in_specs += [pl.BlockSpec((tm, 256), lambda i: (i % tiles_per_seq, 0)),
                     pl.BlockSpec((tm, 256), lambda i: (i % tiles_per_seq, 0)),
                     pl.BlockSpec((tm, 512), lambda i: (i % tiles_per_seq, 0)),
                     pl.BlockSpec((tm, 512), lambda i: (i % tiles_per_seq, 0))]
        args += [c64, s64, cm, sm]
    else:
        in_specs += [_layer_spec(w["pool_w"], layer), _layer_spec(w["pool_scale"], layer)]
        args += [w["pool_w"], w["pool_scale"]]
    assert latent or bool(carried) == (layer > 0)
    state_outs = [j for j, slot in enumerate(slots) if slot[3]]
    aliases = {}
    if carried:
        aliases = {len(args) + k: j for k, j in enumerate(state_outs)}
        in_specs += [pl.BlockSpec(memory_space=pl.ANY)] * len(carried)
        args += list(carried)
    return pl.pallas_call(
        functools.partial(_inproj_kernel, latent=latent, n_carried=len(carried)),
        grid=(t // tm,),
        in_specs=in_specs,
        out_specs=[(pl.BlockSpec((tm // SEQ, None, wd, SEQ), lambda i: (i, layer, 0, 0)) if carried
                    else pl.BlockSpec((tm // SEQ, DEPTH, wd, SEQ), lambda i: (i, 0, 0, 0))) if tr
                   else pl.BlockSpec((tm, wd), lambda i: (i, 0)) for _, wd, _, tr in slots],
        out_shape=[jax.ShapeDtypeStruct((t // SEQ, DEPTH, wd, SEQ) if tr else (t, wd), dt)
                   for _, wd, dt, tr in slots],
        input_output_aliases=aliases,
        compiler_params=_params(1),
        name="inproj_lat" if latent else "inproj_ctx",
    )(*args)


def _ctx_attn_kernel(q_ref, kt_ref, v_ref, o_ref, *, seqs):
    q = q_ref[...]
    kt = kt_ref[...].astype(BF16)
    v = v_ref[...]
    scores = []
    for s in range(seqs):
        rows = slice(s * SEQ, (s + 1) * SEQ)
        for h in range(NA_HEADS):
            sl = slice(h * HEAD_DIM, (h + 1) * HEAD_DIM)
            scores.append(_dot(q[rows, sl], kt[s, sl, :]))
    probs = [_softmax_blocks([sc]) for sc in scores]
    outs = []
    for s in range(seqs):
        rows = slice(s * SEQ, (s + 1) * SEQ)
        ys = []
        for h in range(NA_HEADS):
            sl = slice(h * HEAD_DIM, (h + 1) * HEAD_DIM)
            (e,), l = probs[s * NA_HEADS + h]
            ys.append(_dot(e.astype(BF16), v[rows, sl]) / l)
        outs.append(jnp.concatenate(ys, -1))
    o_ref[...] = (outs[0] if seqs == 1 else jnp.concatenate(outs, 0)).astype(o_ref.dtype)


def _ctx_attn_call(q, k, v, layer):
    seqs = CTX_SEQS
    t = q.shape[0]
    spec = pl.BlockSpec((seqs * SEQ, 256), lambda b: (b, 0))
    kt_spec = pl.BlockSpec((seqs, None, 256, SEQ), lambda b: (b, layer, 0, 0))
    return pl.pallas_call(
        functools.partial(_ctx_attn_kernel, seqs=seqs), grid=(t // SEQ // seqs,),
        in_specs=[spec, kt_spec, spec], out_specs=spec,
        out_shape=jax.ShapeDtypeStruct((t, 256), BF16), compiler_params=_params(1), name="ctx_attn",
    )(q, k, v)


def _gqa_operands(q, k, rows_q, rows_k):
    group = SWA_HEADS // SWA_KV_HEADS
    out = []
    for kv in range(SWA_KV_HEADS):
        qs = jnp.concatenate([q[rows_q, (kv * group + g) * HEAD_DIM:(kv * group + g + 1) * HEAD_DIM]
                              for g in range(group)], 0)
        out.append((qs, k[rows_k, kv * HEAD_DIM:(kv + 1) * HEAD_DIM]))
    return out


def _gqa_sink(sink_ref, kv, m):
    group = SWA_HEADS // SWA_KV_HEADS
    row = lax.broadcasted_iota(jnp.int32, (group * m, 1), 0)
    col = jnp.full((group * m, 1), sink_ref[kv * group + group - 1] * LOG2E, F32)
    for g in range(group - 2, -1, -1):
        col = jnp.where(row < (g + 1) * m, sink_ref[kv * group + g] * LOG2E, col)
    return col


def _ctx_swa_kernel(sink_ref, q_ref, kt_ref, v_ref, o_ref, *, seqs):
    q = q_ref[...]
    kt = kt_ref[...].astype(BF16)
    v = v_ref[...]
    group = SWA_HEADS // SWA_KV_HEADS
    scores = []
    for s in range(seqs):
        rows = slice(s * SEQ, (s + 1) * SEQ)
        for kv, (qs, _) in enumerate(_gqa_operands(q, q, rows, rows)):
            scores.append(_dot(qs, kt[s, kv * HEAD_DIM:(kv + 1) * HEAD_DIM, :]))
    probs = [_softmax_blocks([sc], sink=_gqa_sink(sink_ref, i % SWA_KV_HEADS, SEQ)) for i, sc in enumerate(scores)]
    outs = []
    for s in range(seqs):
        rows = slice(s * SEQ, (s + 1) * SEQ)
        ys = []
        for kv in range(SWA_KV_HEADS):
            (e,), l = probs[s * SWA_KV_HEADS + kv]
            y = _dot(e.astype(BF16), v[rows, kv * HEAD_DIM:(kv + 1) * HEAD_DIM]) / l
            ys += [y[g * SEQ:(g + 1) * SEQ] for g in range(group)]
        outs.append(jnp.concatenate(ys, -1))
    o_ref[...] = (outs[0] if seqs == 1 else jnp.concatenate(outs, 0)).astype(o_ref.dtype)


def _ctx_swa_call(sink, q, k, v, layer):
    seqs = CTX_SEQS
    t = q.shape[0]
    rows = seqs * SEQ
    return pl.pallas_call(
        functools.partial(_ctx_swa_kernel, seqs=seqs), grid=(t // rows,),
        in_specs=[pl.BlockSpec(memory_space=pltpu.SMEM),
                  pl.BlockSpec((rows, 256), lambda b: (b, 0)),
                  pl.BlockSpec((seqs, None, 128, SEQ), lambda b: (b, layer, 0, 0)),
                  pl.BlockSpec((rows, 128), lambda b: (b, 0))],
        out_specs=pl.BlockSpec((rows, 256), lambda b: (b, 0)),
        out_shape=jax.ShapeDtypeStruct((t, 256), BF16), compiler_params=_params(1), name="ctx_swa",
    )(sink, q, k, v)


def _mla_q(cq, qn_ref, wuq_ref):
    return _dot(_rms(cq, qn_ref[...], MLA_Q_RANK).astype(BF16), wuq_ref[...])


def _mla_kv(ckv, kr, kvn_ref, wuk_ref, wuv_ref, values_t=False):
    cn = _rms(ckv, kvn_ref[...]).astype(BF16)
    kcat = _dot(cn, wuk_ref[...]) + jnp.concatenate([kr] * MLA_HEADS, -1)
    return kcat, (_dot_nt(wuv_ref[...], cn) if values_t else _dot(cn, wuv_ref[...]))


def _ctx_mla_kernel(cq_ref, ckv_ref, kr_ref, qn_ref, wuq_ref, kvn_ref, wuk_ref, wuv_ref, o_ref, *, seqs):
    q = (_mla_q(cq_ref[...].astype(F32), qn_ref, wuq_ref) * (MLA_SCALE * LOG2E)).astype(BF16)
    kcat, v = _mla_kv(ckv_ref[...], kr_ref[...], kvn_ref, wuk_ref, wuv_ref)
    kcat = kcat.astype(BF16)
    v = v.astype(BF16)
    scores = []
    for s in range(seqs):
        rows = slice(s * SEQ, (s + 1) * SEQ)
        for h in range(MLA_HEADS):
            sl = slice(h * LANE, (h + 1) * LANE)
            scores.append(_dot_nt(q[rows, sl], kcat[rows, sl]))
    probs = [_softmax_blocks([sc]) for sc in scores]
    outs = []
    for s in range(seqs):
        rows = slice(s * SEQ, (s + 1) * SEQ)
        ys = []
        for h in range(MLA_HEADS):
            (e,), l = probs[s * MLA_HEADS + h]
            ys.append(_dot(e.astype(BF16), v[rows, h * MLA_V:(h + 1) * MLA_V]) / l)
        outs.append(jnp.concatenate(ys, -1))
    o_ref[...] = (outs[0] if seqs == 1 else jnp.concatenate(outs, 0)).astype(o_ref.dtype)


_MLA_W = ("mla_qn", "mla_wuq", "mla_kvn", "mla_wuk", "mla_wuv")
_MLA_W_LAT = _MLA_W[:-1] + ("mla_wuv_t",)


def _ctx_mla_call(cq, ckv, kr, w, layer):
    seqs = CTX_MLA_SEQS
    t = cq.shape[0]
    rows = seqs * SEQ
    return pl.pallas_call(
        functools.partial(_ctx_mla_kernel, seqs=seqs), grid=(t // rows,),
        in_specs=[pl.BlockSpec((rows, 256), lambda b: (b, 0)),
                  pl.BlockSpec((rows, 128), lambda b: (b, 0)),
                  pl.BlockSpec((rows, 128), lambda b: (b, 0))] + [_layer_spec(w[k], layer) for k in _MLA_W],
        out_specs=pl.BlockSpec((rows, 256), lambda b: (b, 0)),
        out_shape=jax.ShapeDtypeStruct((t, 256), BF16), compiler_params=_params(1), name="ctx_mla",
    )(cq, ckv, kr, *[w[k] for k in _MLA_W])


_POOL_PAD = 8


def _pool_mix(xs, w_ref, sc_ref):
    n = xs[0].shape[0]
    ne = n + 2 * _POOL_PAD
    lo, hi = _POOL_PAD, _POOL_PAD + n
    z = jnp.zeros((_POOL_PAD, POOL_WIDTH), F32)
    grp = lax.broadcasted_iota(jnp.int32, (n, POOL_WIDTH), 1) >> 6
    t = lax.broadcasted_iota(jnp.int32, (n, POOL_WIDTH), 0)
    half = jnp.where(grp == 0, 1, jnp.where(grp == 1, 2, jnp.where(grp == 2, 4, 8)))
    cnt = (jnp.minimum(t + half, n) - jnp.maximum(t - half, 0)).astype(F32)

    def pair(a, s):
        return pltpu.roll(a, s, axis=0) + pltpu.roll(a, ne - s, axis=0)

    out = []
    for x in xs:
        xz = jnp.concatenate([z, x, z], 0)
        s2 = xz + pltpu.roll(xz, 1, axis=0)
        s4 = pair(s2, 1)
        s8 = pair(s4, 2)
        s16 = pair(s8, 4)
        tot = jnp.where(grp == 0, s2[lo:hi],
                        jnp.where(grp == 1, s4[lo:hi], jnp.where(grp == 2, s8[lo:hi], s16[lo:hi])))
        dlt = (tot / cnt - x).astype(BF16)
        out.append(_dot(dlt, w_ref[...]) * sc_ref[...])
    return out


def _pool_kernel(x_ref, w_ref, sc_ref, o_ref, *, n):
    xs = [x_ref[s * n:(s + 1) * n, :].astype(F32) for s in range(x_ref.shape[0] // n)]
    for s, y in enumerate(_pool_mix(xs, w_ref, sc_ref)):
        o_ref[s * n:(s + 1) * n, :] = y.astype(o_ref.dtype)


def _pool_call(pd, w, layer, seq):
    t = pd.shape[0]
    rows = seq
    return pl.pallas_call(
        functools.partial(_pool_kernel, n=seq), grid=(t // rows,),
        in_specs=[pl.BlockSpec((rows, POOL_WIDTH), lambda b: (b, 0)),
                  _layer_spec(w["pool_w"], layer), _layer_spec(w["pool_scale"], layer)],
        out_specs=pl.BlockSpec((rows, POOL_WIDTH), lambda b: (b, 0)),
        out_shape=jax.ShapeDtypeStruct((t, POOL_WIDTH), BF16), compiler_params=_params(1), name="pool",
    )(pd, w["pool_w"], w["pool_scale"])


def _lat_na_kernel(q_ref, k_ref, v_ref, kc_ref, vc_ref, t2_ref, o_ref):
    n = pl.program_id(1)
    rows = DEC_SEQ // GRID_W
    q_rows = NA_Q_BLOCK // GRID_W
    row0 = jnp.clip(q_rows * n - NA_WIN_R // 2, 0, rows - NA_SPAN // GRID_W)
    start = pl.multiple_of(row0 * GRID_W, LANE)
    q = q_ref[...]
    k = k_ref[pl.ds(start, NA_SPAN), :]
    v = v_ref[pl.ds(start, NA_SPAN), :]
    kc = kc_ref[...].astype(BF16)
    vc = vc_ref[...].astype(BF16)

    pairs = NA_SPAN // LANE
    low_half = lax.broadcasted_iota(jnp.int32, (GRID_W, LANE), 1) < GRID_W
    entries, masks = [], []
    for a in range(q_rows):
        r = q_rows * n + a
        r_start = jnp.clip(r - NA_WIN_R // 2, 0, rows - NA_WIN_R)
        for p in range(pairs):
            rk = row0 + 2 * p
            ok0 = ((rk >= r_start) & (rk < r_start + NA_WIN_R)).astype(jnp.int32)
            ok1 = ((rk + 1 >= r_start) & (rk + 1 < r_start + NA_WIN_R)).astype(jnp.int32)
            entries.append(jnp.clip(rk - r + NA_WIN_R, 0, NA_DR))
            masks.append(jnp.where(low_half, ok0, ok1) > 0)

    heads = [slice(h * HEAD_DIM, (h + 1) * HEAD_DIM) for h in range(NA_HEADS)]
    raw = [(_dot_nt(q[:, sl], k[:, sl]), _dot_nt(q[:, sl], kc[:, sl])) for sl in heads]
    probs = []
    for h, (s_loc, s_ctx) in enumerate(raw):
        cols = []
        for p in range(pairs):
            blk = [jnp.where(masks[a * pairs + p], t2_ref[h, entries[a * pairs + p]], NEG_INF)
                   for a in range(q_rows)]
            cols.append(jnp.concatenate(blk, 0))
        probs.append(_softmax_blocks([s_loc + jnp.concatenate(cols, 1), s_ctx]))
    ys = []
    for sl, ((e_loc, e_ctx), l) in zip(heads, probs):
        y = _dot(e_loc.astype(BF16), v[:, sl]) + _dot(e_ctx.astype(BF16), vc[:, sl])
        ys.append(y / l)
    o_ref[...] = jnp.concatenate(ys, -1).astype(o_ref.dtype)


def _lat_na_call(q, k, v, cache_k, cache_v, t2, layer):
    nq = DEC_SEQ // NA_Q_BLOCK
    seq_spec = pl.BlockSpec((DEC_SEQ, 256), lambda b, n: (b, 0))
    cache_spec = pl.BlockSpec((None, None, PAST_LEN, 256), lambda b, n: (b, layer, 0, 0))
    return pl.pallas_call(
        _lat_na_kernel, grid=(DEC_BATCH, nq),
        in_specs=[pl.BlockSpec((NA_Q_BLOCK, 256), lambda b, n: (b * nq + n, 0)), seq_spec, seq_spec,
                  cache_spec, cache_spec, _layer_spec(t2, layer)],
        out_specs=pl.BlockSpec((NA_Q_BLOCK, 256), lambda b, n: (b * nq + n, 0)),
        out_shape=jax.ShapeDtypeStruct((DEC_BATCH * DEC_SEQ, 256), BF16),
        compiler_params=_params(2), name="lat_na",
    )(q, k, v, cache_k, cache_v, t2)


def _lat_swa_kernel(sink_ref, q_ref, k_ref, v_ref, kc_ref, vc_ref, o_ref):
    n = pl.program_id(1)
    start = jnp.clip(n * SWA_Q_BLOCK - SWA_WINDOW, 0, DEC_SEQ - SWA_SPAN)
    start = pl.multiple_of(start, LANE)
    q = q_ref[...]
    k = k_ref[pl.ds(start, SWA_SPAN), :]
    v = v_ref[pl.ds(start, SWA_SPAN), :]
    kc = kc_ref[...].astype(BF16)
    vc = vc_ref[...].astype(BF16)
    group = SWA_HEADS // SWA_KV_HEADS
    m = group * SWA_Q_BLOCK
    q_pos = n * SWA_Q_BLOCK + (lax.broadcasted_iota(jnp.int32, (m, SWA_SPAN), 0) & (SWA_Q_BLOCK - 1))
    k_pos = start + lax.broadcasted_iota(jnp.int32, (m, SWA_SPAN), 1)
    valid = jnp.abs(q_pos - k_pos) <= SWA_WINDOW
    everything = slice(None)
    raw = []
    for (qs, ks), (_, kcs) in zip(_gqa_operands(q, k, everything, everything),
                                  _gqa_operands(q, kc, everything, everything)):
        raw.append((_dot_nt(qs, ks), _dot_nt(qs, kcs)))
    probs = [_softmax_blocks([jnp.where(valid, s_loc, NEG_INF), s_ctx], sink=_gqa_sink(sink_ref, kv, SWA_Q_BLOCK))
             for kv, (s_loc, s_ctx) in enumerate(raw)]
    ys = []
    for kv, ((e_loc, e_ctx), l) in enumerate(probs):
        kvsl = slice(kv * HEAD_DIM, (kv + 1) * HEAD_DIM)
        y = (_dot(e_loc.astype(BF16), v[:, kvsl]) + _dot(e_ctx.astype(BF16), vc[:, kvsl])) / l
        ys += [y[g * SWA_Q_BLOCK:(g + 1) * SWA_Q_BLOCK] for g in range(group)]
    o_ref[...] = jnp.concatenate(ys, -1).astype(o_ref.dtype)


def _lat_swa_call(sink, q, k, v, cache_k, cache_v, layer):
    nq = DEC_SEQ // SWA_Q_BLOCK
    seq_spec = pl.BlockSpec((DEC_SEQ, 128), lambda b, n: (b, 0))
    cache_spec = pl.BlockSpec((None, None, PAST_LEN, 128), lambda b, n: (b, layer, 0, 0))
    return pl.pallas_call(
        _lat_swa_kernel, grid=(DEC_BATCH, nq),
        in_specs=[pl.BlockSpec(memory_space=pltpu.SMEM),
                  pl.BlockSpec((SWA_Q_BLOCK, 256), lambda b, n: (b * nq + n, 0)), seq_spec, seq_spec,
                  cache_spec, cache_spec],
        out_specs=pl.BlockSpec((SWA_Q_BLOCK, 256), lambda b, n: (b * nq + n, 0)),
        out_shape=jax.ShapeDtypeStruct((DEC_BATCH * DEC_SEQ, 256), BF16),
        compiler_params=_params(2), name="lat_swa",
    )(sink, q, k, v, cache_k, cache_v)


def _lat_mla_kernel(cq_ref, ckv_ref, kr_ref, cckv_ref, ckr_ref, cm_ref, sm_ref,
                    qn_ref, wuq_ref, kvn_ref, wuk_ref, wuvt_ref, o_ref, kcat_s, vt_s):
    @pl.when(pl.program_id(1) == 0)
    def _():
        kc, vc = _mla_kv(cckv_ref[...], ckr_ref[...], kvn_ref, wuk_ref, wuvt_ref, True)
        kcat_s[0:PAST_LEN, :] = kc.astype(BF16)
        vt_s[:, 0:PAST_LEN] = vc.astype(BF16)
        kl, vl = _mla_kv(ckv_ref[...].astype(F32), kr_ref[...].astype(F32), kvn_ref, wuk_ref, wuvt_ref, True)
        kcat_s[PAST_LEN:, :] = kl.astype(BF16)
        vt_s[:, PAST_LEN:] = vl.astype(BF16)

    q = _mla_q(cq_ref[...].astype(F32), qn_ref, wuq_ref)
    q = (_rope(q, cm_ref[...], sm_ref[...], 8) * (MLA_SCALE * LOG2E)).astype(BF16)
    scores = [_dot_nt(kcat_s[:, h * LANE:(h + 1) * LANE], q[:, h * LANE:(h + 1) * LANE]) for h in range(MLA_HEADS)]
    ys = []
    for h, st in enumerate(scores):
        e = jnp.exp2(st - jnp.max(st, 0, keepdims=True))
        l = jnp.sum(e, 0, keepdims=True)
        ys.append(_dot(vt_s[h * MLA_V:(h + 1) * MLA_V, :], e.astype(BF16)) / l)
    o_ref[...] = jnp.concatenate(ys, 0).T.astype(o_ref.dtype)


def _lat_mla_call(cq, ckv, kr, cache_ckv, cache_kr, cm, sm, w, layer):
    qb = MLA_Q_BLOCK
    nq = DEC_SEQ // qb
    seq_spec = pl.BlockSpec((DEC_SEQ, 128), lambda b, n: (b, 0))
    cache_spec = pl.BlockSpec((None, None, PAST_LEN, 128), lambda b, n: (b, layer, 0, 0))
    tab_spec = pl.BlockSpec((qb, 512), lambda b, n: (n, 0))
    return pl.pallas_call(
        _lat_mla_kernel, grid=(DEC_BATCH, nq),
        in_specs=[pl.BlockSpec((qb, 256), lambda b, n: (b * nq + n, 0)), seq_spec, seq_spec,
                  cache_spec, cache_spec, tab_spec, tab_spec] + [_layer_spec(w[k], layer) for k in _MLA_W_LAT],
        out_specs=pl.BlockSpec((qb, 256), lambda b, n: (b * nq + n, 0)),
        out_shape=jax.ShapeDtypeStruct((DEC_BATCH * DEC_SEQ, 256), BF16),
        scratch_shapes=[pltpu.VMEM((PAST_LEN + DEC_SEQ, MLA_HEADS * LANE), BF16),
                        pltpu.VMEM((MLA_HEADS * MLA_V, PAST_LEN + DEC_SEQ), BF16)],
        compiler_params=_params(2), name="lat_mla",
    )(cq, ckv, kr, cache_ckv, cache_kr, cm, sm, *[w[k] for k in _MLA_W_LAT])


_MERGE_PARTS = 4


def _merge_kernel(x_ref, mod_ref, gpre_ref, ya_ref, yb_ref, yc_ref, yd_ref,
                  wg_ref, bg_ref, wb_ref, wo_ref, gpost_ref, o_ref):
    tm = x_ref.shape[0] // _MERGE_PARTS
    for p in range(_MERGE_PARTS):
        rows = slice(p * tm, (p + 1) * tm)
        x = x_ref[rows, :]
        h = (_rms(x, gpre_ref[...]) * (1.0 + mod_ref[0, 1:2, :]) + mod_ref[0, 0:1, :]).astype(BF16)
        merged = None
        for k, y_ref in enumerate((ya_ref, yb_ref, yc_ref, yd_ref)):
            cols = slice(k * D_MODEL, (k + 1) * D_MODEL)
            gate = _sigmoid(_dot(h, wg_ref[:, cols]) + bg_ref[:, cols])
            term = gate * _dot(y_ref[rows, :], wb_ref[k])
            merged = term if merged is None else merged + term
        o = _dot(merged.astype(BF16), wo_ref[...])
        o_ref[rows, :] = x + mod_ref[0, 2:3, :] * _rms(o, gpost_ref[...])


_MERGE_W = ("w_gate", "b_gate", "w_branch", "w_out", "g_attn_post")


def _merge_call(x2d, mod, ys, w, layer):
    t = x2d.shape[0]
    tm = 1024
    tiles_per_mod = t // tm // mod.shape[0]
    tile = pl.BlockSpec((tm, D_MODEL), lambda i: (i, 0))
    ytile = pl.BlockSpec((tm, BRANCH_W), lambda i: (i, 0))
    return pl.pallas_call(
        _merge_kernel, grid=(t // tm,),
        in_specs=[tile, pl.BlockSpec((1, 6, D_MODEL), lambda i: (i // tiles_per_mod, 0, 0)),
                  _layer_spec(w["g_attn_pre"], layer), ytile, ytile, ytile, ytile]
                 + [_layer_spec(w[k], layer) for k in _MERGE_W],
        out_specs=tile,
        out_shape=jax.ShapeDtypeStruct((t, D_MODEL), F32),
        compiler_params=_params(1), name="merge",
    )(x2d, mod, w["g_attn_pre"], *ys, *[w[k] for k in _MERGE_W])


_GAP = 8


def _ffn_kernel(*refs, seqs, halo, layer):
    if halo:
        x_ref, xp_ref, xn_ref = refs[:3]
        refs = refs[3:]
    else:
        x_ref = refs[0]
        refs = refs[1:]
    mod_ref, gpre_ref, wa_ref, wg_ref, ca_ref, cg_ref, gpost_ref, wd_hbm, o_ref, wd_ref, wd_sem = refs
    first_step = pl.program_id(0) == 0
    wd_copy = pltpu.make_async_copy(wd_hbm.at[layer], wd_ref, wd_sem.at[0])

    @pl.when(first_step)
    def _():
        wd_copy.start()
    tm = x_ref.shape[0]
    shift, scale, gate = mod_ref[0, 3:4, :], mod_ref[0, 4:5, :], mod_ref[0, 5:6, :]

    def pre(xx):
        return _rms(xx, gpre_ref[...]) * (1.0 + scale) + shift

    if halo:
        i = pl.program_id(0) % halo
        x = x_ref[...]
        hp = jnp.where(i == 0, 0.0, pre(xp_ref[...]))
        hn = jnp.where(i == halo - 1, 0.0, pre(xn_ref[...]))
        chains = [(x, jnp.concatenate([hp, pre(x), hn], 0).astype(BF16), _GAP)]
        seq_len = tm
    else:
        seq_len = tm // seqs
        chains = []
        for s in range(seqs):
            x = x_ref[s * seq_len:(s + 1) * seq_len, :]
            chains.append((x, pre(x).astype(BF16), 0))
    edge_row = lax.broadcasted_iota(jnp.int32, (_GAP, FF_CHUNK), 0)

    def conv(u, c_ref, cols):
        rows = u.shape[0]
        prev = pltpu.roll(u, 1, axis=0)
        nxt = pltpu.roll(u, rows - 1, axis=0)
        if not halo:
            prev = jnp.concatenate([jnp.where(edge_row == 0, 0.0, prev[:_GAP]), prev[_GAP:]], 0)
            nxt = jnp.concatenate([nxt[:rows - _GAP], jnp.where(edge_row == _GAP - 1, 0.0, nxt[rows - _GAP:])], 0)
        return prev * c_ref[0:1, cols] + u * c_ref[1:2, cols] + nxt * c_ref[2:3, cols]

    all_acts = []
    for _, hb, _ in chains:
        acts = []
        for c in range(D_FF_PAD // FF_CHUNK):
            cols = slice(c * FF_CHUNK, (c + 1) * FF_CHUNK)
            a = conv(_dot(hb, wa_ref[:, cols]), ca_ref, cols)
            g = conv(_dot(hb, wg_ref[:, cols]), cg_ref, cols)
            acts.append((g * _sigmoid(g) * a).astype(BF16))
        all_acts.append(jnp.concatenate(acts, 1))
    @pl.when(first_step)
    def _():
        wd_copy.wait()

    accs = [_dot(acts, wd_ref[...]) for acts in all_acts]
    for s, ((x, _, first), acc) in enumerate(zip(chains, accs)):
        ys = _rms(acc[first:first + seq_len], gpost_ref[...])
        o_ref[s * seq_len:(s + 1) * seq_len, :] = x + gate * ys


_FFN_W = ("g_ffn_pre", "ffn_wa", "ffn_wg", "ffn_ca", "ffn_cg", "g_ffn_post")


def _ffn_call(x2d, mod, w, layer, seq, tm):
    t = x2d.shape[0]
    tile = pl.BlockSpec((tm, D_MODEL), lambda i: (i, 0))
    in_specs = [tile]
    args = [x2d]
    if tm < seq:
        halo, seqs = seq // tm, 1
        r = tm // _GAP
        last = t // _GAP - 1
        in_specs += [pl.BlockSpec((_GAP, D_MODEL), lambda i: (jnp.maximum(i * r - 1, 0), 0)),
                     pl.BlockSpec((_GAP, D_MODEL), lambda i: (jnp.minimum((i + 1) * r, last), 0))]
        args += [x2d, x2d]
    else:
        halo, seqs = 0, tm // seq
    tiles_per_mod = t // tm // mod.shape[0]
    in_specs += [pl.BlockSpec((1, 6, D_MODEL), lambda i: (i // tiles_per_mod, 0, 0))]
    in_specs += [_layer_spec(w[k], layer) for k in _FFN_W] + [pl.BlockSpec(memory_space=pl.ANY)]
    args += [mod] + [w[k] for k in _FFN_W] + [w["ffn_wd"]]
    return pl.pallas_call(
        functools.partial(_ffn_kernel, seqs=seqs, halo=halo, layer=layer), grid=(t // tm,),
        scratch_shapes=[pltpu.VMEM(w["ffn_wd"].shape[1:], BF16), pltpu.SemaphoreType.DMA((1,))],
        in_specs=in_specs, out_specs=tile,
        out_shape=jax.ShapeDtypeStruct((t, D_MODEL), F32),
        compiler_params=_params(1), name="ffn",
    )(*args)


def _rope_tables():
    t = np.arange(DEC_SEQ)
    pos = (t // GRID_W, t % GRID_W)

    def tab(d):
        half = d // 4
        inv = np.float32(ROPE_BASE) ** (-np.arange(half, dtype=np.float32) / np.float32(half))
        cs, sn = [], []
        for p in pos:
            ang = p.astype(np.float32)[:, None] * inv[None, :]
            cs += [np.cos(ang), np.cos(ang)]
            sn += [-np.sin(ang), np.sin(ang)]
        return np.concatenate(cs, -1), np.concatenate(sn, -1)

    c64, s64 = tab(HEAD_DIM)
    c32, s32 = tab(MLA_ROPE)
    pad = LANE - MLA_NOPE - MLA_ROPE
    cm = np.concatenate([np.ones((DEC_SEQ, MLA_NOPE), np.float32), c32, np.ones((DEC_SEQ, pad), np.float32)], -1)
    sm = np.concatenate([np.zeros((DEC_SEQ, MLA_NOPE), np.float32), s32, np.zeros((DEC_SEQ, pad), np.float32)], -1)
    return tuple(jnp.asarray(np.tile(x, (1, 4)), F32) for x in (c64, s64, cm, sm))


def _na_table_kernel(rpb_ref, e_ref, ok_ref, o_ref):
    r = rpb_ref[...]
    r1 = r.astype(BF16)
    r2 = (r - r1.astype(F32)).astype(BF16)
    r3 = (r - r1.astype(F32) - r2.astype(F32)).astype(BF16)
    e = e_ref[...]
    t = _dot(r1, e) + _dot(r2, e) + _dot(r3, e)
    o_ref[...] = jnp.where(ok_ref[...] > 0, t * LOG2E, NEG_INF)


def _na_bias_tables(na_rpb):
    c = np.arange(GRID_W)[:, None]
    w = np.arange(GRID_W)[None, :]
    dc = (w - c + NA_WIN_C - 1).reshape(-1)
    onehot = (np.arange(LANE)[:, None] == dc[None, :]).astype(np.float32)
    c_start = np.clip(c - NA_WIN_C // 2, 0, GRID_W - NA_WIN_C)
    ok = ((w >= c_start) & (w < c_start + NA_WIN_C)).reshape(1, -1).astype(np.int32)
    rows = DEPTH * NA_HEADS * NA_DR
    rpb2 = jnp.pad(na_rpb.reshape(rows, NA_DC), ((0, LANE - rows), (0, LANE - NA_DC)))
    t = pl.pallas_call(
        _na_table_kernel, out_shape=jax.ShapeDtypeStruct((LANE, GRID_W * GRID_W), F32), name="na_table",
        compiler_params=pltpu.CompilerParams(vmem_limit_bytes=VMEM_LIMIT),
    )(rpb2, jnp.asarray(onehot, BF16), jnp.asarray(ok))
    t = t[:rows].reshape(DEPTH, NA_HEADS, NA_DR, GRID_W, GRID_W)
    t = jnp.pad(t, ((0, 0), (0, 0), (1, 1), (0, 0), (0, 0)), constant_values=NEG_INF)
    return jnp.concatenate([t[:, :, :-1], t[:, :, 1:]], -1)


def _pad_last(w, n):
    return jnp.pad(w, ((0, 0),) * (w.ndim - 1) + ((0, n - w.shape[-1]),))


def _prep_weights(g_attn_pre, g_attn_post, g_ffn_pre, g_ffn_post, w_in, w_gate, b_gate, mla_q_norm, mla_w_uq,
                  mla_kv_norm, mla_w_ukv, pool_w, pool_scale, w_branch, w_out, ffn_w_up, ffn_conv, ffn_w_down):
    q_scale = ATT_SCALE * LOG2E
    b0, c0, d0 = 768, 1120, 1632
    wt = w_in.transpose(0, 2, 1)

    def rows(lo, hi, before=0, after=0):
        return jnp.pad(wt[:, lo:hi], ((0, 0), (before, after), (0, 0)))

    w_in_p = jnp.concatenate([
        wt[:, d0:],
        wt[:, c0:c0 + 256] * q_scale,
        wt[:, 256:768],
        rows(b0 + 320, b0 + 352, _KR_LANE, LANE - _KR_LANE - MLA_ROPE),
        wt[:, c0 + 256:d0],
        wt[:, b0 + MLA_Q_RANK:b0 + MLA_Q_RANK + MLA_KV_RANK],
        wt[:, :256] * q_scale,
        rows(b0, b0 + MLA_Q_RANK, 0, 256 - MLA_Q_RANK)], 1).astype(BF16)

    wuq = mla_w_uq.reshape(DEPTH, MLA_Q_RANK, MLA_HEADS, MLA_NOPE + MLA_ROPE)
    wuq = jnp.pad(wuq, ((0, 0), (0, 256 - MLA_Q_RANK), (0, 0), (0, LANE - MLA_NOPE - MLA_ROPE)))
    wukv = mla_w_ukv.reshape(DEPTH, MLA_KV_RANK, MLA_HEADS, MLA_NOPE + MLA_V)
    wuk = _pad_last(wukv[..., :MLA_NOPE], LANE)
    wuv = wukv[..., MLA_NOPE:].reshape(DEPTH, MLA_KV_RANK, MLA_HEADS * MLA_V).astype(BF16)

    eye = np.eye(len(POOL_WINDOWS), dtype=np.float32)
    w_bd = (pool_w[:, :, :, None, :] * eye[None, :, None, :, None]).reshape(DEPTH, POOL_WIDTH, POOL_WIDTH)

    return dict(
        g_attn_pre=g_attn_pre[:, None, :], g_attn_post=g_attn_post[:, None, :],
        g_ffn_pre=g_ffn_pre[:, None, :], g_ffn_post=g_ffn_post[:, None, :],
        w_in_p=w_in_p,
        mla_qn=_pad_last(mla_q_norm[:, None, :], 256),
        mla_wuq=wuq.reshape(DEPTH, 256, MLA_HEADS * LANE).astype(BF16),
        mla_kvn=mla_kv_norm[:, None, :],
        mla_wuk=wuk.reshape(DEPTH, MLA_KV_RANK, MLA_HEADS * LANE).astype(BF16),
        mla_wuv=wuv, mla_wuv_t=wuv.transpose(0, 2, 1),
        pool_w=w_bd.astype(BF16), pool_scale=pool_scale[:, None, :],
        w_gate=w_gate.astype(BF16), b_gate=b_gate[:, None, :],
        w_branch=w_branch.astype(BF16), w_out=w_out.astype(BF16),
        ffn_wa=_pad_last(ffn_w_up[:, :, :D_FF], D_FF_PAD).astype(BF16),
        ffn_wg=_pad_last(ffn_w_up[:, :, D_FF:], D_FF_PAD).astype(BF16),
        ffn_ca=_pad_last(ffn_conv[:, :, :D_FF], D_FF_PAD), ffn_cg=_pad_last(ffn_conv[:, :, D_FF:], D_FF_PAD),
        ffn_wd=jnp.pad(ffn_w_down, ((0, 0), (0, D_FF_PAD - D_FF), (0, 0))).astype(BF16))


def kernel(x_prompt, x_sample, cache_na_k, cache_na_v, cache_mla_ckv, cache_mla_krope, cache_swa_k, cache_swa_v, c, c_ctx, w_mod, b_mod, g_attn_pre, g_attn_post, g_ffn_pre, g_ffn_post, w_in, w_gate, b_gate, na_rpb, mla_q_norm, mla_w_uq, mla_kv_norm, mla_w_ukv, swa_sink, pool_w, pool_scale, w_branch, w_out, ffn_w_up, ffn_conv, ffn_w_down):
    x_p = x_prompt.reshape(BATCH * SEQ, D_MODEL)
    x_s = x_sample.reshape(DEC_BATCH * DEC_SEQ, D_MODEL)

    cv = jnp.concatenate([c_ctx[None, :], c, jnp.zeros((8 - 1 - DEC_BATCH, D_MODEL), F32)], 0)
    mod = _mod_call(cv, w_mod, b_mod).reshape(DEPTH, 8, 6, D_MODEL)
    w = _prep_weights(g_attn_pre, g_attn_post, g_ffn_pre, g_ffn_post, w_in, w_gate, b_gate, mla_q_norm, mla_w_uq,
                      mla_kv_norm, mla_w_ukv, pool_w, pool_scale, w_branch, w_out, ffn_w_up, ffn_conv, ffn_w_down)
    rope_tabs = _rope_tables()
    na_t2 = _na_bias_tables(na_rpb)
    cache_na_k = cache_na_k.reshape(DEC_BATCH, DEPTH, PAST_LEN, 256)
    cache_na_v = cache_na_v.reshape(DEC_BATCH, DEPTH, PAST_LEN, 256)
    cache_swa_k = cache_swa_k.reshape(DEC_BATCH, DEPTH, PAST_LEN, 128)
    cache_swa_v = cache_swa_v.reshape(DEC_BATCH, DEPTH, PAST_LEN, 128)
    cache_kr = jnp.pad(cache_mla_krope, ((0, 0), (0, 0), (0, 0), (_KR_LANE, LANE - _KR_LANE - MLA_ROPE)))

    carried, ckvs = (), []
    for l in range(DEPTH):
        mod_p = mod[l, 0:1]
        qa, ka_t, va_t, va, cq, ckv, kr, kr_t, qc, kc_t, vc_t, vc, yd = _inproj_call(x_p, mod_p, w, l,
                                                                                      carried=carried)
        carried = (ka_t, va_t, kr_t, kc_t, vc_t)
        ckvs.append(ckv)
        ys = (_ctx_attn_call(qa, ka_t, va, l), _ctx_mla_call(cq, ckv, kr, w, l),
              _ctx_swa_call(swa_sink[l], qc, kc_t, vc, l), yd)
        x_p = _merge_call(x_p, mod_p, ys, w, l)
        x_p = _ffn_call(x_p, mod_p, w, l, SEQ, 2 * SEQ)

        mod_s = mod[l, 1:1 + DEC_BATCH]
        qa, ka, va, cq, ckv, kr, qc, kc, vc, pd = _inproj_call(x_s, mod_s, w, l, rope_tabs)
        ys = (_lat_na_call(qa, ka, va, cache_na_k, cache_na_v, na_t2, l),
              _lat_mla_call(cq, ckv, kr, cache_mla_ckv, cache_kr, rope_tabs[2], rope_tabs[3], w, l),
              _lat_swa_call(swa_sink[l], qc, kc, vc, cache_swa_k, cache_swa_v, l),
              _pool_call(pd, w, l, DEC_SEQ))
        x_s = _merge_call(x_s, mod_s, ys, w, l)
        x_s = _ffn_call(x_s, mod_s, w, l, DEC_SEQ, 1024)

    ka_t, va_t, kr_t, kc_t, vc_t = carried

    def heads_last(a, heads):
        return a.reshape(BATCH, DEPTH, heads, HEAD_DIM, SEQ).transpose(0, 1, 4, 2, 3)

    return (x_p.reshape(BATCH, SEQ, D_MODEL), x_s.reshape(DEC_BATCH, DEC_SEQ, D_MODEL),
            heads_last(ka_t, NA_HEADS), heads_last(va_t, NA_HEADS),
            jnp.stack([c.reshape(BATCH, SEQ, MLA_KV_RANK) for c in ckvs], 1),
            kr_t[:, :, _KR_LANE:_KR_LANE + MLA_ROPE, :].transpose(0, 1, 3, 2),
            heads_last(kc_t, SWA_KV_HEADS), heads_last(vc_t, SWA_KV_HEADS))
```
